```python
import math
import jax, jax.numpy as jnp
from jax import lax
import numpy as np

D_MODEL = 1024
BATCH = 1
SEQ = 16384
DEPTH = 2
DEC_BATCH = 16
DEC_SEQ = 32
PAST_LEN = 1024

CHUNK = 64
A_PAST_CHUNKS = 8
A_PAST = A_PAST_CHUNKS * CHUNK
HA = 8
DHA = 64
A_WIDTH = HA * DHA
REL_CLIP = 128
B_WIDTH = 512
HB = 8
DHB = B_WIDTH // HB
CONV_W = 4
LRU_C = 8.0
HC = 4
DKC = 256
DVC = 512
C_QK = HC * DKC
C_V = HC * DVC
ROPE_BASE = 10000.0
GN_EPS = 1e-5
N_EXPERTS = 32
TOP_K = 4
D_FF = 1024
SWIGLU_LIMIT = 7.0
SWIGLU_ALPHA = 1.702
MOE_BLOCK = 128
IN_AB = 3 * A_WIDTH + 2 * B_WIDTH
IN_C = 2 * C_QK + 2 * C_V
N_AB = (DEPTH + 1) // 2
N_C = DEPTH // 2
ALPHA = (2.0 * DEPTH) ** 0.25
BETA = (8.0 * DEPTH) ** -0.25
LN_EPS = 1e-5

kernel_name = 'hybrid_chunk_streaming_encoder_step'

f32 = jnp.float32


def _layernorm(x, g, b):
    xf = x.astype(f32)
    mu = xf.mean(-1, keepdims=True)
    var = jnp.square(xf - mu).mean(-1, keepdims=True)
    return ((xf - mu) * lax.rsqrt(var + LN_EPS) * g.astype(f32) + b.astype(f32)).astype(x.dtype)


def _rel_index(d):
    return jnp.clip(d, -REL_CLIP, REL_CLIP) + REL_CLIP


def _band_attention_prompt(q, k, v, rel_bias):
    B, S = q.shape[:2]
    nc = S // CHUNK
    nb = A_PAST_CHUNKS + 1
    qc = q.reshape(B, nc, CHUNK, HA, DHA)

    def band(t):
        tc = t.reshape(B, nc, CHUNK, HA, DHA)
        tp = jnp.concatenate([jnp.zeros((B, A_PAST_CHUNKS, CHUNK, HA, DHA), t.dtype), tc], axis=1)
        return jnp.concatenate([tp[:, j:j + nc] for j in range(nb)], axis=2)

    kb, vb = band(k), band(v)
    s = jnp.einsum('bnqhd,bnkhd->bnhqk', qc, kb, preferred_element_type=f32) * (DHA ** -0.5)
    qq = jnp.arange(CHUNK)
    kk = jnp.arange(nb * CHUNK)
    d = (A_PAST_CHUNKS * CHUNK + qq[:, None]) - kk[None, :]
    bias = rel_bias[:, _rel_index(d)].astype(f32)
    valid = (jnp.arange(nc)[:, None] - A_PAST_CHUNKS + kk[None, :] // CHUNK) >= 0
    s = jnp.where(valid[None, :, None, None, :], s + bias[None, None], -jnp.inf)
    p = jax.nn.softmax(s, axis=-1).astype(v.dtype)
    o = jnp.einsum('bnhqk,bnkhd->bnqhd', p, vb)
    return o.reshape(B, S, HA, DHA)


def _band_attention_sample(q, k, v, k_cache, v_cache, rel_bias):
    P = k_cache.shape[1]
    T = q.shape[1]
    kk = jnp.concatenate([k_cache.astype(k.dtype), k], axis=1)
    vv = jnp.concatenate([v_cache.astype(v.dtype), v], axis=1)
    s = jnp.einsum('bqhd,bkhd->bhqk', q, kk, preferred_element_type=f32) * (DHA ** -0.5)
    d = jnp.arange(T)[:, None] - (jnp.arange(P + T)[None, :] - P)
    s = s + rel_bias[:, _rel_index(d)].astype(f32)[None]
    p = jax.nn.softmax(s, axis=-1).astype(vv.dtype)
    return jnp.einsum('bhqk,bkhd->bqhd', p, vv)


def _lin_combine(e1, e2):
    a1, b1 = e1
    a2, b2 = e2
    return a1 * a2, a2 * b1 + b2


def _rglru(xb, conv_state, h0, conv_w, conv_b, w_rg, b_rg, w_ig, b_ig, lam):
    B, T, _ = xb.shape
    xp = jnp.concatenate([conv_state.astype(xb.dtype), xb], axis=1)
    u = conv_b + sum(xp[:, j:j + T] * conv_w[j] for j in range(CONV_W))
    conv_new = xp[:, xp.shape[1] - (CONV_W - 1):]
    uf = u.astype(f32)
    uh = uf.reshape(B, T, HB, DHB)
    r = jax.nn.sigmoid(jnp.einsum('bthi,hij->bthj', uh, w_rg.astype(f32)).reshape(B, T, B_WIDTH) + b_rg.astype(f32))
    i = jax.nn.sigmoid(jnp.einsum('bthi,hij->bthj', uh, w_ig.astype(f32)).reshape(B, T, B_WIDTH) + b_ig.astype(f32))
    log_a = -LRU_C * r * jax.nn.softplus(-lam.astype(f32))
    a = jnp.exp(log_a)
    bterm = jnp.sqrt(-jnp.expm1(2.0 * log_a)) * (i * uf)
    bterm = bterm.at[:, 0].add(a[:, 0] * h0.astype(f32))
    _, h = lax.associative_scan(_lin_combine, (a, bterm), axis=1)
    return h, conv_new, h[:, -1]


def _ab_mixer(x, k_cache, v_cache, conv_state, h0, w_in, rel_bias, conv_w, conv_b, w_rg, b_rg, w_ig, b_ig, lam, w_out):
    B, T, _ = x.shape
    z = jnp.dot(x, w_in)
    q, k, v, xb, gb = jnp.split(z, [A_WIDTH, 2 * A_WIDTH, 3 * A_WIDTH, 3 * A_WIDTH + B_WIDTH], axis=-1)
    q = q.reshape(B, T, HA, DHA)
    k = k.reshape(B, T, HA, DHA)
    v = v.reshape(B, T, HA, DHA)
    if k_cache is None:
        att = _band_attention_prompt(q, k, v, rel_bias)
        keep = min(A_PAST, T)
        k_new, v_new = k[:, T - keep:], v[:, T - keep:]
    else:
        att = _band_attention_sample(q, k, v, k_cache, v_cache, rel_bias)
        k_new, v_new = k, v
    h, conv_new, h_last = _rglru(xb, conv_state, h0, conv_w, conv_b, w_rg, b_rg, w_ig, b_ig, lam)
    rec = h.astype(x.dtype) * jax.nn.gelu(gb)
    out = jnp.dot(jnp.concatenate([att.reshape(B, T, A_WIDTH), rec], axis=-1), w_out)
    return out, k_new, v_new, conv_new, h_last


def _log_gamma():
    return jnp.log(1.0 - jnp.exp2(-5.0 - jnp.arange(HC, dtype=f32)))


def _rotary(x, pos):
    half = x.shape[-1] // 2
    inv = 1.0 / (ROPE_BASE ** (jnp.arange(half, dtype=f32) / half))
    ang = pos.astype(f32)[:, None] * inv[None, :]
    cos = jnp.cos(ang)[None, :, None, :]
    sin = jnp.sin(ang)[None, :, None, :]
    x1, x2 = x[..., :half], x[..., half:]
    return jnp.concatenate([x1 * cos - x2 * sin, x2 * cos + x1 * sin], axis=-1)


def _ret_chunk(S, qkv):
    q, k, v = qkv
    T = q.shape[1]
    lg = _log_gamma()
    n = jnp.arange(T, dtype=f32)
    dmat = jnp.exp(lg[:, None, None] * jnp.abs(n[:, None] - n[None, :]))
    s = jnp.einsum('bnhd,bmhd->bhnm', q, k) * dmat[None]
    o = jnp.einsum('bhnm,bmhe->bnhe', s, v)
    o = o + jnp.einsum('bnhd,bhde->bnhe', q, S) * jnp.exp((n[:, None] + 1.0) * lg[None, :])[None, :, :, None]
    kd = k * jnp.exp((T - 1.0 - n)[:, None] * lg[None, :])[None, :, :, None]
    S_new = jnp.exp(lg * T)[None, :, None, None] * S + jnp.einsum('bmhd,bmhe->bhde', kd, v)
    return S_new, o


def _retention_mixer(x, S0, pos, w_in, gn_g, gn_b, w_out):
    B, T, _ = x.shape
    z = jnp.dot(x, w_in)
    q, k, v, g = jnp.split(z, [C_QK, 2 * C_QK, 2 * C_QK + C_V], axis=-1)
    q = _rotary(q.reshape(B, T, HC, DKC).astype(f32), pos)
    k = _rotary(k.reshape(B, T, HC, DKC).astype(f32), pos) * (DKC ** -0.5)
    v = v.reshape(B, T, HC, DVC).astype(f32)
    S0 = S0.astype(f32)
    if T <= CHUNK:
        S, o = _ret_chunk(S0, (q, k, v))
    else:
        nc = T // CHUNK

        def to_chunks(t):
            return jnp.moveaxis(t.reshape(B, nc, CHUNK, HC, t.shape[-1]), 1, 0)

        S, o = lax.scan(_ret_chunk, S0, (to_chunks(q), to_chunks(k), to_chunks(v)))
        o = jnp.moveaxis(o, 0, 1).reshape(B, T, HC, DVC)
    mu = o.mean(-1, keepdims=True)
    var = jnp.square(o - mu).mean(-1, keepdims=True)
    y = ((o - mu) * lax.rsqrt(var + GN_EPS)).reshape(B, T, C_V) * gn_g.astype(f32) + gn_b.astype(f32)
    y = (jax.nn.silu(g.astype(f32)) * y).astype(x.dtype)
    return jnp.dot(y, w_out), S


def _moe(x, w_router, b_router, w_up, b_up, w_down, b_down):
    shp = x.shape
    xt = x.reshape(-1, D_MODEL)
    N = xt.shape[0]
    logits = jnp.dot(xt.astype(f32), w_router.astype(f32)) + b_router.astype(f32)
    top_v, top_e = lax.top_k(logits, TOP_K)
    gates = jax.nn.softmax(top_v, axis=-1)
    NK = N * TOP_K
    flat_e = top_e.reshape(-1).astype(jnp.int32)
    flat_t = jnp.repeat(jnp.arange(N, dtype=jnp.int32), TOP_K)
    flat_g = gates.reshape(-1)
    order = jnp.argsort(flat_e)
    se, st, sg = flat_e[order], flat_t[order], flat_g[order]
    counts = jnp.zeros((N_EXPERTS,), jnp.int32).at[flat_e].add(1)
    padded = (counts + MOE_BLOCK - 1) // MOE_BLOCK * MOE_BLOCK
    start_sorted = jnp.cumsum(counts) - counts
    end_padded = jnp.cumsum(padded)
    start_padded = end_padded - padded
    dest = start_padded[se] + jnp.arange(NK, dtype=jnp.int32) - start_sorted[se]
    nblk = NK // MOE_BLOCK + N_EXPERTS
    R = nblk * MOE_BLOCK
    buf_t = jnp.full((R,), N, jnp.int32).at[dest].set(st)
    buf_g = jnp.zeros((R,), f32).at[dest].set(sg)
    blk_e = jnp.minimum(jnp.searchsorted(end_padded, jnp.arange(nblk, dtype=jnp.int32) * MOE_BLOCK, side='right'),
                        N_EXPERTS - 1).astype(jnp.int32)
    xg = jnp.concatenate([xt, jnp.zeros((1, D_MODEL), xt.dtype)], axis=0)[buf_t].reshape(nblk, MOE_BLOCK, D_MODEL)

    def expert_block(args):
        xb, e = args
        h = jnp.dot(xb, w_up[e]) + b_up[e]
        glu = jnp.minimum(h[:, :D_FF], SWIGLU_LIMIT)
        lin = jnp.clip(h[:, D_FF:], -SWIGLU_LIMIT, SWIGLU_LIMIT)
        act = glu * jax.nn.sigmoid(SWIGLU_ALPHA * glu) * (lin + 1.0)
        return jnp.dot(act, w_down[e]) + b_down[e]

    y = lax.map(expert_block, (xg, blk_e)).reshape(R, D_MODEL)
    out = jnp.zeros((N + 1, D_MODEL), f32).at[buf_t].add(y.astype(f32) * buf_g[:, None])[:N]
    return out.astype(x.dtype).reshape(shp)


def setup_inputs(seed: int = 0) -> dict:
    key = jax.random.key(seed)
    ks = jax.random.split(key, 32)

    def nrm(k, shape, scale):
        return jax.random.normal(k, shape, f32) * scale

    a_cache_len = min(A_PAST, PAST_LEN)
    u = jax.random.uniform(ks[14], (N_AB, B_WIDTH), f32, 0.9, 0.999)
    s = u ** (1.0 / LRU_C)
    lam = jnp.log(s) - jnp.log1p(-s)
    return {
        'x_prompt': nrm(ks[0], (BATCH, SEQ, D_MODEL), 1.0),
        'x_sample': nrm(ks[1], (DEC_BATCH, DEC_SEQ, D_MODEL), 1.0),
        'cache_a_k': nrm(ks[2], (N_AB, DEC_BATCH, a_cache_len, HA, DHA), 1.0),
        'cache_a_v': nrm(ks[3], (N_AB, DEC_BATCH, a_cache_len, HA, DHA), 1.0),
        'state_b_conv': nrm(ks[4], (N_AB, DEC_BATCH, CONV_W - 1, B_WIDTH), 1.0),
        'state_b_h': nrm(ks[5], (N_AB, DEC_BATCH, B_WIDTH), 0.5),
        'state_c_s': nrm(ks[6], (N_C, DEC_BATCH, HC, DKC, DVC), 0.1),
        'ab_w_in': nrm(ks[7], (N_AB, D_MODEL, IN_AB), D_MODEL ** -0.5),
        'ab_rel_bias': nrm(ks[8], (N_AB, HA, 2 * REL_CLIP + 1), 0.1),
        'ab_conv_w': nrm(ks[9], (N_AB, CONV_W, B_WIDTH), CONV_W ** -0.5),
        'ab_conv_b': nrm(ks[10], (N_AB, B_WIDTH), 0.02),
        'ab_w_rg': nrm(ks[11], (N_AB, HB, DHB, DHB), DHB ** -0.5),
        'ab_b_rg': nrm(ks[12], (N_AB, B_WIDTH), 0.1),
        'ab_w_ig': nrm(ks[13], (N_AB, HB, DHB, DHB), DHB ** -0.5),
        'ab_b_ig': nrm(ks[15], (N_AB, B_WIDTH), 0.1),
        'ab_lambda': lam,
        'ab_w_out': nrm(ks[16], (N_AB, A_WIDTH + B_WIDTH, D_MODEL), BETA * (A_WIDTH + B_WIDTH) ** -0.5),
        'c_w_in': nrm(ks[17], (N_C, D_MODEL, IN_C), D_MODEL ** -0.5),
        'c_gn_g': 1.0 + nrm(ks[18], (N_C, C_V), 0.02),
        'c_gn_b': nrm(ks[19], (N_C, C_V), 0.02),
        'c_w_out': nrm(ks[20], (N_C, C_V, D_MODEL), BETA * C_V ** -0.5),
        'ln1_g': 1.0 + nrm(ks[21], (DEPTH, D_MODEL), 0.02),
        'ln1_b': nrm(ks[22], (DEPTH, D_MODEL), 0.02),
        'ln2_g': 1.0 + nrm(ks[23], (DEPTH, D_MODEL), 0.02),
        'ln2_b': nrm(ks[24], (DEPTH, D_MODEL), 0.02),
        'moe_w_router': nrm(ks[25], (DEPTH, D_MODEL, N_EXPERTS), D_MODEL ** -0.5),
        'moe_b_router': nrm(ks[26], (DEPTH, N_EXPERTS), 0.01),
        'moe_w_up': nrm(ks[27], (DEPTH, N_EXPERTS, D_MODEL, 2 * D_FF), D_MODEL ** -0.5),
        'moe_b_up': nrm(ks[28], (DEPTH, N_EXPERTS, 2 * D_FF), 0.01),
        'moe_w_down': nrm(ks[29], (DEPTH, N_EXPERTS, D_FF, D_MODEL), BETA * D_FF ** -0.5),
        'moe_b_down': nrm(ks[30], (DEPTH, N_EXPERTS, D_MODEL), 0.01),
    }


def reference(x_prompt, x_sample, cache_a_k, cache_a_v, state_b_conv, state_b_h, state_c_s,
              ab_w_in, ab_rel_bias, ab_conv_w, ab_conv_b, ab_w_rg, ab_b_rg, ab_w_ig, ab_b_ig, ab_lambda, ab_w_out,
              c_w_in, c_gn_g, c_gn_b, c_w_out,
              ln1_g, ln1_b, ln2_g, ln2_b,
              moe_w_router, moe_b_router, moe_w_up, moe_b_up, moe_w_down, moe_b_down):
    xp, xs = x_prompt, x_sample
    Bp, Tp = xp.shape[:2]
    Bs, Ts = xs.shape[:2]
    pos_p = jnp.arange(Tp)
    pos_s = PAST_LEN + jnp.arange(Ts)
    akp, avp, bcp, bhp, csp = [], [], [], [], []
    aks, avs, bcs, bhs, css = [], [], [], [], []
    for l in range(DEPTH):
        j = l // 2
        if l % 2 == 0:
            ab = (ab_w_in[j], ab_rel_bias[j], ab_conv_w[j], ab_conv_b[j], ab_w_rg[j], ab_b_rg[j],
                  ab_w_ig[j], ab_b_ig[j], ab_lambda[j], ab_w_out[j])
            hp, k1, v1, c1, h1 = _ab_mixer(xp, None, None, jnp.zeros((Bp, CONV_W - 1, B_WIDTH), xp.dtype),
                                           jnp.zeros((Bp, B_WIDTH), f32), *ab)
            hs, k2, v2, c2, h2 = _ab_mixer(xs, cache_a_k[j], cache_a_v[j], state_b_conv[j], state_b_h[j], *ab)
            akp.append(k1); avp.append(v1); bcp.append(c1); bhp.append(h1)
            aks.append(k2); avs.append(v2); bcs.append(c2); bhs.append(h2)
        else:
            hp, s1 = _retention_mixer(xp, jnp.zeros((Bp, HC, DKC, DVC), f32), pos_p, c_w_in[j], c_gn_g[j], c_gn_b[j], c_w_out[j])
            hs, s2 = _retention_mixer(xs, state_c_s[j], pos_s, c_w_in[j], c_gn_g[j], c_gn_b[j], c_w_out[j])
            csp.append(s1); css.append(s2)
        moe = (moe_w_router[l], moe_b_router[l], moe_w_up[l], moe_b_up[l], moe_w_down[l], moe_b_down[l])
        xp = _layernorm(ALPHA * xp + hp, ln1_g[l], ln1_b[l])
        xp = _layernorm(ALPHA * xp + _moe(xp, *moe), ln2_g[l], ln2_b[l])
        xs = _layernorm(ALPHA * xs + hs, ln1_g[l], ln1_b[l])
        xs = _layernorm(ALPHA * xs + _moe(xs, *moe), ln2_g[l], ln2_b[l])
    return (xp, xs,
            jnp.stack(akp), jnp.stack(avp), jnp.stack(bcp), jnp.stack(bhp), jnp.stack(csp),
            jnp.stack(aks), jnp.stack(avs), jnp.stack(bcs), jnp.stack(bhs), jnp.stack(css))
```

```python
import functools
import math

import jax
import jax.numpy as jnp
from jax import lax
from jax.experimental import pallas as pl
from jax.experimental.pallas import tpu as pltpu

f32 = jnp.float32
bf16 = jnp.bfloat16
i32 = jnp.int32

DEPTH = 2
PAST_LEN = 1024
CHUNK = 64
A_PAST_CHUNKS = 8
REL_CLIP = 128
HA, DHA = 8, 64
A_WIDTH = HA * DHA
B_WIDTH = 512
HB = 8
CONV_W = 4
LRU_C = 8.0
HC, DKC, DVC = 4, 256, 512
C_QK, C_V = HC * DKC, HC * DVC
ROPE_BASE = 10000.0
GN_EPS = 1e-5
N_EXPERTS = 32
TOP_K = 4
D_FF = 1024
SWIGLU_LIMIT = 7.0
SWIGLU_ALPHA = 1.702
ALPHA = (2.0 * DEPTH) ** 0.25
LN_EPS = 1e-5
NEG = -1e30

LANES = 128
VMEM_LIMIT = 56 * 1024 * 1024
ATT_TQ = 256
MOE_BM = 256
RET_TB = 512
LRU_TB = 256


def _pick(n, cands):
    for c in cands:
        if n % c == 0:
            return c
    raise ValueError(f"no tile for {n} in {cands}")


def _cparams(sem):
    return pltpu.CompilerParams(dimension_semantics=sem, vmem_limit_bytes=VMEM_LIMIT)


def _layernorm(v, g, b):
    mu = jnp.mean(v, axis=-1, keepdims=True)
    d = v - mu
    var = jnp.mean(d * d, axis=-1, keepdims=True)
    return d * lax.rsqrt(var + LN_EPS) * g + b


def _mm_kernel(x_ref, w_ref, o_ref):
    o_ref[...] = jnp.dot(x_ref[...].astype(bf16), w_ref[...], preferred_element_type=f32)


def _matmul(x, w, tm, tn):
    m, k = x.shape
    n = w.shape[1]
    return pl.pallas_call(
        _mm_kernel,
        grid=(m // tm, n // tn),
        in_specs=[pl.BlockSpec((tm, k), lambda i, j: (i, 0)),
                  pl.BlockSpec((k, tn), lambda i, j: (0, j))],
        out_specs=pl.BlockSpec((tm, tn), lambda i, j: (i, j)),
        out_shape=jax.ShapeDtypeStruct((m, n), f32),
        compiler_params=_cparams(("parallel", "parallel")),
        name="dense_proj",
    )(x, w)


def _proj_ln_kernel(*refs, n_in):
    a_refs = refs[:n_in]
    w_refs = refs[n_in:2 * n_in]
    x_ref, g_ref, b_ref, o_ref = refs[2 * n_in:]
    acc = ALPHA * x_ref[...]
    for a_ref, w_ref in zip(a_refs, w_refs):
        acc = acc + jnp.dot(a_ref[...].astype(bf16), w_ref[...], preferred_element_type=f32)
    o_ref[...] = _layernorm(acc, g_ref[...], b_ref[...])


def _proj_ln(acts, ws, x, g, b, tm):
    m, d = x.shape
    n_in = len(acts)
    in_specs = ([pl.BlockSpec((tm, a.shape[1]), lambda i: (i, 0)) for a in acts]
                + [pl.BlockSpec(w.shape, lambda i: (0, 0)) for w in ws]
                + [pl.BlockSpec((tm, d), lambda i: (i, 0)),
                   pl.BlockSpec((1, d), lambda i: (0, 0)),
                   pl.BlockSpec((1, d), lambda i: (0, 0))])
    return pl.pallas_call(
        functools.partial(_proj_ln_kernel, n_in=n_in),
        grid=(m // tm,),
        in_specs=in_specs,
        out_specs=pl.BlockSpec((tm, d), lambda i: (i, 0)),
        out_shape=jax.ShapeDtypeStruct((m, d), f32),
        compiler_params=_cparams(("parallel",)),
        name="proj_ln",
    )(*acts, *ws, x, g.reshape(1, d), b.reshape(1, d))


def _attn_prompt_kernel(q_ref, k0_ref, k1_ref, k2_ref, v0_ref, v1_ref, v2_ref, bias_ref, o_ref):
    i = pl.program_id(0)
    tq = q_ref.shape[0]
    q = q_ref[...]
    k = jnp.concatenate([k0_ref[...], k1_ref[...], k2_ref[...]], axis=0)
    v = jnp.concatenate([v0_ref[...], v1_ref[...], v2_ref[...]], axis=0)
    kcol = lax.broadcasted_iota(i32, (1, 3 * tq), 1)
    tile_ok = (kcol // tq + i) >= 2
    outs = []
    for h in range(HA):
        sl = slice(h * DHA, (h + 1) * DHA)
        s = lax.dot_general(q[:, sl].astype(bf16), k[:, sl].astype(bf16),
                            (((1,), (1,)), ((), ())), preferred_element_type=f32)
        s = s * (DHA ** -0.5) + bias_ref[h]
        s = jnp.where(tile_ok, s, NEG)
        s = s - jnp.max(s, axis=-1, keepdims=True)
        p = jnp.exp(s)
        p = p / jnp.sum(p, axis=-1, keepdims=True)
        outs.append(jnp.dot(p.astype(bf16), v[:, sl].astype(bf16), preferred_element_type=f32))
    o_ref[...] = jnp.concatenate(outs, axis=-1)


def _attn_prompt(z, rel_bias, tp):
    tq = ATT_TQ
    nq = tp // tq
    qq = jnp.arange(tq)[:, None]
    kk = jnp.arange(3 * tq)[None, :]
    d = qq - kk + 2 * tq
    idx = jnp.clip(d, -REL_CLIP, REL_CLIP) + REL_CLIP
    band = (kk // CHUNK >= qq // CHUNK) & (kk // CHUNK <= qq // CHUNK + A_PAST_CHUNKS)
    bias = jnp.where(band[None], rel_bias[:, idx], NEG).astype(f32)

    def kv_spec(back, col):
        return pl.BlockSpec((tq, A_WIDTH), lambda i: (jnp.maximum(i - back, 0), col))

    return pl.pallas_call(
        _attn_prompt_kernel,
        grid=(nq,),
        in_specs=[pl.BlockSpec((tq, A_WIDTH), lambda i: (i, 0)),
                  kv_spec(2, 1), kv_spec(1, 1), kv_spec(0, 1),
                  kv_spec(2, 2), kv_spec(1, 2), kv_spec(0, 2),
                  pl.BlockSpec((HA, tq, 3 * tq), lambda i: (0, 0, 0))],
        out_specs=pl.BlockSpec((tq, A_WIDTH), lambda i: (i, 0)),
        out_shape=jax.ShapeDtypeStruct((tp, A_WIDTH), f32),
        compiler_params=_cparams(("parallel",)),
        name="attn_prompt",
    )(z, z, z, z, z, z, z, bias)


def _attn_sample_kernel(q_ref, kn_ref, vn_ref, kc_ref, vc_ref, bias_ref, o_ref):
    p_len = kc_ref.shape[1]
    q, kn, vn = q_ref[...], kn_ref[...], vn_ref[...]
    kc, vc = kc_ref[0], vc_ref[0]
    outs = []
    for h in range(HA):
        sl = slice(h * DHA, (h + 1) * DHA)
        qh = q[:, sl].astype(bf16)
        dims = (((1,), (1,)), ((), ()))
        sc = lax.dot_general(qh, kc[:, sl].astype(bf16), dims, preferred_element_type=f32)
        sn = lax.dot_general(qh, kn[:, sl].astype(bf16), dims, preferred_element_type=f32)
        b = bias_ref[h]
        sc = sc * (DHA ** -0.5) + b[:, :p_len]
        sn = sn * (DHA ** -0.5) + b[:, p_len:]
        m = jnp.maximum(jnp.max(sc, axis=-1, keepdims=True), jnp.max(sn, axis=-1, keepdims=True))
        pc, pn = jnp.exp(sc - m), jnp.exp(sn - m)
        den = jnp.sum(pc, axis=-1, keepdims=True) + jnp.sum(pn, axis=-1, keepdims=True)
        pc, pn = pc / den, pn / den
        outs.append(jnp.dot(pc.astype(bf16), vc[:, sl].astype(bf16), preferred_element_type=f32)
                    + jnp.dot(pn.astype(bf16), vn[:, sl].astype(bf16), preferred_element_type=f32))
    o_ref[...] = jnp.concatenate(outs, axis=-1)


def _attn_sample(z, k_cache, v_cache, rel_bias, row0, bs, ts):
    p_len = k_cache.shape[1]
    kc = k_cache.reshape(bs, p_len, A_WIDTH)
    vc = v_cache.reshape(bs, p_len, A_WIDTH)
    d = jnp.arange(ts)[:, None] - (jnp.arange(p_len + ts)[None, :] - p_len)
    bias = rel_bias[:, jnp.clip(d, -REL_CLIP, REL_CLIP) + REL_CLIP].astype(f32)
    blk0 = row0 // ts

    def z_spec(col):
        return pl.BlockSpec((ts, A_WIDTH), lambda b: (blk0 + b, col))

    return pl.pallas_call(
        _attn_sample_kernel,
        grid=(bs,),
        in_specs=[z_spec(0), z_spec(1), z_spec(2),
                  pl.BlockSpec((1, p_len, A_WIDTH), lambda b: (b, 0, 0)),
                  pl.BlockSpec((1, p_len, A_WIDTH), lambda b: (b, 0, 0)),
                  pl.BlockSpec((HA, ts, p_len + ts), lambda b: (0, 0, 0))],
        out_specs=pl.BlockSpec((ts, A_WIDTH), lambda b: (b, 0)),
        out_shape=jax.ShapeDtypeStruct((bs * ts, A_WIDTH), f32),
        compiler_params=_cparams(("parallel",)),
        name="attn_sample",
    )(z, z, z, kc, vc, bias)


def _gelu_tanh(x):
    return 0.5 * x * (1.0 + jnp.tanh(math.sqrt(2.0 / math.pi) * (x + 0.044715 * (x * x * x))))


def _rglru_kernel(xb_ref, gb_ref, conv0_ref, h0_ref, cw_ref, cb_ref, wrg_ref, brg_ref,
                  wig_ref, big_ref, sp_ref, rec_ref, convn_ref, hl_ref, xp_ref, hc_ref):
    t = pl.program_id(1)
    tb = xb_ref.shape[0]
    pad = 8

    @pl.when(t == 0)
    def _():
        xp_ref[0:pad, :] = jnp.zeros((pad, B_WIDTH), f32)
        xp_ref[pad - (CONV_W - 1):pad, :] = conv0_ref[0]
        hc_ref[...] = h0_ref[0]

    xb = xb_ref[...]
    xp_ref[pad:pad + tb, :] = xb
    cw = cw_ref[...]
    u = cb_ref[...] + cw[CONV_W - 1:CONV_W, :] * xb
    for j in range(CONV_W - 1):
        sh = CONV_W - 1 - j
        u = u + cw[j:j + 1, :] * xp_ref[pad - sh:pad - sh + tb, :]
    convn_ref[0] = xp_ref[pad + tb - (CONV_W - 1):pad + tb, :]
    xp_ref[0:pad, :] = xp_ref[tb:tb + pad, :]

    ub = u.astype(bf16)
    r = jax.nn.sigmoid(jnp.dot(ub, wrg_ref[...], preferred_element_type=f32) + brg_ref[...])
    ig = jax.nn.sigmoid(jnp.dot(ub, wig_ref[...], preferred_element_type=f32) + big_ref[...])
    log_a = -LRU_C * r * sp_ref[...]
    a = jnp.exp(log_a)
    bt = jnp.sqrt(-jnp.tanh(log_a) * (a * a + 1.0)) * (ig * u)

    row = lax.broadcasted_iota(i32, (tb, 1), 0)
    s = 1
    while s < tb:
        keep = row >= s
        a_sh = pltpu.roll(a, s, axis=0)
        b_sh = pltpu.roll(bt, s, axis=0)
        bt = jnp.where(keep, a * b_sh + bt, bt)
        a = jnp.where(keep, a * a_sh, a)
        s *= 2
    h = a * hc_ref[...] + bt
    hc_ref[...] = h[tb - 1:tb, :]
    hl_ref[0] = h[tb - 1:tb, :]
    rec_ref[...] = h * _gelu_tanh(gb_ref[...])


def _rglru(z, conv0, h0, cw, cb, wrg, brg, wig, big, sp, row0, bsz, t_len, tb):
    nt = t_len // tb
    blk0 = row0 // tb

    def z_spec(col):
        return pl.BlockSpec((tb, B_WIDTH), lambda b, t: (blk0 + b * nt + t, col))

    def const(shape):
        return pl.BlockSpec(shape, lambda b, t: (0,) * len(shape))

    row = lambda v: v.reshape(1, B_WIDTH)
    return pl.pallas_call(
        _rglru_kernel,
        grid=(bsz, nt),
        in_specs=[z_spec(3), z_spec(4),
                  pl.BlockSpec((1, CONV_W - 1, B_WIDTH), lambda b, t: (b, 0, 0)),
                  pl.BlockSpec((1, 1, B_WIDTH), lambda b, t: (b, 0, 0)),
                  const((CONV_W, B_WIDTH)), const((1, B_WIDTH)),
                  const((B_WIDTH, B_WIDTH)), const((1, B_WIDTH)),
                  const((B_WIDTH, B_WIDTH)), const((1, B_WIDTH)), const((1, B_WIDTH))],
        out_specs=[pl.BlockSpec((tb, B_WIDTH), lambda b, t: (b * nt + t, 0)),
                   pl.BlockSpec((1, CONV_W - 1, B_WIDTH), lambda b, t: (b, 0, 0)),
                   pl.BlockSpec((1, 1, B_WIDTH), lambda b, t: (b, 0, 0))],
        out_shape=[jax.ShapeDtypeStruct((bsz * t_len, B_WIDTH), f32),
                   jax.ShapeDtypeStruct((bsz, CONV_W - 1, B_WIDTH), f32),
                   jax.ShapeDtypeStruct((bsz, 1, B_WIDTH), f32)],
        scratch_shapes=[pltpu.VMEM((tb + 8, B_WIDTH), f32), pltpu.VMEM((1, B_WIDTH), f32)],
        compiler_params=_cparams(("parallel", "arbitrary")),
        name="rglru",
    )(z, z, conv0, h0.reshape(bsz, 1, B_WIDTH), cw, row(cb), wrg, row(brg), wig, row(big), row(sp))


def _block_diag(w):
    hb, d, _ = w.shape
    eye = jnp.eye(hb, dtype=w.dtype)
    return (eye[:, None, :, None] * w[:, :, None, :]).reshape(hb * d, hb * d)


def _retention_kernel(q_ref, k_ref, v_ref, g_ref, cos_ref, sin_ref, s0_ref, gng_ref, gnb_ref,
                      y_ref, sout_ref, s_ref, *, chunk):
    t = pl.program_id(1)
    nt = pl.num_programs(1)
    tb = q_ref.shape[0]
    half = DKC // 2

    @pl.when(t == 0)
    def _():
        s_ref[...] = s0_ref[0]

    cos, sin = cos_ref[...], sin_ref[...]
    n = lax.broadcasted_iota(i32, (tb, 1), 0)
    m = lax.broadcasted_iota(i32, (1, tb), 1)
    dist = n - m
    same = (n // chunk) == (m // chunk)
    expo = jnp.where(same, jnp.abs(dist), dist).astype(f32)
    visible = same | (dist > 0)
    nf = n.astype(f32)
    for h in range(HC):
        lg = math.log(1.0 - 2.0 ** (-5.0 - h))
        base = h * DKC

        def rot(ref, scale):
            x1 = ref[:, base:base + half]
            x2 = ref[:, base + half:base + DKC]
            return jnp.concatenate([x1 * cos - x2 * sin, x2 * cos + x1 * sin], axis=-1) * scale

        qr = rot(q_ref, 1.0)
        kr = rot(k_ref, DKC ** -0.5)
        vb = v_ref[:, h * DVC:(h + 1) * DVC].astype(bf16)
        qb = qr.astype(bf16)
        dmat = jnp.where(visible, jnp.exp(lg * expo), 0.0)
        s = lax.dot_general(qb, kr.astype(bf16), (((1,), (1,)), ((), ())),
                            preferred_element_type=f32) * dmat
        s_old = s_ref[h]
        o = jnp.dot(s.astype(bf16), vb, preferred_element_type=f32)
        o = o + jnp.dot(qb, s_old.astype(bf16), preferred_element_type=f32) * jnp.exp(lg * (nf + 1.0))
        kd = (kr * jnp.exp(lg * (tb - 1.0 - nf))).astype(bf16)
        s_ref[h] = math.exp(lg * tb) * s_old + lax.dot_general(
            kd, vb, (((0,), (0,)), ((), ())), preferred_element_type=f32)

        mu = jnp.mean(o, axis=-1, keepdims=True)
        dlt = o - mu
        var = jnp.mean(dlt * dlt, axis=-1, keepdims=True)
        vs = slice(h * DVC, (h + 1) * DVC)
        yn = dlt * lax.rsqrt(var + GN_EPS) * gng_ref[:, vs] + gnb_ref[:, vs]
        y_ref[:, vs] = (jax.nn.silu(g_ref[:, vs]) * yn).astype(y_ref.dtype)

    @pl.when(t == nt - 1)
    def _():
        sout_ref[0] = s_ref[...]


def _retention(z, s0, pos0, gn_g, gn_b, row0, bsz, t_len, tb, chunk):
    nt = t_len // tb
    blk0 = row0 // tb
    half = DKC // 2
    inv = 1.0 / (ROPE_BASE ** (jnp.arange(half, dtype=f32) / half))
    ang = (pos0 + jnp.arange(t_len)).astype(f32)[:, None] * inv[None, :]
    cos, sin = jnp.cos(ang), jnp.sin(ang)

    def z_spec(width, col):
        return pl.BlockSpec((tb, width), lambda b, t: (blk0 + b * nt + t, col))

    return pl.pallas_call(
        functools.partial(_retention_kernel, chunk=chunk),
        grid=(bsz, nt),
        in_specs=[z_spec(C_QK, 0), z_spec(C_QK, 1), z_spec(C_V, 1), z_spec(C_V, 2),
                  pl.BlockSpec((tb, half), lambda b, t: (t, 0)),
                  pl.BlockSpec((tb, half), lambda b, t: (t, 0)),
                  pl.BlockSpec((1, HC, DKC, DVC), lambda b, t: (b, 0, 0, 0)),
                  pl.BlockSpec((1, C_V), lambda b, t: (0, 0)),
                  pl.BlockSpec((1, C_V), lambda b, t: (0, 0))],
        out_specs=[pl.BlockSpec((tb, C_V), lambda b, t: (b * nt + t, 0)),
                   pl.BlockSpec((1, HC, DKC, DVC), lambda b, t: (b, 0, 0, 0))],
        out_shape=[jax.ShapeDtypeStruct((bsz * t_len, C_V), bf16),
                   jax.ShapeDtypeStruct((bsz, HC, DKC, DVC), f32)],
        scratch_shapes=[pltpu.VMEM((HC, DKC, DVC), f32)],
        compiler_params=_cparams(("parallel", "arbitrary")),
        name="retention",
    )(z, z, z, z, cos, sin, s0, gn_g.reshape(1, C_V), gn_b.reshape(1, C_V))


def _router_kernel(x_ref, w_ref, b_ref, e_ref, g_ref):
    logits = jnp.dot(x_ref[...], w_ref[...], preferred_element_type=f32,
                     precision=lax.Precision.HIGHEST) + b_ref[...]
    tm = logits.shape[0]
    lane = lax.broadcasted_iota(i32, (tm, N_EXPERTS), 1)
    out_lane = lax.broadcasted_iota(i32, (tm, LANES), 1)
    e_out = jnp.zeros((tm, LANES), i32)
    g_out = jnp.zeros((tm, LANES), f32)
    v0 = None
    den = jnp.zeros((tm, 1), f32)
    for k in range(TOP_K):
        v = jnp.max(logits, axis=-1, keepdims=True)
        idx = jnp.min(jnp.where(logits == v, lane, N_EXPERTS), axis=-1, keepdims=True)
        logits = jnp.where(lane == idx, -jnp.inf, logits)
        if k == 0:
            v0 = v
        p = jnp.exp(v - v0)
        den = den + p
        e_out = jnp.where(out_lane == k, idx, e_out)
        g_out = jnp.where(out_lane == k, p, g_out)
    e_ref[...] = e_out
    g_ref[...] = g_out / den


def _router(x, w, b, tm):
    m, d = x.shape
    return pl.pallas_call(
        _router_kernel,
        grid=(m // tm,),
        in_specs=[pl.BlockSpec((tm, d), lambda i: (i, 0)),
                  pl.BlockSpec((d, N_EXPERTS), lambda i: (0, 0)),
                  pl.BlockSpec((1, N_EXPERTS), lambda i: (0, 0))],
        out_specs=[pl.BlockSpec((tm, LANES), lambda i: (i, 0)),
                   pl.BlockSpec((tm, LANES), lambda i: (i, 0))],
        out_shape=[jax.ShapeDtypeStruct((m, LANES), i32),
                   jax.ShapeDtypeStruct((m, LANES), f32)],
        compiler_params=_cparams(("parallel",)),
        name="router",
    )(x, w, b.reshape(1, N_EXPERTS))


def _moe_kernel(blk_e_ref, nact_ref, tok_cur_ref, tok_nxt_ref, x_hbm, wup_ref, bup_ref,
                wdn_ref, bdn_ref, y_ref, xbuf, sem, wup_bf, wdn_bf):
    b = pl.program_id(0)
    nact = nact_ref[0]
    bm = xbuf.shape[1]

    def gather(tok_ref, slot):
        def body(r, carry):
            tok = tok_ref[0, 0, r]
            pltpu.make_async_copy(x_hbm.at[pl.ds(tok, 1)], xbuf.at[slot, pl.ds(r, 1)],
                                  sem.at[slot]).start()
            return carry
        lax.fori_loop(0, bm, body, 0, unroll=8)

    @pl.when(b == 0)
    def _():
        gather(tok_cur_ref, 0)

    @pl.when(b + 1 < nact)
    def _():
        gather(tok_nxt_ref, (b + 1) % 2)

    @pl.when(b < nact)
    def _():
        slot = b % 2
        pltpu.make_async_copy(xbuf.at[slot], xbuf.at[slot], sem.at[slot]).wait()

        @pl.when((b == 0) | (blk_e_ref[b] != blk_e_ref[jnp.maximum(b - 1, 0)]))
        def _():
            wup_bf[...] = wup_ref[0].astype(bf16)
            wdn_bf[...] = wdn_ref[0].astype(bf16)

        x = xbuf[slot].astype(bf16)
        hdn = jnp.dot(x, wup_bf[...], preferred_element_type=f32) + bup_ref[0]
        glu = jnp.minimum(hdn[:, :D_FF], SWIGLU_LIMIT)
        lin = jnp.clip(hdn[:, D_FF:], -SWIGLU_LIMIT, SWIGLU_LIMIT)
        act = glu * jax.nn.sigmoid(SWIGLU_ALPHA * glu) * (lin + 1.0)
        y_ref[...] = jnp.dot(act.astype(bf16), wdn_bf[...], preferred_element_type=f32) + bdn_ref[0]

    @pl.when(b >= nact)
    def _():
        y_ref[...] = jnp.zeros(y_ref.shape, f32)


def _moe_experts(x, blk_e, nact, tok, w_up, b_up, w_down, b_down):
    n, d = x.shape
    nblk = blk_e.shape[0]
    bm = MOE_BM
    tok3 = tok.reshape(nblk, 1, bm)
    grid_spec = pltpu.PrefetchScalarGridSpec(
        num_scalar_prefetch=2,
        grid=(nblk,),
        in_specs=[
            pl.BlockSpec((1, 1, bm), lambda b, be, na: (b, 0, 0), memory_space=pltpu.SMEM),
            pl.BlockSpec((1, 1, bm), lambda b, be, na: (jnp.minimum(b + 1, nblk - 1), 0, 0),
                         memory_space=pltpu.SMEM),
            pl.BlockSpec(memory_space=pl.ANY),
            pl.BlockSpec((1, d, 2 * D_FF), lambda b, be, na: (be[b], 0, 0)),
            pl.BlockSpec((1, 1, 2 * D_FF), lambda b, be, na: (be[b], 0, 0)),
            pl.BlockSpec((1, D_FF, d), lambda b, be, na: (be[b], 0, 0)),
            pl.BlockSpec((1, 1, d), lambda b, be, na: (be[b], 0, 0)),
        ],
        out_specs=pl.BlockSpec((bm, d), lambda b, be, na: (b, 0)),
        scratch_shapes=[pltpu.VMEM((2, bm, d), f32), pltpu.SemaphoreType.DMA((2,)),
                        pltpu.VMEM((d, 2 * D_FF), bf16), pltpu.VMEM((D_FF, d), bf16)],
    )
    return pl.pallas_call(
        _moe_kernel,
        grid_spec=grid_spec,
        out_shape=jax.ShapeDtypeStruct((nblk * bm, d), f32),
        compiler_params=_cparams(("arbitrary",)),
        name="moe_experts",
    )(blk_e, nact, tok3, tok3, x, w_up, b_up.reshape(N_EXPERTS, 1, 2 * D_FF),
      w_down, b_down.reshape(N_EXPERTS, 1, d))


def _combine_ln_kernel(pos_ref, y_hbm, gate_ref, x_ref, g_ref, b_ref, o_ref, gbuf, sem):
    tm = x_ref.shape[0]

    def body(r, carry):
        for k in range(TOP_K):
            p = pos_ref[0, 0, r * TOP_K + k]
            pltpu.make_async_copy(y_hbm.at[pl.ds(p, 1)], gbuf.at[k, pl.ds(r, 1)], sem.at[0]).start()
        return carry
    lax.fori_loop(0, tm, body, 0, unroll=4)
    pltpu.make_async_copy(gbuf, gbuf, sem.at[0]).wait()

    gates = gate_ref[...]
    acc = ALPHA * x_ref[...]
    for k in range(TOP_K):
        acc = acc + gates[:, k:k + 1] * gbuf[k]
    o_ref[...] = _layernorm(acc, g_ref[...], b_ref[...])


def _combine_ln(y, pos, gates, x, g, b, tm, row0=0, rows=None):
    n, d = x.shape
    rows = n if rows is None else rows
    blk0 = row0 // tm
    pos3 = pos.reshape(n // tm, 1, tm * TOP_K)
    grid_spec = pltpu.PrefetchScalarGridSpec(
        num_scalar_prefetch=0,
        grid=(rows // tm,),
        in_specs=[
            pl.BlockSpec((1, 1, tm * TOP_K), lambda i: (blk0 + i, 0, 0), memory_space=pltpu.SMEM),
            pl.BlockSpec(memory_space=pl.ANY),
            pl.BlockSpec((tm, LANES), lambda i: (blk0 + i, 0)),
            pl.BlockSpec((tm, d), lambda i: (blk0 + i, 0)),
            pl.BlockSpec((1, d), lambda i: (0, 0)),
            pl.BlockSpec((1, d), lambda i: (0, 0)),
        ],
        out_specs=pl.BlockSpec((tm, d), lambda i: (i, 0)),
        scratch_shapes=[pltpu.VMEM((TOP_K, tm, d), f32), pltpu.SemaphoreType.DMA((1,))],
    )
    return pl.pallas_call(
        _combine_ln_kernel,
        grid_spec=grid_spec,
        out_shape=jax.ShapeDtypeStruct((rows, d), f32),
        compiler_params=_cparams(("arbitrary",)),
        name="combine_ln",
    )(pos3, y, gates, x, g.reshape(1, d), b.reshape(1, d))


def _route(top_e, n):
    bm = MOE_BM
    nk = n * TOP_K
    nblk = -(-nk // bm) + N_EXPERTS
    flat_e = top_e.reshape(-1)
    onehot = (flat_e[:, None] == jnp.arange(N_EXPERTS, dtype=i32)[None, :]).astype(i32)
    csum = jnp.cumsum(onehot, axis=0)
    rank = jnp.sum(csum * onehot, axis=1) - 1
    counts = csum[-1]
    padded = (counts + bm - 1) // bm * bm
    end_padded = jnp.cumsum(padded)
    start_padded = end_padded - padded
    dest = start_padded[flat_e] + rank
    flat_t = jnp.arange(nk, dtype=i32) // TOP_K
    tok = jnp.zeros((nblk * bm,), i32).at[dest].set(flat_t)
    blk_start = jnp.arange(nblk, dtype=i32) * bm
    blk_e = jnp.minimum(jnp.searchsorted(end_padded, blk_start, side="right"),
                        N_EXPERTS - 1).astype(i32)
    nact = (end_padded[-1] // bm).astype(i32).reshape(1)
    return dest.astype(i32), tok, blk_e, nact


def _moe_ln(x, l, w_router, b_router, w_up, b_up, w_down, b_down, g, b, splits=None):
    n = x.shape[0]
    tm = _pick(n, (512, 256, 128, 64))
    top_e, gates = _router(x, w_router[l], b_router[l], tm)
    dest, tok, blk_e, nact = _route(top_e[:, :TOP_K], n)
    y = _moe_experts(x, blk_e, nact, tok, w_up[l], b_up[l], w_down[l], b_down[l])
    tmc = _pick(n, (256, 128, 64))
    if splits is None:
        return _combine_ln(y, dest, gates, x, g, b, tmc)
    return [_combine_ln(y, dest, gates, x, g, b, _pick(math.gcd(math.gcd(r0, rows), n), (256, 128, 64, 32)),
                        r0, rows)
            for (r0, rows) in splits]


def kernel(x_prompt, x_sample, cache_a_k, cache_a_v, state_b_conv, state_b_h, state_c_s, ab_w_in, ab_rel_bias, ab_conv_w, ab_conv_b, ab_w_rg, ab_b_rg, ab_w_ig, ab_b_ig, ab_lambda, ab_w_out, c_w_in, c_gn_g, c_gn_b, c_w_out, ln1_g, ln1_b, ln2_g, ln2_b, moe_w_router, moe_b_router, moe_w_up, moe_b_up, moe_w_down, moe_b_down):
    bp, tp, d = x_prompt.shape
    bs, ts, _ = x_sample.shape
    assert bp == 1 and tp % RET_TB == 0 and tp % ts == 0
    n_p, n_s = bp * tp, bs * ts
    n = n_p + n_s
    x = jnp.concatenate([x_prompt.reshape(n_p, d), x_sample.reshape(n_s, d)], axis=0)
    tm = _pick(n, (512, 256, 128, 64))
    moe = (moe_w_router, moe_b_router, moe_w_up, moe_b_up, moe_w_down, moe_b_down)

    j = 0
    z = _matmul(x, ab_w_in[j].astype(bf16), tm, 512)
    att = jnp.concatenate([
        _attn_prompt(z, ab_rel_bias[j], tp),
        _attn_sample(z, cache_a_k[j], cache_a_v[j], ab_rel_bias[j], n_p, bs, ts)], axis=0)
    lru_w = (ab_conv_w[j], ab_conv_b[j], _block_diag(ab_w_rg[j]).astype(bf16), ab_b_rg[j],
             _block_diag(ab_w_ig[j]).astype(bf16), ab_b_ig[j], jax.nn.softplus(-ab_lambda[j]))
    rec_p, bc_p, bh_p = _rglru(z, jnp.zeros((bp, CONV_W - 1, B_WIDTH), f32), jnp.zeros((bp, B_WIDTH), f32),
                               *lru_w, 0, bp, tp, _pick(tp, (LRU_TB, 128, 64)))
    rec_s, bc_s, bh_s = _rglru(z, state_b_conv[j], state_b_h[j], *lru_w, n_p, bs, ts, ts)
    rec = jnp.concatenate([rec_p, rec_s], axis=0)
    w_out = ab_w_out[j].astype(bf16)
    x = _proj_ln([att, rec], [w_out[:A_WIDTH], w_out[A_WIDTH:]], x, ln1_g[0], ln1_b[0], tm)
    x = _moe_ln(x, 0, *moe, ln2_g[0], ln2_b[0])

    keep = min(A_PAST_CHUNKS * CHUNK, tp)
    ak_p = z[n_p - keep:n_p, A_WIDTH:2 * A_WIDTH].reshape(1, bp, keep, HA, DHA)
    av_p = z[n_p - keep:n_p, 2 * A_WIDTH:3 * A_WIDTH].reshape(1, bp, keep, HA, DHA)
    ak_s = z[n_p:, A_WIDTH:2 * A_WIDTH].reshape(1, bs, ts, HA, DHA)
    av_s = z[n_p:, 2 * A_WIDTH:3 * A_WIDTH].reshape(1, bs, ts, HA, DHA)

    z = _matmul(x, c_w_in[j].astype(bf16), tm, 1024)
    y_p, cs_p = _retention(z, jnp.zeros((bp, HC, DKC, DVC), f32), 0, c_gn_g[j], c_gn_b[j],
                           0, bp, tp, RET_TB, CHUNK)
    y_s, cs_s = _retention(z, state_c_s[j], PAST_LEN, c_gn_g[j], c_gn_b[j], n_p, bs, ts, ts, ts)
    y = jnp.concatenate([y_p, y_s], axis=0)
    x = _proj_ln([y], [c_w_out[j].astype(bf16)], x, ln1_g[1], ln1_b[1], tm)
    out_p, out_s = _moe_ln(x, 1, *moe, ln2_g[1], ln2_b[1], splits=[(0, n_p), (n_p, n_s)])

    return (out_p.reshape(bp, tp, d), out_s.reshape(bs, ts, d),
            ak_p, av_p, bc_p[None], bh_p.reshape(1, bp, B_WIDTH), cs_p[None],
            ak_s, av_s, bc_s[None], bh_s.reshape(1, bs, B_WIDTH), cs_s[None])
```

```python
import functools
import math

import jax
import jax.numpy as jnp
from jax import lax
from jax.experimental import pallas as pl
from jax.experimental.pallas import tpu as pltpu

f32 = jnp.float32
bf16 = jnp.bfloat16
i32 = jnp.int32

DEPTH = 2
PAST_LEN = 1024
CHUNK = 64
A_PAST_CHUNKS = 8
REL_CLIP = 128
HA, DHA = 8, 64
A_WIDTH = HA * DHA
B_WIDTH = 512
HB = 8
CONV_W = 4
LRU_C = 8.0
HC, DKC, DVC = 4, 256, 512
C_QK, C_V = HC * DKC, HC * DVC
ROPE_BASE = 10000.0
GN_EPS = 1e-5
N_EXPERTS = 32
TOP_K = 4
D_FF = 1024
SWIGLU_LIMIT = 7.0
SWIGLU_ALPHA = 1.702
ALPHA = (2.0 * DEPTH) ** 0.25
LN_EPS = 1e-5
NEG = -1e30

LANES = 128
VMEM_LIMIT = 56 * 1024 * 1024
ATT_TQ = 256
MOE_BM = 256
MOE_BM_SHIFT = MOE_BM.bit_length() - 1
RET_TB = 512
LRU_TB = 256


def _pick(n, cands):
    for c in cands:
        if n % c == 0:
            return c
    raise ValueError(f"no tile for {n} in {cands}")


def _cparams(sem):
    return pltpu.CompilerParams(dimension_semantics=sem, vmem_limit_bytes=VMEM_LIMIT)


def _layernorm(v, g, b):
    mu = jnp.mean(v, axis=-1, keepdims=True)
    d = v - mu
    var = jnp.mean(d * d, axis=-1, keepdims=True)
    return d * lax.rsqrt(var + LN_EPS) * g + b


def _round_up_bm(v):
    return lax.shift_left(lax.shift_right_logical(v + (MOE_BM - 1), MOE_BM_SHIFT), MOE_BM_SHIFT)


def _rel_bias_matrix(table, nq, nk, d0):
    p = nq + nk - 1
    j = jnp.arange(p)
    u = table[:, jnp.clip(d0 + nq - 1 - j, -REL_CLIP, REL_CLIP) + REL_CLIP]
    u = jnp.roll(u, -(nq - 1), axis=1)
    flat = jnp.tile(u, (1, nq))[:, :nq * (p - 1)]
    return flat.reshape(table.shape[0], nq, p - 1)[:, :, :nk].astype(f32)


def _mm_kernel(x_ref, w_ref, o_ref):
    o_ref[...] = jnp.dot(x_ref[...].astype(bf16), w_ref[...], preferred_element_type=f32)


def _matmul(x, w, tm, tn):
    m, k = x.shape
    n = w.shape[1]
    return pl.pallas_call(
        _mm_kernel,
        grid=(m // tm, n // tn),
        in_specs=[pl.BlockSpec((tm, k), lambda i, j: (i, 0)),
                  pl.BlockSpec((k, tn), lambda i, j: (0, j))],
        out_specs=pl.BlockSpec((tm, tn), lambda i, j: (i, j)),
        out_shape=jax.ShapeDtypeStruct((m, n), f32),
        compiler_params=_cparams(("parallel", "parallel")),
        name="dense_proj",
    )(x, w)


def _proj_ln_kernel(*refs, n_in):
    a_refs = refs[:n_in]
    w_refs = refs[n_in:2 * n_in]
    x_ref, g_ref, b_ref, o_ref = refs[2 * n_in:]
    acc = ALPHA * x_ref[...]
    for a_ref, w_ref in zip(a_refs, w_refs):
        acc = acc + jnp.dot(a_ref[...].astype(bf16), w_ref[...], preferred_element_type=f32)
    o_ref[...] = _layernorm(acc, g_ref[...], b_ref[...])


def _proj_ln(acts, ws, x, g, b, tm):
    m, d = x.shape
    n_in = len(acts)
    in_specs = ([pl.BlockSpec((tm, a.shape[1]), lambda i: (i, 0)) for a in acts]
                + [pl.BlockSpec(w.shape, lambda i: (0, 0)) for w in ws]
                + [pl.BlockSpec((tm, d), lambda i: (i, 0)),
                   pl.BlockSpec((1, d), lambda i: (0, 0)),
                   pl.BlockSpec((1, d), lambda i: (0, 0))])
    return pl.pallas_call(
        functools.partial(_proj_ln_kernel, n_in=n_in),
        grid=(m // tm,),
        in_specs=in_specs,
        out_specs=pl.BlockSpec((tm, d), lambda i: (i, 0)),
        out_shape=jax.ShapeDtypeStruct((m, d), f32),
        compiler_params=_cparams(("parallel",)),
        name="proj_ln",
    )(*acts, *ws, x, g.reshape(1, d), b.reshape(1, d))


def _attn_prompt_kernel(q_ref, k0_ref, k1_ref, k2_ref, v0_ref, v1_ref, v2_ref, bias_ref, o_ref):
    i = pl.program_id(0)
    tq = q_ref.shape[0]
    q = q_ref[...]
    k = jnp.concatenate([k0_ref[...], k1_ref[...], k2_ref[...]], axis=0)
    v = jnp.concatenate([v0_ref[...], v1_ref[...], v2_ref[...]], axis=0)
    kcol = lax.broadcasted_iota(i32, (1, 3 * tq), 1)
    tile_ok = (kcol // tq + i) >= 2
    outs = []
    for h in range(HA):
        sl = slice(h * DHA, (h + 1) * DHA)
        s = lax.dot_general(q[:, sl].astype(bf16), k[:, sl].astype(bf16),
                            (((1,), (1,)), ((), ())), preferred_element_type=f32)
        s = s * (DHA ** -0.5) + bias_ref[h]
        s = jnp.where(tile_ok, s, NEG)
        s = s - jnp.max(s, axis=-1, keepdims=True)
        p = jnp.exp(s)
        p = p / jnp.sum(p, axis=-1, keepdims=True)
        outs.append(jnp.dot(p.astype(bf16), v[:, sl].astype(bf16), preferred_element_type=f32))
    o_ref[...] = jnp.concatenate(outs, axis=-1)


def _attn_prompt(z, rel_bias, tp):
    tq = ATT_TQ
    nq = tp // tq
    qq = jnp.arange(tq)[:, None]
    kk = jnp.arange(3 * tq)[None, :]
    band = (kk // CHUNK >= qq // CHUNK) & (kk // CHUNK <= qq // CHUNK + A_PAST_CHUNKS)
    bias = jnp.where(band[None], _rel_bias_matrix(rel_bias, tq, 3 * tq, 2 * tq), NEG)

    def kv_spec(back, col):
        return pl.BlockSpec((tq, A_WIDTH), lambda i: (jnp.maximum(i - back, 0), col))

    return pl.pallas_call(
        _attn_prompt_kernel,
        grid=(nq,),
        in_specs=[pl.BlockSpec((tq, A_WIDTH), lambda i: (i, 0)),
                  kv_spec(2, 1), kv_spec(1, 1), kv_spec(0, 1),
                  kv_spec(2, 2), kv_spec(1, 2), kv_spec(0, 2),
                  pl.BlockSpec((HA, tq, 3 * tq), lambda i: (0, 0, 0))],
        out_specs=pl.BlockSpec((tq, A_WIDTH), lambda i: (i, 0)),
        out_shape=jax.ShapeDtypeStruct((tp, A_WIDTH), f32),
        compiler_params=_cparams(("parallel",)),
        name="attn_prompt",
    )(z, z, z, z, z, z, z, bias)


def _attn_sample_kernel(q_ref, kn_ref, vn_ref, kc_ref, vc_ref, bias_ref, o_ref):
    p_len = kc_ref.shape[1]
    q, kn, vn = q_ref[...], kn_ref[...], vn_ref[...]
    kc, vc = kc_ref[0], vc_ref[0]
    outs = []
    for h in range(HA):
        sl = slice(h * DHA, (h + 1) * DHA)
        qh = q[:, sl].astype(bf16)
        dims = (((1,), (1,)), ((), ()))
        sc = lax.dot_general(qh, kc[:, sl].astype(bf16), dims, preferred_element_type=f32)
        sn = lax.dot_general(qh, kn[:, sl].astype(bf16), dims, preferred_element_type=f32)
        b = bias_ref[h]
        sc = sc * (DHA ** -0.5) + b[:, :p_len]
        sn = sn * (DHA ** -0.5) + b[:, p_len:]
        m = jnp.maximum(jnp.max(sc, axis=-1, keepdims=True), jnp.max(sn, axis=-1, keepdims=True))
        pc, pn = jnp.exp(sc - m), jnp.exp(sn - m)
        den = jnp.sum(pc, axis=-1, keepdims=True) + jnp.sum(pn, axis=-1, keepdims=True)
        pc, pn = pc / den, pn / den
        outs.append(jnp.dot(pc.astype(bf16), vc[:, sl].astype(bf16), preferred_element_type=f32)
                    + jnp.dot(pn.astype(bf16), vn[:, sl].astype(bf16), preferred_element_type=f32))
    o_ref[...] = jnp.concatenate(outs, axis=-1)


def _attn_sample(z, k_cache, v_cache, rel_bias, row0, bs, ts):
    p_len = k_cache.shape[1]
    kc = k_cache.reshape(bs, p_len, A_WIDTH)
    vc = v_cache.reshape(bs, p_len, A_WIDTH)
    bias = _rel_bias_matrix(rel_bias, ts, p_len + ts, p_len)
    blk0 = row0 // ts

    def z_spec(col):
        return pl.BlockSpec((ts, A_WIDTH), lambda b: (blk0 + b, col))

    return pl.pallas_call(
        _attn_sample_kernel,
        grid=(bs,),
        in_specs=[z_spec(0), z_spec(1), z_spec(2),
                  pl.BlockSpec((1, p_len, A_WIDTH), lambda b: (b, 0, 0)),
                  pl.BlockSpec((1, p_len, A_WIDTH), lambda b: (b, 0, 0)),
                  pl.BlockSpec((HA, ts, p_len + ts), lambda b: (0, 0, 0))],
        out_specs=pl.BlockSpec((ts, A_WIDTH), lambda b: (b, 0)),
        out_shape=jax.ShapeDtypeStruct((bs * ts, A_WIDTH), f32),
        compiler_params=_cparams(("parallel",)),
        name="attn_sample",
    )(z, z, z, kc, vc, bias)


def _gelu_tanh(x):
    return 0.5 * x * (1.0 + jnp.tanh(math.sqrt(2.0 / math.pi) * (x + 0.044715 * (x * x * x))))


def _rglru_kernel(xb_ref, gb_ref, conv0_ref, h0_ref, cw_ref, cb_ref, wrg_ref, brg_ref,
                  wig_ref, big_ref, sp_ref, rec_ref, convn_ref, hl_ref, xp_ref, hc_ref):
    t = pl.program_id(1)
    tb = xb_ref.shape[0]
    pad = 8

    @pl.when(t == 0)
    def _():
        xp_ref[0:pad, :] = jnp.zeros((pad, B_WIDTH), f32)
        xp_ref[pad - (CONV_W - 1):pad, :] = conv0_ref[0]
        hc_ref[...] = h0_ref[0]

    xb = xb_ref[...]
    xp_ref[pad:pad + tb, :] = xb
    cw = cw_ref[...]
    u = cb_ref[...] + cw[CONV_W - 1:CONV_W, :] * xb
    for j in range(CONV_W - 1):
        sh = CONV_W - 1 - j
        u = u + cw[j:j + 1, :] * xp_ref[pad - sh:pad - sh + tb, :]
    convn_ref[0] = xp_ref[pad + tb - (CONV_W - 1):pad + tb, :]
    xp_ref[0:pad, :] = xp_ref[tb:tb + pad, :]

    ub = u.astype(bf16)
    r = jax.nn.sigmoid(jnp.dot(ub, wrg_ref[...], preferred_element_type=f32) + brg_ref[...])
    ig = jax.nn.sigmoid(jnp.dot(ub, wig_ref[...], preferred_element_type=f32) + big_ref[...])
    log_a = -LRU_C * r * sp_ref[...]
    a = jnp.exp(log_a)
    bt = jnp.sqrt(-jnp.tanh(log_a) * (a * a + 1.0)) * (ig * u)

    row = lax.broadcasted_iota(i32, (tb, 1), 0)
    s = 1
    while s < tb:
        keep = row >= s
        a_sh = pltpu.roll(a, s, axis=0)
        b_sh = pltpu.roll(bt, s, axis=0)
        bt = jnp.where(keep, a * b_sh + bt, bt)
        a = jnp.where(keep, a * a_sh, a)
        s *= 2
    h = a * hc_ref[...] + bt
    hc_ref[...] = h[tb - 1:tb, :]
    hl_ref[0] = h[tb - 1:tb, :]
    rec_ref[...] = h * _gelu_tanh(gb_ref[...])


def _rglru(z, conv0, h0, cw, cb, wrg, brg, wig, big, sp, row0, bsz, t_len, tb):
    nt = t_len // tb
    blk0 = row0 // tb

    def z_spec(col):
        return pl.BlockSpec((tb, B_WIDTH), lambda b, t: (blk0 + b * nt + t, col))

    def const(shape):
        return pl.BlockSpec(shape, lambda b, t: (0,) * len(shape))

    row = lambda v: v.reshape(1, B_WIDTH)
    return pl.pallas_call(
        _rglru_kernel,
        grid=(bsz, nt),
        in_specs=[z_spec(3), z_spec(4),
                  pl.BlockSpec((1, CONV_W - 1, B_WIDTH), lambda b, t: (b, 0, 0)),
                  pl.BlockSpec((1, 1, B_WIDTH), lambda b, t: (b, 0, 0)),
                  const((CONV_W, B_WIDTH)), const((1, B_WIDTH)),
                  const((B_WIDTH, B_WIDTH)), const((1, B_WIDTH)),
                  const((B_WIDTH, B_WIDTH)), const((1, B_WIDTH)), const((1, B_WIDTH))],
        out_specs=[pl.BlockSpec((tb, B_WIDTH), lambda b, t: (b * nt + t, 0)),
                   pl.BlockSpec((1, CONV_W - 1, B_WIDTH), lambda b, t: (b, 0, 0)),
                   pl.BlockSpec((1, 1, B_WIDTH), lambda b, t: (b, 0, 0))],
        out_shape=[jax.ShapeDtypeStruct((bsz * t_len, B_WIDTH), f32),
                   jax.ShapeDtypeStruct((bsz, CONV_W - 1, B_WIDTH), f32),
                   jax.ShapeDtypeStruct((bsz, 1, B_WIDTH), f32)],
        scratch_shapes=[pltpu.VMEM((tb + 8, B_WIDTH), f32), pltpu.VMEM((1, B_WIDTH), f32)],
        compiler_params=_cparams(("parallel", "arbitrary")),
        name="rglru",
    )(z, z, conv0, h0.reshape(bsz, 1, B_WIDTH), cw, row(cb), wrg, row(brg), wig, row(big), row(sp))


def _block_diag(w):
    hb, d, _ = w.shape
    eye = jnp.eye(hb, dtype=w.dtype)
    return (eye[:, None, :, None] * w[:, :, None, :]).reshape(hb * d, hb * d)


def _retention_kernel(q_ref, k_ref, v_ref, g_ref, cos_ref, sin_ref, s0_ref, gng_ref, gnb_ref,
                      y_ref, sout_ref, s_ref, *, chunk, nt):
    t = pl.program_id(1)
    tb = q_ref.shape[0]
    half = DKC // 2

    @pl.when(t == 0)
    def _():
        s_ref[...] = s0_ref[0]

    cos, sin = cos_ref[...], sin_ref[...]
    n = lax.broadcasted_iota(i32, (tb, 1), 0)
    m = lax.broadcasted_iota(i32, (1, tb), 1)
    dist = n - m
    same = (n // chunk) == (m // chunk)
    expo = jnp.where(same, jnp.abs(dist), dist).astype(f32)
    visible = same | (dist > 0)
    nf = n.astype(f32)
    for h in range(HC):
        lg = math.log(1.0 - 2.0 ** (-5.0 - h))
        base = h * DKC

        def rot(ref, scale):
            x1 = ref[:, base:base + half]
            x2 = ref[:, base + half:base + DKC]
            return jnp.concatenate([x1 * cos - x2 * sin, x2 * cos + x1 * sin], axis=-1) * scale

        qr = rot(q_ref, 1.0)
        kr = rot(k_ref, DKC ** -0.5)
        vb = v_ref[:, h * DVC:(h + 1) * DVC].astype(bf16)
        qb = qr.astype(bf16)
        dmat = jnp.where(visible, jnp.exp(lg * expo), 0.0)
        s = lax.dot_general(qb, kr.astype(bf16), (((1,), (1,)), ((), ())),
                            preferred_element_type=f32) * dmat
        s_old = s_ref[h]
        o = jnp.dot(s.astype(bf16), vb, preferred_element_type=f32)
        o = o + jnp.dot(qb, s_old.astype(bf16), preferred_element_type=f32) * jnp.exp(lg * (nf + 1.0))
        kd = (kr * jnp.exp(lg * (tb - 1.0 - nf))).astype(bf16)
        s_ref[h] = math.exp(lg * tb) * s_old + lax.dot_general(
            kd, vb, (((0,), (0,)), ((), ())), preferred_element_type=f32)

        mu = jnp.mean(o, axis=-1, keepdims=True)
        dlt = o - mu
        var = jnp.mean(dlt * dlt, axis=-1, keepdims=True)
        vs = slice(h * DVC, (h + 1) * DVC)
        yn = dlt * lax.rsqrt(var + GN_EPS) * gng_ref[:, vs] + gnb_ref[:, vs]
        y_ref[:, vs] = (jax.nn.silu(g_ref[:, vs]) * yn).astype(y_ref.dtype)

    @pl.when(t == nt - 1)
    def _():
        sout_ref[0] = s_ref[...]


def _retention(z, s0, pos0, gn_g, gn_b, row0, bsz, t_len, tb, chunk):
    nt = t_len // tb
    blk0 = row0 // tb
    half = DKC // 2
    inv = 1.0 / (ROPE_BASE ** (jnp.arange(half, dtype=f32) / half))
    ang = (pos0 + jnp.arange(t_len)).astype(f32)[:, None] * inv[None, :]
    cos, sin = jnp.cos(ang), jnp.sin(ang)

    def z_spec(width, col):
        return pl.BlockSpec((tb, width), lambda b, t: (blk0 + b * nt + t, col))

    return pl.pallas_call(
        functools.partial(_retention_kernel, chunk=chunk, nt=nt),
        grid=(bsz, nt),
        in_specs=[z_spec(C_QK, 0), z_spec(C_QK, 1), z_spec(C_V, 1), z_spec(C_V, 2),
                  pl.BlockSpec((tb, half), lambda b, t: (t, 0)),
                  pl.BlockSpec((tb, half), lambda b, t: (t, 0)),
                  pl.BlockSpec((1, HC, DKC, DVC), lambda b, t: (b, 0, 0, 0)),
                  pl.BlockSpec((1, C_V), lambda b, t: (0, 0)),
                  pl.BlockSpec((1, C_V), lambda b, t: (0, 0))],
        out_specs=[pl.BlockSpec((tb, C_V), lambda b, t: (b * nt + t, 0)),
                   pl.BlockSpec((1, HC, DKC, DVC), lambda b, t: (b, 0, 0, 0))],
        out_shape=[jax.ShapeDtypeStruct((bsz * t_len, C_V), bf16),
                   jax.ShapeDtypeStruct((bsz, HC, DKC, DVC), f32)],
        scratch_shapes=[pltpu.VMEM((HC, DKC, DVC), f32)],
        compiler_params=_cparams(("parallel", "arbitrary")),
        name="retention",
    )(z, z, z, z, cos, sin, s0, gn_g.reshape(1, C_V), gn_b.reshape(1, C_V))


def _router_kernel(x_ref, w_ref, b_ref, e_ref, g_ref):
    logits = jnp.dot(x_ref[...], w_ref[0], preferred_element_type=f32,
                     precision=lax.Precision.HIGHEST) + b_ref[0]
    tm = logits.shape[0]
    lane = lax.broadcasted_iota(i32, (tm, N_EXPERTS), 1)
    out_lane = lax.broadcasted_iota(i32, (tm, LANES), 1)
    e_out = jnp.zeros((tm, LANES), i32)
    g_out = jnp.zeros((tm, LANES), f32)
    v0 = None
    den = jnp.zeros((tm, 1), f32)
    for k in range(TOP_K):
        v = jnp.max(logits, axis=-1, keepdims=True)
        idx = jnp.min(jnp.where(logits == v, lane, N_EXPERTS), axis=-1, keepdims=True)
        logits = jnp.where(lane == idx, -jnp.inf, logits)
        if k == 0:
            v0 = v
        p = jnp.exp(v - v0)
        den = den + p
        e_out = jnp.where(out_lane == k, idx, e_out)
        g_out = jnp.where(out_lane == k, p, g_out)
    e_ref[...] = e_out
    g_ref[...] = g_out / den


def _router(x, w, b, l, tm):
    m, d = x.shape
    depth = w.shape[0]
    return pl.pallas_call(
        _router_kernel,
        grid=(m // tm,),
        in_specs=[pl.BlockSpec((tm, d), lambda i: (i, 0)),
                  pl.BlockSpec((1, d, N_EXPERTS), lambda i: (l, 0, 0)),
                  pl.BlockSpec((1, 1, N_EXPERTS), lambda i: (l, 0, 0))],
        out_specs=[pl.BlockSpec((tm, LANES), lambda i: (i, 0)),
                   pl.BlockSpec((tm, LANES), lambda i: (i, 0))],
        out_shape=[jax.ShapeDtypeStruct((m, LANES), i32),
                   jax.ShapeDtypeStruct((m, LANES), f32)],
        compiler_params=_cparams(("parallel",)),
        name="router",
    )(x, w, b.reshape(depth, 1, N_EXPERTS))


def _lane_prefix_sum(v):
    lane = lax.broadcasted_iota(i32, v.shape, 1)
    s = 1
    while s < LANES:
        v = v + jnp.where(lane >= s, pltpu.roll(v, s, axis=1), 0)
        s *= 2
    return v


def _slots_kernel(e_ref, dest_ref, cnt_ref, run_ref, start_ref, *, nt):
    ph = pl.program_id(0)
    i = pl.program_id(1)
    last = nt - 1
    tt = e_ref.shape[0]
    e = e_ref[...]
    lane = lax.broadcasted_iota(i32, (tt, LANES), 1)
    member = jnp.zeros((tt, LANES), f32)
    for k in range(TOP_K):
        member = member + (lane == e[:, k:k + 1]).astype(f32)
    tile_cnt = jnp.broadcast_to(jnp.sum(member, axis=0, keepdims=True), run_ref.shape)

    @pl.when((ph == 0) & (i == 0))
    def _():
        run_ref[...] = jnp.zeros(run_ref.shape, f32)

    @pl.when(ph == 0)
    def _():
        run_ref[...] += tile_cnt

        @pl.when(i == last)
        def _():
            cnt = run_ref[...].astype(i32)
            cnt_ref[...] = cnt
            padded = _round_up_bm(cnt)
            start_ref[...] = (_lane_prefix_sum(padded) - padded).astype(f32)
            run_ref[...] = jnp.zeros(run_ref.shape, f32)

    @pl.when(ph == 1)
    def _():
        r = lax.broadcasted_iota(i32, (tt, tt), 0)
        c = lax.broadcasted_iota(i32, (tt, tt), 1)
        lower = (c < r).astype(bf16)
        before = jnp.dot(lower, member.astype(bf16), preferred_element_type=f32)
        base = before + run_ref[0:1, :] + start_ref[0:1, :]
        out = jnp.zeros((tt, LANES), i32)
        for k in range(TOP_K):
            dk = jnp.sum(jnp.where(lane == e[:, k:k + 1], base, 0.0), axis=-1, keepdims=True)
            out = jnp.where(lane == k, dk.astype(i32), out)
        dest_ref[...] = out
        run_ref[...] += tile_cnt


def _slots(top_e, tt):
    n = top_e.shape[0]
    return pl.pallas_call(
        functools.partial(_slots_kernel, nt=n // tt),
        grid=(2, n // tt),
        in_specs=[pl.BlockSpec((tt, LANES), lambda p, i: (i, 0))],
        out_specs=[pl.BlockSpec((tt, LANES), lambda p, i: (i * p, 0)),
                   pl.BlockSpec((8, LANES), lambda p, i: (0, 0))],
        out_shape=[jax.ShapeDtypeStruct((n, LANES), i32),
                   jax.ShapeDtypeStruct((8, LANES), i32)],
        scratch_shapes=[pltpu.VMEM((8, LANES), f32), pltpu.VMEM((8, LANES), f32)],
        compiler_params=_cparams(("arbitrary", "arbitrary")),
        name="moe_slots",
    )(top_e)


def _dispatch_kernel(cnt_ref, start_ref, dest_ref, x_ref, xg_hbm, zrow, sem):
    i = pl.program_id(0)
    tm = x_ref.shape[0]

    @pl.when(i == 0)
    def _():
        zrow[...] = jnp.zeros(zrow.shape, f32)

        def per_expert(e, total):
            c = cnt_ref[e]
            npad = _round_up_bm(c) - c
            first = start_ref[e] + c

            def put(r, carry):
                pltpu.make_async_copy(zrow.at[pl.ds(0, 1)], xg_hbm.at[pl.ds(first + r, 1)], sem.at[1]).start()
                return carry
            lax.fori_loop(0, npad, put, 0)
            return total + npad
        total = lax.fori_loop(0, N_EXPERTS, per_expert, 0)

        def drain(r, carry):
            pltpu.make_async_copy(zrow.at[pl.ds(0, 1)], xg_hbm.at[pl.ds(0, 1)], sem.at[1]).wait()
            return carry
        lax.fori_loop(0, total, drain, 0)

        nblk = xg_hbm.shape[0] // MOE_BM
        end = start_ref[N_EXPERTS - 1] + _round_up_bm(cnt_ref[N_EXPERTS - 1])
        nact = lax.shift_right_logical(end, MOE_BM_SHIFT)

        def put_blk(bi, carry):
            pltpu.make_async_copy(zrow, xg_hbm.at[pl.ds(bi * MOE_BM, MOE_BM)], sem.at[1]).start()
            return carry
        lax.fori_loop(nact, nblk, put_blk, 0)

        def drain_blk(bi, carry):
            pltpu.make_async_copy(zrow, xg_hbm.at[pl.ds(0, MOE_BM)], sem.at[1]).wait()
            return carry
        lax.fori_loop(nact, nblk, drain_blk, 0)

    def body(r, carry):
        for k in range(TOP_K):
            dst = dest_ref[0, 0, r * TOP_K + k]
            pltpu.make_async_copy(x_ref.at[pl.ds(r, 1)], xg_hbm.at[pl.ds(dst, 1)], sem.at[0]).start()
        return carry
    lax.fori_loop(0, tm, body, 0, unroll=4)
    pltpu.make_async_copy(xg_hbm.at[pl.ds(0, tm * TOP_K)], xg_hbm.at[pl.ds(0, tm * TOP_K)], sem.at[0]).wait()


def _dispatch(x, dest3, counts, starts, nblk, tm):
    n, d = x.shape
    grid_spec = pltpu.PrefetchScalarGridSpec(
        num_scalar_prefetch=2,
        grid=(n // tm,),
        in_specs=[pl.BlockSpec((1, 1, tm * TOP_K), lambda i, c, s: (i, 0, 0), memory_space=pltpu.SMEM),
                  pl.BlockSpec((tm, d), lambda i, c, s: (i, 0))],
        out_specs=pl.BlockSpec(memory_space=pl.ANY),
        scratch_shapes=[pltpu.VMEM((MOE_BM, d), f32), pltpu.SemaphoreType.DMA((2,))],
    )
    return pl.pallas_call(
        _dispatch_kernel,
        grid_spec=grid_spec,
        out_shape=jax.ShapeDtypeStruct((nblk * MOE_BM, d), f32),
        compiler_params=_cparams(("arbitrary",)),
        name="moe_dispatch",
    )(counts, starts, dest3, x)


def _moe_kernel(blk_e_ref, nact_ref, x_ref, wup_ref, bup_ref, wdn_ref, bdn_ref, y_ref, wup_bf, wdn_bf):
    b = pl.program_id(0)
    nact = nact_ref[0]

    @pl.when(b < nact)
    def _():
        @pl.when((b == 0) | (blk_e_ref[b] != blk_e_ref[jnp.maximum(b - 1, 0)]))
        def _():
            wup_bf[...] = wup_ref[0, 0].astype(bf16)
            wdn_bf[...] = wdn_ref[0, 0].astype(bf16)

        hdn = jnp.dot(x_ref[...].astype(bf16), wup_bf[...], preferred_element_type=f32) + bup_ref[0, 0]
        glu = jnp.minimum(hdn[:, :D_FF], SWIGLU_LIMIT)
        lin = jnp.clip(hdn[:, D_FF:], -SWIGLU_LIMIT, SWIGLU_LIMIT)
        act = glu * jax.nn.sigmoid(SWIGLU_ALPHA * glu) * (lin + 1.0)
        y_ref[...] = jnp.dot(act.astype(bf16), wdn_bf[...], preferred_element_type=f32) + bdn_ref[0, 0]

    @pl.when(b >= nact)
    def _():
        y_ref[...] = jnp.zeros(y_ref.shape, f32)


def _moe_experts(xg, blk_e, nact, l, w_up, b_up, w_down, b_down):
    r, d = xg.shape
    bm = MOE_BM
    nblk = r // bm
    depth = w_up.shape[0]
    wsel = lambda b, be, na: (l, be[b], 0, 0)
    grid_spec = pltpu.PrefetchScalarGridSpec(
        num_scalar_prefetch=2,
        grid=(nblk,),
        in_specs=[
            pl.BlockSpec((bm, d), lambda b, be, na: (jnp.minimum(b, na[0] - 1), 0)),
            pl.BlockSpec((1, 1, d, 2 * D_FF), wsel),
            pl.BlockSpec((1, 1, 1, 2 * D_FF), wsel),
            pl.BlockSpec((1, 1, D_FF, d), wsel),
            pl.BlockSpec((1, 1, 1, d), wsel),
        ],
        out_specs=pl.BlockSpec((bm, d), lambda b, be, na: (b, 0)),
        scratch_shapes=[pltpu.VMEM((d, 2 * D_FF), bf16), pltpu.VMEM((D_FF, d), bf16)],
    )
    return pl.pallas_call(
        _moe_kernel,
        grid_spec=grid_spec,
        out_shape=jax.ShapeDtypeStruct((r, d), f32),
        compiler_params=_cparams(("arbitrary",)),
        name="moe_experts",
    )(blk_e, nact, xg, w_up, b_up.reshape(depth, N_EXPERTS, 1, 2 * D_FF),
      w_down, b_down.reshape(depth, N_EXPERTS, 1, d))


def _combine_ln_kernel(pos_cur_ref, pos_nxt_ref, y_hbm, gate_ref, x_ref, g_ref, b_ref, o_ref, gbuf, sem,
                       *, nt):
    i = pl.program_id(0)
    tm = x_ref.shape[0]

    def gather(pos_ref, slot):
        def body(r, carry):
            for k in range(TOP_K):
                p = pos_ref[0, 0, r * TOP_K + k]
                pltpu.make_async_copy(y_hbm.at[pl.ds(p, 1)], gbuf.at[slot, k, pl.ds(r, 1)],
                                      sem.at[slot]).start()
            return carry
        lax.fori_loop(0, tm, body, 0, unroll=4)

    @pl.when(i == 0)
    def _():
        gather(pos_cur_ref, 0)

    @pl.when(i + 1 < nt)
    def _():
        gather(pos_nxt_ref, (i + 1) % 2)

    slot = i % 2
    pltpu.make_async_copy(gbuf.at[slot], gbuf.at[slot], sem.at[slot]).wait()
    gates = gate_ref[...]
    acc = ALPHA * x_ref[...]
    for k in range(TOP_K):
        acc = acc + gates[:, k:k + 1] * gbuf[slot, k]
    o_ref[...] = _layernorm(acc, g_ref[...], b_ref[...])


def _combine_ln(y, pos3, gates, x, g, b, tm, row0, rows):
    n, d = x.shape
    blk0 = row0 // tm
    nt = rows // tm
    grid_spec = pltpu.PrefetchScalarGridSpec(
        num_scalar_prefetch=0,
        grid=(nt,),
        in_specs=[
            pl.BlockSpec((1, 1, tm * TOP_K), lambda i: (blk0 + i, 0, 0), memory_space=pltpu.SMEM),
            pl.BlockSpec((1, 1, tm * TOP_K), lambda i: (blk0 + jnp.minimum(i + 1, nt - 1), 0, 0),
                         memory_space=pltpu.SMEM),
            pl.BlockSpec(memory_space=pl.ANY),
            pl.BlockSpec((tm, LANES), lambda i: (blk0 + i, 0)),
            pl.BlockSpec((tm, d), lambda i: (blk0 + i, 0)),
            pl.BlockSpec((1, d), lambda i: (0, 0)),
            pl.BlockSpec((1, d), lambda i: (0, 0)),
        ],
        out_specs=pl.BlockSpec((tm, d), lambda i: (i, 0)),
        scratch_shapes=[pltpu.VMEM((2, TOP_K, tm, d), f32), pltpu.SemaphoreType.DMA((2,))],
    )
    return pl.pallas_call(
        functools.partial(_combine_ln_kernel, nt=nt),
        grid_spec=grid_spec,
        out_shape=jax.ShapeDtypeStruct((rows, d), f32),
        compiler_params=_cparams(("arbitrary",)),
        name="combine_ln",
    )(pos3, pos3, y, gates, x, g.reshape(1, d), b.reshape(1, d))


def _moe_ln(x, l, w_router, b_router, w_up, b_up, w_down, b_down, g, b, splits):
    n = x.shape[0]
    bm = MOE_BM
    nblk = -(-(n * TOP_K) // bm) + N_EXPERTS
    tm = _pick(n, (512, 256, 128, 64))
    top_e, gates = _router(x, w_router, b_router, l, tm)
    dest, counts = _slots(top_e, tm)
    counts = counts[0, :N_EXPERTS]
    padded = (counts + bm - 1) // bm * bm
    ends = jnp.cumsum(padded)
    starts = (ends - padded).astype(i32)
    blk_e = jnp.minimum(jnp.searchsorted(ends, jnp.arange(nblk, dtype=i32) * bm, side="right"),
                        N_EXPERTS - 1).astype(i32)
    nact = (ends[-1] // bm).astype(i32).reshape(1)
    tmc = _pick(math.gcd(*[v for s in splits for v in s if v], n), (256, 128, 64, 32))
    dest3 = dest[:, :TOP_K].reshape(n // tmc, 1, tmc * TOP_K)
    xg = _dispatch(x, dest3, counts, starts, nblk, tmc)
    y = _moe_experts(xg, blk_e, nact, l, w_up, b_up, w_down, b_down)
    outs = [_combine_ln(y, dest3, gates, x, g, b, tmc, r0, rows) for (r0, rows) in splits]
    return outs[0] if len(outs) == 1 else outs


def kernel(x_prompt, x_sample, cache_a_k, cache_a_v, state_b_conv, state_b_h, state_c_s, ab_w_in, ab_rel_bias, ab_conv_w, ab_conv_b, ab_w_rg, ab_b_rg, ab_w_ig, ab_b_ig, ab_lambda, ab_w_out, c_w_in, c_gn_g, c_gn_b, c_w_out, ln1_g, ln1_b, ln2_g, ln2_b, moe_w_router, moe_b_router, moe_w_up, moe_b_up, moe_w_down, moe_b_down):
    bp, tp, d = x_prompt.shape
    bs, ts, _ = x_sample.shape
    assert bp == 1 and tp % RET_TB == 0 and tp % ts == 0 and ts <= CHUNK
    n_p, n_s = bp * tp, bs * ts
    n = n_p + n_s
    x = jnp.concatenate([x_prompt.reshape(n_p, d), x_sample.reshape(n_s, d)], axis=0)
    tm = _pick(n, (512, 256, 128, 64))
    moe = (moe_w_router, moe_b_router, moe_w_up, moe_b_up, moe_w_down, moe_b_down)

    j = 0
    z = _matmul(x, ab_w_in[j].astype(bf16), tm, 512)
    att = jnp.concatenate([
        _attn_prompt(z, ab_rel_bias[j], tp),
        _attn_sample(z, cache_a_k[j], cache_a_v[j], ab_rel_bias[j], n_p, bs, ts)], axis=0)
    lru_w = (ab_conv_w[j], ab_conv_b[j], _block_diag(ab_w_rg[j]).astype(bf16), ab_b_rg[j],
             _block_diag(ab_w_ig[j]).astype(bf16), ab_b_ig[j], jax.nn.softplus(-ab_lambda[j]))
    rec_p, bc_p, bh_p = _rglru(z, jnp.zeros((bp, CONV_W - 1, B_WIDTH), f32), jnp.zeros((bp, B_WIDTH), f32),
                               *lru_w, 0, bp, tp, _pick(tp, (LRU_TB, 128, 64)))
    rec_s, bc_s, bh_s = _rglru(z, state_b_conv[j], state_b_h[j], *lru_w, n_p, bs, ts, ts)
    rec = jnp.concatenate([rec_p, rec_s], axis=0)
    w_out = ab_w_out[j].astype(bf16)
    x = _proj_ln([att, rec], [w_out[:A_WIDTH], w_out[A_WIDTH:]], x, ln1_g[0], ln1_b[0], tm)
    x = _moe_ln(x, 0, *moe, ln2_g[0], ln2_b[0], splits=[(0, n)])

    keep = min(A_PAST_CHUNKS * CHUNK, tp)
    ak_p = z[n_p - keep:n_p, A_WIDTH:2 * A_WIDTH].reshape(1, bp, keep, HA, DHA)
    av_p = z[n_p - keep:n_p, 2 * A_WIDTH:3 * A_WIDTH].reshape(1, bp, keep, HA, DHA)
    ak_s = z[n_p:, A_WIDTH:2 * A_WIDTH].reshape(1, bs, ts, HA, DHA)
    av_s = z[n_p:, 2 * A_WIDTH:3 * A_WIDTH].reshape(1, bs, ts, HA, DHA)

    z = _matmul(x, c_w_in[j].astype(bf16), tm, 1024)
    y_p, cs_p = _retention(z, jnp.zeros((bp, HC, DKC, DVC), f32), 0, c_gn_g[j], c_gn_b[j],
                           0, bp, tp, RET_TB, CHUNK)
    y_s, cs_s = _retention(z, state_c_s[j], PAST_LEN, c_gn_g[j], c_gn_b[j], n_p, bs, ts, ts, ts)
    y = jnp.concatenate([y_p, y_s], axis=0)
    x = _proj_ln([y], [c_w_out[j].astype(bf16)], x, ln1_g[1], ln1_b[1], tm)
    out_p, out_s = _moe_ln(x, 1, *moe, ln2_g[1], ln2_b[1], splits=[(0, n_p), (n_p, n_s)])

    return (out_p.reshape(bp, tp, d), out_s.reshape(bs, ts, d),
            ak_p, av_p, bc_p[None], bh_p.reshape(1, bp, B_WIDTH), cs_p[None],
            ak_s, av_s, bc_s[None], bh_s.reshape(1, bs, B_WIDTH), cs_s[None])
```

```python
import functools
import math

import jax
import jax.numpy as jnp
from jax import lax
from jax.experimental import pallas as pl
from jax.experimental.pallas import tpu as pltpu

f32 = jnp.float32
bf16 = jnp.bfloat16
i32 = jnp.int32

DEPTH = 2
PAST_LEN = 1024
CHUNK = 64
A_PAST_CHUNKS = 8
REL_CLIP = 128
HA, DHA = 8, 64
A_WIDTH = HA * DHA
B_WIDTH = 512
HB = 8
CONV_W = 4
LRU_C = 8.0
HC, DKC, DVC = 4, 256, 512
C_QK, C_V = HC * DKC, HC * DVC
ROPE_BASE = 10000.0
GN_EPS = 1e-5
N_EXPERTS = 32
TOP_K = 4
D_FF = 1024
SWIGLU_LIMIT = 7.0
SWIGLU_ALPHA = 1.702
ALPHA = (2.0 * DEPTH) ** 0.25
LN_EPS = 1e-5
NEG = -1e30

LANES = 128
VMEM_LIMIT = 56 * 1024 * 1024
ATT_TQ = 256
MOE_BM = 256
MOE_BM_SHIFT = MOE_BM.bit_length() - 1
RET_TB = 512
LRU_TB = 256


def _pick(n, cands):
    for c in cands:
        if n % c == 0:
            return c
    raise ValueError(f"no tile for {n} in {cands}")


def _cparams(sem):
    return pltpu.CompilerParams(dimension_semantics=sem, vmem_limit_bytes=VMEM_LIMIT)


def _layernorm(v, g, b):
    mu = jnp.mean(v, axis=-1, keepdims=True)
    d = v - mu
    var = jnp.mean(d * d, axis=-1, keepdims=True)
    return d * lax.rsqrt(var + LN_EPS) * g + b


def _round_up_bm(v):
    return lax.shift_left(lax.shift_right_logical(v + (MOE_BM - 1), MOE_BM_SHIFT), MOE_BM_SHIFT)


def _rel_bias_matrix(table, nq, nk, d0):
    p = nq + nk - 1
    j = jnp.arange(p)
    u = table[:, jnp.clip(d0 + nq - 1 - j, -REL_CLIP, REL_CLIP) + REL_CLIP]
    u = jnp.roll(u, -(nq - 1), axis=1)
    flat = jnp.tile(u, (1, nq))[:, :nq * (p - 1)]
    return flat.reshape(table.shape[0], nq, p - 1)[:, :, :nk].astype(f32)


def _mm_kernel(x_ref, w_ref, o_ref):
    o_ref[...] = jnp.dot(x_ref[...].astype(bf16), w_ref[...], preferred_element_type=f32)


def _matmul(x, w, tm, tn):
    m, k = x.shape
    n = w.shape[1]
    return pl.pallas_call(
        _mm_kernel,
        grid=(m // tm, n // tn),
        in_specs=[pl.BlockSpec((tm, k), lambda i, j: (i, 0)),
                  pl.BlockSpec((k, tn), lambda i, j: (0, j))],
        out_specs=pl.BlockSpec((tm, tn), lambda i, j: (i, j)),
        out_shape=jax.ShapeDtypeStruct((m, n), f32),
        compiler_params=_cparams(("parallel", "parallel")),
        name="dense_proj",
    )(x, w)


def _route_top_k(x, w, b):
    logits = jnp.dot(x, w, preferred_element_type=f32, precision=lax.Precision.HIGHEST) + b
    tm = logits.shape[0]
    lane = lax.broadcasted_iota(i32, (tm, N_EXPERTS), 1)
    out_lane = lax.broadcasted_iota(i32, (tm, LANES), 1)
    e_out = jnp.zeros((tm, LANES), i32)
    g_out = jnp.zeros((tm, LANES), f32)
    v0 = None
    den = jnp.zeros((tm, 1), f32)
    for k in range(TOP_K):
        v = jnp.max(logits, axis=-1, keepdims=True)
        idx = jnp.min(jnp.where(logits == v, lane, N_EXPERTS), axis=-1, keepdims=True)
        logits = jnp.where(lane == idx, -jnp.inf, logits)
        if k == 0:
            v0 = v
        p = jnp.exp(v - v0)
        den = den + p
        e_out = jnp.where(out_lane == k, idx, e_out)
        g_out = jnp.where(out_lane == k, p, g_out)
    return e_out, g_out / den


def _proj_ln_kernel(*refs, n_in):
    a_refs = refs[:n_in]
    w_refs = refs[n_in:2 * n_in]
    x_ref, g_ref, b_ref, wr_ref, br_ref, o_ref, e_ref, gate_ref = refs[2 * n_in:]
    acc = ALPHA * x_ref[...]
    for a_ref, w_ref in zip(a_refs, w_refs):
        acc = acc + jnp.dot(a_ref[...].astype(bf16), w_ref[...], preferred_element_type=f32)
    y = _layernorm(acc, g_ref[...], b_ref[...])
    o_ref[...] = y
    e_ref[...], gate_ref[...] = _route_top_k(y, wr_ref[0], br_ref[0])


def _proj_ln(acts, ws, x, g, b, w_router, b_router, l, tm):
    m, d = x.shape
    n_in = len(acts)
    depth = w_router.shape[0]
    in_specs = ([pl.BlockSpec((tm, a.shape[1]), lambda i: (i, 0)) for a in acts]
                + [pl.BlockSpec(w.shape, lambda i: (0, 0)) for w in ws]
                + [pl.BlockSpec((tm, d), lambda i: (i, 0)),
                   pl.BlockSpec((1, d), lambda i: (0, 0)),
                   pl.BlockSpec((1, d), lambda i: (0, 0)),
                   pl.BlockSpec((1, d, N_EXPERTS), lambda i: (l, 0, 0)),
                   pl.BlockSpec((1, 1, N_EXPERTS), lambda i: (l, 0, 0))])
    return pl.pallas_call(
        functools.partial(_proj_ln_kernel, n_in=n_in),
        grid=(m // tm,),
        in_specs=in_specs,
        out_specs=[pl.BlockSpec((tm, d), lambda i: (i, 0)),
                   pl.BlockSpec((tm, LANES), lambda i: (i, 0)),
                   pl.BlockSpec((tm, LANES), lambda i: (i, 0))],
        out_shape=[jax.ShapeDtypeStruct((m, d), f32),
                   jax.ShapeDtypeStruct((m, LANES), i32),
                   jax.ShapeDtypeStruct((m, LANES), f32)],
        compiler_params=_cparams(("parallel",)),
        name="proj_ln",
    )(*acts, *ws, x, g.reshape(1, d), b.reshape(1, d), w_router, b_router.reshape(depth, 1, N_EXPERTS))


def _attn_prompt_kernel(q_ref, k0_ref, k1_ref, k2_ref, v0_ref, v1_ref, v2_ref, bias_ref, o_ref):
    i = pl.program_id(0)
    tq = q_ref.shape[0]
    q = q_ref[...]
    k = jnp.concatenate([k0_ref[...], k1_ref[...], k2_ref[...]], axis=0)
    v = jnp.concatenate([v0_ref[...], v1_ref[...], v2_ref[...]], axis=0)
    kcol = lax.broadcasted_iota(i32, (1, 3 * tq), 1)
    tile_ok = (kcol // tq + i) >= 2
    outs = []
    for h in range(HA):
        sl = slice(h * DHA, (h + 1) * DHA)
        s = lax.dot_general(q[:, sl].astype(bf16), k[:, sl].astype(bf16),
                            (((1,), (1,)), ((), ())), preferred_element_type=f32)
        s = s * (DHA ** -0.5) + bias_ref[h]
        s = jnp.where(tile_ok, s, NEG)
        s = s - jnp.max(s, axis=-1, keepdims=True)
        p = jnp.exp(s)
        p = p / jnp.sum(p, axis=-1, keepdims=True)
        outs.append(jnp.dot(p.astype(bf16), v[:, sl].astype(bf16), preferred_element_type=f32))
    o_ref[...] = jnp.concatenate(outs, axis=-1)


def _attn_prompt(z, rel_bias, tp):
    tq = ATT_TQ
    nq = tp // tq
    qq = jnp.arange(tq)[:, None]
    kk = jnp.arange(3 * tq)[None, :]
    band = (kk // CHUNK >= qq // CHUNK) & (kk // CHUNK <= qq // CHUNK + A_PAST_CHUNKS)
    bias = jnp.where(band[None], _rel_bias_matrix(rel_bias, tq, 3 * tq, 2 * tq), NEG)

    def kv_spec(back, col):
        return pl.BlockSpec((tq, A_WIDTH), lambda i: (jnp.maximum(i - back, 0), col))

    return pl.pallas_call(
        _attn_prompt_kernel,
        grid=(nq,),
        in_specs=[pl.BlockSpec((tq, A_WIDTH), lambda i: (i, 0)),
                  kv_spec(2, 1), kv_spec(1, 1), kv_spec(0, 1),
                  kv_spec(2, 2), kv_spec(1, 2), kv_spec(0, 2),
                  pl.BlockSpec((HA, tq, 3 * tq), lambda i: (0, 0, 0))],
        out_specs=pl.BlockSpec((tq, A_WIDTH), lambda i: (i, 0)),
        out_shape=jax.ShapeDtypeStruct((tp, A_WIDTH), f32),
        compiler_params=_cparams(("parallel",)),
        name="attn_prompt",
    )(z, z, z, z, z, z, z, bias)


def _attn_sample_kernel(q_ref, kn_ref, vn_ref, kc_ref, vc_ref, bias_ref, o_ref):
    p_len = kc_ref.shape[1]
    q, kn, vn = q_ref[...], kn_ref[...], vn_ref[...]
    kc, vc = kc_ref[0], vc_ref[0]
    outs = []
    for h in range(HA):
        sl = slice(h * DHA, (h + 1) * DHA)
        qh = q[:, sl].astype(bf16)
        dims = (((1,), (1,)), ((), ()))
        sc = lax.dot_general(qh, kc[:, sl].astype(bf16), dims, preferred_element_type=f32)
        sn = lax.dot_general(qh, kn[:, sl].astype(bf16), dims, preferred_element_type=f32)
        b = bias_ref[h]
        sc = sc * (DHA ** -0.5) + b[:, :p_len]
        sn = sn * (DHA ** -0.5) + b[:, p_len:]
        m = jnp.maximum(jnp.max(sc, axis=-1, keepdims=True), jnp.max(sn, axis=-1, keepdims=True))
        pc, pn = jnp.exp(sc - m), jnp.exp(sn - m)
        den = jnp.sum(pc, axis=-1, keepdims=True) + jnp.sum(pn, axis=-1, keepdims=True)
        pc, pn = pc / den, pn / den
        outs.append(jnp.dot(pc.astype(bf16), vc[:, sl].astype(bf16), preferred_element_type=f32)
                    + jnp.dot(pn.astype(bf16), vn[:, sl].astype(bf16), preferred_element_type=f32))
    o_ref[...] = jnp.concatenate(outs, axis=-1)


def _attn_sample(z, k_cache, v_cache, rel_bias, row0, bs, ts):
    p_len = k_cache.shape[1]
    kc = k_cache.reshape(bs, p_len, A_WIDTH)
    vc = v_cache.reshape(bs, p_len, A_WIDTH)
    bias = _rel_bias_matrix(rel_bias, ts, p_len + ts, p_len)
    blk0 = row0 // ts

    def z_spec(col):
        return pl.BlockSpec((ts, A_WIDTH), lambda b: (blk0 + b, col))

    return pl.pallas_call(
        _attn_sample_kernel,
        grid=(bs,),
        in_specs=[z_spec(0), z_spec(1), z_spec(2),
                  pl.BlockSpec((1, p_len, A_WIDTH), lambda b: (b, 0, 0)),
                  pl.BlockSpec((1, p_len, A_WIDTH), lambda b: (b, 0, 0)),
                  pl.BlockSpec((HA, ts, p_len + ts), lambda b: (0, 0, 0))],
        out_specs=pl.BlockSpec((ts, A_WIDTH), lambda b: (b, 0)),
        out_shape=jax.ShapeDtypeStruct((bs * ts, A_WIDTH), f32),
        compiler_params=_cparams(("parallel",)),
        name="attn_sample",
    )(z, z, z, kc, vc, bias)


def _gelu_tanh(x):
    return 0.5 * x * (1.0 + jnp.tanh(math.sqrt(2.0 / math.pi) * (x + 0.044715 * (x * x * x))))


def _rglru_kernel(xb_ref, gb_ref, conv0_ref, h0_ref, cw_ref, cb_ref, wrg_ref, brg_ref,
                  wig_ref, big_ref, sp_ref, rec_ref, convn_ref, hl_ref, xp_ref, hc_ref):
    t = pl.program_id(1)
    tb = xb_ref.shape[0]
    pad = 8

    @pl.when(t == 0)
    def _():
        xp_ref[0:pad, :] = jnp.zeros((pad, B_WIDTH), f32)
        xp_ref[pad - (CONV_W - 1):pad, :] = conv0_ref[0]
        hc_ref[...] = h0_ref[0]

    xb = xb_ref[...]
    xp_ref[pad:pad + tb, :] = xb
    cw = cw_ref[...]
    u = cb_ref[...] + cw[CONV_W - 1:CONV_W, :] * xb
    for j in range(CONV_W - 1):
        sh = CONV_W - 1 - j
        u = u + cw[j:j + 1, :] * xp_ref[pad - sh:pad - sh + tb, :]
    convn_ref[0] = xp_ref[pad + tb - (CONV_W - 1):pad + tb, :]
    xp_ref[0:pad, :] = xp_ref[tb:tb + pad, :]

    ub = u.astype(bf16)
    r = jax.nn.sigmoid(jnp.dot(ub, wrg_ref[...], preferred_element_type=f32) + brg_ref[...])
    ig = jax.nn.sigmoid(jnp.dot(ub, wig_ref[...], preferred_element_type=f32) + big_ref[...])
    log_a = -LRU_C * r * sp_ref[...]
    a = jnp.exp(log_a)
    bt = jnp.sqrt(-jnp.tanh(log_a) * (a * a + 1.0)) * (ig * u)

    row = lax.broadcasted_iota(i32, (tb, 1), 0)
    s = 1
    while s < tb:
        keep = row >= s
        a_sh = pltpu.roll(a, s, axis=0)
        b_sh = pltpu.roll(bt, s, axis=0)
        bt = jnp.where(keep, a * b_sh + bt, bt)
        a = jnp.where(keep, a * a_sh, a)
        s *= 2
    h = a * hc_ref[...] + bt
    hc_ref[...] = h[tb - 1:tb, :]
    hl_ref[0] = h[tb - 1:tb, :]
    rec_ref[...] = h * _gelu_tanh(gb_ref[...])


def _rglru(z, conv0, h0, cw, cb, wrg, brg, wig, big, sp, row0, bsz, t_len, tb):
    nt = t_len // tb
    blk0 = row0 // tb

    def z_spec(col):
        return pl.BlockSpec((tb, B_WIDTH), lambda b, t: (blk0 + b * nt + t, col))

    def const(shape):
        return pl.BlockSpec(shape, lambda b, t: (0,) * len(shape))

    row = lambda v: v.reshape(1, B_WIDTH)
    return pl.pallas_call(
        _rglru_kernel,
        grid=(bsz, nt),
        in_specs=[z_spec(3), z_spec(4),
                  pl.BlockSpec((1, CONV_W - 1, B_WIDTH), lambda b, t: (b, 0, 0)),
                  pl.BlockSpec((1, 1, B_WIDTH), lambda b, t: (b, 0, 0)),
                  const((CONV_W, B_WIDTH)), const((1, B_WIDTH)),
                  const((B_WIDTH, B_WIDTH)), const((1, B_WIDTH)),
                  const((B_WIDTH, B_WIDTH)), const((1, B_WIDTH)), const((1, B_WIDTH))],
        out_specs=[pl.BlockSpec((tb, B_WIDTH), lambda b, t: (b * nt + t, 0)),
                   pl.BlockSpec((1, CONV_W - 1, B_WIDTH), lambda b, t: (b, 0, 0)),
                   pl.BlockSpec((1, 1, B_WIDTH), lambda b, t: (b, 0, 0))],
        out_shape=[jax.ShapeDtypeStruct((bsz * t_len, B_WIDTH), f32),
                   jax.ShapeDtypeStruct((bsz, CONV_W - 1, B_WIDTH), f32),
                   jax.ShapeDtypeStruct((bsz, 1, B_WIDTH), f32)],
        scratch_shapes=[pltpu.VMEM((tb + 8, B_WIDTH), f32), pltpu.VMEM((1, B_WIDTH), f32)],
        compiler_params=_cparams(("parallel", "arbitrary")),
        name="rglru",
    )(z, z, conv0, h0.reshape(bsz, 1, B_WIDTH), cw, row(cb), wrg, row(brg), wig, row(big), row(sp))


def _block_diag(w):
    hb, d, _ = w.shape
    eye = jnp.eye(hb, dtype=w.dtype)
    return (eye[:, None, :, None] * w[:, :, None, :]).reshape(hb * d, hb * d)


def _retention_kernel(q_ref, k_ref, v_ref, g_ref, cos_ref, sin_ref, s0_ref, gng_ref, gnb_ref,
                      y_ref, sout_ref, s_ref, *, chunk, nt):
    t = pl.program_id(1)
    tb = q_ref.shape[0]
    half = DKC // 2

    @pl.when(t == 0)
    def _():
        s_ref[...] = s0_ref[0]

    cos, sin = cos_ref[...], sin_ref[...]
    n = lax.broadcasted_iota(i32, (tb, 1), 0)
    m = lax.broadcasted_iota(i32, (1, tb), 1)
    dist = n - m
    same = (n // chunk) == (m // chunk)
    expo = jnp.where(same, jnp.abs(dist), dist).astype(f32)
    visible = same | (dist > 0)
    nf = n.astype(f32)
    for h in range(HC):
        lg = math.log(1.0 - 2.0 ** (-5.0 - h))
        base = h * DKC

        def rot(ref, scale):
            x1 = ref[:, base:base + half]
            x2 = ref[:, base + half:base + DKC]
            return jnp.concatenate([x1 * cos - x2 * sin, x2 * cos + x1 * sin], axis=-1) * scale

        qr = rot(q_ref, 1.0)
        kr = rot(k_ref, DKC ** -0.5)
        vb = v_ref[:, h * DVC:(h + 1) * DVC].astype(bf16)
        qb = qr.astype(bf16)
        dmat = jnp.where(visible, jnp.exp(lg * expo), 0.0)
        s = lax.dot_general(qb, kr.astype(bf16), (((1,), (1,)), ((), ())),
                            preferred_element_type=f32) * dmat
        s_old = s_ref[h]
        o = jnp.dot(s.astype(bf16), vb, preferred_element_type=f32)
        o = o + jnp.dot(qb, s_old.astype(bf16), preferred_element_type=f32) * jnp.exp(lg * (nf + 1.0))
        kd = (kr * jnp.exp(lg * (tb - 1.0 - nf))).astype(bf16)
        s_ref[h] = math.exp(lg * tb) * s_old + lax.dot_general(
            kd, vb, (((0,), (0,)), ((), ())), preferred_element_type=f32)

        mu = jnp.mean(o, axis=-1, keepdims=True)
        dlt = o - mu
        var = jnp.mean(dlt * dlt, axis=-1, keepdims=True)
        vs = slice(h * DVC, (h + 1) * DVC)
        yn = dlt * lax.rsqrt(var + GN_EPS) * gng_ref[:, vs] + gnb_ref[:, vs]
        y_ref[:, vs] = (jax.nn.silu(g_ref[:, vs]) * yn).astype(y_ref.dtype)

    @pl.when(t == nt - 1)
    def _():
        sout_ref[0] = s_ref[...]


def _retention(z, s0, pos0, gn_g, gn_b, row0, bsz, t_len, tb, chunk):
    nt = t_len // tb
    blk0 = row0 // tb
    half = DKC // 2
    inv = 1.0 / (ROPE_BASE ** (jnp.arange(half, dtype=f32) / half))
    ang = (pos0 + jnp.arange(t_len)).astype(f32)[:, None] * inv[None, :]
    cos, sin = jnp.cos(ang), jnp.sin(ang)

    def z_spec(width, col):
        return pl.BlockSpec((tb, width), lambda b, t: (blk0 + b * nt + t, col))

    return pl.pallas_call(
        functools.partial(_retention_kernel, chunk=chunk, nt=nt),
        grid=(bsz, nt),
        in_specs=[z_spec(C_QK, 0), z_spec(C_QK, 1), z_spec(C_V, 1), z_spec(C_V, 2),
                  pl.BlockSpec((tb, half), lambda b, t: (t, 0)),
                  pl.BlockSpec((tb, half), lambda b, t: (t, 0)),
                  pl.BlockSpec((1, HC, DKC, DVC), lambda b, t: (b, 0, 0, 0)),
                  pl.BlockSpec((1, C_V), lambda b, t: (0, 0)),
                  pl.BlockSpec((1, C_V), lambda b, t: (0, 0))],
        out_specs=[pl.BlockSpec((tb, C_V), lambda b, t: (b * nt + t, 0)),
                   pl.BlockSpec((1, HC, DKC, DVC), lambda b, t: (b, 0, 0, 0))],
        out_shape=[jax.ShapeDtypeStruct((bsz * t_len, C_V), bf16),
                   jax.ShapeDtypeStruct((bsz, HC, DKC, DVC), f32)],
        scratch_shapes=[pltpu.VMEM((HC, DKC, DVC), f32)],
        compiler_params=_cparams(("parallel", "arbitrary")),
        name="retention",
    )(z, z, z, z, cos, sin, s0, gn_g.reshape(1, C_V), gn_b.reshape(1, C_V))


def _lane_prefix_sum(v):
    lane = lax.broadcasted_iota(i32, v.shape, 1)
    s = 1
    while s < LANES:
        v = v + jnp.where(lane >= s, pltpu.roll(v, s, axis=1), 0)
        s *= 2
    return v


def _slots_kernel(e_ref, dest_ref, cnt_ref, run_ref, start_ref, *, nt):
    ph = pl.program_id(0)
    i = pl.program_id(1)
    last = nt - 1
    tt = e_ref.shape[0]
    e = e_ref[...]
    lane = lax.broadcasted_iota(i32, (tt, LANES), 1)
    member = jnp.zeros((tt, LANES), f32)
    for k in range(TOP_K):
        member = member + (lane == e[:, k:k + 1]).astype(f32)
    tile_cnt = jnp.broadcast_to(jnp.sum(member, axis=0, keepdims=True), run_ref.shape)

    @pl.when((ph == 0) & (i == 0))
    def _():
        run_ref[...] = jnp.zeros(run_ref.shape, f32)

    @pl.when(ph == 0)
    def _():
        run_ref[...] += tile_cnt

        @pl.when(i == last)
        def _():
            cnt = run_ref[...].astype(i32)
            cnt_ref[...] = cnt
            padded = _round_up_bm(cnt)
            start_ref[...] = (_lane_prefix_sum(padded) - padded).astype(f32)
            run_ref[...] = jnp.zeros(run_ref.shape, f32)

    @pl.when(ph == 1)
    def _():
        r = lax.broadcasted_iota(i32, (tt, tt), 0)
        c = lax.broadcasted_iota(i32, (tt, tt), 1)
        lower = (c < r).astype(bf16)
        before = jnp.dot(lower, member.astype(bf16), preferred_element_type=f32)
        base = before + run_ref[0:1, :] + start_ref[0:1, :]
        out = jnp.zeros((tt, LANES), i32)
        for k in range(TOP_K):
            dk = jnp.sum(jnp.where(lane == e[:, k:k + 1], base, 0.0), axis=-1, keepdims=True)
            out = jnp.where(lane == k, dk.astype(i32), out)
        dest_ref[...] = out
        run_ref[...] += tile_cnt


def _slots(top_e, tt):
    n = top_e.shape[0]
    return pl.pallas_call(
        functools.partial(_slots_kernel, nt=n // tt),
        grid=(2, n // tt),
        in_specs=[pl.BlockSpec((tt, LANES), lambda p, i: (i, 0))],
        out_specs=[pl.BlockSpec((tt, LANES), lambda p, i: (i * p, 0)),
                   pl.BlockSpec((8, LANES), lambda p, i: (0, 0))],
        out_shape=[jax.ShapeDtypeStruct((n, LANES), i32),
                   jax.ShapeDtypeStruct((8, LANES), i32)],
        scratch_shapes=[pltpu.VMEM((8, LANES), f32), pltpu.VMEM((8, LANES), f32)],
        compiler_params=_cparams(("arbitrary", "arbitrary")),
        name="moe_slots",
    )(top_e)


def _dispatch_kernel(cnt_ref, start_ref, dest_ref, x_ref, xg_hbm, zrow, sem):
    i = pl.program_id(0)
    tm = x_ref.shape[0]

    @pl.when(i == 0)
    def _():
        zrow[...] = jnp.zeros(zrow.shape, f32)

        def per_expert(e, total):
            c = cnt_ref[e]
            npad = _round_up_bm(c) - c
            first = start_ref[e] + c

            def put(r, carry):
                pltpu.make_async_copy(zrow.at[pl.ds(0, 1)], xg_hbm.at[pl.ds(first + r, 1)], sem.at[1]).start()
                return carry
            lax.fori_loop(0, npad, put, 0)
            return total + npad
        total = lax.fori_loop(0, N_EXPERTS, per_expert, 0)

        def drain(r, carry):
            pltpu.make_async_copy(zrow.at[pl.ds(0, 1)], xg_hbm.at[pl.ds(0, 1)], sem.at[1]).wait()
            return carry
        lax.fori_loop(0, total, drain, 0)

        nblk = xg_hbm.shape[0] // MOE_BM
        end = start_ref[N_EXPERTS - 1] + _round_up_bm(cnt_ref[N_EXPERTS - 1])
        nact = lax.shift_right_logical(end, MOE_BM_SHIFT)

        def put_blk(bi, carry):
            pltpu.make_async_copy(zrow, xg_hbm.at[pl.ds(bi * MOE_BM, MOE_BM)], sem.at[1]).start()
            return carry
        lax.fori_loop(nact, nblk, put_blk, 0)

        def drain_blk(bi, carry):
            pltpu.make_async_copy(zrow, xg_hbm.at[pl.ds(0, MOE_BM)], sem.at[1]).wait()
            return carry
        lax.fori_loop(nact, nblk, drain_blk, 0)

    def body(r, carry):
        for k in range(TOP_K):
            dst = dest_ref[0, 0, r * TOP_K + k]
            pltpu.make_async_copy(x_ref.at[pl.ds(r, 1)], xg_hbm.at[pl.ds(dst, 1)],
                                  sem.at[0]).start(priority=k % 2)
        return carry
    lax.fori_loop(0, tm, body, 0, unroll=4)
    pltpu.make_async_copy(xg_hbm.at[pl.ds(0, tm * TOP_K)], xg_hbm.at[pl.ds(0, tm * TOP_K)], sem.at[0]).wait()


def _dispatch(x, dest3, counts, starts, nblk, tm):
    n, d = x.shape
    grid_spec = pltpu.PrefetchScalarGridSpec(
        num_scalar_prefetch=2,
        grid=(n // tm,),
        in_specs=[pl.BlockSpec((1, 1, tm * TOP_K), lambda i, c, s: (i, 0, 0), memory_space=pltpu.SMEM),
                  pl.BlockSpec((tm, d), lambda i, c, s: (i, 0))],
        out_specs=pl.BlockSpec(memory_space=pl.ANY),
        scratch_shapes=[pltpu.VMEM((MOE_BM, d), f32), pltpu.SemaphoreType.DMA((2,))],
    )
    return pl.pallas_call(
        _dispatch_kernel,
        grid_spec=grid_spec,
        out_shape=jax.ShapeDtypeStruct((nblk * MOE_BM, d), f32),
        compiler_params=_cparams(("arbitrary",)),
        name="moe_dispatch",
    )(counts, starts, dest3, x)


def _moe_kernel(blk_e_ref, nact_ref, x_ref, wup_ref, bup_ref, wdn_ref, bdn_ref, y_ref, wup_bf, wdn_bf):
    b = pl.program_id(0)
    nact = nact_ref[0]

    @pl.when(b < nact)
    def _():
        @pl.when((b == 0) | (blk_e_ref[b] != blk_e_ref[jnp.maximum(b - 1, 0)]))
        def _():
            wup_bf[...] = wup_ref[0, 0].astype(bf16)
            wdn_bf[...] = wdn_ref[0, 0].astype(bf16)

        hdn = jnp.dot(x_ref[...].astype(bf16), wup_bf[...], preferred_element_type=f32) + bup_ref[0, 0]
        glu = jnp.minimum(hdn[:, :D_FF], SWIGLU_LIMIT)
        lin = jnp.clip(hdn[:, D_FF:], -SWIGLU_LIMIT, SWIGLU_LIMIT)
        act = glu * jax.nn.sigmoid(SWIGLU_ALPHA * glu) * (lin + 1.0)
        y_ref[...] = jnp.dot(act.astype(bf16), wdn_bf[...], preferred_element_type=f32) + bdn_ref[0, 0]

    @pl.when(b >= nact)
    def _():
        y_ref[...] = jnp.zeros(y_ref.shape, f32)


def _moe_experts(xg, blk_e, nact, l, w_up, b_up, w_down, b_down):
    r, d = xg.shape
    bm = MOE_BM
    nblk = r // bm
    depth = w_up.shape[0]
    wsel = lambda b, be, na: (l, be[b], 0, 0)
    grid_spec = pltpu.PrefetchScalarGridSpec(
        num_scalar_prefetch=2,
        grid=(nblk,),
        in_specs=[
            pl.BlockSpec((bm, d), lambda b, be, na: (jnp.minimum(b, na[0] - 1), 0)),
            pl.BlockSpec((1, 1, d, 2 * D_FF), wsel),
            pl.BlockSpec((1, 1, 1, 2 * D_FF), wsel),
            pl.BlockSpec((1, 1, D_FF, d), wsel),
            pl.BlockSpec((1, 1, 1, d), wsel),
        ],
        out_specs=pl.BlockSpec((bm, d), lambda b, be, na: (b, 0)),
        scratch_shapes=[pltpu.VMEM((d, 2 * D_FF), bf16), pltpu.VMEM((D_FF, d), bf16)],
    )
    return pl.pallas_call(
        _moe_kernel,
        grid_spec=grid_spec,
        out_shape=jax.ShapeDtypeStruct((r, d), f32),
        compiler_params=_cparams(("arbitrary",)),
        name="moe_experts",
    )(blk_e, nact, xg, w_up, b_up.reshape(depth, N_EXPERTS, 1, 2 * D_FF),
      w_down, b_down.reshape(depth, N_EXPERTS, 1, d))


def _combine_ln_kernel(pos_cur_ref, pos_nxt_ref, y_hbm, gate_ref, x_ref, g_ref, b_ref, o_ref, gbuf, sem,
                       *, nt):
    i = pl.program_id(0)
    tm = x_ref.shape[0]

    def gather(pos_ref, slot):
        def body(r, carry):
            for k in range(TOP_K):
                p = pos_ref[0, 0, r * TOP_K + k]
                pltpu.make_async_copy(y_hbm.at[pl.ds(p, 1)], gbuf.at[slot, k, pl.ds(r, 1)],
                                      sem.at[slot]).start(priority=k % 2)
            return carry
        lax.fori_loop(0, tm, body, 0, unroll=4)

    @pl.when(i == 0)
    def _():
        gather(pos_cur_ref, 0)

    @pl.when(i + 1 < nt)
    def _():
        gather(pos_nxt_ref, (i + 1) % 2)

    slot = i % 2
    pltpu.make_async_copy(gbuf.at[slot], gbuf.at[slot], sem.at[slot]).wait()
    gates = gate_ref[...]
    acc = ALPHA * x_ref[...]
    for k in range(TOP_K):
        acc = acc + gates[:, k:k + 1] * gbuf[slot, k]
    o_ref[...] = _layernorm(acc, g_ref[...], b_ref[...])


def _combine_ln(y, pos3, gates, x, g, b, tm, row0, rows):
    n, d = x.shape
    blk0 = row0 // tm
    nt = rows // tm
    grid_spec = pltpu.PrefetchScalarGridSpec(
        num_scalar_prefetch=0,
        grid=(nt,),
        in_specs=[
            pl.BlockSpec((1, 1, tm * TOP_K), lambda i: (blk0 + i, 0, 0), memory_space=pltpu.SMEM),
            pl.BlockSpec((1, 1, tm * TOP_K), lambda i: (blk0 + jnp.minimum(i + 1, nt - 1), 0, 0),
                         memory_space=pltpu.SMEM),
            pl.BlockSpec(memory_space=pl.ANY),
            pl.BlockSpec((tm, LANES), lambda i: (blk0 + i, 0)),
            pl.BlockSpec((tm, d), lambda i: (blk0 + i, 0)),
            pl.BlockSpec((1, d), lambda i: (0, 0)),
            pl.BlockSpec((1, d), lambda i: (0, 0)),
        ],
        out_specs=pl.BlockSpec((tm, d), lambda i: (i, 0)),
        scratch_shapes=[pltpu.VMEM((2, TOP_K, tm, d), f32), pltpu.SemaphoreType.DMA((2,))],
    )
    return pl.pallas_call(
        functools.partial(_combine_ln_kernel, nt=nt),
        grid_spec=grid_spec,
        out_shape=jax.ShapeDtypeStruct((rows, d), f32),
        compiler_params=_cparams(("arbitrary",)),
        name="combine_ln",
    )(pos3, pos3, y, gates, x, g.reshape(1, d), b.reshape(1, d))


def _moe_ln(x, top_e, gates, l, w_up, b_up, w_down, b_down, g, b, splits):
    n = x.shape[0]
    bm = MOE_BM
    nblk = -(-(n * TOP_K) // bm) + N_EXPERTS
    tm = _pick(n, (512, 256, 128, 64))
    dest, counts = _slots(top_e, tm)
    counts = counts[0, :N_EXPERTS]
    padded = (counts + bm - 1) // bm * bm
    ends = jnp.cumsum(padded)
    starts = (ends - padded).astype(i32)
    blk_start = jnp.arange(nblk, dtype=i32) * bm
    blk_e = jnp.minimum(jnp.sum((ends[None, :] <= blk_start[:, None]).astype(i32), axis=1), N_EXPERTS - 1)
    nact = (ends[-1] // bm).astype(i32).reshape(1)
    tmc = _pick(math.gcd(*[v for s in splits for v in s if v], n), (256, 128, 64, 32))
    dest3 = dest[:, :TOP_K].reshape(n // tmc, 1, tmc * TOP_K)
    xg = _dispatch(x, dest3, counts, starts, nblk, tmc)
    y = _moe_experts(xg, blk_e, nact, l, w_up, b_up, w_down, b_down)
    outs = [_combine_ln(y, dest3, gates, x, g, b, tmc, r0, rows) for (r0, rows) in splits]
    return outs[0] if len(outs) == 1 else outs


def kernel(x_prompt, x_sample, cache_a_k, cache_a_v, state_b_conv, state_b_h, state_c_s, ab_w_in, ab_rel_bias, ab_conv_w, ab_conv_b, ab_w_rg, ab_b_rg, ab_w_ig, ab_b_ig, ab_lambda, ab_w_out, c_w_in, c_gn_g, c_gn_b, c_w_out, ln1_g, ln1_b, ln2_g, ln2_b, moe_w_router, moe_b_router, moe_w_up, moe_b_up, moe_w_down, moe_b_down):
    bp, tp, d = x_prompt.shape
    bs, ts, _ = x_sample.shape
    assert bp == 1 and tp % RET_TB == 0 and tp % ts == 0 and ts <= CHUNK
    n_p, n_s = bp * tp, bs * ts
    n = n_p + n_s
    x = jnp.concatenate([x_prompt.reshape(n_p, d), x_sample.reshape(n_s, d)], axis=0)
    tm = _pick(n, (512, 256, 128, 64))
    moe = (moe_w_up, moe_b_up, moe_w_down, moe_b_down)

    j = 0
    z = _matmul(x, ab_w_in[j].astype(bf16), tm, 512)
    att = jnp.concatenate([
        _attn_prompt(z, ab_rel_bias[j], tp),
        _attn_sample(z, cache_a_k[j], cache_a_v[j], ab_rel_bias[j], n_p, bs, ts)], axis=0)
    lru_w = (ab_conv_w[j], ab_conv_b[j], _block_diag(ab_w_rg[j]).astype(bf16), ab_b_rg[j],
             _block_diag(ab_w_ig[j]).astype(bf16), ab_b_ig[j], jax.nn.softplus(-ab_lambda[j]))
    rec_p, bc_p, bh_p = _rglru(z, jnp.zeros((bp, CONV_W - 1, B_WIDTH), f32), jnp.zeros((bp, B_WIDTH), f32),
                               *lru_w, 0, bp, tp, _pick(tp, (LRU_TB, 128, 64)))
    rec_s, bc_s, bh_s = _rglru(z, state_b_conv[j], state_b_h[j], *lru_w, n_p, bs, ts, ts)
    rec = jnp.concatenate([rec_p, rec_s], axis=0)
    w_out = ab_w_out[j].astype(bf16)
    x, top_e, gates = _proj_ln([att, rec], [w_out[:A_WIDTH], w_out[A_WIDTH:]], x, ln1_g[0], ln1_b[0],
                               moe_w_router, moe_b_router, 0, tm)
    x = _moe_ln(x, top_e, gates, 0, *moe, ln2_g[0], ln2_b[0], splits=[(0, n)])

    keep = min(A_PAST_CHUNKS * CHUNK, tp)
    ak_p = z[n_p - keep:n_p, A_WIDTH:2 * A_WIDTH].reshape(1, bp, keep, HA, DHA)
    av_p = z[n_p - keep:n_p, 2 * A_WIDTH:3 * A_WIDTH].reshape(1, bp, keep, HA, DHA)
    ak_s = z[n_p:, A_WIDTH:2 * A_WIDTH].reshape(1, bs, ts, HA, DHA)
    av_s = z[n_p:, 2 * A_WIDTH:3 * A_WIDTH].reshape(1, bs, ts, HA, DHA)

    z = _matmul(x, c_w_in[j].astype(bf16), tm, 1024)
    y_p, cs_p = _retention(z, jnp.zeros((bp, HC, DKC, DVC), f32), 0, c_gn_g[j], c_gn_b[j],
                           0, bp, tp, RET_TB, CHUNK)
    y_s, cs_s = _retention(z, state_c_s[j], PAST_LEN, c_gn_g[j], c_gn_b[j], n_p, bs, ts, ts, ts)
    y = jnp.concatenate([y_p, y_s], axis=0)
    x, top_e, gates = _proj_ln([y], [c_w_out[j].astype(bf16)], x, ln1_g[1], ln1_b[1],
                               moe_w_router, moe_b_router, 1, tm)
    out_p, out_s = _moe_ln(x, top_e, gates, 1, *moe, ln2_g[1], ln2_b[1], splits=[(0, n_p), (n_p, n_s)])

    return (out_p.reshape(bp, tp, d), out_s.reshape(bs, ts, d),
            ak_p, av_p, bc_p[None], bh_p.reshape(1, bp, B_WIDTH), cs_p[None],
            ak_s, av_s, bc_s[None], bh_s.reshape(1, bs, B_WIDTH), cs_s[None])
```

```python
import functools
import math

import jax
import jax.numpy as jnp
from jax import lax
from jax.experimental import pallas as pl
from jax.experimental.pallas import tpu as pltpu

f32 = jnp.float32
bf16 = jnp.bfloat16
i32 = jnp.int32

DEPTH = 2
PAST_LEN = 1024
CHUNK = 64
A_PAST_CHUNKS = 8
REL_CLIP = 128
HA, DHA = 8, 64
A_WIDTH = HA * DHA
B_WIDTH = 512
HB = 8
CONV_W = 4
LRU_C = 8.0
HC, DKC, DVC = 4, 256, 512
C_QK, C_V = HC * DKC, HC * DVC
ROPE_BASE = 10000.0
GN_EPS = 1e-5
N_EXPERTS = 32
TOP_K = 4
D_FF = 1024
SWIGLU_LIMIT = 7.0
SWIGLU_ALPHA = 1.702
ALPHA = (2.0 * DEPTH) ** 0.25
LN_EPS = 1e-5
NEG = -1e30

LANES = 128
VMEM_LIMIT = 56 * 1024 * 1024
ATT_TQ = 256
MOE_BM = 256
MOE_BM_SHIFT = MOE_BM.bit_length() - 1
RET_TB = 512
LRU_TB = 256


def _pick(n, cands):
    for c in cands:
        if n % c == 0:
            return c
    raise ValueError(f"no tile for {n} in {cands}")


def _cparams(sem):
    return pltpu.CompilerParams(dimension_semantics=sem, vmem_limit_bytes=VMEM_LIMIT)


def _layernorm(v, g, b):
    mu = jnp.mean(v, axis=-1, keepdims=True)
    d = v - mu
    var = jnp.mean(d * d, axis=-1, keepdims=True)
    return d * lax.rsqrt(var + LN_EPS) * g + b


def _round_up_bm(v):
    return lax.shift_left(lax.shift_right_logical(v + (MOE_BM - 1), MOE_BM_SHIFT), MOE_BM_SHIFT)


def _rel_bias_matrix(table, nq, nk, d0):
    p = nq + nk - 1
    j = jnp.arange(p)
    u = table[:, jnp.clip(d0 + nq - 1 - j, -REL_CLIP, REL_CLIP) + REL_CLIP]
    u = jnp.roll(u, -(nq - 1), axis=1)
    flat = jnp.tile(u, (1, nq))[:, :nq * (p - 1)]
    return flat.reshape(table.shape[0], nq, p - 1)[:, :, :nk].astype(f32)


def _mm_kernel(x_ref, w_ref, o_ref):
    o_ref[...] = jnp.dot(x_ref[...].astype(bf16), w_ref[...], preferred_element_type=f32)


def _matmul(x, w, tm, tn):
    m, k = x.shape
    n = w.shape[1]
    return pl.pallas_call(
        _mm_kernel,
        grid=(m // tm, n // tn),
        in_specs=[pl.BlockSpec((tm, k), lambda i, j: (i, 0)),
                  pl.BlockSpec((k, tn), lambda i, j: (0, j))],
        out_specs=pl.BlockSpec((tm, tn), lambda i, j: (i, j)),
        out_shape=jax.ShapeDtypeStruct((m, n), f32),
        compiler_params=_cparams(("parallel", "parallel")),
        name="dense_proj",
    )(x, w)


def _route_top_k(x, w, b):
    logits = jnp.dot(x.astype(bf16), w.astype(bf16), preferred_element_type=f32) + b
    tm = logits.shape[0]
    lane = lax.broadcasted_iota(i32, (tm, N_EXPERTS), 1)
    out_lane = lax.broadcasted_iota(i32, (tm, LANES), 1)
    e_out = jnp.zeros((tm, LANES), i32)
    g_out = jnp.zeros((tm, LANES), f32)
    v0 = None
    den = jnp.zeros((tm, 1), f32)
    for k in range(TOP_K):
        v = jnp.max(logits, axis=-1, keepdims=True)
        idx = jnp.min(jnp.where(logits == v, lane, N_EXPERTS), axis=-1, keepdims=True)
        logits = jnp.where(lane == idx, -jnp.inf, logits)
        if k == 0:
            v0 = v
        p = jnp.exp(v - v0)
        den = den + p
        e_out = jnp.where(out_lane == k, idx, e_out)
        g_out = jnp.where(out_lane == k, p, g_out)
    return e_out, g_out / den


def _proj_ln_kernel(*refs, n_in):
    a_refs = refs[:n_in]
    w_refs = refs[n_in:2 * n_in]
    x_ref, g_ref, b_ref, wr_ref, br_ref, o_ref, e_ref, gate_ref = refs[2 * n_in:]
    acc = ALPHA * x_ref[...]
    for a_ref, w_ref in zip(a_refs, w_refs):
        acc = acc + jnp.dot(a_ref[...].astype(bf16), w_ref[...], preferred_element_type=f32)
    y = _layernorm(acc, g_ref[...], b_ref[...])
    o_ref[...] = y
    e_ref[...], gate_ref[...] = _route_top_k(y, wr_ref[0], br_ref[0])


def _proj_ln(acts, ws, x, g, b, w_router, b_router, l, tm):
    m, d = x.shape
    n_in = len(acts)
    depth = w_router.shape[0]
    in_specs = ([pl.BlockSpec((tm, a.shape[1]), lambda i: (i, 0)) for a in acts]
                + [pl.BlockSpec(w.shape, lambda i: (0, 0)) for w in ws]
                + [pl.BlockSpec((tm, d), lambda i: (i, 0)),
                   pl.BlockSpec((1, d), lambda i: (0, 0)),
                   pl.BlockSpec((1, d), lambda i: (0, 0)),
                   pl.BlockSpec((1, d, N_EXPERTS), lambda i: (l, 0, 0)),
                   pl.BlockSpec((1, 1, N_EXPERTS), lambda i: (l, 0, 0))])
    return pl.pallas_call(
        functools.partial(_proj_ln_kernel, n_in=n_in),
        grid=(m // tm,),
        in_specs=in_specs,
        out_specs=[pl.BlockSpec((tm, d), lambda i: (i, 0)),
                   pl.BlockSpec((tm, LANES), lambda i: (i, 0)),
                   pl.BlockSpec((tm, LANES), lambda i: (i, 0))],
        out_shape=[jax.ShapeDtypeStruct((m, d), f32),
                   jax.ShapeDtypeStruct((m, LANES), i32),
                   jax.ShapeDtypeStruct((m, LANES), f32)],
        compiler_params=_cparams(("parallel",)),
        name="proj_ln",
    )(*acts, *ws, x, g.reshape(1, d), b.reshape(1, d), w_router, b_router.reshape(depth, 1, N_EXPERTS))


def _attn_prompt_kernel(q_ref, k0_ref, k1_ref, k2_ref, v0_ref, v1_ref, v2_ref, bias_ref, o_ref):
    i = pl.program_id(0)
    tq = q_ref.shape[0]
    q = q_ref[...]
    k = jnp.concatenate([k0_ref[...], k1_ref[...], k2_ref[...]], axis=0)
    v = jnp.concatenate([v0_ref[...], v1_ref[...], v2_ref[...]], axis=0)
    kcol = lax.broadcasted_iota(i32, (1, 3 * tq), 1)
    tile_ok = (kcol // tq + i) >= 2
    outs = []
    for h in range(HA):
        sl = slice(h * DHA, (h + 1) * DHA)
        s = lax.dot_general(q[:, sl].astype(bf16), k[:, sl].astype(bf16),
                            (((1,), (1,)), ((), ())), preferred_element_type=f32)
        s = s * (DHA ** -0.5) + bias_ref[h]
        s = jnp.where(tile_ok, s, NEG)
        s = s - jnp.max(s, axis=-1, keepdims=True)
        p = jnp.exp(s)
        p = p / jnp.sum(p, axis=-1, keepdims=True)
        outs.append(jnp.dot(p.astype(bf16), v[:, sl].astype(bf16), preferred_element_type=f32))
    o_ref[...] = jnp.concatenate(outs, axis=-1)


def _attn_prompt(z, rel_bias, tp):
    tq = ATT_TQ
    nq = tp // tq
    qq = jnp.arange(tq)[:, None]
    kk = jnp.arange(3 * tq)[None, :]
    band = (kk // CHUNK >= qq // CHUNK) & (kk // CHUNK <= qq // CHUNK + A_PAST_CHUNKS)
    bias = jnp.where(band[None], _rel_bias_matrix(rel_bias, tq, 3 * tq, 2 * tq), NEG)

    def kv_spec(back, col):
        return pl.BlockSpec((tq, A_WIDTH), lambda i: (jnp.maximum(i - back, 0), col))

    return pl.pallas_call(
        _attn_prompt_kernel,
        grid=(nq,),
        in_specs=[pl.BlockSpec((tq, A_WIDTH), lambda i: (i, 0)),
                  kv_spec(2, 1), kv_spec(1, 1), kv_spec(0, 1),
                  kv_spec(2, 2), kv_spec(1, 2), kv_spec(0, 2),
                  pl.BlockSpec((HA, tq, 3 * tq), lambda i: (0, 0, 0))],
        out_specs=pl.BlockSpec((tq, A_WIDTH), lambda i: (i, 0)),
        out_shape=jax.ShapeDtypeStruct((tp, A_WIDTH), f32),
        compiler_params=_cparams(("parallel",)),
        name="attn_prompt",
    )(z, z, z, z, z, z, z, bias)


def _attn_sample_kernel(q_ref, kn_ref, vn_ref, kc_ref, vc_ref, bias_ref, o_ref):
    p_len = kc_ref.shape[1]
    q, kn, vn = q_ref[...], kn_ref[...], vn_ref[...]
    kc, vc = kc_ref[0], vc_ref[0]
    outs = []
    for h in range(HA):
        sl = slice(h * DHA, (h + 1) * DHA)
        qh = q[:, sl].astype(bf16)
        dims = (((1,), (1,)), ((), ()))
        sc = lax.dot_general(qh, kc[:, sl].astype(bf16), dims, preferred_element_type=f32)
        sn = lax.dot_general(qh, kn[:, sl].astype(bf16), dims, preferred_element_type=f32)
        b = bias_ref[h]
        sc = sc * (DHA ** -0.5) + b[:, :p_len]
        sn = sn * (DHA ** -0.5) + b[:, p_len:]
        m = jnp.maximum(jnp.max(sc, axis=-1, keepdims=True), jnp.max(sn, axis=-1, keepdims=True))
        pc, pn = jnp.exp(sc - m), jnp.exp(sn - m)
        den = jnp.sum(pc, axis=-1, keepdims=True) + jnp.sum(pn, axis=-1, keepdims=True)
        pc, pn = pc / den, pn / den
        outs.append(jnp.dot(pc.astype(bf16), vc[:, sl].astype(bf16), preferred_element_type=f32)
                    + jnp.dot(pn.astype(bf16), vn[:, sl].astype(bf16), preferred_element_type=f32))
    o_ref[...] = jnp.concatenate(outs, axis=-1)


def _attn_sample(z, k_cache, v_cache, rel_bias, row0, bs, ts):
    p_len = k_cache.shape[1]
    kc = k_cache.reshape(bs, p_len, A_WIDTH)
    vc = v_cache.reshape(bs, p_len, A_WIDTH)
    bias = _rel_bias_matrix(rel_bias, ts, p_len + ts, p_len)
    blk0 = row0 // ts

    def z_spec(col):
        return pl.BlockSpec((ts, A_WIDTH), lambda b: (blk0 + b, col))

    return pl.pallas_call(
        _attn_sample_kernel,
        grid=(bs,),
        in_specs=[z_spec(0), z_spec(1), z_spec(2),
                  pl.BlockSpec((1, p_len, A_WIDTH), lambda b: (b, 0, 0)),
                  pl.BlockSpec((1, p_len, A_WIDTH), lambda b: (b, 0, 0)),
                  pl.BlockSpec((HA, ts, p_len + ts), lambda b: (0, 0, 0))],
        out_specs=pl.BlockSpec((ts, A_WIDTH), lambda b: (b, 0)),
        out_shape=jax.ShapeDtypeStruct((bs * ts, A_WIDTH), f32),
        compiler_params=_cparams(("parallel",)),
        name="attn_sample",
    )(z, z, z, kc, vc, bias)


def _gelu_tanh(x):
    return 0.5 * x * (1.0 + jnp.tanh(math.sqrt(2.0 / math.pi) * (x + 0.044715 * (x * x * x))))


def _rglru_kernel(xb_ref, gb_ref, conv0_ref, h0_ref, cw_ref, cb_ref, wrg_ref, brg_ref,
                  wig_ref, big_ref, sp_ref, rec_ref, convn_ref, hl_ref, xp_ref, hc_ref):
    t = pl.program_id(1)
    tb = xb_ref.shape[0]
    pad = 8

    @pl.when(t == 0)
    def _():
        xp_ref[0:pad, :] = jnp.zeros((pad, B_WIDTH), f32)
        xp_ref[pad - (CONV_W - 1):pad, :] = conv0_ref[0]
        hc_ref[...] = h0_ref[0]

    xb = xb_ref[...]
    xp_ref[pad:pad + tb, :] = xb
    cw = cw_ref[...]
    u = cb_ref[...] + cw[CONV_W - 1:CONV_W, :] * xb
    for j in range(CONV_W - 1):
        sh = CONV_W - 1 - j
        u = u + cw[j:j + 1, :] * xp_ref[pad - sh:pad - sh + tb, :]
    convn_ref[0] = xp_ref[pad + tb - (CONV_W - 1):pad + tb, :]
    xp_ref[0:pad, :] = xp_ref[tb:tb + pad, :]

    ub = u.astype(bf16)
    r = jax.nn.sigmoid(jnp.dot(ub, wrg_ref[...], preferred_element_type=f32) + brg_ref[...])
    ig = jax.nn.sigmoid(jnp.dot(ub, wig_ref[...], preferred_element_type=f32) + big_ref[...])
    log_a = -LRU_C * r * sp_ref[...]
    a = jnp.exp(log_a)
    bt = jnp.sqrt(-jnp.tanh(log_a) * (a * a + 1.0)) * (ig * u)

    row = lax.broadcasted_iota(i32, (tb, 1), 0)
    s = 1
    while s < tb:
        keep = row >= s
        a_sh = pltpu.roll(a, s, axis=0)
        b_sh = pltpu.roll(bt, s, axis=0)
        bt = jnp.where(keep, a * b_sh + bt, bt)
        a = jnp.where(keep, a * a_sh, a)
        s *= 2
    h = a * hc_ref[...] + bt
    hc_ref[...] = h[tb - 1:tb, :]
    hl_ref[0] = h[tb - 1:tb, :]
    rec_ref[...] = h * _gelu_tanh(gb_ref[...])


def _rglru(z, conv0, h0, cw, cb, wrg, brg, wig, big, sp, row0, bsz, t_len, tb):
    nt = t_len // tb
    blk0 = row0 // tb

    def z_spec(col):
        return pl.BlockSpec((tb, B_WIDTH), lambda b, t: (blk0 + b * nt + t, col))

    def const(shape):
        return pl.BlockSpec(shape, lambda b, t: (0,) * len(shape))

    row = lambda v: v.reshape(1, B_WIDTH)
    return pl.pallas_call(
        _rglru_kernel,
        grid=(bsz, nt),
        in_specs=[z_spec(3), z_spec(4),
                  pl.BlockSpec((1, CONV_W - 1, B_WIDTH), lambda b, t: (b, 0, 0)),
                  pl.BlockSpec((1, 1, B_WIDTH), lambda b, t: (b, 0, 0)),
                  const((CONV_W, B_WIDTH)), const((1, B_WIDTH)),
                  const((B_WIDTH, B_WIDTH)), const((1, B_WIDTH)),
                  const((B_WIDTH, B_WIDTH)), const((1, B_WIDTH)), const((1, B_WIDTH))],
        out_specs=[pl.BlockSpec((tb, B_WIDTH), lambda b, t: (b * nt + t, 0)),
                   pl.BlockSpec((1, CONV_W - 1, B_WIDTH), lambda b, t: (b, 0, 0)),
                   pl.BlockSpec((1, 1, B_WIDTH), lambda b, t: (b, 0, 0))],
        out_shape=[jax.ShapeDtypeStruct((bsz * t_len, B_WIDTH), f32),
                   jax.ShapeDtypeStruct((bsz, CONV_W - 1, B_WIDTH), f32),
                   jax.ShapeDtypeStruct((bsz, 1, B_WIDTH), f32)],
        scratch_shapes=[pltpu.VMEM((tb + 8, B_WIDTH), f32), pltpu.VMEM((1, B_WIDTH), f32)],
        compiler_params=_cparams(("parallel", "arbitrary")),
        name="rglru",
    )(z, z, conv0, h0.reshape(bsz, 1, B_WIDTH), cw, row(cb), wrg, row(brg), wig, row(big), row(sp))


def _block_diag(w):
    hb, d, _ = w.shape
    eye = jnp.eye(hb, dtype=w.dtype)
    return (eye[:, None, :, None] * w[:, :, None, :]).reshape(hb * d, hb * d)


def _retention_kernel(q_ref, k_ref, v_ref, g_ref, cos_ref, sin_ref, s0_ref, gng_ref, gnb_ref,
                      y_ref, sout_ref, s_ref, *, chunk, nt):
    t = pl.program_id(1)
    tb = q_ref.shape[0]
    half = DKC // 2

    @pl.when(t == 0)
    def _():
        s_ref[...] = s0_ref[0]

    cos, sin = cos_ref[...], sin_ref[...]
    n = lax.broadcasted_iota(i32, (tb, 1), 0)
    m = lax.broadcasted_iota(i32, (1, tb), 1)
    dist = n - m
    same = (n // chunk) == (m // chunk)
    expo = jnp.where(same, jnp.abs(dist), dist).astype(f32)
    visible = same | (dist > 0)
    nf = n.astype(f32)
    for h in range(HC):
        lg = math.log(1.0 - 2.0 ** (-5.0 - h))
        base = h * DKC

        def rot(ref, scale):
            x1 = ref[:, base:base + half]
            x2 = ref[:, base + half:base + DKC]
            return jnp.concatenate([x1 * cos - x2 * sin, x2 * cos + x1 * sin], axis=-1) * scale

        qr = rot(q_ref, 1.0)
        kr = rot(k_ref, DKC ** -0.5)
        vb = v_ref[:, h * DVC:(h + 1) * DVC].astype(bf16)
        qb = qr.astype(bf16)
        dmat = jnp.where(visible, jnp.exp(lg * expo), 0.0)
        s = lax.dot_general(qb, kr.astype(bf16), (((1,), (1,)), ((), ())),
                            preferred_element_type=f32) * dmat
        s_old = s_ref[h]
        o = jnp.dot(s.astype(bf16), vb, preferred_element_type=f32)
        o = o + jnp.dot(qb, s_old.astype(bf16), preferred_element_type=f32) * jnp.exp(lg * (nf + 1.0))
        kd = (kr * jnp.exp(lg * (tb - 1.0 - nf))).astype(bf16)
        s_ref[h] = math.exp(lg * tb) * s_old + lax.dot_general(
            kd, vb, (((0,), (0,)), ((), ())), preferred_element_type=f32)

        mu = jnp.mean(o, axis=-1, keepdims=True)
        dlt = o - mu
        var = jnp.mean(dlt * dlt, axis=-1, keepdims=True)
        vs = slice(h * DVC, (h + 1) * DVC)
        yn = dlt * lax.rsqrt(var + GN_EPS) * gng_ref[:, vs] + gnb_ref[:, vs]
        y_ref[:, vs] = (jax.nn.silu(g_ref[:, vs]) * yn).astype(y_ref.dtype)

    @pl.when(t == nt - 1)
    def _():
        sout_ref[0] = s_ref[...]


def _retention(z, s0, pos0, gn_g, gn_b, row0, bsz, t_len, tb, chunk):
    nt = t_len // tb
    blk0 = row0 // tb
    half = DKC // 2
    inv = 1.0 / (ROPE_BASE ** (jnp.arange(half, dtype=f32) / half))
    ang = (pos0 + jnp.arange(t_len)).astype(f32)[:, None] * inv[None, :]
    cos, sin = jnp.cos(ang), jnp.sin(ang)

    def z_spec(width, col):
        return pl.BlockSpec((tb, width), lambda b, t: (blk0 + b * nt + t, col))

    return pl.pallas_call(
        functools.partial(_retention_kernel, chunk=chunk, nt=nt),
        grid=(bsz, nt),
        in_specs=[z_spec(C_QK, 0), z_spec(C_QK, 1), z_spec(C_V, 1), z_spec(C_V, 2),
                  pl.BlockSpec((tb, half), lambda b, t: (t, 0)),
                  pl.BlockSpec((tb, half), lambda b, t: (t, 0)),
                  pl.BlockSpec((1, HC, DKC, DVC), lambda b, t: (b, 0, 0, 0)),
                  pl.BlockSpec((1, C_V), lambda b, t: (0, 0)),
                  pl.BlockSpec((1, C_V), lambda b, t: (0, 0))],
        out_specs=[pl.BlockSpec((tb, C_V), lambda b, t: (b * nt + t, 0)),
                   pl.BlockSpec((1, HC, DKC, DVC), lambda b, t: (b, 0, 0, 0))],
        out_shape=[jax.ShapeDtypeStruct((bsz * t_len, C_V), bf16),
                   jax.ShapeDtypeStruct((bsz, HC, DKC, DVC), f32)],
        scratch_shapes=[pltpu.VMEM((HC, DKC, DVC), f32)],
        compiler_params=_cparams(("parallel", "arbitrary")),
        name="retention",
    )(z, z, z, z, cos, sin, s0, gn_g.reshape(1, C_V), gn_b.reshape(1, C_V))


def _lane_prefix_sum(v):
    lane = lax.broadcasted_iota(i32, v.shape, 1)
    s = 1
    while s < LANES:
        v = v + jnp.where(lane >= s, pltpu.roll(v, s, axis=1), 0)
        s *= 2
    return v


def _slots_kernel(e_ref, dest_ref, cnt_ref, run_ref, start_ref, *, nt):
    ph = pl.program_id(0)
    i = pl.program_id(1)
    last = nt - 1
    tt = e_ref.shape[0]
    e = e_ref[...]
    lane = lax.broadcasted_iota(i32, (tt, LANES), 1)
    member = jnp.zeros((tt, LANES), f32)
    for k in range(TOP_K):
        member = member + (lane == e[:, k:k + 1]).astype(f32)
    tile_cnt = jnp.broadcast_to(jnp.sum(member, axis=0, keepdims=True), run_ref.shape)

    @pl.when((ph == 0) & (i == 0))
    def _():
        run_ref[...] = jnp.zeros(run_ref.shape, f32)

    @pl.when(ph == 0)
    def _():
        run_ref[...] += tile_cnt

        @pl.when(i == last)
        def _():
            cnt = run_ref[...].astype(i32)
            cnt_ref[...] = cnt
            padded = _round_up_bm(cnt)
            start_ref[...] = (_lane_prefix_sum(padded) - padded).astype(f32)
            run_ref[...] = jnp.zeros(run_ref.shape, f32)

    @pl.when(ph == 1)
    def _():
        r = lax.broadcasted_iota(i32, (tt, tt), 0)
        c = lax.broadcasted_iota(i32, (tt, tt), 1)
        lower = (c < r).astype(bf16)
        before = jnp.dot(lower, member.astype(bf16), preferred_element_type=f32)
        base = before + run_ref[0:1, :] + start_ref[0:1, :]
        out = jnp.zeros((tt, LANES), i32)
        for k in range(TOP_K):
            dk = jnp.sum(jnp.where(lane == e[:, k:k + 1], base, 0.0), axis=-1, keepdims=True)
            out = jnp.where(lane == k, dk.astype(i32), out)
        dest_ref[...] = out
        run_ref[...] += tile_cnt


def _slots(top_e, tt):
    n = top_e.shape[0]
    return pl.pallas_call(
        functools.partial(_slots_kernel, nt=n // tt),
        grid=(2, n // tt),
        in_specs=[pl.BlockSpec((tt, LANES), lambda p, i: (i, 0))],
        out_specs=[pl.BlockSpec((tt, LANES), lambda p, i: (i * p, 0)),
                   pl.BlockSpec((8, LANES), lambda p, i: (0, 0))],
        out_shape=[jax.ShapeDtypeStruct((n, LANES), i32),
                   jax.ShapeDtypeStruct((8, LANES), i32)],
        scratch_shapes=[pltpu.VMEM((8, LANES), f32), pltpu.VMEM((8, LANES), f32)],
        compiler_params=_cparams(("arbitrary", "arbitrary")),
        name="moe_slots",
    )(top_e)


def _dispatch_kernel(cnt_ref, start_ref, dest_ref, xp_ref, xs_ref, xg_hbm, zrow, sem, *, np_tiles):
    i = pl.program_id(0)
    tm = xp_ref.shape[0]

    @pl.when(i == 0)
    def _():
        zrow[...] = jnp.zeros(zrow.shape, f32)

        def per_expert(e, total):
            c = cnt_ref[e]
            npad = _round_up_bm(c) - c
            first = start_ref[e] + c

            def put(r, carry):
                pltpu.make_async_copy(zrow.at[pl.ds(0, 1)], xg_hbm.at[pl.ds(first + r, 1)], sem.at[1]).start()
                return carry
            lax.fori_loop(0, npad, put, 0)
            return total + npad
        total = lax.fori_loop(0, N_EXPERTS, per_expert, 0)

        def drain(r, carry):
            pltpu.make_async_copy(zrow.at[pl.ds(0, 1)], xg_hbm.at[pl.ds(0, 1)], sem.at[1]).wait()
            return carry
        lax.fori_loop(0, total, drain, 0)

        nblk = xg_hbm.shape[0] // MOE_BM
        end = start_ref[N_EXPERTS - 1] + _round_up_bm(cnt_ref[N_EXPERTS - 1])
        nact = lax.shift_right_logical(end, MOE_BM_SHIFT)

        def put_blk(bi, carry):
            pltpu.make_async_copy(zrow, xg_hbm.at[pl.ds(bi * MOE_BM, MOE_BM)], sem.at[1]).start()
            return carry
        lax.fori_loop(nact, nblk, put_blk, 0)

        def drain_blk(bi, carry):
            pltpu.make_async_copy(zrow, xg_hbm.at[pl.ds(0, MOE_BM)], sem.at[1]).wait()
            return carry
        lax.fori_loop(nact, nblk, drain_blk, 0)

    def scatter_rows(x_ref):
        def body(r, carry):
            for k in range(TOP_K):
                dst = dest_ref[0, 0, r * TOP_K + k]
                pltpu.make_async_copy(x_ref.at[pl.ds(r, 1)], xg_hbm.at[pl.ds(dst, 1)],
                                      sem.at[0]).start(priority=k % 2)
            return carry
        lax.fori_loop(0, tm, body, 0, unroll=4)
        pltpu.make_async_copy(xg_hbm.at[pl.ds(0, tm * TOP_K)], xg_hbm.at[pl.ds(0, tm * TOP_K)],
                              sem.at[0]).wait()

    @pl.when(i < np_tiles)
    def _():
        scatter_rows(xp_ref)

    @pl.when(i >= np_tiles)
    def _():
        scatter_rows(xs_ref)


def _dispatch(x_p, x_s, dest3, counts, starts, nblk, tm):
    d = x_p.shape[1]
    np_tiles, ns_tiles = x_p.shape[0] // tm, x_s.shape[0] // tm
    grid_spec = pltpu.PrefetchScalarGridSpec(
        num_scalar_prefetch=2,
        grid=(np_tiles + ns_tiles,),
        in_specs=[pl.BlockSpec((1, 1, tm * TOP_K), lambda i, c, s: (i, 0, 0), memory_space=pltpu.SMEM),
                  pl.BlockSpec((tm, d), lambda i, c, s: (jnp.minimum(i, np_tiles - 1), 0)),
                  pl.BlockSpec((tm, d), lambda i, c, s: (jnp.maximum(i - np_tiles, 0), 0))],
        out_specs=pl.BlockSpec(memory_space=pl.ANY),
        scratch_shapes=[pltpu.VMEM((MOE_BM, d), f32), pltpu.SemaphoreType.DMA((2,))],
    )
    return pl.pallas_call(
        functools.partial(_dispatch_kernel, np_tiles=np_tiles),
        grid_spec=grid_spec,
        out_shape=jax.ShapeDtypeStruct((nblk * MOE_BM, d), f32),
        compiler_params=_cparams(("arbitrary",)),
        name="moe_dispatch",
    )(counts, starts, dest3, x_p, x_s)


def _moe_kernel(blk_e_ref, nact_ref, first_ref, seg_ref, nxt_ref, x_ref, wup_hbm, bup_ref, wdn_hbm, bdn_ref,
                y_ref, wup_f, wdn_f, wup_bf, wdn_bf, sem, *, layer):
    b = pl.program_id(0)
    nact = nact_ref[0]

    def weight_copies(e, slot):
        return (pltpu.make_async_copy(wup_hbm.at[layer, e], wup_f.at[slot], sem.at[slot, 0]),
                pltpu.make_async_copy(wdn_hbm.at[layer, e], wdn_f.at[slot], sem.at[slot, 1]))

    @pl.when(b < nact)
    def _():
        @pl.when(first_ref[b] == 1)
        def _():
            slot = seg_ref[b] % 2

            @pl.when(b == 0)
            def _():
                for c in weight_copies(blk_e_ref[0], 0):
                    c.start()

            for c in weight_copies(blk_e_ref[b], slot):
                c.wait()

            @pl.when(nxt_ref[b] >= 0)
            def _():
                for c in weight_copies(nxt_ref[b], 1 - slot):
                    c.start()

            wup_bf[...] = wup_f[slot].astype(bf16)
            wdn_bf[...] = wdn_f[slot].astype(bf16)

        hdn = jnp.dot(x_ref[...].astype(bf16), wup_bf[...], preferred_element_type=f32) + bup_ref[0, 0]
        glu = jnp.minimum(hdn[:, :D_FF], SWIGLU_LIMIT)
        lin = jnp.clip(hdn[:, D_FF:], -SWIGLU_LIMIT, SWIGLU_LIMIT)
        act = glu * jax.nn.sigmoid(SWIGLU_ALPHA * glu) * (lin + 1.0)
        y_ref[...] = jnp.dot(act.astype(bf16), wdn_bf[...], preferred_element_type=f32) + bdn_ref[0, 0]

    @pl.when(b >= nact)
    def _():
        y_ref[...] = jnp.zeros(y_ref.shape, f32)


def _moe_experts(xg, blk_e, nact, first, seg, nxt, l, w_up, b_up, w_down, b_down):
    r, d = xg.shape
    bm = MOE_BM
    nblk = r // bm
    depth = w_up.shape[0]
    bsel = lambda b, be, *_: (l, be[b], 0, 0)
    grid_spec = pltpu.PrefetchScalarGridSpec(
        num_scalar_prefetch=5,
        grid=(nblk,),
        in_specs=[
            pl.BlockSpec((bm, d), lambda b, be, na, *_: (jnp.minimum(b, na[0] - 1), 0)),
            pl.BlockSpec(memory_space=pl.ANY),
            pl.BlockSpec((1, 1, 1, 2 * D_FF), bsel),
            pl.BlockSpec(memory_space=pl.ANY),
            pl.BlockSpec((1, 1, 1, d), bsel),
        ],
        out_specs=pl.BlockSpec((bm, d), lambda b, *_: (b, 0)),
        scratch_shapes=[pltpu.VMEM((2, d, 2 * D_FF), f32), pltpu.VMEM((2, D_FF, d), f32),
                        pltpu.VMEM((d, 2 * D_FF), bf16), pltpu.VMEM((D_FF, d), bf16),
                        pltpu.SemaphoreType.DMA((2, 2))],
    )
    return pl.pallas_call(
        functools.partial(_moe_kernel, layer=l),
        grid_spec=grid_spec,
        out_shape=jax.ShapeDtypeStruct((r, d), f32),
        compiler_params=_cparams(("arbitrary",)),
        name="moe_experts",
    )(blk_e, nact, first, seg, nxt, xg, w_up, b_up.reshape(depth, N_EXPERTS, 1, 2 * D_FF),
      w_down, b_down.reshape(depth, N_EXPERTS, 1, d))


def _combine_ln_kernel(pos_cur_ref, pos_nxt_ref, y_hbm, gate_ref, x_ref, g_ref, b_ref, o_ref, gbuf, sem,
                       *, nt):
    i = pl.program_id(0)
    tm = x_ref.shape[0]

    def gather(pos_ref, slot):
        def body(r, carry):
            for k in range(TOP_K):
                p = pos_ref[0, 0, r * TOP_K + k]
                pltpu.make_async_copy(y_hbm.at[pl.ds(p, 1)], gbuf.at[slot, k, pl.ds(r, 1)],
                                      sem.at[slot]).start(priority=k % 2)
            return carry
        lax.fori_loop(0, tm, body, 0, unroll=4)

    @pl.when(i == 0)
    def _():
        gather(pos_cur_ref, 0)

    @pl.when(i + 1 < nt)
    def _():
        gather(pos_nxt_ref, (i + 1) % 2)

    slot = i % 2
    pltpu.make_async_copy(gbuf.at[slot], gbuf.at[slot], sem.at[slot]).wait()
    gates = gate_ref[...]
    acc = ALPHA * x_ref[...]
    for k in range(TOP_K):
        acc = acc + gates[:, k:k + 1] * gbuf[slot, k]
    o_ref[...] = _layernorm(acc, g_ref[...], b_ref[...])


def _combine_ln(y, pos3, blk0, gates, x, g, b, tm):
    rows, d = x.shape
    nt = rows // tm
    grid_spec = pltpu.PrefetchScalarGridSpec(
        num_scalar_prefetch=0,
        grid=(nt,),
        in_specs=[
            pl.BlockSpec((1, 1, tm * TOP_K), lambda i: (blk0 + i, 0, 0), memory_space=pltpu.SMEM),
            pl.BlockSpec((1, 1, tm * TOP_K), lambda i: (blk0 + jnp.minimum(i + 1, nt - 1), 0, 0),
                         memory_space=pltpu.SMEM),
            pl.BlockSpec(memory_space=pl.ANY),
            pl.BlockSpec((tm, LANES), lambda i: (i, 0)),
            pl.BlockSpec((tm, d), lambda i: (i, 0)),
            pl.BlockSpec((1, d), lambda i: (0, 0)),
            pl.BlockSpec((1, d), lambda i: (0, 0)),
        ],
        out_specs=pl.BlockSpec((tm, d), lambda i: (i, 0)),
        scratch_shapes=[pltpu.VMEM((2, TOP_K, tm, d), f32), pltpu.SemaphoreType.DMA((2,))],
    )
    return pl.pallas_call(
        functools.partial(_combine_ln_kernel, nt=nt),
        grid_spec=grid_spec,
        out_shape=jax.ShapeDtypeStruct((rows, d), f32),
        compiler_params=_cparams(("arbitrary",)),
        name="combine_ln",
    )(pos3, pos3, y, gates, x, g.reshape(1, d), b.reshape(1, d))


def _moe_ln(xs, top_es, gates, l, w_up, b_up, w_down, b_down, g, b):
    x_p, x_s = xs
    n_p, n_s = x_p.shape[0], x_s.shape[0]
    n = n_p + n_s
    bm = MOE_BM
    nblk = -(-(n * TOP_K) // bm) + N_EXPERTS
    top_e = jnp.concatenate(top_es, axis=0)
    dest, counts = _slots(top_e, _pick(n, (512, 256, 128, 64)))
    counts = counts[0, :N_EXPERTS]
    padded = (counts + bm - 1) // bm * bm
    ends = jnp.cumsum(padded)
    starts = (ends - padded).astype(i32)
    blk = jnp.arange(nblk, dtype=i32)
    blk_e = jnp.minimum(jnp.sum((ends[None, :] <= (blk * bm)[:, None]).astype(i32), axis=1), N_EXPERTS - 1)
    nact = (ends[-1] // bm).astype(i32).reshape(1)
    first = ((blk < nact[0]) & ((blk == 0) | (blk_e != jnp.roll(blk_e, 1)))).astype(i32)
    seg = jnp.cumsum(first) - 1
    eid = jnp.arange(N_EXPERTS, dtype=i32)
    later_nonempty = (padded[None, :] > 0) & (eid[None, :] > eid[:, None])
    nxt_of_e = jnp.min(jnp.where(later_nonempty, eid[None, :], N_EXPERTS), axis=1)
    nxt = jnp.where(nxt_of_e < N_EXPERTS, nxt_of_e, -1)[blk_e].astype(i32)
    tmc = _pick(math.gcd(n_p, n_s), (256, 128, 64, 32))
    dest3 = dest[:, :TOP_K].reshape(n // tmc, 1, tmc * TOP_K)
    xg = _dispatch(x_p, x_s, dest3, counts, starts, nblk, tmc)
    y = _moe_experts(xg, blk_e, nact, first, seg, nxt, l, w_up, b_up, w_down, b_down)
    return (_combine_ln(y, dest3, 0, gates[0], x_p, g, b, tmc),
            _combine_ln(y, dest3, n_p // tmc, gates[1], x_s, g, b, tmc))


def kernel(x_prompt, x_sample, cache_a_k, cache_a_v, state_b_conv, state_b_h, state_c_s, ab_w_in, ab_rel_bias, ab_conv_w, ab_conv_b, ab_w_rg, ab_b_rg, ab_w_ig, ab_b_ig, ab_lambda, ab_w_out, c_w_in, c_gn_g, c_gn_b, c_w_out, ln1_g, ln1_b, ln2_g, ln2_b, moe_w_router, moe_b_router, moe_w_up, moe_b_up, moe_w_down, moe_b_down):
    bp, tp, d = x_prompt.shape
    bs, ts, _ = x_sample.shape
    assert bp == 1 and tp % RET_TB == 0 and ts <= CHUNK
    n_p, n_s = bp * tp, bs * ts
    xs = (x_prompt.reshape(n_p, d), x_sample.reshape(n_s, d))
    tms = tuple(_pick(m, (512, 256, 128, 64)) for m in (n_p, n_s))
    moe = (moe_w_up, moe_b_up, moe_w_down, moe_b_down)
    router = (moe_w_router, moe_b_router)

    def per_group(fn, *groups):
        return tuple(zip(*[fn(*args) for args in zip(*groups)]))

    j = 0
    w_in = ab_w_in[j].astype(bf16)
    z_p, z_s = (_matmul(x, w_in, tm, 512) for x, tm in zip(xs, tms))
    att = (_attn_prompt(z_p, ab_rel_bias[j], tp),
           _attn_sample(z_s, cache_a_k[j], cache_a_v[j], ab_rel_bias[j], 0, bs, ts))
    lru_w = (ab_conv_w[j], ab_conv_b[j], _block_diag(ab_w_rg[j]).astype(bf16), ab_b_rg[j],
             _block_diag(ab_w_ig[j]).astype(bf16), ab_b_ig[j], jax.nn.softplus(-ab_lambda[j]))
    rec_p, bc_p, bh_p = _rglru(z_p, jnp.zeros((bp, CONV_W - 1, B_WIDTH), f32), jnp.zeros((bp, B_WIDTH), f32),
                               *lru_w, 0, bp, tp, _pick(tp, (LRU_TB, 128, 64)))
    rec_s, bc_s, bh_s = _rglru(z_s, state_b_conv[j], state_b_h[j], *lru_w, 0, bs, ts, ts)
    w_out = ab_w_out[j].astype(bf16)
    w_out = [w_out[:A_WIDTH], w_out[A_WIDTH:]]
    xs, top_es, gates = per_group(
        lambda a, r, x, tm: _proj_ln([a, r], w_out, x, ln1_g[0], ln1_b[0], *router, 0, tm),
        att, (rec_p, rec_s), xs, tms)
    xs = _moe_ln(xs, top_es, gates, 0, *moe, ln2_g[0], ln2_b[0])

    keep = min(A_PAST_CHUNKS * CHUNK, tp)
    ak_p = z_p[n_p - keep:, A_WIDTH:2 * A_WIDTH].reshape(1, bp, keep, HA, DHA)
    av_p = z_p[n_p - keep:, 2 * A_WIDTH:3 * A_WIDTH].reshape(1, bp, keep, HA, DHA)
    ak_s = z_s[:, A_WIDTH:2 * A_WIDTH].reshape(1, bs, ts, HA, DHA)
    av_s = z_s[:, 2 * A_WIDTH:3 * A_WIDTH].reshape(1, bs, ts, HA, DHA)

    w_in = c_w_in[j].astype(bf16)
    z_p, z_s = (_matmul(x, w_in, tm, 1024) for x, tm in zip(xs, tms))
    y_p, cs_p = _retention(z_p, jnp.zeros((bp, HC, DKC, DVC), f32), 0, c_gn_g[j], c_gn_b[j],
                           0, bp, tp, RET_TB, CHUNK)
    y_s, cs_s = _retention(z_s, state_c_s[j], PAST_LEN, c_gn_g[j], c_gn_b[j], 0, bs, ts, ts, ts)
    w_out = [c_w_out[j].astype(bf16)]
    xs, top_es, gates = per_group(
        lambda y, x, tm: _proj_ln([y], w_out, x, ln1_g[1], ln1_b[1], *router, 1, tm),
        (y_p, y_s), xs, tms)
    out_p, out_s = _moe_ln(xs, top_es, gates, 1, *moe, ln2_g[1], ln2_b[1])

    return (out_p.reshape(bp, tp, d), out_s.reshape(bs, ts, d),
            ak_p, av_p, bc_p[None], bh_p.reshape(1, bp, B_WIDTH), cs_p[None],
            ak_s, av_s, bc_s[None], bh_s.reshape(1, bs, B_WIDTH), cs_s[None])
```

```python
import functools
import math

import jax
import jax.numpy as jnp
from jax import lax
from jax.experimental import pallas as pl
from jax.experimental.pallas import tpu as pltpu

f32 = jnp.float32
bf16 = jnp.bfloat16
i32 = jnp.int32

DEPTH = 2
PAST_LEN = 1024
CHUNK = 64
A_PAST_CHUNKS = 8
REL_CLIP = 128
HA, DHA = 8, 64
A_WIDTH = HA * DHA
B_WIDTH = 512
HB = 8
CONV_W = 4
LRU_C = 8.0
HC, DKC, DVC = 4, 256, 512
C_QK, C_V = HC * DKC, HC * DVC
ROPE_BASE = 10000.0
GN_EPS = 1e-5
N_EXPERTS = 32
TOP_K = 4
D_FF = 1024
SWIGLU_LIMIT = 7.0
SWIGLU_ALPHA = 1.702
ALPHA = (2.0 * DEPTH) ** 0.25
LN_EPS = 1e-5
NEG = -1e30

LANES = 128
VMEM_LIMIT = 56 * 1024 * 1024
ATT_TQ = 256
MOE_BM = 256
MOE_BM_SHIFT = MOE_BM.bit_length() - 1
RET_TB = 512
LRU_TB = 256


def _pick(n, cands):
    for c in cands:
        if n % c == 0:
            return c
    raise ValueError(f"no tile for {n} in {cands}")


def _cparams(sem):
    return pltpu.CompilerParams(dimension_semantics=sem, vmem_limit_bytes=VMEM_LIMIT)


def _layernorm(v, g, b):
    mu = jnp.mean(v, axis=-1, keepdims=True)
    d = v - mu
    var = jnp.mean(d * d, axis=-1, keepdims=True)
    return d * lax.rsqrt(var + LN_EPS) * g + b


def _round_up_bm(v):
    return lax.shift_left(lax.shift_right_logical(v + (MOE_BM - 1), MOE_BM_SHIFT), MOE_BM_SHIFT)


def _rel_bias_matrix(table, nq, nk, d0):
    p = nq + nk - 1
    j = jnp.arange(p)
    u = table[:, jnp.clip(d0 + nq - 1 - j, -REL_CLIP, REL_CLIP) + REL_CLIP]
    u = jnp.roll(u, -(nq - 1), axis=1)
    flat = jnp.tile(u, (1, nq))[:, :nq * (p - 1)]
    return flat.reshape(table.shape[0], nq, p - 1)[:, :, :nk].astype(f32)


def _mm_kernel(x_ref, w_ref, o_ref):
    o_ref[...] = jnp.dot(x_ref[...].astype(bf16), w_ref[...], preferred_element_type=f32)


def _matmul(x, w, tm, tn):
    m, k = x.shape
    n = w.shape[1]
    return pl.pallas_call(
        _mm_kernel,
        grid=(m // tm, n // tn),
        in_specs=[pl.BlockSpec((tm, k), lambda i, j: (i, 0)),
                  pl.BlockSpec((k, tn), lambda i, j: (0, j))],
        out_specs=pl.BlockSpec((tm, tn), lambda i, j: (i, j)),
        out_shape=jax.ShapeDtypeStruct((m, n), f32),
        compiler_params=_cparams(("parallel", "parallel")),
        name="dense_proj",
    )(x, w)


def _route_top_k(x, w, b):
    logits = jnp.dot(x.astype(bf16), w.astype(bf16), preferred_element_type=f32) + b
    tm = logits.shape[0]
    lane = lax.broadcasted_iota(i32, (tm, N_EXPERTS), 1)
    out_lane = lax.broadcasted_iota(i32, (tm, LANES), 1)
    e_out = jnp.zeros((tm, LANES), i32)
    g_out = jnp.zeros((tm, LANES), f32)
    v0 = None
    den = jnp.zeros((tm, 1), f32)
    for k in range(TOP_K):
        v = jnp.max(logits, axis=-1, keepdims=True)
        idx = jnp.min(jnp.where(logits == v, lane, N_EXPERTS), axis=-1, keepdims=True)
        logits = jnp.where(lane == idx, -jnp.inf, logits)
        if k == 0:
            v0 = v
        p = jnp.exp(v - v0)
        den = den + p
        e_out = jnp.where(out_lane == k, idx, e_out)
        g_out = jnp.where(out_lane == k, p, g_out)
    return e_out, g_out / den


def _proj_ln_kernel(*refs, n_in):
    a_refs = refs[:n_in]
    w_refs = refs[n_in:2 * n_in]
    x_ref, g_ref, b_ref, wr_ref, br_ref, o_ref, e_ref, gate_ref = refs[2 * n_in:]
    acc = ALPHA * x_ref[...]
    for a_ref, w_ref in zip(a_refs, w_refs):
        acc = acc + jnp.dot(a_ref[...].astype(bf16), w_ref[...], preferred_element_type=f32)
    y = _layernorm(acc, g_ref[...], b_ref[...])
    o_ref[...] = y
    e_ref[...], gate_ref[...] = _route_top_k(y, wr_ref[0], br_ref[0])


def _proj_ln(acts, ws, x, g, b, w_router, b_router, l, tm):
    m, d = x.shape
    n_in = len(acts)
    depth = w_router.shape[0]
    in_specs = ([pl.BlockSpec((tm, a.shape[1]), lambda i: (i, 0)) for a in acts]
                + [pl.BlockSpec(w.shape, lambda i: (0, 0)) for w in ws]
                + [pl.BlockSpec((tm, d), lambda i: (i, 0)),
                   pl.BlockSpec((1, d), lambda i: (0, 0)),
                   pl.BlockSpec((1, d), lambda i: (0, 0)),
                   pl.BlockSpec((1, d, N_EXPERTS), lambda i: (l, 0, 0)),
                   pl.BlockSpec((1, 1, N_EXPERTS), lambda i: (l, 0, 0))])
    return pl.pallas_call(
        functools.partial(_proj_ln_kernel, n_in=n_in),
        grid=(m // tm,),
        in_specs=in_specs,
        out_specs=[pl.BlockSpec((tm, d), lambda i: (i, 0)),
                   pl.BlockSpec((tm, LANES), lambda i: (i, 0)),
                   pl.BlockSpec((tm, LANES), lambda i: (i, 0))],
        out_shape=[jax.ShapeDtypeStruct((m, d), f32),
                   jax.ShapeDtypeStruct((m, LANES), i32),
                   jax.ShapeDtypeStruct((m, LANES), f32)],
        compiler_params=_cparams(("parallel",)),
        name="proj_ln",
    )(*acts, *ws, x, g.reshape(1, d), b.reshape(1, d), w_router, b_router.reshape(depth, 1, N_EXPERTS))


def _attn_prompt_kernel(q_ref, k0_ref, k1_ref, k2_ref, v0_ref, v1_ref, v2_ref, bias_ref, o_ref):
    i = pl.program_id(0)
    tq = q_ref.shape[0]
    q = q_ref[...]
    k = jnp.concatenate([k0_ref[...], k1_ref[...], k2_ref[...]], axis=0)
    v = jnp.concatenate([v0_ref[...], v1_ref[...], v2_ref[...]], axis=0)
    kcol = lax.broadcasted_iota(i32, (1, 3 * tq), 1)
    tile_ok = (kcol // tq + i) >= 2
    qb = (q * (DHA ** -0.5)).astype(bf16)
    kb = k.astype(bf16)
    vb = v.astype(bf16)
    outs = []
    for h in range(HA):
        sl = slice(h * DHA, (h + 1) * DHA)
        s = lax.dot_general(qb[:, sl], kb[:, sl], (((1,), (1,)), ((), ())), preferred_element_type=f32)
        s = jnp.where(tile_ok, s + bias_ref[h], NEG)
        p = jnp.exp(s - jnp.max(s, axis=-1, keepdims=True))
        o = jnp.dot(p.astype(bf16), vb[:, sl], preferred_element_type=f32)
        outs.append(o / jnp.sum(p, axis=-1, keepdims=True))
    o_ref[...] = jnp.concatenate(outs, axis=-1)


def _attn_prompt(z, rel_bias, tp):
    tq = ATT_TQ
    nq = tp // tq
    qq = jnp.arange(tq)[:, None]
    kk = jnp.arange(3 * tq)[None, :]
    band = (kk // CHUNK >= qq // CHUNK) & (kk // CHUNK <= qq // CHUNK + A_PAST_CHUNKS)
    bias = jnp.where(band[None], _rel_bias_matrix(rel_bias, tq, 3 * tq, 2 * tq), NEG)

    def kv_spec(back, col):
        return pl.BlockSpec((tq, A_WIDTH), lambda i: (jnp.maximum(i - back, 0), col))

    return pl.pallas_call(
        _attn_prompt_kernel,
        grid=(nq,),
        in_specs=[pl.BlockSpec((tq, A_WIDTH), lambda i: (i, 0)),
                  kv_spec(2, 1), kv_spec(1, 1), kv_spec(0, 1),
                  kv_spec(2, 2), kv_spec(1, 2), kv_spec(0, 2),
                  pl.BlockSpec((HA, tq, 3 * tq), lambda i: (0, 0, 0))],
        out_specs=pl.BlockSpec((tq, A_WIDTH), lambda i: (i, 0)),
        out_shape=jax.ShapeDtypeStruct((tp, A_WIDTH), f32),
        compiler_params=_cparams(("parallel",)),
        name="attn_prompt",
    )(z, z, z, z, z, z, z, bias)


def _attn_sample_kernel(q_ref, kn_ref, vn_ref, kc_ref, vc_ref, bias_ref, o_ref):
    p_len = kc_ref.shape[1]
    q, kn, vn = q_ref[...], kn_ref[...], vn_ref[...]
    kc, vc = kc_ref[0], vc_ref[0]
    outs = []
    for h in range(HA):
        sl = slice(h * DHA, (h + 1) * DHA)
        qh = q[:, sl].astype(bf16)
        dims = (((1,), (1,)), ((), ()))
        sc = lax.dot_general(qh, kc[:, sl].astype(bf16), dims, preferred_element_type=f32)
        sn = lax.dot_general(qh, kn[:, sl].astype(bf16), dims, preferred_element_type=f32)
        b = bias_ref[h]
        sc = sc * (DHA ** -0.5) + b[:, :p_len]
        sn = sn * (DHA ** -0.5) + b[:, p_len:]
        m = jnp.maximum(jnp.max(sc, axis=-1, keepdims=True), jnp.max(sn, axis=-1, keepdims=True))
        pc, pn = jnp.exp(sc - m), jnp.exp(sn - m)
        den = jnp.sum(pc, axis=-1, keepdims=True) + jnp.sum(pn, axis=-1, keepdims=True)
        pc, pn = pc / den, pn / den
        outs.append(jnp.dot(pc.astype(bf16), vc[:, sl].astype(bf16), preferred_element_type=f32)
                    + jnp.dot(pn.astype(bf16), vn[:, sl].astype(bf16), preferred_element_type=f32))
    o_ref[...] = jnp.concatenate(outs, axis=-1)


def _attn_sample(z, k_cache, v_cache, rel_bias, row0, bs, ts):
    p_len = k_cache.shape[1]
    kc = k_cache.reshape(bs, p_len, A_WIDTH)
    vc = v_cache.reshape(bs, p_len, A_WIDTH)
    bias = _rel_bias_matrix(rel_bias, ts, p_len + ts, p_len)
    blk0 = row0 // ts

    def z_spec(col):
        return pl.BlockSpec((ts, A_WIDTH), lambda b: (blk0 + b, col))

    return pl.pallas_call(
        _attn_sample_kernel,
        grid=(bs,),
        in_specs=[z_spec(0), z_spec(1), z_spec(2),
                  pl.BlockSpec((1, p_len, A_WIDTH), lambda b: (b, 0, 0)),
                  pl.BlockSpec((1, p_len, A_WIDTH), lambda b: (b, 0, 0)),
                  pl.BlockSpec((HA, ts, p_len + ts), lambda b: (0, 0, 0))],
        out_specs=pl.BlockSpec((ts, A_WIDTH), lambda b: (b, 0)),
        out_shape=jax.ShapeDtypeStruct((bs * ts, A_WIDTH), f32),
        compiler_params=_cparams(("parallel",)),
        name="attn_sample",
    )(z, z, z, kc, vc, bias)


def _gelu_tanh(x):
    return 0.5 * x * (1.0 + jnp.tanh(math.sqrt(2.0 / math.pi) * (x + 0.044715 * (x * x * x))))


def _rglru_kernel(xb_ref, gb_ref, conv0_ref, h0_ref, cw_ref, cb_ref, wrg_ref, brg_ref,
                  wig_ref, big_ref, sp_ref, rec_ref, convn_ref, hl_ref, xp_ref, hc_ref):
    t = pl.program_id(1)
    tb = xb_ref.shape[0]
    pad = 8

    @pl.when(t == 0)
    def _():
        xp_ref[0:pad, :] = jnp.zeros((pad, B_WIDTH), f32)
        xp_ref[pad - (CONV_W - 1):pad, :] = conv0_ref[0]
        hc_ref[...] = h0_ref[0]

    xb = xb_ref[...]
    xp_ref[pad:pad + tb, :] = xb
    cw = cw_ref[...]
    u = cb_ref[...] + cw[CONV_W - 1:CONV_W, :] * xb
    for j in range(CONV_W - 1):
        sh = CONV_W - 1 - j
        u = u + cw[j:j + 1, :] * xp_ref[pad - sh:pad - sh + tb, :]
    convn_ref[0] = xp_ref[pad + tb - (CONV_W - 1):pad + tb, :]
    xp_ref[0:pad, :] = xp_ref[tb:tb + pad, :]

    ub = u.astype(bf16)
    r = jax.nn.sigmoid(jnp.dot(ub, wrg_ref[...], preferred_element_type=f32) + brg_ref[...])
    ig = jax.nn.sigmoid(jnp.dot(ub, wig_ref[...], preferred_element_type=f32) + big_ref[...])
    log_a = -LRU_C * r * sp_ref[...]
    a = jnp.exp(log_a)
    bt = jnp.sqrt(-jnp.tanh(log_a) * (a * a + 1.0)) * (ig * u)

    row = lax.broadcasted_iota(i32, (tb, 1), 0)
    s = 1
    while s < tb:
        keep = row >= s
        a_sh = pltpu.roll(a, s, axis=0)
        b_sh = pltpu.roll(bt, s, axis=0)
        bt = jnp.where(keep, a * b_sh + bt, bt)
        a = jnp.where(keep, a * a_sh, a)
        s *= 2
    h = a * hc_ref[...] + bt
    hc_ref[...] = h[tb - 1:tb, :]
    hl_ref[0] = h[tb - 1:tb, :]
    rec_ref[...] = h * _gelu_tanh(gb_ref[...])


def _rglru(z, conv0, h0, cw, cb, wrg, brg, wig, big, sp, row0, bsz, t_len, tb):
    nt = t_len // tb
    blk0 = row0 // tb

    def z_spec(col):
        return pl.BlockSpec((tb, B_WIDTH), lambda b, t: (blk0 + b * nt + t, col))

    def const(shape):
        return pl.BlockSpec(shape, lambda b, t: (0,) * len(shape))

    row = lambda v: v.reshape(1, B_WIDTH)
    return pl.pallas_call(
        _rglru_kernel,
        grid=(bsz, nt),
        in_specs=[z_spec(3), z_spec(4),
                  pl.BlockSpec((1, CONV_W - 1, B_WIDTH), lambda b, t: (b, 0, 0)),
                  pl.BlockSpec((1, 1, B_WIDTH), lambda b, t: (b, 0, 0)),
                  const((CONV_W, B_WIDTH)), const((1, B_WIDTH)),
                  const((B_WIDTH, B_WIDTH)), const((1, B_WIDTH)),
                  const((B_WIDTH, B_WIDTH)), const((1, B_WIDTH)), const((1, B_WIDTH))],
        out_specs=[pl.BlockSpec((tb, B_WIDTH), lambda b, t: (b * nt + t, 0)),
                   pl.BlockSpec((1, CONV_W - 1, B_WIDTH), lambda b, t: (b, 0, 0)),
                   pl.BlockSpec((1, 1, B_WIDTH), lambda b, t: (b, 0, 0))],
        out_shape=[jax.ShapeDtypeStruct((bsz * t_len, B_WIDTH), f32),
                   jax.ShapeDtypeStruct((bsz, CONV_W - 1, B_WIDTH), f32),
                   jax.ShapeDtypeStruct((bsz, 1, B_WIDTH), f32)],
        scratch_shapes=[pltpu.VMEM((tb + 8, B_WIDTH), f32), pltpu.VMEM((1, B_WIDTH), f32)],
        compiler_params=_cparams(("parallel", "arbitrary")),
        name="rglru",
    )(z, z, conv0, h0.reshape(bsz, 1, B_WIDTH), cw, row(cb), wrg, row(brg), wig, row(big), row(sp))


def _block_diag(w):
    hb, d, _ = w.shape
    eye = jnp.eye(hb, dtype=w.dtype)
    return (eye[:, None, :, None] * w[:, :, None, :]).reshape(hb * d, hb * d)


def _retention_kernel(x_ref, w_hbm, cos_ref, sin_ref, s0_hbm, gng_ref, gnb_ref,
                      y_ref, sout_hbm, w_ref, s_ref, sem, *, chunk, nt, zero_state):
    b = pl.program_id(0)
    t = pl.program_id(1)
    tb = x_ref.shape[0]
    half = DKC // 2

    @pl.when((b == 0) & (t == 0))
    def _():
        cp = pltpu.make_async_copy(w_hbm, w_ref, sem.at[0])
        cp.start()
        cp.wait()

    @pl.when(t == 0)
    def _():
        if zero_state:
            s_ref[...] = jnp.zeros(s_ref.shape, f32)
        else:
            cp = pltpu.make_async_copy(s0_hbm.at[b], s_ref, sem.at[1])
            cp.start()
            cp.wait()

    xb = x_ref[...].astype(bf16)
    cos, sin = cos_ref[...], sin_ref[...]
    n = lax.broadcasted_iota(i32, (tb, 1), 0)
    m = lax.broadcasted_iota(i32, (1, tb), 1)
    dist = n - m
    same = (n // chunk) == (m // chunk)
    expo = jnp.where(same, jnp.abs(dist), dist).astype(f32)
    visible = same | (dist > 0)
    nf = n.astype(f32)

    def proj(col, width):
        return jnp.dot(xb, w_ref[:, col:col + width], preferred_element_type=f32)

    def rot(v, scale):
        x1, x2 = v[:, :half], v[:, half:]
        return jnp.concatenate([x1 * cos - x2 * sin, x2 * cos + x1 * sin], axis=-1) * scale

    for h in range(HC):
        lg = math.log(1.0 - 2.0 ** (-5.0 - h))
        qr = rot(proj(h * DKC, DKC), 1.0)
        kr = rot(proj(C_QK + h * DKC, DKC), DKC ** -0.5)
        vb = proj(2 * C_QK + h * DVC, DVC).astype(bf16)
        gate = proj(2 * C_QK + C_V + h * DVC, DVC)
        qb = qr.astype(bf16)
        dmat = jnp.where(visible, jnp.exp(lg * expo), 0.0)
        s = lax.dot_general(qb, kr.astype(bf16), (((1,), (1,)), ((), ())),
                            preferred_element_type=f32) * dmat
        s_old = s_ref[h]
        o = jnp.dot(s.astype(bf16), vb, preferred_element_type=f32)
        o = o + jnp.dot(qb, s_old.astype(bf16), preferred_element_type=f32) * jnp.exp(lg * (nf + 1.0))
        kd = (kr * jnp.exp(lg * (tb - 1.0 - nf))).astype(bf16)
        s_ref[h] = math.exp(lg * tb) * s_old + lax.dot_general(
            kd, vb, (((0,), (0,)), ((), ())), preferred_element_type=f32)

        mu = jnp.mean(o, axis=-1, keepdims=True)
        dlt = o - mu
        var = jnp.mean(dlt * dlt, axis=-1, keepdims=True)
        vs = slice(h * DVC, (h + 1) * DVC)
        yn = dlt * lax.rsqrt(var + GN_EPS) * gng_ref[:, vs] + gnb_ref[:, vs]
        y_ref[:, vs] = (jax.nn.silu(gate) * yn).astype(y_ref.dtype)

    @pl.when(t == nt - 1)
    def _():
        cp = pltpu.make_async_copy(s_ref, sout_hbm.at[b], sem.at[1])
        cp.start()
        cp.wait()


def _retention(x, w_in, s0, pos0, gn_g, gn_b, bsz, t_len, tb, chunk):
    nt = t_len // tb
    d = x.shape[1]
    half = DKC // 2
    inv = 1.0 / (ROPE_BASE ** (jnp.arange(half, dtype=f32) / half))
    ang = (pos0 + jnp.arange(t_len)).astype(f32)[:, None] * inv[None, :]
    cos, sin = jnp.cos(ang), jnp.sin(ang)
    zero_state = s0 is None
    if zero_state:
        s0 = jnp.zeros((1, 8, LANES), f32)
    return pl.pallas_call(
        functools.partial(_retention_kernel, chunk=chunk, nt=nt, zero_state=zero_state),
        grid=(bsz, nt),
        in_specs=[pl.BlockSpec((tb, d), lambda b, t: (b * nt + t, 0)),
                  pl.BlockSpec(memory_space=pl.ANY),
                  pl.BlockSpec((tb, half), lambda b, t: (t, 0)),
                  pl.BlockSpec((tb, half), lambda b, t: (t, 0)),
                  pl.BlockSpec(memory_space=pl.ANY),
                  pl.BlockSpec((1, C_V), lambda b, t: (0, 0)),
                  pl.BlockSpec((1, C_V), lambda b, t: (0, 0))],
        out_specs=[pl.BlockSpec((tb, C_V), lambda b, t: (b * nt + t, 0)),
                   pl.BlockSpec(memory_space=pl.ANY)],
        out_shape=[jax.ShapeDtypeStruct((bsz * t_len, C_V), bf16),
                   jax.ShapeDtypeStruct((bsz, HC, DKC, DVC), f32)],
        scratch_shapes=[pltpu.VMEM(w_in.shape, bf16), pltpu.VMEM((HC, DKC, DVC), f32),
                        pltpu.SemaphoreType.DMA((2,))],
        compiler_params=_cparams(("arbitrary", "arbitrary")),
        name="retention",
    )(x, w_in, cos, sin, s0, gn_g.reshape(1, C_V), gn_b.reshape(1, C_V))


def _lane_prefix_sum(v):
    lane = lax.broadcasted_iota(i32, v.shape, 1)
    s = 1
    while s < LANES:
        v = v + jnp.where(lane >= s, pltpu.roll(v, s, axis=1), 0)
        s *= 2
    return v


def _slots_kernel(e_ref, dest_ref, cnt_ref, run_ref, start_ref, *, nt):
    ph = pl.program_id(0)
    i = pl.program_id(1)
    last = nt - 1
    tt = e_ref.shape[0]
    e = e_ref[...]
    lane = lax.broadcasted_iota(i32, (tt, LANES), 1)
    member = jnp.zeros((tt, LANES), f32)
    for k in range(TOP_K):
        member = member + (lane == e[:, k:k + 1]).astype(f32)
    tile_cnt = jnp.broadcast_to(jnp.sum(member, axis=0, keepdims=True), run_ref.shape)

    @pl.when((ph == 0) & (i == 0))
    def _():
        run_ref[...] = jnp.zeros(run_ref.shape, f32)

    @pl.when(ph == 0)
    def _():
        run_ref[...] += tile_cnt

        @pl.when(i == last)
        def _():
            cnt = run_ref[...].astype(i32)
            cnt_ref[...] = cnt
            padded = _round_up_bm(cnt)
            start_ref[...] = (_lane_prefix_sum(padded) - padded).astype(f32)
            run_ref[...] = jnp.zeros(run_ref.shape, f32)

    @pl.when(ph == 1)
    def _():
        r = lax.broadcasted_iota(i32, (tt, tt), 0)
        c = lax.broadcasted_iota(i32, (tt, tt), 1)
        lower = (c < r).astype(bf16)
        before = jnp.dot(lower, member.astype(bf16), preferred_element_type=f32)
        base = before + run_ref[0:1, :] + start_ref[0:1, :]
        out = jnp.zeros((tt, LANES), i32)
        for k in range(TOP_K):
            dk = jnp.sum(jnp.where(lane == e[:, k:k + 1], base, 0.0), axis=-1, keepdims=True)
            out = jnp.where(lane == k, dk.astype(i32), out)
        dest_ref[...] = out
        run_ref[...] += tile_cnt


def _slots(top_e, tt):
    n = top_e.shape[0]
    return pl.pallas_call(
        functools.partial(_slots_kernel, nt=n // tt),
        grid=(2, n // tt),
        in_specs=[pl.BlockSpec((tt, LANES), lambda p, i: (i, 0))],
        out_specs=[pl.BlockSpec((tt, LANES), lambda p, i: (i * p, 0)),
                   pl.BlockSpec((8, LANES), lambda p, i: (0, 0))],
        out_shape=[jax.ShapeDtypeStruct((n, LANES), i32),
                   jax.ShapeDtypeStruct((8, LANES), i32)],
        scratch_shapes=[pltpu.VMEM((8, LANES), f32), pltpu.VMEM((8, LANES), f32)],
        compiler_params=_cparams(("arbitrary", "arbitrary")),
        name="moe_slots",
    )(top_e)


def _dispatch_kernel(cnt_ref, start_ref, dest_ref, xp_ref, xs_ref, xg_hbm, zrow, sem, *, np_tiles):
    i = pl.program_id(0)
    tm = xp_ref.shape[0]

    @pl.when(i == 0)
    def _():
        zrow[...] = jnp.zeros(zrow.shape, f32)

        def per_expert(e, total):
            c = cnt_ref[e]
            npad = _round_up_bm(c) - c
            first = start_ref[e] + c

            def put(r, carry):
                pltpu.make_async_copy(zrow.at[pl.ds(0, 1)], xg_hbm.at[pl.ds(first + r, 1)], sem.at[1]).start()
                return carry
            lax.fori_loop(0, npad, put, 0)
            return total + npad
        total = lax.fori_loop(0, N_EXPERTS, per_expert, 0)

        def drain(r, carry):
            pltpu.make_async_copy(zrow.at[pl.ds(0, 1)], xg_hbm.at[pl.ds(0, 1)], sem.at[1]).wait()
            return carry
        lax.fori_loop(0, total, drain, 0)

        nblk = xg_hbm.shape[0] // MOE_BM
        end = start_ref[N_EXPERTS - 1] + _round_up_bm(cnt_ref[N_EXPERTS - 1])
        nact = lax.shift_right_logical(end, MOE_BM_SHIFT)

        def put_blk(bi, carry):
            pltpu.make_async_copy(zrow, xg_hbm.at[pl.ds(bi * MOE_BM, MOE_BM)], sem.at[1]).start()
            return carry
        lax.fori_loop(nact, nblk, put_blk, 0)

        def drain_blk(bi, carry):
            pltpu.make_async_copy(zrow, xg_hbm.at[pl.ds(0, MOE_BM)], sem.at[1]).wait()
            return carry
        lax.fori_loop(nact, nblk, drain_blk, 0)

    def scatter_rows(x_ref):
        def body(r, carry):
            for k in range(TOP_K):
                dst = dest_ref[0, 0, r * TOP_K + k]
                pltpu.make_async_copy(x_ref.at[pl.ds(r, 1)], xg_hbm.at[pl.ds(dst, 1)],
                                      sem.at[0]).start(priority=k % 2)
            return carry
        lax.fori_loop(0, tm, body, 0, unroll=4)
        pltpu.make_async_copy(xg_hbm.at[pl.ds(0, tm * TOP_K)], xg_hbm.at[pl.ds(0, tm * TOP_K)],
                              sem.at[0]).wait()

    @pl.when(i < np_tiles)
    def _():
        scatter_rows(xp_ref)

    @pl.when(i >= np_tiles)
    def _():
        scatter_rows(xs_ref)


def _dispatch(x_p, x_s, dest3, counts, starts, nblk, tm):
    d = x_p.shape[1]
    np_tiles, ns_tiles = x_p.shape[0] // tm, x_s.shape[0] // tm
    grid_spec = pltpu.PrefetchScalarGridSpec(
        num_scalar_prefetch=2,
        grid=(np_tiles + ns_tiles,),
        in_specs=[pl.BlockSpec((1, 1, tm * TOP_K), lambda i, c, s: (i, 0, 0), memory_space=pltpu.SMEM),
                  pl.BlockSpec((tm, d), lambda i, c, s: (jnp.minimum(i, np_tiles - 1), 0)),
                  pl.BlockSpec((tm, d), lambda i, c, s: (jnp.maximum(i - np_tiles, 0), 0))],
        out_specs=pl.BlockSpec(memory_space=pl.ANY),
        scratch_shapes=[pltpu.VMEM((MOE_BM, d), f32), pltpu.SemaphoreType.DMA((2,))],
    )
    return pl.pallas_call(
        functools.partial(_dispatch_kernel, np_tiles=np_tiles),
        grid_spec=grid_spec,
        out_shape=jax.ShapeDtypeStruct((nblk * MOE_BM, d), f32),
        compiler_params=_cparams(("arbitrary",)),
        name="moe_dispatch",
    )(counts, starts, dest3, x_p, x_s)


def _moe_kernel(blk_e_ref, nact_ref, first_ref, seg_ref, nxt_ref, x_ref, wup_hbm, bup_ref, wdn_hbm, bdn_ref,
                y_ref, wup_f, wdn_f, wup_bf, wdn_bf, sem, *, layer):
    b = pl.program_id(0)
    nact = nact_ref[0]

    def weight_copies(e, slot):
        return (pltpu.make_async_copy(wup_hbm.at[layer, e], wup_f.at[slot], sem.at[slot, 0]),
                pltpu.make_async_copy(wdn_hbm.at[layer, e], wdn_f.at[slot], sem.at[slot, 1]))

    @pl.when(b < nact)
    def _():
        @pl.when(first_ref[b] == 1)
        def _():
            slot = seg_ref[b] % 2

            @pl.when(b == 0)
            def _():
                for c in weight_copies(blk_e_ref[0], 0):
                    c.start()

            for c in weight_copies(blk_e_ref[b], slot):
                c.wait()

            @pl.when(nxt_ref[b] >= 0)
            def _():
                for c in weight_copies(nxt_ref[b], 1 - slot):
                    c.start()

            wup_bf[...] = wup_f[slot].astype(bf16)
            wdn_bf[...] = wdn_f[slot].astype(bf16)

        hdn = jnp.dot(x_ref[...].astype(bf16), wup_bf[...], preferred_element_type=f32) + bup_ref[0, 0]
        glu = jnp.minimum(hdn[:, :D_FF], SWIGLU_LIMIT)
        lin = jnp.clip(hdn[:, D_FF:], -SWIGLU_LIMIT, SWIGLU_LIMIT)
        act = glu * jax.nn.sigmoid(SWIGLU_ALPHA * glu) * (lin + 1.0)
        y_ref[...] = jnp.dot(act.astype(bf16), wdn_bf[...], preferred_element_type=f32) + bdn_ref[0, 0]

    @pl.when(b >= nact)
    def _():
        y_ref[...] = jnp.zeros(y_ref.shape, f32)


def _moe_experts(xg, blk_e, nact, first, seg, nxt, l, w_up, b_up, w_down, b_down):
    r, d = xg.shape
    bm = MOE_BM
    nblk = r // bm
    depth = w_up.shape[0]
    bsel = lambda b, be, *_: (l, be[b], 0, 0)
    grid_spec = pltpu.PrefetchScalarGridSpec(
        num_scalar_prefetch=5,
        grid=(nblk,),
        in_specs=[
            pl.BlockSpec((bm, d), lambda b, be, na, *_: (jnp.minimum(b, na[0] - 1), 0)),
            pl.BlockSpec(memory_space=pl.ANY),
            pl.BlockSpec((1, 1, 1, 2 * D_FF), bsel),
            pl.BlockSpec(memory_space=pl.ANY),
            pl.BlockSpec((1, 1, 1, d), bsel),
        ],
        out_specs=pl.BlockSpec((bm, d), lambda b, *_: (b, 0)),
        scratch_shapes=[pltpu.VMEM((2, d, 2 * D_FF), f32), pltpu.VMEM((2, D_FF, d), f32),
                        pltpu.VMEM((d, 2 * D_FF), bf16), pltpu.VMEM((D_FF, d), bf16),
                        pltpu.SemaphoreType.DMA((2, 2))],
    )
    return pl.pallas_call(
        functools.partial(_moe_kernel, layer=l),
        grid_spec=grid_spec,
        out_shape=jax.ShapeDtypeStruct((r, d), f32),
        compiler_params=_cparams(("arbitrary",)),
        name="moe_experts",
    )(blk_e, nact, first, seg, nxt, xg, w_up, b_up.reshape(depth, N_EXPERTS, 1, 2 * D_FF),
      w_down, b_down.reshape(depth, N_EXPERTS, 1, d))


def _combine_ln_kernel(pos_cur_ref, pos_nxt_ref, y_hbm, gate_ref, x_ref, g_ref, b_ref, o_ref, gbuf, sem,
                       *, nt):
    i = pl.program_id(0)
    tm = x_ref.shape[0]

    def gather(pos_ref, slot):
        def body(r, carry):
            for k in range(TOP_K):
                p = pos_ref[0, 0, r * TOP_K + k]
                pltpu.make_async_copy(y_hbm.at[pl.ds(p, 1)], gbuf.at[slot, k, pl.ds(r, 1)],
                                      sem.at[slot]).start(priority=k % 2)
            return carry
        lax.fori_loop(0, tm, body, 0, unroll=4)

    @pl.when(i == 0)
    def _():
        gather(pos_cur_ref, 0)

    @pl.when(i + 1 < nt)
    def _():
        gather(pos_nxt_ref, (i + 1) % 2)

    slot = i % 2
    pltpu.make_async_copy(gbuf.at[slot], gbuf.at[slot], sem.at[slot]).wait()
    gates = gate_ref[...]
    acc = ALPHA * x_ref[...]
    for k in range(TOP_K):
        acc = acc + gates[:, k:k + 1] * gbuf[slot, k]
    o_ref[...] = _layernorm(acc, g_ref[...], b_ref[...])


def _combine_ln(y, pos3, blk0, gates, x, g, b, tm):
    rows, d = x.shape
    nt = rows // tm
    grid_spec = pltpu.PrefetchScalarGridSpec(
        num_scalar_prefetch=0,
        grid=(nt,),
        in_specs=[
            pl.BlockSpec((1, 1, tm * TOP_K), lambda i: (blk0 + i, 0, 0), memory_space=pltpu.SMEM),
            pl.BlockSpec((1, 1, tm * TOP_K), lambda i: (blk0 + jnp.minimum(i + 1, nt - 1), 0, 0),
                         memory_space=pltpu.SMEM),
            pl.BlockSpec(memory_space=pl.ANY),
            pl.BlockSpec((tm, LANES), lambda i: (i, 0)),
            pl.BlockSpec((tm, d), lambda i: (i, 0)),
            pl.BlockSpec((1, d), lambda i: (0, 0)),
            pl.BlockSpec((1, d), lambda i: (0, 0)),
        ],
        out_specs=pl.BlockSpec((tm, d), lambda i: (i, 0)),
        scratch_shapes=[pltpu.VMEM((2, TOP_K, tm, d), f32), pltpu.SemaphoreType.DMA((2,))],
    )
    return pl.pallas_call(
        functools.partial(_combine_ln_kernel, nt=nt),
        grid_spec=grid_spec,
        out_shape=jax.ShapeDtypeStruct((rows, d), f32),
        compiler_params=_cparams(("arbitrary",)),
        name="combine_ln",
    )(pos3, pos3, y, gates, x, g.reshape(1, d), b.reshape(1, d))


def _moe_ln(xs, top_es, gates, l, w_up, b_up, w_down, b_down, g, b):
    x_p, x_s = xs
    n_p, n_s = x_p.shape[0], x_s.shape[0]
    n = n_p + n_s
    bm = MOE_BM
    nblk = -(-(n * TOP_K) // bm) + N_EXPERTS
    top_e = jnp.concatenate(top_es, axis=0)
    dest, counts = _slots(top_e, _pick(n, (512, 256, 128, 64)))
    counts = counts[0, :N_EXPERTS]
    padded = (counts + bm - 1) // bm * bm
    ends = jnp.cumsum(padded)
    starts = (ends - padded).astype(i32)
    blk = jnp.arange(nblk, dtype=i32)
    blk_e = jnp.minimum(jnp.sum((ends[None, :] <= (blk * bm)[:, None]).astype(i32), axis=1), N_EXPERTS - 1)
    nact = (ends[-1] // bm).astype(i32).reshape(1)
    first = ((blk < nact[0]) & ((blk == 0) | (blk_e != jnp.roll(blk_e, 1)))).astype(i32)
    seg = jnp.cumsum(first) - 1
    eid = jnp.arange(N_EXPERTS, dtype=i32)
    later_nonempty = (padded[None, :] > 0) & (eid[None, :] > eid[:, None])
    nxt_of_e = jnp.min(jnp.where(later_nonempty, eid[None, :], N_EXPERTS), axis=1)
    nxt = jnp.where(nxt_of_e < N_EXPERTS, nxt_of_e, -1)[blk_e].astype(i32)
    tmc = _pick(math.gcd(n_p, n_s), (256, 128, 64, 32))
    dest3 = dest[:, :TOP_K].reshape(n // tmc, 1, tmc * TOP_K)
    xg = _dispatch(x_p, x_s, dest3, counts, starts, nblk, tmc)
    y = _moe_experts(xg, blk_e, nact, first, seg, nxt, l, w_up, b_up, w_down, b_down)
    return (_combine_ln(y, dest3, 0, gates[0], x_p, g, b, tmc),
            _combine_ln(y, dest3, n_p // tmc, gates[1], x_s, g, b, tmc))


def kernel(x_prompt, x_sample, cache_a_k, cache_a_v, state_b_conv, state_b_h, state_c_s, ab_w_in, ab_rel_bias, ab_conv_w, ab_conv_b, ab_w_rg, ab_b_rg, ab_w_ig, ab_b_ig, ab_lambda, ab_w_out, c_w_in, c_gn_g, c_gn_b, c_w_out, ln1_g, ln1_b, ln2_g, ln2_b, moe_w_router, moe_b_router, moe_w_up, moe_b_up, moe_w_down, moe_b_down):
    bp, tp, d = x_prompt.shape
    bs, ts, _ = x_sample.shape
    assert bp == 1 and tp % RET_TB == 0 and ts <= CHUNK
    n_p, n_s = bp * tp, bs * ts
    xs = (x_prompt.reshape(n_p, d), x_sample.reshape(n_s, d))
    tms = tuple(_pick(m, (512, 256, 128, 64)) for m in (n_p, n_s))
    moe = (moe_w_up, moe_b_up, moe_w_down, moe_b_down)
    router = (moe_w_router, moe_b_router)

    def per_group(fn, *groups):
        return tuple(zip(*[fn(*args) for args in zip(*groups)]))

    j = 0
    w_in = ab_w_in[j].astype(bf16)
    z_p, z_s = (_matmul(x, w_in, tm, w_in.shape[1]) for x, tm in zip(xs, tms))
    att = (_attn_prompt(z_p, ab_rel_bias[j], tp),
           _attn_sample(z_s, cache_a_k[j], cache_a_v[j], ab_rel_bias[j], 0, bs, ts))
    lru_w = (ab_conv_w[j], ab_conv_b[j], _block_diag(ab_w_rg[j]).astype(bf16), ab_b_rg[j],
             _block_diag(ab_w_ig[j]).astype(bf16), ab_b_ig[j], jax.nn.softplus(-ab_lambda[j]))
    rec_p, bc_p, bh_p = _rglru(z_p, jnp.zeros((bp, CONV_W - 1, B_WIDTH), f32), jnp.zeros((bp, B_WIDTH), f32),
                               *lru_w, 0, bp, tp, _pick(tp, (LRU_TB, 128, 64)))
    rec_s, bc_s, bh_s = _rglru(z_s, state_b_conv[j], state_b_h[j], *lru_w, 0, bs, ts, ts)
    w_out = ab_w_out[j].astype(bf16)
    w_out = [w_out[:A_WIDTH], w_out[A_WIDTH:]]
    xs, top_es, gates = per_group(
        lambda a, r, x, tm: _proj_ln([a, r], w_out, x, ln1_g[0], ln1_b[0], *router, 0, tm),
        att, (rec_p, rec_s), xs, tms)
    xs = _moe_ln(xs, top_es, gates, 0, *moe, ln2_g[0], ln2_b[0])

    keep = min(A_PAST_CHUNKS * CHUNK, tp)
    ak_p = z_p[n_p - keep:, A_WIDTH:2 * A_WIDTH].reshape(1, bp, keep, HA, DHA)
    av_p = z_p[n_p - keep:, 2 * A_WIDTH:3 * A_WIDTH].reshape(1, bp, keep, HA, DHA)
    ak_s = z_s[:, A_WIDTH:2 * A_WIDTH].reshape(1, bs, ts, HA, DHA)
    av_s = z_s[:, 2 * A_WIDTH:3 * A_WIDTH].reshape(1, bs, ts, HA, DHA)

    w_in = c_w_in[j].astype(bf16)
    y_p, cs_p = _retention(xs[0], w_in, None, 0, c_gn_g[j], c_gn_b[j], bp, tp, RET_TB, CHUNK)
    y_s, cs_s = _retention(xs[1], w_in, state_c_s[j], PAST_LEN, c_gn_g[j], c_gn_b[j], bs, ts, ts, ts)
    w_out = [c_w_out[j].astype(bf16)]
    xs, top_es, gates = per_group(
        lambda y, x, tm: _proj_ln([y], w_out, x, ln1_g[1], ln1_b[1], *router, 1, tm),
        (y_p, y_s), xs, tms)
    out_p, out_s = _moe_ln(xs, top_es, gates, 1, *moe, ln2_g[1], ln2_b[1])

    return (out_p.reshape(bp, tp, d), out_s.reshape(bs, ts, d),
            ak_p, av_p, bc_p[None], bh_p.reshape(1, bp, B_WIDTH), cs_p[None],
            ak_s, av_s, bc_s[None], bh_s.reshape(1, bs, B_WIDTH), cs_s[None])
```

```python
import functools
import math

import jax
import jax.numpy as jnp
from jax import lax
from jax.experimental import pallas as pl
from jax.experimental.pallas import tpu as pltpu

f32 = jnp.float32
bf16 = jnp.bfloat16
i32 = jnp.int32

DEPTH = 2
PAST_LEN = 1024
CHUNK = 64
A_PAST_CHUNKS = 8
REL_CLIP = 128
HA, DHA = 8, 64
A_WIDTH = HA * DHA
B_WIDTH = 512
HB = 8
CONV_W = 4
LRU_C = 8.0
HC, DKC, DVC = 4, 256, 512
C_QK, C_V = HC * DKC, HC * DVC
ROPE_BASE = 10000.0
GN_EPS = 1e-5
N_EXPERTS = 32
TOP_K = 4
D_FF = 1024
SWIGLU_LIMIT = 7.0
SWIGLU_ALPHA = 1.702
ALPHA = (2.0 * DEPTH) ** 0.25
LN_EPS = 1e-5
NEG = -1e30

LANES = 128
VMEM_LIMIT = 56 * 1024 * 1024
ATT_TQ = 256
MOE_BM = 256
MOE_BM_SHIFT = MOE_BM.bit_length() - 1
MOE_PAIR = 2
RET_TB = 512
LRU_TB = 256


def _pick(n, cands):
    for c in cands:
        if n % c == 0:
            return c
    raise ValueError(f"no tile for {n} in {cands}")


def _cparams(sem):
    return pltpu.CompilerParams(dimension_semantics=sem, vmem_limit_bytes=VMEM_LIMIT)


def _layernorm(v, g, b):
    mu = jnp.mean(v, axis=-1, keepdims=True)
    d = v - mu
    var = jnp.mean(d * d, axis=-1, keepdims=True)
    return d * lax.rsqrt(var + LN_EPS) * g + b


def _round_up_bm(v):
    return lax.shift_left(lax.shift_right_logical(v + (MOE_BM - 1), MOE_BM_SHIFT), MOE_BM_SHIFT)


def _rel_bias_matrix(table, nq, nk, d0):
    p = nq + nk - 1
    j = jnp.arange(p)
    u = table[:, jnp.clip(d0 + nq - 1 - j, -REL_CLIP, REL_CLIP) + REL_CLIP]
    u = jnp.roll(u, -(nq - 1), axis=1)
    flat = jnp.tile(u, (1, nq))[:, :nq * (p - 1)]
    return flat.reshape(table.shape[0], nq, p - 1)[:, :, :nk].astype(f32)


def _mm_kernel(x_ref, w_ref, o_ref):
    o_ref[...] = jnp.dot(x_ref[...].astype(bf16), w_ref[...], preferred_element_type=f32)


def _matmul(x, w, tm, tn):
    m, k = x.shape
    n = w.shape[1]
    return pl.pallas_call(
        _mm_kernel,
        grid=(m // tm, n // tn),
        in_specs=[pl.BlockSpec((tm, k), lambda i, j: (i, 0)),
                  pl.BlockSpec((k, tn), lambda i, j: (0, j))],
        out_specs=pl.BlockSpec((tm, tn), lambda i, j: (i, j)),
        out_shape=jax.ShapeDtypeStruct((m, n), f32),
        compiler_params=_cparams(("parallel", "parallel")),
        name="dense_proj",
    )(x, w)


def _route_top_k(x, w, b):
    logits = jnp.dot(x.astype(bf16), w.astype(bf16), preferred_element_type=f32) + b
    tm = logits.shape[0]
    lane = lax.broadcasted_iota(i32, (tm, N_EXPERTS), 1)
    out_lane = lax.broadcasted_iota(i32, (tm, LANES), 1)
    e_out = jnp.zeros((tm, LANES), i32)
    g_out = jnp.zeros((tm, LANES), f32)
    v0 = None
    den = jnp.zeros((tm, 1), f32)
    for k in range(TOP_K):
        v = jnp.max(logits, axis=-1, keepdims=True)
        idx = jnp.min(jnp.where(logits == v, lane, N_EXPERTS), axis=-1, keepdims=True)
        logits = jnp.where(lane == idx, -jnp.inf, logits)
        if k == 0:
            v0 = v
        p = jnp.exp(v - v0)
        den = den + p
        e_out = jnp.where(out_lane == k, idx, e_out)
        g_out = jnp.where(out_lane == k, p, g_out)
    return e_out, g_out / den


def _proj_ln_kernel(*refs, n_in):
    a_refs = refs[:n_in]
    w_refs = refs[n_in:2 * n_in]
    x_ref, g_ref, b_ref, wr_ref, br_ref, o_ref, e_ref, gate_ref = refs[2 * n_in:]
    acc = ALPHA * x_ref[...]
    for a_ref, w_ref in zip(a_refs, w_refs):
        acc = acc + jnp.dot(a_ref[...].astype(bf16), w_ref[...], preferred_element_type=f32)
    y = _layernorm(acc, g_ref[...], b_ref[...])
    o_ref[...] = y
    e_ref[...], gate_ref[...] = _route_top_k(y, wr_ref[0], br_ref[0])


def _proj_ln(acts, ws, x, g, b, w_router, b_router, l, tm):
    m, d = x.shape
    n_in = len(acts)
    depth = w_router.shape[0]
    in_specs = ([pl.BlockSpec((tm, a.shape[1]), lambda i: (i, 0)) for a in acts]
                + [pl.BlockSpec(w.shape, lambda i: (0, 0)) for w in ws]
                + [pl.BlockSpec((tm, d), lambda i: (i, 0)),
                   pl.BlockSpec((1, d), lambda i: (0, 0)),
                   pl.BlockSpec((1, d), lambda i: (0, 0)),
                   pl.BlockSpec((1, d, N_EXPERTS), lambda i: (l, 0, 0)),
                   pl.BlockSpec((1, 1, N_EXPERTS), lambda i: (l, 0, 0))])
    return pl.pallas_call(
        functools.partial(_proj_ln_kernel, n_in=n_in),
        grid=(m // tm,),
        in_specs=in_specs,
        out_specs=[pl.BlockSpec((tm, d), lambda i: (i, 0)),
                   pl.BlockSpec((tm, LANES), lambda i: (i, 0)),
                   pl.BlockSpec((tm, LANES), lambda i: (i, 0))],
        out_shape=[jax.ShapeDtypeStruct((m, d), f32),
                   jax.ShapeDtypeStruct((m, LANES), i32),
                   jax.ShapeDtypeStruct((m, LANES), f32)],
        compiler_params=_cparams(("parallel",)),
        name="proj_ln",
    )(*acts, *ws, x, g.reshape(1, d), b.reshape(1, d), w_router, b_router.reshape(depth, 1, N_EXPERTS))


def _attn_prompt_kernel(q_ref, k0_ref, k1_ref, k2_ref, v0_ref, v1_ref, v2_ref, bias_ref, o_ref):
    i = pl.program_id(0)
    tq = q_ref.shape[0]
    q = q_ref[...]
    k = jnp.concatenate([k0_ref[...], k1_ref[...], k2_ref[...]], axis=0)
    v = jnp.concatenate([v0_ref[...], v1_ref[...], v2_ref[...]], axis=0)
    kcol = lax.broadcasted_iota(i32, (1, 3 * tq), 1)
    tile_ok = (kcol // tq + i) >= 2
    qb = (q * (DHA ** -0.5)).astype(bf16)
    kb = k.astype(bf16)
    vb = v.astype(bf16)
    outs = []
    for h in range(HA):
        sl = slice(h * DHA, (h + 1) * DHA)
        s = lax.dot_general(qb[:, sl], kb[:, sl], (((1,), (1,)), ((), ())), preferred_element_type=f32)
        s = jnp.where(tile_ok, s + bias_ref[h], NEG)
        p = jnp.exp(s - jnp.max(s, axis=-1, keepdims=True))
        o = jnp.dot(p.astype(bf16), vb[:, sl], preferred_element_type=f32)
        outs.append(o / jnp.sum(p, axis=-1, keepdims=True))
    o_ref[...] = jnp.concatenate(outs, axis=-1)


def _attn_prompt(z, rel_bias, tp):
    tq = ATT_TQ
    nq = tp // tq
    qq = jnp.arange(tq)[:, None]
    kk = jnp.arange(3 * tq)[None, :]
    band = (kk // CHUNK >= qq // CHUNK) & (kk // CHUNK <= qq // CHUNK + A_PAST_CHUNKS)
    bias = jnp.where(band[None], _rel_bias_matrix(rel_bias, tq, 3 * tq, 2 * tq), NEG)

    def kv_spec(back, col):
        return pl.BlockSpec((tq, A_WIDTH), lambda i: (jnp.maximum(i - back, 0), col))

    return pl.pallas_call(
        _attn_prompt_kernel,
        grid=(nq,),
        in_specs=[pl.BlockSpec((tq, A_WIDTH), lambda i: (i, 0)),
                  kv_spec(2, 1), kv_spec(1, 1), kv_spec(0, 1),
                  kv_spec(2, 2), kv_spec(1, 2), kv_spec(0, 2),
                  pl.BlockSpec((HA, tq, 3 * tq), lambda i: (0, 0, 0))],
        out_specs=pl.BlockSpec((tq, A_WIDTH), lambda i: (i, 0)),
        out_shape=jax.ShapeDtypeStruct((tp, A_WIDTH), f32),
        compiler_params=_cparams(("parallel",)),
        name="attn_prompt",
    )(z, z, z, z, z, z, z, bias)


def _attn_sample_kernel(q_ref, kn_ref, vn_ref, kc_ref, vc_ref, bias_ref, o_ref):
    p_len = kc_ref.shape[1]
    q, kn, vn = q_ref[...], kn_ref[...], vn_ref[...]
    kc, vc = kc_ref[0], vc_ref[0]
    outs = []
    for h in range(HA):
        sl = slice(h * DHA, (h + 1) * DHA)
        qh = q[:, sl].astype(bf16)
        dims = (((1,), (1,)), ((), ()))
        sc = lax.dot_general(qh, kc[:, sl].astype(bf16), dims, preferred_element_type=f32)
        sn = lax.dot_general(qh, kn[:, sl].astype(bf16), dims, preferred_element_type=f32)
        b = bias_ref[h]
        sc = sc * (DHA ** -0.5) + b[:, :p_len]
        sn = sn * (DHA ** -0.5) + b[:, p_len:]
        m = jnp.maximum(jnp.max(sc, axis=-1, keepdims=True), jnp.max(sn, axis=-1, keepdims=True))
        pc, pn = jnp.exp(sc - m), jnp.exp(sn - m)
        den = jnp.sum(pc, axis=-1, keepdims=True) + jnp.sum(pn, axis=-1, keepdims=True)
        pc, pn = pc / den, pn / den
        outs.append(jnp.dot(pc.astype(bf16), vc[:, sl].astype(bf16), preferred_element_type=f32)
                    + jnp.dot(pn.astype(bf16), vn[:, sl].astype(bf16), preferred_element_type=f32))
    o_ref[...] = jnp.concatenate(outs, axis=-1)


def _attn_sample(z, k_cache, v_cache, rel_bias, row0, bs, ts):
    p_len = k_cache.shape[1]
    kc = k_cache.reshape(bs, p_len, A_WIDTH)
    vc = v_cache.reshape(bs, p_len, A_WIDTH)
    bias = _rel_bias_matrix(rel_bias, ts, p_len + ts, p_len)
    blk0 = row0 // ts

    def z_spec(col):
        return pl.BlockSpec((ts, A_WIDTH), lambda b: (blk0 + b, col))

    return pl.pallas_call(
        _attn_sample_kernel,
        grid=(bs,),
        in_specs=[z_spec(0), z_spec(1), z_spec(2),
                  pl.BlockSpec((1, p_len, A_WIDTH), lambda b: (b, 0, 0)),
                  pl.BlockSpec((1, p_len, A_WIDTH), lambda b: (b, 0, 0)),
                  pl.BlockSpec((HA, ts, p_len + ts), lambda b: (0, 0, 0))],
        out_specs=pl.BlockSpec((ts, A_WIDTH), lambda b: (b, 0)),
        out_shape=jax.ShapeDtypeStruct((bs * ts, A_WIDTH), f32),
        compiler_params=_cparams(("parallel",)),
        name="attn_sample",
    )(z, z, z, kc, vc, bias)


def _gelu_tanh(x):
    return 0.5 * x * (1.0 + jnp.tanh(math.sqrt(2.0 / math.pi) * (x + 0.044715 * (x * x * x))))


def _rglru_kernel(xb_ref, gb_ref, conv0_ref, h0_ref, cw_ref, cb_ref, wrg_ref, brg_ref,
                  wig_ref, big_ref, sp_ref, rec_ref, convn_ref, hl_ref, xp_ref, hc_ref):
    t = pl.program_id(1)
    tb = xb_ref.shape[0]
    pad = 8

    @pl.when(t == 0)
    def _():
        xp_ref[0:pad, :] = jnp.zeros((pad, B_WIDTH), f32)
        xp_ref[pad - (CONV_W - 1):pad, :] = conv0_ref[0]
        hc_ref[...] = h0_ref[0]

    xb = xb_ref[...]
    xp_ref[pad:pad + tb, :] = xb
    cw = cw_ref[...]
    u = cb_ref[...] + cw[CONV_W - 1:CONV_W, :] * xb
    for j in range(CONV_W - 1):
        sh = CONV_W - 1 - j
        u = u + cw[j:j + 1, :] * xp_ref[pad - sh:pad - sh + tb, :]
    convn_ref[0] = xp_ref[pad + tb - (CONV_W - 1):pad + tb, :]
    xp_ref[0:pad, :] = xp_ref[tb:tb + pad, :]

    ub = u.astype(bf16)
    r = jax.nn.sigmoid(jnp.dot(ub, wrg_ref[...], preferred_element_type=f32) + brg_ref[...])
    ig = jax.nn.sigmoid(jnp.dot(ub, wig_ref[...], preferred_element_type=f32) + big_ref[...])
    log_a = -LRU_C * r * sp_ref[...]
    a = jnp.exp(log_a)
    bt = jnp.sqrt(-jnp.tanh(log_a) * (a * a + 1.0)) * (ig * u)

    row = lax.broadcasted_iota(i32, (tb, 1), 0)
    s = 1
    while s < tb:
        keep = row >= s
        a_sh = pltpu.roll(a, s, axis=0)
        b_sh = pltpu.roll(bt, s, axis=0)
        bt = jnp.where(keep, a * b_sh + bt, bt)
        a = jnp.where(keep, a * a_sh, a)
        s *= 2
    h = a * hc_ref[...] + bt
    hc_ref[...] = h[tb - 1:tb, :]
    hl_ref[0] = h[tb - 1:tb, :]
    rec_ref[...] = h * _gelu_tanh(gb_ref[...])


def _rglru(z, conv0, h0, cw, cb, wrg, brg, wig, big, sp, row0, bsz, t_len, tb):
    nt = t_len // tb
    blk0 = row0 // tb

    def z_spec(col):
        return pl.BlockSpec((tb, B_WIDTH), lambda b, t: (blk0 + b * nt + t, col))

    def const(shape):
        return pl.BlockSpec(shape, lambda b, t: (0,) * len(shape))

    row = lambda v: v.reshape(1, B_WIDTH)
    return pl.pallas_call(
        _rglru_kernel,
        grid=(bsz, nt),
        in_specs=[z_spec(3), z_spec(4),
                  pl.BlockSpec((1, CONV_W - 1, B_WIDTH), lambda b, t: (b, 0, 0)),
                  pl.BlockSpec((1, 1, B_WIDTH), lambda b, t: (b, 0, 0)),
                  const((CONV_W, B_WIDTH)), const((1, B_WIDTH)),
                  const((B_WIDTH, B_WIDTH)), const((1, B_WIDTH)),
                  const((B_WIDTH, B_WIDTH)), const((1, B_WIDTH)), const((1, B_WIDTH))],
        out_specs=[pl.BlockSpec((tb, B_WIDTH), lambda b, t: (b * nt + t, 0)),
                   pl.BlockSpec((1, CONV_W - 1, B_WIDTH), lambda b, t: (b, 0, 0)),
                   pl.BlockSpec((1, 1, B_WIDTH), lambda b, t: (b, 0, 0))],
        out_shape=[jax.ShapeDtypeStruct((bsz * t_len, B_WIDTH), f32),
                   jax.ShapeDtypeStruct((bsz, CONV_W - 1, B_WIDTH), f32),
                   jax.ShapeDtypeStruct((bsz, 1, B_WIDTH), f32)],
        scratch_shapes=[pltpu.VMEM((tb + 8, B_WIDTH), f32), pltpu.VMEM((1, B_WIDTH), f32)],
        compiler_params=_cparams(("parallel", "arbitrary")),
        name="rglru",
    )(z, z, conv0, h0.reshape(bsz, 1, B_WIDTH), cw, row(cb), wrg, row(brg), wig, row(big), row(sp))


def _block_diag(w):
    hb, d, _ = w.shape
    eye = jnp.eye(hb, dtype=w.dtype)
    return (eye[:, None, :, None] * w[:, :, None, :]).reshape(hb * d, hb * d)


def _retention_kernel(x_ref, w_hbm, cos_ref, sin_ref, s0_hbm, gng_ref, gnb_ref,
                      y_ref, sout_hbm, w_ref, s_ref, z_ref, sem, *, chunk, nt, tb, zero_state, hoist):
    b = pl.program_id(0)
    t = pl.program_id(1)
    half = DKC // 2

    @pl.when((b == 0) & (t == 0))
    def _():
        cp = pltpu.make_async_copy(w_hbm, w_ref, sem.at[0])
        cp.start()
        cp.wait()
        if hoist:
            xa = x_ref[...].astype(bf16)
            for c in range(0, w_ref.shape[1], C_QK):
                z_ref[:, c:c + C_QK] = jnp.dot(xa, w_ref[:, c:c + C_QK], preferred_element_type=f32)

    @pl.when(t == 0)
    def _():
        if zero_state:
            s_ref[...] = jnp.zeros(s_ref.shape, f32)
        else:
            cp = pltpu.make_async_copy(s0_hbm.at[b], s_ref, sem.at[1])
            cp.start()
            cp.wait()

    cos, sin = cos_ref[...], sin_ref[...]
    n = lax.broadcasted_iota(i32, (tb, 1), 0)
    m = lax.broadcasted_iota(i32, (1, tb), 1)
    dist = n - m
    same = (n // chunk) == (m // chunk)
    expo = jnp.where(same, jnp.abs(dist), dist).astype(f32)
    visible = same | (dist > 0)
    nf = n.astype(f32)

    if hoist:
        row0 = pl.multiple_of(b * tb, tb)

        def proj(col, width):
            return z_ref[pl.ds(row0, tb), col:col + width]
    else:
        xb = x_ref[...].astype(bf16)

        def proj(col, width):
            return jnp.dot(xb, w_ref[:, col:col + width], preferred_element_type=f32)

    def rot(v, scale):
        x1, x2 = v[:, :half], v[:, half:]
        return jnp.concatenate([x1 * cos - x2 * sin, x2 * cos + x1 * sin], axis=-1) * scale

    for h in range(HC):
        lg = math.log(1.0 - 2.0 ** (-5.0 - h))
        qr = rot(proj(h * DKC, DKC), 1.0)
        kr = rot(proj(C_QK + h * DKC, DKC), DKC ** -0.5)
        vb = proj(2 * C_QK + h * DVC, DVC).astype(bf16)
        gate = proj(2 * C_QK + C_V + h * DVC, DVC)
        qb = qr.astype(bf16)
        dmat = jnp.where(visible, jnp.exp(lg * expo), 0.0)
        s = lax.dot_general(qb, kr.astype(bf16), (((1,), (1,)), ((), ())),
                            preferred_element_type=f32) * dmat
        s_old = s_ref[h]
        o = jnp.dot(s.astype(bf16), vb, preferred_element_type=f32)
        o = o + jnp.dot(qb, s_old.astype(bf16), preferred_element_type=f32) * jnp.exp(lg * (nf + 1.0))
        kd = (kr * jnp.exp(lg * (tb - 1.0 - nf))).astype(bf16)
        s_ref[h] = math.exp(lg * tb) * s_old + lax.dot_general(
            kd, vb, (((0,), (0,)), ((), ())), preferred_element_type=f32)

        mu = jnp.mean(o, axis=-1, keepdims=True)
        dlt = o - mu
        var = jnp.mean(dlt * dlt, axis=-1, keepdims=True)
        vs = slice(h * DVC, (h + 1) * DVC)
        yn = dlt * lax.rsqrt(var + GN_EPS) * gng_ref[:, vs] + gnb_ref[:, vs]
        y_ref[:, vs] = (jax.nn.silu(gate) * yn).astype(y_ref.dtype)

    @pl.when(t == nt - 1)
    def _():
        cp = pltpu.make_async_copy(s_ref, sout_hbm.at[b], sem.at[1])
        cp.start()
        cp.wait()


def _retention(x, w_in, s0, pos0, gn_g, gn_b, bsz, t_len, tb, chunk):
    nt = t_len // tb
    d = x.shape[1]
    half = DKC // 2
    inv = 1.0 / (ROPE_BASE ** (jnp.arange(half, dtype=f32) / half))
    ang = (pos0 + jnp.arange(t_len)).astype(f32)[:, None] * inv[None, :]
    cos, sin = jnp.cos(ang), jnp.sin(ang)
    zero_state = s0 is None
    if zero_state:
        s0 = jnp.zeros((1, 8, LANES), f32)
    hoist = nt == 1 and bsz > 1
    x_spec = (pl.BlockSpec(x.shape, lambda b, t: (0, 0)) if hoist
              else pl.BlockSpec((tb, d), lambda b, t: (b * nt + t, 0)))
    z_shape = (bsz * tb, w_in.shape[1]) if hoist else (8, LANES)
    return pl.pallas_call(
        functools.partial(_retention_kernel, chunk=chunk, nt=nt, tb=tb, zero_state=zero_state, hoist=hoist),
        grid=(bsz, nt),
        in_specs=[x_spec,
                  pl.BlockSpec(memory_space=pl.ANY),
                  pl.BlockSpec((tb, half), lambda b, t: (t, 0)),
                  pl.BlockSpec((tb, half), lambda b, t: (t, 0)),
                  pl.BlockSpec(memory_space=pl.ANY),
                  pl.BlockSpec((1, C_V), lambda b, t: (0, 0)),
                  pl.BlockSpec((1, C_V), lambda b, t: (0, 0))],
        out_specs=[pl.BlockSpec((tb, C_V), lambda b, t: (b * nt + t, 0)),
                   pl.BlockSpec(memory_space=pl.ANY)],
        out_shape=[jax.ShapeDtypeStruct((bsz * t_len, C_V), bf16),
                   jax.ShapeDtypeStruct((bsz, HC, DKC, DVC), f32)],
        scratch_shapes=[pltpu.VMEM(w_in.shape, bf16), pltpu.VMEM((HC, DKC, DVC), f32),
                        pltpu.VMEM(z_shape, f32), pltpu.SemaphoreType.DMA((2,))],
        compiler_params=_cparams(("arbitrary", "arbitrary")),
        name="retention",
    )(x, w_in, cos, sin, s0, gn_g.reshape(1, C_V), gn_b.reshape(1, C_V))


def _lane_prefix_sum(v):
    lane = lax.broadcasted_iota(i32, v.shape, 1)
    s = 1
    while s < LANES:
        v = v + jnp.where(lane >= s, pltpu.roll(v, s, axis=1), 0)
        s *= 2
    return v


def _slots_kernel(e_ref, dest_ref, cnt_ref, run_ref, start_ref, *, nt):
    ph = pl.program_id(0)
    i = pl.program_id(1)
    last = nt - 1
    tt = e_ref.shape[0]
    e = e_ref[...]
    lane = lax.broadcasted_iota(i32, (tt, LANES), 1)
    member = jnp.zeros((tt, LANES), f32)
    for k in range(TOP_K):
        member = member + (lane == e[:, k:k + 1]).astype(f32)
    tile_cnt = jnp.broadcast_to(jnp.sum(member, axis=0, keepdims=True), run_ref.shape)

    @pl.when((ph == 0) & (i == 0))
    def _():
        run_ref[...] = jnp.zeros(run_ref.shape, f32)

    @pl.when(ph == 0)
    def _():
        run_ref[...] += tile_cnt

        @pl.when(i == last)
        def _():
            cnt = run_ref[...].astype(i32)
            cnt_ref[...] = cnt
            padded = _round_up_bm(cnt)
            start_ref[...] = (_lane_prefix_sum(padded) - padded).astype(f32)
            run_ref[...] = jnp.zeros(run_ref.shape, f32)

    @pl.when(ph == 1)
    def _():
        r = lax.broadcasted_iota(i32, (tt, tt), 0)
        c = lax.broadcasted_iota(i32, (tt, tt), 1)
        lower = (c < r).astype(bf16)
        before = jnp.dot(lower, member.astype(bf16), preferred_element_type=f32)
        base = before + run_ref[0:1, :] + start_ref[0:1, :]
        out = jnp.zeros((tt, LANES), i32)
        for k in range(TOP_K):
            dk = jnp.sum(jnp.where(lane == e[:, k:k + 1], base, 0.0), axis=-1, keepdims=True)
            out = jnp.where(lane == k, dk.astype(i32), out)
        dest_ref[...] = out
        run_ref[...] += tile_cnt


def _slots(top_e, tt):
    n = top_e.shape[0]
    return pl.pallas_call(
        functools.partial(_slots_kernel, nt=n // tt),
        grid=(2, n // tt),
        in_specs=[pl.BlockSpec((tt, LANES), lambda p, i: (i, 0))],
        out_specs=[pl.BlockSpec((tt, LANES), lambda p, i: (i * p, 0)),
                   pl.BlockSpec((8, LANES), lambda p, i: (0, 0))],
        out_shape=[jax.ShapeDtypeStruct((n, LANES), i32),
                   jax.ShapeDtypeStruct((8, LANES), i32)],
        scratch_shapes=[pltpu.VMEM((8, LANES), f32), pltpu.VMEM((8, LANES), f32)],
        compiler_params=_cparams(("arbitrary", "arbitrary")),
        name="moe_slots",
    )(top_e)


def _dispatch_kernel(cnt_ref, start_ref, dest_ref, xp_ref, xs_ref, xg_hbm, zrow, sem, *, np_tiles):
    i = pl.program_id(0)
    tm = xp_ref.shape[0]

    @pl.when(i == 0)
    def _():
        zrow[...] = jnp.zeros(zrow.shape, f32)

        def per_expert(e, total):
            c = cnt_ref[e]
            npad = _round_up_bm(c) - c
            first = start_ref[e] + c

            def put(r, carry):
                pltpu.make_async_copy(zrow.at[pl.ds(0, 1)], xg_hbm.at[pl.ds(first + r, 1)], sem.at[1]).start()
                return carry
            lax.fori_loop(0, npad, put, 0)
            return total + npad
        total = lax.fori_loop(0, N_EXPERTS, per_expert, 0)

        def drain(r, carry):
            pltpu.make_async_copy(zrow.at[pl.ds(0, 1)], xg_hbm.at[pl.ds(0, 1)], sem.at[1]).wait()
            return carry
        lax.fori_loop(0, total, drain, 0)

        nblk = xg_hbm.shape[0] // MOE_BM
        end = start_ref[N_EXPERTS - 1] + _round_up_bm(cnt_ref[N_EXPERTS - 1])
        nact = lax.shift_right_logical(end, MOE_BM_SHIFT)

        def put_blk(bi, carry):
            pltpu.make_async_copy(zrow, xg_hbm.at[pl.ds(bi * MOE_BM, MOE_BM)], sem.at[1]).start()
            return carry
        lax.fori_loop(nact, nblk, put_blk, 0)

        def drain_blk(bi, carry):
            pltpu.make_async_copy(zrow, xg_hbm.at[pl.ds(0, MOE_BM)], sem.at[1]).wait()
            return carry
        lax.fori_loop(nact, nblk, drain_blk, 0)

    def scatter_rows(x_ref):
        def body(r, carry):
            for k in range(TOP_K):
                dst = dest_ref[0, 0, r * TOP_K + k]
                pltpu.make_async_copy(x_ref.at[pl.ds(r, 1)], xg_hbm.at[pl.ds(dst, 1)],
                                      sem.at[0]).start(priority=k % 2)
            return carry
        lax.fori_loop(0, tm, body, 0, unroll=4)
        pltpu.make_async_copy(xg_hbm.at[pl.ds(0, tm * TOP_K)], xg_hbm.at[pl.ds(0, tm * TOP_K)],
                              sem.at[0]).wait()

    @pl.when(i < np_tiles)
    def _():
        scatter_rows(xp_ref)

    @pl.when(i >= np_tiles)
    def _():
        scatter_rows(xs_ref)


def _dispatch(x_p, x_s, dest3, counts, starts, nblk, tm):
    d = x_p.shape[1]
    np_tiles, ns_tiles = x_p.shape[0] // tm, x_s.shape[0] // tm
    grid_spec = pltpu.PrefetchScalarGridSpec(
        num_scalar_prefetch=2,
        grid=(np_tiles + ns_tiles,),
        in_specs=[pl.BlockSpec((1, 1, tm * TOP_K), lambda i, c, s: (i, 0, 0), memory_space=pltpu.SMEM),
                  pl.BlockSpec((tm, d), lambda i, c, s: (jnp.minimum(i, np_tiles - 1), 0)),
                  pl.BlockSpec((tm, d), lambda i, c, s: (jnp.maximum(i - np_tiles, 0), 0))],
        out_specs=pl.BlockSpec(memory_space=pl.ANY),
        scratch_shapes=[pltpu.VMEM((MOE_BM, d), f32), pltpu.SemaphoreType.DMA((2,))],
    )
    return pl.pallas_call(
        functools.partial(_dispatch_kernel, np_tiles=np_tiles),
        grid_spec=grid_spec,
        out_shape=jax.ShapeDtypeStruct((nblk * MOE_BM, d), f32),
        compiler_params=_cparams(("arbitrary",)),
        name="moe_dispatch",
    )(counts, starts, dest3, x_p, x_s)


def _moe_kernel(blk_e_ref, nact_ref, first_ref, seg_ref, nxt_ref, x_ref, wup_hbm, bup_ref, wdn_hbm, bdn_ref,
                y_ref, wup_f, wdn_f, wup_bf, wdn_bf, sem, *, layer):
    b0 = pl.program_id(0) * MOE_PAIR
    nact = nact_ref[0]
    bm = MOE_BM

    def weight_copies(e, slot):
        return (pltpu.make_async_copy(wup_hbm.at[layer, e], wup_f.at[slot], sem.at[slot, 0]),
                pltpu.make_async_copy(wdn_hbm.at[layer, e], wdn_f.at[slot], sem.at[slot, 1]))

    @pl.when(b0 < nact)
    def _():
        for sub in range(MOE_PAIR):
            b = b0 + sub

            @pl.when(first_ref[b] == 1)
            def _():
                slot = seg_ref[b] % 2

                @pl.when(b == 0)
                def _():
                    for c in weight_copies(blk_e_ref[0], 0):
                        c.start()

                for c in weight_copies(blk_e_ref[b], slot):
                    c.wait()

                @pl.when(nxt_ref[b] >= 0)
                def _():
                    for c in weight_copies(nxt_ref[b], 1 - slot):
                        c.start()

                wup_bf[slot] = wup_f[slot].astype(bf16)
                wdn_bf[slot] = wdn_f[slot].astype(bf16)

        for sub in range(MOE_PAIR):
            b = b0 + sub
            slot = seg_ref[b] % 2
            e = layer * N_EXPERTS + blk_e_ref[b]
            rows = slice(sub * bm, (sub + 1) * bm)
            hdn = jnp.dot(x_ref[rows, :].astype(bf16), wup_bf[slot], preferred_element_type=f32)
            hdn = hdn + bup_ref[pl.ds(e, 1), :]
            glu = jnp.minimum(hdn[:, :D_FF], SWIGLU_LIMIT)
            lin = jnp.clip(hdn[:, D_FF:], -SWIGLU_LIMIT, SWIGLU_LIMIT)
            act = glu * jax.nn.sigmoid(SWIGLU_ALPHA * glu) * (lin + 1.0)
            y_ref[rows, :] = (jnp.dot(act.astype(bf16), wdn_bf[slot], preferred_element_type=f32)
                              + bdn_ref[pl.ds(e, 1), :])

    @pl.when(b0 >= nact)
    def _():
        y_ref[...] = jnp.zeros(y_ref.shape, f32)


def _moe_experts(xg, blk_e, nact, first, seg, nxt, l, w_up, b_up, w_down, b_down):
    r, d = xg.shape
    rows = MOE_BM * MOE_PAIR
    depth = w_up.shape[0]
    whole = lambda shape: pl.BlockSpec(shape, lambda s, *_: (0,) * len(shape))
    grid_spec = pltpu.PrefetchScalarGridSpec(
        num_scalar_prefetch=5,
        grid=(r // rows,),
        in_specs=[
            pl.BlockSpec((rows, d), lambda s, be, na, *_: (jnp.minimum(s, (na[0] - 1) // MOE_PAIR), 0)),
            pl.BlockSpec(memory_space=pl.ANY),
            whole((depth * N_EXPERTS, 2 * D_FF)),
            pl.BlockSpec(memory_space=pl.ANY),
            whole((depth * N_EXPERTS, d)),
        ],
        out_specs=pl.BlockSpec((rows, d), lambda s, *_: (s, 0)),
        scratch_shapes=[pltpu.VMEM((2, d, 2 * D_FF), f32), pltpu.VMEM((2, D_FF, d), f32),
                        pltpu.VMEM((2, d, 2 * D_FF), bf16), pltpu.VMEM((2, D_FF, d), bf16),
                        pltpu.SemaphoreType.DMA((2, 2))],
    )
    return pl.pallas_call(
        functools.partial(_moe_kernel, layer=l),
        grid_spec=grid_spec,
        out_shape=jax.ShapeDtypeStruct((r, d), f32),
        compiler_params=_cparams(("arbitrary",)),
        name="moe_experts",
    )(blk_e, nact, first, seg, nxt, xg, w_up, b_up.reshape(depth * N_EXPERTS, 2 * D_FF),
      w_down, b_down.reshape(depth * N_EXPERTS, d))


def _combine_ln_kernel(pos_cur_ref, pos_nxt_ref, y_hbm, gate_ref, x_ref, g_ref, b_ref, o_ref, gbuf, sem,
                       *, nt):
    i = pl.program_id(0)
    tm = x_ref.shape[0]

    def gather(pos_ref, slot):
        def body(r, carry):
            for k in range(TOP_K):
                p = pos_ref[0, 0, r * TOP_K + k]
                pltpu.make_async_copy(y_hbm.at[pl.ds(p, 1)], gbuf.at[slot, k, pl.ds(r, 1)],
                                      sem.at[slot]).start(priority=k % 2)
            return carry
        lax.fori_loop(0, tm, body, 0, unroll=4)

    @pl.when(i == 0)
    def _():
        gather(pos_cur_ref, 0)

    @pl.when(i + 1 < nt)
    def _():
        gather(pos_nxt_ref, (i + 1) % 2)

    slot = i % 2
    pltpu.make_async_copy(gbuf.at[slot], gbuf.at[slot], sem.at[slot]).wait()
    gates = gate_ref[...]
    acc = ALPHA * x_ref[...]
    for k in range(TOP_K):
        acc = acc + gates[:, k:k + 1] * gbuf[slot, k]
    o_ref[...] = _layernorm(acc, g_ref[...], b_ref[...])


def _combine_ln(y, pos3, blk0, gates, x, g, b, tm):
    rows, d = x.shape
    nt = rows // tm
    grid_spec = pltpu.PrefetchScalarGridSpec(
        num_scalar_prefetch=0,
        grid=(nt,),
        in_specs=[
            pl.BlockSpec((1, 1, tm * TOP_K), lambda i: (blk0 + i, 0, 0), memory_space=pltpu.SMEM),
            pl.BlockSpec((1, 1, tm * TOP_K), lambda i: (blk0 + jnp.minimum(i + 1, nt - 1), 0, 0),
                         memory_space=pltpu.SMEM),
            pl.BlockSpec(memory_space=pl.ANY),
            pl.BlockSpec((tm, LANES), lambda i: (i, 0)),
            pl.BlockSpec((tm, d), lambda i: (i, 0)),
            pl.BlockSpec((1, d), lambda i: (0, 0)),
            pl.BlockSpec((1, d), lambda i: (0, 0)),
        ],
        out_specs=pl.BlockSpec((tm, d), lambda i: (i, 0)),
        scratch_shapes=[pltpu.VMEM((2, TOP_K, tm, d), f32), pltpu.SemaphoreType.DMA((2,))],
    )
    return pl.pallas_call(
        functools.partial(_combine_ln_kernel, nt=nt),
        grid_spec=grid_spec,
        out_shape=jax.ShapeDtypeStruct((rows, d), f32),
        compiler_params=_cparams(("arbitrary",)),
        name="combine_ln",
    )(pos3, pos3, y, gates, x, g.reshape(1, d), b.reshape(1, d))


def _moe_ln(xs, top_es, gates, l, w_up, b_up, w_down, b_down, g, b):
    x_p, x_s = xs
    n_p, n_s = x_p.shape[0], x_s.shape[0]
    n = n_p + n_s
    bm = MOE_BM
    nblk = -(-(n * TOP_K) // bm) + N_EXPERTS
    nblk = -(-nblk // MOE_PAIR) * MOE_PAIR
    top_e = jnp.concatenate(top_es, axis=0)
    dest, counts = _slots(top_e, _pick(n, (512, 256, 128, 64)))
    counts = counts[0, :N_EXPERTS]
    padded = (counts + bm - 1) // bm * bm
    ends = jnp.cumsum(padded)
    starts = (ends - padded).astype(i32)
    blk = jnp.arange(nblk, dtype=i32)
    blk_e = jnp.minimum(jnp.sum((ends[None, :] <= (blk * bm)[:, None]).astype(i32), axis=1), N_EXPERTS - 1)
    nact = (ends[-1] // bm).astype(i32).reshape(1)
    first = ((blk < nact[0]) & ((blk == 0) | (blk_e != jnp.roll(blk_e, 1)))).astype(i32)
    seg = jnp.cumsum(first) - 1
    eid = jnp.arange(N_EXPERTS, dtype=i32)
    later_nonempty = (padded[None, :] > 0) & (eid[None, :] > eid[:, None])
    nxt_of_e = jnp.min(jnp.where(later_nonempty, eid[None, :], N_EXPERTS), axis=1)
    nxt = jnp.where(nxt_of_e < N_EXPERTS, nxt_of_e, -1)[blk_e].astype(i32)
    tmc = _pick(math.gcd(n_p, n_s), (256, 128, 64, 32))
    dest3 = dest[:, :TOP_K].reshape(n // tmc, 1, tmc * TOP_K)
    xg = _dispatch(x_p, x_s, dest3, counts, starts, nblk, tmc)
    y = _moe_experts(xg, blk_e, nact, first, seg, nxt, l, w_up, b_up, w_down, b_down)
    return (_combine_ln(y, dest3, 0, gates[0], x_p, g, b, tmc),
            _combine_ln(y, dest3, n_p // tmc, gates[1], x_s, g, b, tmc))


def kernel(x_prompt, x_sample, cache_a_k, cache_a_v, state_b_conv, state_b_h, state_c_s, ab_w_in, ab_rel_bias, ab_conv_w, ab_conv_b, ab_w_rg, ab_b_rg, ab_w_ig, ab_b_ig, ab_lambda, ab_w_out, c_w_in, c_gn_g, c_gn_b, c_w_out, ln1_g, ln1_b, ln2_g, ln2_b, moe_w_router, moe_b_router, moe_w_up, moe_b_up, moe_w_down, moe_b_down):
    bp, tp, d = x_prompt.shape
    bs, ts, _ = x_sample.shape
    assert bp == 1 and tp % RET_TB == 0 and ts <= CHUNK
    n_p, n_s = bp * tp, bs * ts
    xs = (x_prompt.reshape(n_p, d), x_sample.reshape(n_s, d))
    tms = tuple(_pick(m, (512, 256, 128, 64)) for m in (n_p, n_s))
    moe = (moe_w_up, moe_b_up, moe_w_down, moe_b_down)
    router = (moe_w_router, moe_b_router)

    def per_group(fn, *groups):
        return tuple(zip(*[fn(*args) for args in zip(*groups)]))

    j = 0
    w_in = ab_w_in[j].astype(bf16)
    z_p, z_s = (_matmul(x, w_in, tm, w_in.shape[1]) for x, tm in zip(xs, tms))
    att = (_attn_prompt(z_p, ab_rel_bias[j], tp),
           _attn_sample(z_s, cache_a_k[j], cache_a_v[j], ab_rel_bias[j], 0, bs, ts))
    lru_w = (ab_conv_w[j], ab_conv_b[j], _block_diag(ab_w_rg[j]).astype(bf16), ab_b_rg[j],
             _block_diag(ab_w_ig[j]).astype(bf16), ab_b_ig[j], jax.nn.softplus(-ab_lambda[j]))
    rec_p, bc_p, bh_p = _rglru(z_p, jnp.zeros((bp, CONV_W - 1, B_WIDTH), f32), jnp.zeros((bp, B_WIDTH), f32),
                               *lru_w, 0, bp, tp, _pick(tp, (LRU_TB, 128, 64)))
    rec_s, bc_s, bh_s = _rglru(z_s, state_b_conv[j], state_b_h[j], *lru_w, 0, bs, ts, ts)
    w_out = ab_w_out[j].astype(bf16)
    w_out = [w_out[:A_WIDTH], w_out[A_WIDTH:]]
    xs, top_es, gates = per_group(
        lambda a, r, x, tm: _proj_ln([a, r], w_out, x, ln1_g[0], ln1_b[0], *router, 0, tm),
        att, (rec_p, rec_s), xs, tms)
    xs = _moe_ln(xs, top_es, gates, 0, *moe, ln2_g[0], ln2_b[0])

    keep = min(A_PAST_CHUNKS * CHUNK, tp)
    ak_p = z_p[n_p - keep:, A_WIDTH:2 * A_WIDTH].reshape(1, bp, keep, HA, DHA)
    av_p = z_p[n_p - keep:, 2 * A_WIDTH:3 * A_WIDTH].reshape(1, bp, keep, HA, DHA)
    ak_s = z_s[:, A_WIDTH:2 * A_WIDTH].reshape(1, bs, ts, HA, DHA)
    av_s = z_s[:, 2 * A_WIDTH:3 * A_WIDTH].reshape(1, bs, ts, HA, DHA)

    w_in = c_w_in[j].astype(bf16)
    y_p, cs_p = _retention(xs[0], w_in, None, 0, c_gn_g[j], c_gn_b[j], bp, tp, RET_TB, CHUNK)
    y_s, cs_s = _retention(xs[1], w_in, state_c_s[j], PAST_LEN, c_gn_g[j], c_gn_b[j], bs, ts, ts, ts)
    w_out = [c_w_out[j].astype(bf16)]
    xs, top_es, gates = per_group(
        lambda y, x, tm: _proj_ln([y], w_out, x, ln1_g[1], ln1_b[1], *router, 1, tm),
        (y_p, y_s), xs, tms)
    out_p, out_s = _moe_ln(xs, top_es, gates, 1, *moe, ln2_g[1], ln2_b[1])

    return (out_p.reshape(bp, tp, d), out_s.reshape(bs, ts, d),
            ak_p, av_p, bc_p[None], bh_p.reshape(1, bp, B_WIDTH), cs_p[None],
            ak_s, av_s, bc_s[None], bh_s.reshape(1, bs, B_WIDTH), cs_s[None])
```

```python
import functools
import math

import jax
import jax.numpy as jnp
from jax import lax
from jax.experimental import pallas as pl
from jax.experimental.pallas import tpu as pltpu

f32 = jnp.float32
bf16 = jnp.bfloat16
i32 = jnp.int32

DEPTH = 2
PAST_LEN = 1024
CHUNK = 64
A_PAST_CHUNKS = 8
REL_CLIP = 128
HA, DHA = 8, 64
A_WIDTH = HA * DHA
B_WIDTH = 512
HB = 8
CONV_W = 4
LRU_C = 8.0
HC, DKC, DVC = 4, 256, 512
C_QK, C_V = HC * DKC, HC * DVC
ROPE_BASE = 10000.0
GN_EPS = 1e-5
N_EXPERTS = 32
TOP_K = 4
D_FF = 1024
SWIGLU_LIMIT = 7.0
SWIGLU_ALPHA = 1.702
ALPHA = (2.0 * DEPTH) ** 0.25
LN_EPS = 1e-5
NEG = -1e30

LANES = 128
VMEM_LIMIT = 56 * 1024 * 1024
ATT_TQ = 256
MOE_BM = 256
MOE_BM_SHIFT = MOE_BM.bit_length() - 1
MOE_PAIR = 2
MOE_CH = 8
RET_TB = 512
LRU_TB = 256


def _pick(n, cands):
    for c in cands:
        if n % c == 0:
            return c
    raise ValueError(f"no tile for {n} in {cands}")


def _cparams(sem):
    return pltpu.CompilerParams(dimension_semantics=sem, vmem_limit_bytes=VMEM_LIMIT)


def _layernorm(v, g, b):
    mu = jnp.mean(v, axis=-1, keepdims=True)
    d = v - mu
    var = jnp.mean(d * d, axis=-1, keepdims=True)
    return d * lax.rsqrt(var + LN_EPS) * g + b


def _round_up_bm(v):
    return lax.shift_left(lax.shift_right_logical(v + (MOE_BM - 1), MOE_BM_SHIFT), MOE_BM_SHIFT)


def _rel_bias_matrix(table, nq, nk, d0):
    p = nq + nk - 1
    j = jnp.arange(p)
    u = table[:, jnp.clip(d0 + nq - 1 - j, -REL_CLIP, REL_CLIP) + REL_CLIP]
    u = jnp.roll(u, -(nq - 1), axis=1)
    flat = jnp.tile(u, (1, nq))[:, :nq * (p - 1)]
    return flat.reshape(table.shape[0], nq, p - 1)[:, :, :nk].astype(f32)


def _mm_kernel(x_ref, w_ref, o_ref):
    o_ref[...] = jnp.dot(x_ref[...].astype(bf16), w_ref[...], preferred_element_type=f32)


def _matmul(x, w, tm, tn):
    m, k = x.shape
    n = w.shape[1]
    return pl.pallas_call(
        _mm_kernel,
        grid=(m // tm, n // tn),
        in_specs=[pl.BlockSpec((tm, k), lambda i, j: (i, 0)),
                  pl.BlockSpec((k, tn), lambda i, j: (0, j))],
        out_specs=pl.BlockSpec((tm, tn), lambda i, j: (i, j)),
        out_shape=jax.ShapeDtypeStruct((m, n), f32),
        compiler_params=_cparams(("parallel", "parallel")),
        name="dense_proj",
    )(x, w)


def _route_top_k(x, w, b):
    logits = jnp.dot(x.astype(bf16), w.astype(bf16), preferred_element_type=f32) + b
    tm = logits.shape[0]
    lane = lax.broadcasted_iota(i32, (tm, N_EXPERTS), 1)
    out_lane = lax.broadcasted_iota(i32, (tm, LANES), 1)
    e_out = jnp.zeros((tm, LANES), i32)
    g_out = jnp.zeros((tm, LANES), f32)
    v0 = None
    den = jnp.zeros((tm, 1), f32)
    for k in range(TOP_K):
        v = jnp.max(logits, axis=-1, keepdims=True)
        idx = jnp.min(jnp.where(logits == v, lane, N_EXPERTS), axis=-1, keepdims=True)
        logits = jnp.where(lane == idx, -jnp.inf, logits)
        if k == 0:
            v0 = v
        p = jnp.exp(v - v0)
        den = den + p
        e_out = jnp.where(out_lane == k, idx, e_out)
        g_out = jnp.where(out_lane == k, p, g_out)
    return e_out, g_out / den


def _proj_ln_kernel(*refs, n_in):
    a_refs = refs[:n_in]
    w_refs = refs[n_in:2 * n_in]
    x_ref, g_ref, b_ref, wr_ref, br_ref, o_ref, e_ref, gate_ref, cnt_ref = refs[2 * n_in:]
    acc = ALPHA * x_ref[...]
    for a_ref, w_ref in zip(a_refs, w_refs):
        acc = acc + jnp.dot(a_ref[...].astype(bf16), w_ref[...], preferred_element_type=f32)
    y = _layernorm(acc, g_ref[...], b_ref[...])
    o_ref[...] = y
    e_out, gates = _route_top_k(y, wr_ref[0], br_ref[0])
    e_ref[...] = e_out
    gate_ref[...] = gates
    nsub = cnt_ref.shape[0]
    td = e_out.shape[0] // nsub
    lane = lax.broadcasted_iota(i32, (td, LANES), 1)
    for u in range(nsub):
        eu = e_out[u * td:(u + 1) * td]
        cnt = jnp.zeros((1, LANES), f32)
        for k in range(TOP_K):
            cnt = cnt + jnp.sum((lane == eu[:, k:k + 1]).astype(f32), axis=0, keepdims=True)
        cnt_ref[u] = jnp.broadcast_to(cnt, cnt_ref.shape[1:])


def _proj_ln(acts, ws, x, g, b, w_router, b_router, l, tm, td):
    m, d = x.shape
    n_in = len(acts)
    depth = w_router.shape[0]
    nt = m // tm
    nsub = tm // td
    cur = lambda i: (i, 0)
    in_specs = ([pl.BlockSpec((tm, a.shape[1]), cur) for a in acts]
                + [pl.BlockSpec(w.shape, lambda i: (0, 0)) for w in ws]
                + [pl.BlockSpec((tm, d), cur),
                   pl.BlockSpec((1, d), lambda i: (0, 0)),
                   pl.BlockSpec((1, d), lambda i: (0, 0)),
                   pl.BlockSpec((1, d, N_EXPERTS), lambda i: (l, 0, 0)),
                   pl.BlockSpec((1, 1, N_EXPERTS), lambda i: (l, 0, 0))])
    return pl.pallas_call(
        functools.partial(_proj_ln_kernel, n_in=n_in),
        grid=(nt,),
        in_specs=in_specs,
        out_specs=[pl.BlockSpec((tm, d), cur),
                   pl.BlockSpec((tm, LANES), cur),
                   pl.BlockSpec((tm, LANES), cur),
                   pl.BlockSpec((nsub, 8, LANES), lambda i: (i, 0, 0))],
        out_shape=[jax.ShapeDtypeStruct((m, d), f32),
                   jax.ShapeDtypeStruct((m, LANES), i32),
                   jax.ShapeDtypeStruct((m, LANES), f32),
                   jax.ShapeDtypeStruct((m // td, 8, LANES), f32)],
        compiler_params=_cparams(("parallel",)),
        name="proj_ln",
    )(*acts, *ws, x, g.reshape(1, d), b.reshape(1, d), w_router, b_router.reshape(depth, 1, N_EXPERTS))


def _attn_prompt_kernel(q_ref, k0_ref, k1_ref, k2_ref, v0_ref, v1_ref, v2_ref, bias_ref, o_ref):
    i = pl.program_id(0)
    tq = q_ref.shape[0]
    q = q_ref[...]
    k = jnp.concatenate([k0_ref[...], k1_ref[...], k2_ref[...]], axis=0)
    v = jnp.concatenate([v0_ref[...], v1_ref[...], v2_ref[...]], axis=0)
    kcol = lax.broadcasted_iota(i32, (1, 3 * tq), 1)
    tile_ok = (kcol // tq + i) >= 2
    qb = (q * (DHA ** -0.5)).astype(bf16)
    kb = k.astype(bf16)
    vb = v.astype(bf16)
    outs = []
    for h in range(HA):
        sl = slice(h * DHA, (h + 1) * DHA)
        s = lax.dot_general(qb[:, sl], kb[:, sl], (((1,), (1,)), ((), ())), preferred_element_type=f32)
        s = jnp.where(tile_ok, s + bias_ref[h], NEG)
        p = jnp.exp(s - jnp.max(s, axis=-1, keepdims=True))
        o = jnp.dot(p.astype(bf16), vb[:, sl], preferred_element_type=f32)
        outs.append(o / jnp.sum(p, axis=-1, keepdims=True))
    o_ref[...] = jnp.concatenate(outs, axis=-1)


def _attn_prompt(z, rel_bias, tp):
    tq = ATT_TQ
    nq = tp // tq
    qq = jnp.arange(tq)[:, None]
    kk = jnp.arange(3 * tq)[None, :]
    band = (kk // CHUNK >= qq // CHUNK) & (kk // CHUNK <= qq // CHUNK + A_PAST_CHUNKS)
    bias = jnp.where(band[None], _rel_bias_matrix(rel_bias, tq, 3 * tq, 2 * tq), NEG)

    def kv_spec(back, col):
        return pl.BlockSpec((tq, A_WIDTH), lambda i: (jnp.maximum(i - back, 0), col))

    return pl.pallas_call(
        _attn_prompt_kernel,
        grid=(nq,),
        in_specs=[pl.BlockSpec((tq, A_WIDTH), lambda i: (i, 0)),
                  kv_spec(2, 1), kv_spec(1, 1), kv_spec(0, 1),
                  kv_spec(2, 2), kv_spec(1, 2), kv_spec(0, 2),
                  pl.BlockSpec((HA, tq, 3 * tq), lambda i: (0, 0, 0))],
        out_specs=pl.BlockSpec((tq, A_WIDTH), lambda i: (i, 0)),
        out_shape=jax.ShapeDtypeStruct((tp, A_WIDTH), f32),
        compiler_params=_cparams(("parallel",)),
        name="attn_prompt",
    )(z, z, z, z, z, z, z, bias)


def _attn_sample_kernel(q_ref, kn_ref, vn_ref, kc_ref, vc_ref, bias_ref, o_ref):
    p_len = kc_ref.shape[1]
    q, kn, vn = q_ref[...], kn_ref[...], vn_ref[...]
    kc, vc = kc_ref[0], vc_ref[0]
    outs = []
    for h in range(HA):
        sl = slice(h * DHA, (h + 1) * DHA)
        qh = q[:, sl].astype(bf16)
        dims = (((1,), (1,)), ((), ()))
        sc = lax.dot_general(qh, kc[:, sl].astype(bf16), dims, preferred_element_type=f32)
        sn = lax.dot_general(qh, kn[:, sl].astype(bf16), dims, preferred_element_type=f32)
        b = bias_ref[h]
        sc = sc * (DHA ** -0.5) + b[:, :p_len]
        sn = sn * (DHA ** -0.5) + b[:, p_len:]
        m = jnp.maximum(jnp.max(sc, axis=-1, keepdims=True), jnp.max(sn, axis=-1, keepdims=True))
        pc, pn = jnp.exp(sc - m), jnp.exp(sn - m)
        den = jnp.sum(pc, axis=-1, keepdims=True) + jnp.sum(pn, axis=-1, keepdims=True)
        pc, pn = pc / den, pn / den
        outs.append(jnp.dot(pc.astype(bf16), vc[:, sl].astype(bf16), preferred_element_type=f32)
                    + jnp.dot(pn.astype(bf16), vn[:, sl].astype(bf16), preferred_element_type=f32))
    o_ref[...] = jnp.concatenate(outs, axis=-1)


def _attn_sample(z, k_cache, v_cache, rel_bias, row0, bs, ts):
    p_len = k_cache.shape[1]
    kc = k_cache.reshape(bs, p_len, A_WIDTH)
    vc = v_cache.reshape(bs, p_len, A_WIDTH)
    bias = _rel_bias_matrix(rel_bias, ts, p_len + ts, p_len)
    blk0 = row0 // ts

    def z_spec(col):
        return pl.BlockSpec((ts, A_WIDTH), lambda b: (blk0 + b, col))

    return pl.pallas_call(
        _attn_sample_kernel,
        grid=(bs,),
        in_specs=[z_spec(0), z_spec(1), z_spec(2),
                  pl.BlockSpec((1, p_len, A_WIDTH), lambda b: (b, 0, 0)),
                  pl.BlockSpec((1, p_len, A_WIDTH), lambda b: (b, 0, 0)),
                  pl.BlockSpec((HA, ts, p_len + ts), lambda b: (0, 0, 0))],
        out_specs=pl.BlockSpec((ts, A_WIDTH), lambda b: (b, 0)),
        out_shape=jax.ShapeDtypeStruct((bs * ts, A_WIDTH), f32),
        compiler_params=_cparams(("parallel",)),
        name="attn_sample",
    )(z, z, z, kc, vc, bias)


def _gelu_tanh(x):
    return 0.5 * x * (1.0 + jnp.tanh(math.sqrt(2.0 / math.pi) * (x + 0.044715 * (x * x * x))))


def _rglru_kernel(xb_ref, gb_ref, conv0_ref, h0_ref, cw_ref, cb_ref, wrg_ref, brg_ref,
                  wig_ref, big_ref, sp_ref, rec_ref, convn_ref, hl_ref, xp_ref, hc_ref):
    t = pl.program_id(1)
    tb = xb_ref.shape[0]
    pad = 8

    @pl.when(t == 0)
    def _():
        xp_ref[0:pad, :] = jnp.zeros((pad, B_WIDTH), f32)
        xp_ref[pad - (CONV_W - 1):pad, :] = conv0_ref[0]
        hc_ref[...] = h0_ref[0]

    xb = xb_ref[...]
    xp_ref[pad:pad + tb, :] = xb
    cw = cw_ref[...]
    u = cb_ref[...] + cw[CONV_W - 1:CONV_W, :] * xb
    for j in range(CONV_W - 1):
        sh = CONV_W - 1 - j
        u = u + cw[j:j + 1, :] * xp_ref[pad - sh:pad - sh + tb, :]
    convn_ref[0] = xp_ref[pad + tb - (CONV_W - 1):pad + tb, :]
    xp_ref[0:pad, :] = xp_ref[tb:tb + pad, :]

    ub = u.astype(bf16)
    r = jax.nn.sigmoid(jnp.dot(ub, wrg_ref[...], preferred_element_type=f32) + brg_ref[...])
    ig = jax.nn.sigmoid(jnp.dot(ub, wig_ref[...], preferred_element_type=f32) + big_ref[...])
    log_a = -LRU_C * r * sp_ref[...]
    a = jnp.exp(log_a)
    bt = jnp.sqrt(-jnp.tanh(log_a) * (a * a + 1.0)) * (ig * u)

    row = lax.broadcasted_iota(i32, (tb, 1), 0)
    s = 1
    while s < tb:
        keep = row >= s
        a_sh = pltpu.roll(a, s, axis=0)
        b_sh = pltpu.roll(bt, s, axis=0)
        bt = jnp.where(keep, a * b_sh + bt, bt)
        a = jnp.where(keep, a * a_sh, a)
        s *= 2
    h = a * hc_ref[...] + bt
    hc_ref[...] = h[tb - 1:tb, :]
    hl_ref[0] = h[tb - 1:tb, :]
    rec_ref[...] = h * _gelu_tanh(gb_ref[...])


def _rglru(z, conv0, h0, cw, cb, wrg, brg, wig, big, sp, row0, bsz, t_len, tb):
    nt = t_len // tb
    blk0 = row0 // tb

    def z_spec(col):
        return pl.BlockSpec((tb, B_WIDTH), lambda b, t: (blk0 + b * nt + t, col))

    def const(shape):
        return pl.BlockSpec(shape, lambda b, t: (0,) * len(shape))

    row = lambda v: v.reshape(1, B_WIDTH)
    return pl.pallas_call(
        _rglru_kernel,
        grid=(bsz, nt),
        in_specs=[z_spec(3), z_spec(4),
                  pl.BlockSpec((1, CONV_W - 1, B_WIDTH), lambda b, t: (b, 0, 0)),
                  pl.BlockSpec((1, 1, B_WIDTH), lambda b, t: (b, 0, 0)),
                  const((CONV_W, B_WIDTH)), const((1, B_WIDTH)),
                  const((B_WIDTH, B_WIDTH)), const((1, B_WIDTH)),
                  const((B_WIDTH, B_WIDTH)), const((1, B_WIDTH)), const((1, B_WIDTH))],
        out_specs=[pl.BlockSpec((tb, B_WIDTH), lambda b, t: (b * nt + t, 0)),
                   pl.BlockSpec((1, CONV_W - 1, B_WIDTH), lambda b, t: (b, 0, 0)),
                   pl.BlockSpec((1, 1, B_WIDTH), lambda b, t: (b, 0, 0))],
        out_shape=[jax.ShapeDtypeStruct((bsz * t_len, B_WIDTH), f32),
                   jax.ShapeDtypeStruct((bsz, CONV_W - 1, B_WIDTH), f32),
                   jax.ShapeDtypeStruct((bsz, 1, B_WIDTH), f32)],
        scratch_shapes=[pltpu.VMEM((tb + 8, B_WIDTH), f32), pltpu.VMEM((1, B_WIDTH), f32)],
        compiler_params=_cparams(("parallel", "arbitrary")),
        name="rglru",
    )(z, z, conv0, h0.reshape(bsz, 1, B_WIDTH), cw, row(cb), wrg, row(brg), wig, row(big), row(sp))


def _block_diag(w):
    hb, d, _ = w.shape
    eye = jnp.eye(hb, dtype=w.dtype)
    return (eye[:, None, :, None] * w[:, :, None, :]).reshape(hb * d, hb * d)


def _retention_kernel(x_ref, w_hbm, cos_ref, sin_ref, s0_hbm, gng_ref, gnb_ref,
                      y_ref, sout_hbm, w_ref, s_ref, z_ref, sem, *, chunk, nt, tb, zero_state, hoist):
    b = pl.program_id(0)
    t = pl.program_id(1)
    half = DKC // 2

    @pl.when((b == 0) & (t == 0))
    def _():
        cp = pltpu.make_async_copy(w_hbm, w_ref, sem.at[0])
        cp.start()
        cp.wait()
        if hoist:
            xa = x_ref[...].astype(bf16)
            for c in range(0, w_ref.shape[1], C_QK):
                z_ref[:, c:c + C_QK] = jnp.dot(xa, w_ref[:, c:c + C_QK], preferred_element_type=f32)

    @pl.when(t == 0)
    def _():
        if zero_state:
            s_ref[...] = jnp.zeros(s_ref.shape, f32)
        else:
            cp = pltpu.make_async_copy(s0_hbm.at[b], s_ref, sem.at[1])
            cp.start()
            cp.wait()

    cos, sin = cos_ref[...], sin_ref[...]
    n = lax.broadcasted_iota(i32, (tb, 1), 0)
    m = lax.broadcasted_iota(i32, (1, tb), 1)
    dist = n - m
    same = (n // chunk) == (m // chunk)
    expo = jnp.where(same, jnp.abs(dist), dist).astype(f32)
    visible = same | (dist > 0)
    nf = n.astype(f32)

    if hoist:
        row0 = pl.multiple_of(b * tb, tb)

        def proj(col, width):
            return z_ref[pl.ds(row0, tb), col:col + width]
    else:
        xb = x_ref[...].astype(bf16)

        def proj(col, width):
            return jnp.dot(xb, w_ref[:, col:col + width], preferred_element_type=f32)

    def rot(v, scale):
        x1, x2 = v[:, :half], v[:, half:]
        return jnp.concatenate([x1 * cos - x2 * sin, x2 * cos + x1 * sin], axis=-1) * scale

    for h in range(HC):
        lg = math.log(1.0 - 2.0 ** (-5.0 - h))
        qr = rot(proj(h * DKC, DKC), 1.0)
        kr = rot(proj(C_QK + h * DKC, DKC), DKC ** -0.5)
        vb = proj(2 * C_QK + h * DVC, DVC).astype(bf16)
        gate = proj(2 * C_QK + C_V + h * DVC, DVC)
        qb = qr.astype(bf16)
        dmat = jnp.where(visible, jnp.exp(lg * expo), 0.0)
        s = lax.dot_general(qb, kr.astype(bf16), (((1,), (1,)), ((), ())),
                            preferred_element_type=f32) * dmat
        s_old = s_ref[h]
        o = jnp.dot(s.astype(bf16), vb, preferred_element_type=f32)
        o = o + jnp.dot(qb, s_old.astype(bf16), preferred_element_type=f32) * jnp.exp(lg * (nf + 1.0))
        kd = (kr * jnp.exp(lg * (tb - 1.0 - nf))).astype(bf16)
        s_ref[h] = math.exp(lg * tb) * s_old + lax.dot_general(
            kd, vb, (((0,), (0,)), ((), ())), preferred_element_type=f32)

        mu = jnp.mean(o, axis=-1, keepdims=True)
        dlt = o - mu
        var = jnp.mean(dlt * dlt, axis=-1, keepdims=True)
        vs = slice(h * DVC, (h + 1) * DVC)
        yn = dlt * lax.rsqrt(var + GN_EPS) * gng_ref[:, vs] + gnb_ref[:, vs]
        y_ref[:, vs] = (jax.nn.silu(gate) * yn).astype(y_ref.dtype)

    @pl.when(t == nt - 1)
    def _():
        cp = pltpu.make_async_copy(s_ref, sout_hbm.at[b], sem.at[1])
        cp.start()
        cp.wait()


def _retention(x, w_in, s0, pos0, gn_g, gn_b, bsz, t_len, tb, chunk):
    nt = t_len // tb
    d = x.shape[1]
    half = DKC // 2
    inv = 1.0 / (ROPE_BASE ** (jnp.arange(half, dtype=f32) / half))
    ang = (pos0 + jnp.arange(t_len)).astype(f32)[:, None] * inv[None, :]
    cos, sin = jnp.cos(ang), jnp.sin(ang)
    zero_state = s0 is None
    if zero_state:
        s0 = jnp.zeros((1, 8, LANES), f32)
    hoist = nt == 1 and bsz > 1
    x_spec = (pl.BlockSpec(x.shape, lambda b, t: (0, 0)) if hoist
              else pl.BlockSpec((tb, d), lambda b, t: (b * nt + t, 0)))
    z_shape = (bsz * tb, w_in.shape[1]) if hoist else (8, LANES)
    return pl.pallas_call(
        functools.partial(_retention_kernel, chunk=chunk, nt=nt, tb=tb, zero_state=zero_state, hoist=hoist),
        grid=(bsz, nt),
        in_specs=[x_spec,
                  pl.BlockSpec(memory_space=pl.ANY),
                  pl.BlockSpec((tb, half), lambda b, t: (t, 0)),
                  pl.BlockSpec((tb, half), lambda b, t: (t, 0)),
                  pl.BlockSpec(memory_space=pl.ANY),
                  pl.BlockSpec((1, C_V), lambda b, t: (0, 0)),
                  pl.BlockSpec((1, C_V), lambda b, t: (0, 0))],
        out_specs=[pl.BlockSpec((tb, C_V), lambda b, t: (b * nt + t, 0)),
                   pl.BlockSpec(memory_space=pl.ANY)],
        out_shape=[jax.ShapeDtypeStruct((bsz * t_len, C_V), bf16),
                   jax.ShapeDtypeStruct((bsz, HC, DKC, DVC), f32)],
        scratch_shapes=[pltpu.VMEM(w_in.shape, bf16), pltpu.VMEM((HC, DKC, DVC), f32),
                        pltpu.VMEM(z_shape, f32), pltpu.SemaphoreType.DMA((2,))],
        compiler_params=_cparams(("arbitrary", "arbitrary")),
        name="retention",
    )(x, w_in, cos, sin, s0, gn_g.reshape(1, C_V), gn_b.reshape(1, C_V))


def _dispatch_kernel(ech_ref, est_ref, tab_ref, tabv_ref, ep_ref, es_ref, xp_ref, xs_ref,
                     xg_hbm, dest_ref, sbuf, zchunk, pending, sem, *, np_tiles, nt):
    i = pl.program_id(0)
    td = xp_ref.shape[0]
    ls = sbuf.shape[1]
    slot = i % 2
    nchunks = xg_hbm.shape[0]
    blk_chunks = MOE_BM // MOE_CH

    def chunk_wait(s):
        pltpu.make_async_copy(zchunk, xg_hbm.at[0], sem.at[s]).wait()

    def drain(s):
        def body(c, carry):
            chunk_wait(s)
            return carry
        lax.fori_loop(0, pending[s], body, 0)

    @pl.when(i == 0)
    def _():
        pending[0] = 0
        pending[1] = 0
        zchunk[...] = jnp.zeros(zchunk.shape, f32)

        def zero_chunks(lo, hi):
            def put(c, carry):
                pltpu.make_async_copy(zchunk, xg_hbm.at[c], sem.at[2]).start()
                return carry
            lax.fori_loop(lo, hi, put, 0)

            def done(c, carry):
                chunk_wait(2)
                return carry
            lax.fori_loop(lo, hi, done, 0)

        def per_expert(e, carry):
            used = est_ref[e] + ech_ref[e]
            zero_chunks(used, used + jnp.bitwise_and(-ech_ref[e], blk_chunks - 1))
            return carry
        lax.fori_loop(0, N_EXPERTS, per_expert, 0)
        last = N_EXPERTS - 1
        end = est_ref[last] + ech_ref[last] + jnp.bitwise_and(-ech_ref[last], blk_chunks - 1)
        zero_chunks(end, nchunks)

    def sort_and_copy(e_ref, x_ref):
        e = e_ref[...]
        lane = lax.broadcasted_iota(i32, (td, LANES), 1)
        member = jnp.zeros((td, LANES), f32)
        for k in range(TOP_K):
            member = member + (lane == e[:, k:k + 1]).astype(f32)
        r = lax.broadcasted_iota(i32, (td, td), 0)
        c = lax.broadcasted_iota(i32, (td, td), 1)
        before = jnp.dot((c < r).astype(bf16), member.astype(bf16), preferred_element_type=f32)
        in_tile = before + tabv_ref[0, 0:1, :]
        in_all = before + tabv_ref[0, 1:2, :]
        col = lax.broadcasted_iota(i32, (td, ls), 1)
        place = jnp.zeros((td, ls), f32)
        dest = jnp.zeros((td, LANES), i32)
        for k in range(TOP_K):
            sel = lane == e[:, k:k + 1]
            row_k = jnp.sum(jnp.where(sel, in_tile, 0.0), axis=-1, keepdims=True).astype(i32)
            dst_k = jnp.sum(jnp.where(sel, in_all, 0.0), axis=-1, keepdims=True).astype(i32)
            place = place + (col == row_k).astype(f32)
            dest = jnp.where(lane == k, dst_k, dest)
        dest_ref[...] = dest
        srt = lax.dot_general(place.astype(bf16), x_ref[...].astype(bf16), (((0,), (0,)), ((), ())),
                              preferred_element_type=f32)
        drain(slot)
        sbuf[slot] = srt

        def per_expert(ei, total):
            dst0 = tab_ref[0, 0, ei]
            n = tab_ref[0, 0, N_EXPERTS + ei]
            src0 = tab_ref[0, 0, 2 * N_EXPERTS + ei]

            def put(cc, carry):
                row = pl.multiple_of((src0 + cc) * MOE_CH, MOE_CH)
                pltpu.make_async_copy(sbuf.at[slot, pl.ds(row, MOE_CH)], xg_hbm.at[dst0 + cc],
                                      sem.at[slot]).start()
                return carry
            lax.fori_loop(0, n, put, 0)
            return total + n
        pending[slot] = lax.fori_loop(0, N_EXPERTS, per_expert, 0)

    @pl.when(i < np_tiles)
    def _():
        sort_and_copy(ep_ref, xp_ref)

    @pl.when(i >= np_tiles)
    def _():
        sort_and_copy(es_ref, xs_ref)

    @pl.when(i == nt - 1)
    def _():
        drain(0)
        drain(1)


def _dispatch(xs, top_es, tab, tabv, ech, est, nblk, td):
    x_p, x_s = xs
    d = x_p.shape[1]
    np_tiles, ns_tiles = x_p.shape[0] // td, x_s.shape[0] // td
    nt = np_tiles + ns_tiles
    ls = td * TOP_K + N_EXPERTS * MOE_CH
    pspec = lambda w: pl.BlockSpec((td, w), lambda i, *_: (jnp.minimum(i, np_tiles - 1), 0))
    sspec = lambda w: pl.BlockSpec((td, w), lambda i, *_: (jnp.maximum(i - np_tiles, 0), 0))
    grid_spec = pltpu.PrefetchScalarGridSpec(
        num_scalar_prefetch=2,
        grid=(nt,),
        in_specs=[pl.BlockSpec((1, 1, LANES), lambda i, *_: (i, 0, 0), memory_space=pltpu.SMEM),
                  pl.BlockSpec((1, 8, LANES), lambda i, *_: (i, 0, 0)),
                  pspec(LANES), sspec(LANES), pspec(d), sspec(d)],
        out_specs=[pl.BlockSpec(memory_space=pl.ANY),
                   pl.BlockSpec((td, LANES), lambda i, *_: (i, 0))],
        scratch_shapes=[pltpu.VMEM((2, ls, d), f32), pltpu.VMEM((MOE_CH, d), f32),
                        pltpu.SMEM((2,), i32), pltpu.SemaphoreType.DMA((3,))],
    )
    xg, dest = pl.pallas_call(
        functools.partial(_dispatch_kernel, np_tiles=np_tiles, nt=nt),
        grid_spec=grid_spec,
        out_shape=[jax.ShapeDtypeStruct((nblk * MOE_BM // MOE_CH, MOE_CH, d), f32),
                   jax.ShapeDtypeStruct((nt * td, LANES), i32)],
        compiler_params=_cparams(("arbitrary",)),
        name="moe_dispatch",
    )(ech, est, tab, tabv, top_es[0], top_es[1], x_p, x_s)
    return xg.reshape(nblk * MOE_BM, d), dest


def _moe_kernel(blk_e_ref, nact_ref, first_ref, seg_ref, nxt_ref, x_ref, wup_hbm, bup_ref, wdn_hbm, bdn_ref,
                y_ref, wup_f, wdn_f, wup_bf, wdn_bf, sem, *, layer):
    b0 = pl.program_id(0) * MOE_PAIR
    nact = nact_ref[0]
    bm = MOE_BM

    def weight_copies(e, slot):
        return (pltpu.make_async_copy(wup_hbm.at[layer, e], wup_f.at[slot], sem.at[slot, 0]),
                pltpu.make_async_copy(wdn_hbm.at[layer, e], wdn_f.at[slot], sem.at[slot, 1]))

    @pl.when(b0 < nact)
    def _():
        for sub in range(MOE_PAIR):
            b = b0 + sub

            @pl.when(first_ref[b] == 1)
            def _():
                slot = seg_ref[b] % 2

                @pl.when(b == 0)
                def _():
                    for c in weight_copies(blk_e_ref[0], 0):
                        c.start()

                for c in weight_copies(blk_e_ref[b], slot):
                    c.wait()

                @pl.when(nxt_ref[b] >= 0)
                def _():
                    for c in weight_copies(nxt_ref[b], 1 - slot):
                        c.start()

                wup_bf[slot] = wup_f[slot].astype(bf16)
                wdn_bf[slot] = wdn_f[slot].astype(bf16)

        for sub in range(MOE_PAIR):
            b = b0 + sub
            slot = seg_ref[b] % 2
            e = layer * N_EXPERTS + blk_e_ref[b]
            rows = slice(sub * bm, (sub + 1) * bm)
            hdn = jnp.dot(x_ref[rows, :].astype(bf16), wup_bf[slot], preferred_element_type=f32)
            hdn = hdn + bup_ref[pl.ds(e, 1), :]
            glu = jnp.minimum(hdn[:, :D_FF], SWIGLU_LIMIT)
            lin = jnp.clip(hdn[:, D_FF:], -SWIGLU_LIMIT, SWIGLU_LIMIT)
            act = glu * jax.nn.sigmoid(SWIGLU_ALPHA * glu) * (lin + 1.0)
            y_ref[rows, :] = (jnp.dot(act.astype(bf16), wdn_bf[slot], preferred_element_type=f32)
                              + bdn_ref[pl.ds(e, 1), :])

    @pl.when(b0 >= nact)
    def _():
        y_ref[...] = jnp.zeros(y_ref.shape, f32)


def _moe_experts(xg, blk_e, nact, first, seg, nxt, l, w_up, b_up, w_down, b_down):
    r, d = xg.shape
    rows = MOE_BM * MOE_PAIR
    depth = w_up.shape[0]
    whole = lambda shape: pl.BlockSpec(shape, lambda s, *_: (0,) * len(shape))
    grid_spec = pltpu.PrefetchScalarGridSpec(
        num_scalar_prefetch=5,
        grid=(r // rows,),
        in_specs=[
            pl.BlockSpec((rows, d), lambda s, be, na, *_: (jnp.minimum(s, (na[0] - 1) // MOE_PAIR), 0)),
            pl.BlockSpec(memory_space=pl.ANY),
            whole((depth * N_EXPERTS, 2 * D_FF)),
            pl.BlockSpec(memory_space=pl.ANY),
            whole((depth * N_EXPERTS, d)),
        ],
        out_specs=pl.BlockSpec((rows, d), lambda s, *_: (s, 0)),
        scratch_shapes=[pltpu.VMEM((2, d, 2 * D_FF), f32), pltpu.VMEM((2, D_FF, d), f32),
                        pltpu.VMEM((2, d, 2 * D_FF), bf16), pltpu.VMEM((2, D_FF, d), bf16),
                        pltpu.SemaphoreType.DMA((2, 2))],
    )
    return pl.pallas_call(
        functools.partial(_moe_kernel, layer=l),
        grid_spec=grid_spec,
        out_shape=jax.ShapeDtypeStruct((r, d), f32),
        compiler_params=_cparams(("arbitrary",)),
        name="moe_experts",
    )(blk_e, nact, first, seg, nxt, xg, w_up, b_up.reshape(depth * N_EXPERTS, 2 * D_FF),
      w_down, b_down.reshape(depth * N_EXPERTS, d))


def _combine_ln_kernel(pos_cur_ref, pos_nxt_ref, y_hbm, gate_ref, x_ref, g_ref, b_ref, o_ref, gbuf, sem,
                       *, nt):
    i = pl.program_id(0)
    tm = x_ref.shape[0]

    def gather(pos_ref, slot):
        def body(r, carry):
            for k in range(TOP_K):
                p = pos_ref[0, 0, r * TOP_K + k]
                pltpu.make_async_copy(y_hbm.at[pl.ds(p, 1)], gbuf.at[slot, k, pl.ds(r, 1)],
                                      sem.at[slot]).start(priority=k % 2)
            return carry
        lax.fori_loop(0, tm, body, 0, unroll=4)

    @pl.when(i == 0)
    def _():
        gather(pos_cur_ref, 0)

    @pl.when(i + 1 < nt)
    def _():
        gather(pos_nxt_ref, (i + 1) % 2)

    slot = i % 2
    pltpu.make_async_copy(gbuf.at[slot], gbuf.at[slot], sem.at[slot]).wait()
    gates = gate_ref[...]
    acc = ALPHA * x_ref[...]
    for k in range(TOP_K):
        acc = acc + gates[:, k:k + 1] * gbuf[slot, k]
    o_ref[...] = _layernorm(acc, g_ref[...], b_ref[...])


def _combine_ln(y, pos3, blk0, gates, x, g, b, tm):
    rows, d = x.shape
    nt = rows // tm
    grid_spec = pltpu.PrefetchScalarGridSpec(
        num_scalar_prefetch=0,
        grid=(nt,),
        in_specs=[
            pl.BlockSpec((1, 1, tm * TOP_K), lambda i: (blk0 + i, 0, 0), memory_space=pltpu.SMEM),
            pl.BlockSpec((1, 1, tm * TOP_K), lambda i: (blk0 + jnp.minimum(i + 1, nt - 1), 0, 0),
                         memory_space=pltpu.SMEM),
            pl.BlockSpec(memory_space=pl.ANY),
            pl.BlockSpec((tm, LANES), lambda i: (i, 0)),
            pl.BlockSpec((tm, d), lambda i: (i, 0)),
            pl.BlockSpec((1, d), lambda i: (0, 0)),
            pl.BlockSpec((1, d), lambda i: (0, 0)),
        ],
        out_specs=pl.BlockSpec((tm, d), lambda i: (i, 0)),
        scratch_shapes=[pltpu.VMEM((2, TOP_K, tm, d), f32), pltpu.SemaphoreType.DMA((2,))],
    )
    return pl.pallas_call(
        functools.partial(_combine_ln_kernel, nt=nt),
        grid_spec=grid_spec,
        out_shape=jax.ShapeDtypeStruct((rows, d), f32),
        compiler_params=_cparams(("arbitrary",)),
        name="combine_ln",
    )(pos3, pos3, y, gates, x, g.reshape(1, d), b.reshape(1, d))


def _moe_ln(xs, top_es, gates, cnts, td, l, w_up, b_up, w_down, b_down, g, b):
    x_p, x_s = xs
    n_p, n_s = x_p.shape[0], x_s.shape[0]
    n = n_p + n_s
    bm = MOE_BM
    nt = n // td
    nblk = -(-(n * TOP_K + nt * N_EXPERTS * (MOE_CH - 1)) // bm) + N_EXPERTS
    nblk = -(-nblk // MOE_PAIR) * MOE_PAIR
    cnt = jnp.concatenate([c[:, 0, :N_EXPERTS] for c in cnts], axis=0).astype(i32)
    run = (cnt + MOE_CH - 1) // MOE_CH
    in_tile = jnp.cumsum(run, axis=1) - run
    ech = jnp.sum(run, axis=0)
    blk_chunks = bm // MOE_CH
    padded_ch = (ech + blk_chunks - 1) // blk_chunks * blk_chunks
    est = jnp.cumsum(padded_ch) - padded_ch
    in_all = est[None, :] + jnp.cumsum(run, axis=0) - run
    tab = jnp.concatenate([in_all, run, in_tile, jnp.zeros_like(run)], axis=1).reshape(nt, 1, LANES)
    tabv = jnp.zeros((nt, 8, LANES), f32)
    tabv = tabv.at[:, 0, :N_EXPERTS].set((in_tile * MOE_CH).astype(f32))
    tabv = tabv.at[:, 1, :N_EXPERTS].set((in_all * MOE_CH).astype(f32))
    xg, dest = _dispatch(xs, top_es, tab, tabv, ech, est, nblk, td)
    padded = padded_ch * MOE_CH
    ends = jnp.cumsum(padded)
    blk = jnp.arange(nblk, dtype=i32)
    blk_e = jnp.minimum(jnp.sum((ends[None, :] <= (blk * bm)[:, None]).astype(i32), axis=1), N_EXPERTS - 1)
    nact = (ends[-1] // bm).astype(i32).reshape(1)
    first = ((blk < nact[0]) & ((blk == 0) | (blk_e != jnp.roll(blk_e, 1)))).astype(i32)
    seg = jnp.cumsum(first) - 1
    eid = jnp.arange(N_EXPERTS, dtype=i32)
    later_nonempty = (padded[None, :] > 0) & (eid[None, :] > eid[:, None])
    nxt_of_e = jnp.min(jnp.where(later_nonempty, eid[None, :], N_EXPERTS), axis=1)
    nxt = jnp.where(nxt_of_e < N_EXPERTS, nxt_of_e, -1)[blk_e].astype(i32)
    dest3 = dest[:, :TOP_K].reshape(nt, 1, td * TOP_K)
    y = _moe_experts(xg, blk_e, nact, first, seg, nxt, l, w_up, b_up, w_down, b_down)
    return (_combine_ln(y, dest3, 0, gates[0], x_p, g, b, td),
            _combine_ln(y, dest3, n_p // td, gates[1], x_s, g, b, td))


def kernel(x_prompt, x_sample, cache_a_k, cache_a_v, state_b_conv, state_b_h, state_c_s, ab_w_in, ab_rel_bias, ab_conv_w, ab_conv_b, ab_w_rg, ab_b_rg, ab_w_ig, ab_b_ig, ab_lambda, ab_w_out, c_w_in, c_gn_g, c_gn_b, c_w_out, ln1_g, ln1_b, ln2_g, ln2_b, moe_w_router, moe_b_router, moe_w_up, moe_b_up, moe_w_down, moe_b_down):
    bp, tp, d = x_prompt.shape
    bs, ts, _ = x_sample.shape
    assert bp == 1 and tp % RET_TB == 0 and ts <= CHUNK
    n_p, n_s = bp * tp, bs * ts
    xs = (x_prompt.reshape(n_p, d), x_sample.reshape(n_s, d))
    tms = tuple(_pick(m, (512, 256, 128, 64)) for m in (n_p, n_s))
    td = _pick(math.gcd(n_p, n_s), (256, 128, 64, 32))
    moe = (moe_w_up, moe_b_up, moe_w_down, moe_b_down)
    router = (moe_w_router, moe_b_router)

    def per_group(fn, *groups):
        return tuple(zip(*[fn(*args) for args in zip(*groups)]))

    j = 0
    w_in = ab_w_in[j].astype(bf16)
    z_p, z_s = (_matmul(x, w_in, tm, w_in.shape[1]) for x, tm in zip(xs, tms))
    att = (_attn_prompt(z_p, ab_rel_bias[j], tp),
           _attn_sample(z_s, cache_a_k[j], cache_a_v[j], ab_rel_bias[j], 0, bs, ts))
    lru_w = (ab_conv_w[j], ab_conv_b[j], _block_diag(ab_w_rg[j]).astype(bf16), ab_b_rg[j],
             _block_diag(ab_w_ig[j]).astype(bf16), ab_b_ig[j], jax.nn.softplus(-ab_lambda[j]))
    rec_p, bc_p, bh_p = _rglru(z_p, jnp.zeros((bp, CONV_W - 1, B_WIDTH), f32), jnp.zeros((bp, B_WIDTH), f32),
                               *lru_w, 0, bp, tp, _pick(tp, (LRU_TB, 128, 64)))
    rec_s, bc_s, bh_s = _rglru(z_s, state_b_conv[j], state_b_h[j], *lru_w, 0, bs, ts, ts)
    w_out = ab_w_out[j].astype(bf16)
    w_out = [w_out[:A_WIDTH], w_out[A_WIDTH:]]
    xs, top_es, gates, cnts = per_group(
        lambda a, r, x, tm: _proj_ln([a, r], w_out, x, ln1_g[0], ln1_b[0], *router, 0, tm, td),
        att, (rec_p, rec_s), xs, tms)
    xs = _moe_ln(xs, top_es, gates, cnts, td, 0, *moe, ln2_g[0], ln2_b[0])

    keep = min(A_PAST_CHUNKS * CHUNK, tp)
    ak_p = z_p[n_p - keep:, A_WIDTH:2 * A_WIDTH].reshape(1, bp, keep, HA, DHA)
    av_p = z_p[n_p - keep:, 2 * A_WIDTH:3 * A_WIDTH].reshape(1, bp, keep, HA, DHA)
    ak_s = z_s[:, A_WIDTH:2 * A_WIDTH].reshape(1, bs, ts, HA, DHA)
    av_s = z_s[:, 2 * A_WIDTH:3 * A_WIDTH].reshape(1, bs, ts, HA, DHA)

    w_in = c_w_in[j].astype(bf16)
    y_p, cs_p = _retention(xs[0], w_in, None, 0, c_gn_g[j], c_gn_b[j], bp, tp, RET_TB, CHUNK)
    y_s, cs_s = _retention(xs[1], w_in, state_c_s[j], PAST_LEN, c_gn_g[j], c_gn_b[j], bs, ts, ts, ts)
    w_out = [c_w_out[j].astype(bf16)]
    xs, top_es, gates, cnts = per_group(
        lambda y, x, tm: _proj_ln([y], w_out, x, ln1_g[1], ln1_b[1], *router, 1, tm, td),
        (y_p, y_s), xs, tms)
    out_p, out_s = _moe_ln(xs, top_es, gates, cnts, td, 1, *moe, ln2_g[1], ln2_b[1])

    return (out_p.reshape(bp, tp, d), out_s.reshape(bs, ts, d),
            ak_p, av_p, bc_p[None], bh_p.reshape(1, bp, B_WIDTH), cs_p[None],
            ak_s, av_s, bc_s[None], bh_s.reshape(1, bs, B_WIDTH), cs_s[None])
```

```python
import functools
import math

import jax
import jax.numpy as jnp
from jax import lax
from jax.experimental import pallas as pl
from jax.experimental.pallas import tpu as pltpu

f32 = jnp.float32
bf16 = jnp.bfloat16
i32 = jnp.int32

DEPTH = 2
PAST_LEN = 1024
CHUNK = 64
A_PAST_CHUNKS = 8
REL_CLIP = 128
HA, DHA = 8, 64
A_WIDTH = HA * DHA
B_WIDTH = 512
HB = 8
CONV_W = 4
LRU_C = 8.0
HC, DKC, DVC = 4, 256, 512
C_QK, C_V = HC * DKC, HC * DVC
ROPE_BASE = 10000.0
GN_EPS = 1e-5
N_EXPERTS = 32
TOP_K = 4
D_FF = 1024
SWIGLU_LIMIT = 7.0
SWIGLU_ALPHA = 1.702
ALPHA = (2.0 * DEPTH) ** 0.25
LN_EPS = 1e-5
NEG = -1e30

LANES = 128
VMEM_LIMIT = 56 * 1024 * 1024
ATT_TQ = 256
MOE_BM = 256
MOE_BM_SHIFT = MOE_BM.bit_length() - 1
MOE_PAIR = 2
MOE_CH = 8
RET_TB = 512
LRU_TB = 256


def _pick(n, cands):
    for c in cands:
        if n % c == 0:
            return c
    raise ValueError(f"no tile for {n} in {cands}")


def _cparams(sem):
    return pltpu.CompilerParams(dimension_semantics=sem, vmem_limit_bytes=VMEM_LIMIT)


def _layernorm(v, g, b):
    mu = jnp.mean(v, axis=-1, keepdims=True)
    d = v - mu
    var = jnp.mean(d * d, axis=-1, keepdims=True)
    return d * lax.rsqrt(var + LN_EPS) * g + b


def _round_up_bm(v):
    return lax.shift_left(lax.shift_right_logical(v + (MOE_BM - 1), MOE_BM_SHIFT), MOE_BM_SHIFT)


def _rel_bias_matrix(table, nq, nk, d0):
    p = nq + nk - 1
    j = jnp.arange(p)
    u = table[:, jnp.clip(d0 + nq - 1 - j, -REL_CLIP, REL_CLIP) + REL_CLIP]
    u = jnp.roll(u, -(nq - 1), axis=1)
    flat = jnp.tile(u, (1, nq))[:, :nq * (p - 1)]
    return flat.reshape(table.shape[0], nq, p - 1)[:, :, :nk].astype(f32)


def _mm_kernel(x_ref, w_ref, o_ref):
    o_ref[...] = jnp.dot(x_ref[...].astype(bf16), w_ref[...], preferred_element_type=f32)


def _matmul(x, w, tm, tn):
    m, k = x.shape
    n = w.shape[1]
    return pl.pallas_call(
        _mm_kernel,
        grid=(m // tm, n // tn),
        in_specs=[pl.BlockSpec((tm, k), lambda i, j: (i, 0)),
                  pl.BlockSpec((k, tn), lambda i, j: (0, j))],
        out_specs=pl.BlockSpec((tm, tn), lambda i, j: (i, j)),
        out_shape=jax.ShapeDtypeStruct((m, n), f32),
        compiler_params=_cparams(("parallel", "parallel")),
        name="dense_proj",
    )(x, w)


def _route_top_k(x, w, b):
    logits = jnp.dot(x.astype(bf16), w.astype(bf16), preferred_element_type=f32) + b
    tm = logits.shape[0]
    lane = lax.broadcasted_iota(i32, (tm, N_EXPERTS), 1)
    out_lane = lax.broadcasted_iota(i32, (tm, LANES), 1)
    e_out = jnp.zeros((tm, LANES), i32)
    g_out = jnp.zeros((tm, LANES), f32)
    v0 = None
    den = jnp.zeros((tm, 1), f32)
    for k in range(TOP_K):
        v = jnp.max(logits, axis=-1, keepdims=True)
        idx = jnp.min(jnp.where(logits == v, lane, N_EXPERTS), axis=-1, keepdims=True)
        logits = jnp.where(lane == idx, -jnp.inf, logits)
        if k == 0:
            v0 = v
        p = jnp.exp(v - v0)
        den = den + p
        e_out = jnp.where(out_lane == k, idx, e_out)
        g_out = jnp.where(out_lane == k, p, g_out)
    return e_out, g_out / den


def _proj_ln_kernel(*refs, n_in):
    a_refs = refs[:n_in]
    w_refs = refs[n_in:2 * n_in]
    x_ref, g_ref, b_ref, wr_ref, br_ref, o_ref, e_ref, gate_ref, cnt_ref = refs[2 * n_in:]
    acc = ALPHA * x_ref[...]
    for a_ref, w_ref in zip(a_refs, w_refs):
        acc = acc + jnp.dot(a_ref[...].astype(bf16), w_ref[...], preferred_element_type=f32)
    y = _layernorm(acc, g_ref[...], b_ref[...])
    o_ref[...] = y
    e_out, gates = _route_top_k(y, wr_ref[0], br_ref[0])
    e_ref[...] = e_out
    gate_ref[...] = gates
    nsub = cnt_ref.shape[0]
    td = e_out.shape[0] // nsub
    lane = lax.broadcasted_iota(i32, (td, LANES), 1)
    for u in range(nsub):
        eu = e_out[u * td:(u + 1) * td]
        cnt = jnp.zeros((1, LANES), f32)
        for k in range(TOP_K):
            cnt = cnt + jnp.sum((lane == eu[:, k:k + 1]).astype(f32), axis=0, keepdims=True)
        cnt_ref[u] = jnp.broadcast_to(cnt, cnt_ref.shape[1:])


def _proj_ln(acts, ws, x, g, b, w_router, b_router, l, tm, td):
    m, d = x.shape
    n_in = len(acts)
    depth = w_router.shape[0]
    nt = m // tm
    nsub = tm // td
    cur = lambda i: (i, 0)
    in_specs = ([pl.BlockSpec((tm, a.shape[1]), cur) for a in acts]
                + [pl.BlockSpec(w.shape, lambda i: (0, 0)) for w in ws]
                + [pl.BlockSpec((tm, d), cur),
                   pl.BlockSpec((1, d), lambda i: (0, 0)),
                   pl.BlockSpec((1, d), lambda i: (0, 0)),
                   pl.BlockSpec((1, d, N_EXPERTS), lambda i: (l, 0, 0)),
                   pl.BlockSpec((1, 1, N_EXPERTS), lambda i: (l, 0, 0))])
    return pl.pallas_call(
        functools.partial(_proj_ln_kernel, n_in=n_in),
        grid=(nt,),
        in_specs=in_specs,
        out_specs=[pl.BlockSpec((tm, d), cur),
                   pl.BlockSpec((tm, LANES), cur),
                   pl.BlockSpec((tm, LANES), cur),
                   pl.BlockSpec((nsub, 8, LANES), lambda i: (i, 0, 0))],
        out_shape=[jax.ShapeDtypeStruct((m, d), f32),
                   jax.ShapeDtypeStruct((m, LANES), i32),
                   jax.ShapeDtypeStruct((m, LANES), f32),
                   jax.ShapeDtypeStruct((m // td, 8, LANES), f32)],
        compiler_params=_cparams(("parallel",)),
        name="proj_ln",
    )(*acts, *ws, x, g.reshape(1, d), b.reshape(1, d), w_router, b_router.reshape(depth, 1, N_EXPERTS))


def _attn_prompt_kernel(q_ref, k0_ref, k1_ref, k2_ref, v0_ref, v1_ref, v2_ref, bias_ref, o_ref):
    i = pl.program_id(0)
    tq = q_ref.shape[0]
    q = q_ref[...]
    k = jnp.concatenate([k0_ref[...], k1_ref[...], k2_ref[...]], axis=0)
    v = jnp.concatenate([v0_ref[...], v1_ref[...], v2_ref[...]], axis=0)
    kcol = lax.broadcasted_iota(i32, (1, 3 * tq), 1)
    tile_ok = (kcol // tq + i) >= 2
    qb = (q * (DHA ** -0.5)).astype(bf16)
    kb = k.astype(bf16)
    vb = v.astype(bf16)
    outs = []
    for h in range(HA):
        sl = slice(h * DHA, (h + 1) * DHA)
        s = lax.dot_general(qb[:, sl], kb[:, sl], (((1,), (1,)), ((), ())), preferred_element_type=f32)
        s = jnp.where(tile_ok, s + bias_ref[h], NEG)
        p = jnp.exp(s - jnp.max(s, axis=-1, keepdims=True))
        o = jnp.dot(p.astype(bf16), vb[:, sl], preferred_element_type=f32)
        outs.append(o / jnp.sum(p, axis=-1, keepdims=True))
    o_ref[...] = jnp.concatenate(outs, axis=-1)


def _attn_prompt(z, rel_bias, tp):
    tq = ATT_TQ
    nq = tp // tq
    qq = jnp.arange(tq)[:, None]
    kk = jnp.arange(3 * tq)[None, :]
    band = (kk // CHUNK >= qq // CHUNK) & (kk // CHUNK <= qq // CHUNK + A_PAST_CHUNKS)
    bias = jnp.where(band[None], _rel_bias_matrix(rel_bias, tq, 3 * tq, 2 * tq), NEG)

    def kv_spec(back, col):
        return pl.BlockSpec((tq, A_WIDTH), lambda i: (jnp.maximum(i - back, 0), col))

    return pl.pallas_call(
        _attn_prompt_kernel,
        grid=(nq,),
        in_specs=[pl.BlockSpec((tq, A_WIDTH), lambda i: (i, 0)),
                  kv_spec(2, 1), kv_spec(1, 1), kv_spec(0, 1),
                  kv_spec(2, 2), kv_spec(1, 2), kv_spec(0, 2),
                  pl.BlockSpec((HA, tq, 3 * tq), lambda i: (0, 0, 0))],
        out_specs=pl.BlockSpec((tq, A_WIDTH), lambda i: (i, 0)),
        out_shape=jax.ShapeDtypeStruct((tp, A_WIDTH), f32),
        compiler_params=_cparams(("parallel",)),
        name="attn_prompt",
    )(z, z, z, z, z, z, z, bias)


def _attn_sample_kernel(q_ref, kn_ref, vn_ref, kc_ref, vc_ref, bias_ref, o_ref):
    p_len = kc_ref.shape[1]
    q, kn, vn = q_ref[...], kn_ref[...], vn_ref[...]
    kc, vc = kc_ref[0], vc_ref[0]
    outs = []
    for h in range(HA):
        sl = slice(h * DHA, (h + 1) * DHA)
        qh = q[:, sl].astype(bf16)
        dims = (((1,), (1,)), ((), ()))
        sc = lax.dot_general(qh, kc[:, sl].astype(bf16), dims, preferred_element_type=f32)
        sn = lax.dot_general(qh, kn[:, sl].astype(bf16), dims, preferred_element_type=f32)
        b = bias_ref[h]
        sc = sc * (DHA ** -0.5) + b[:, :p_len]
        sn = sn * (DHA ** -0.5) + b[:, p_len:]
        m = jnp.maximum(jnp.max(sc, axis=-1, keepdims=True), jnp.max(sn, axis=-1, keepdims=True))
        pc, pn = jnp.exp(sc - m), jnp.exp(sn - m)
        den = jnp.sum(pc, axis=-1, keepdims=True) + jnp.sum(pn, axis=-1, keepdims=True)
        pc, pn = pc / den, pn / den
        outs.append(jnp.dot(pc.astype(bf16), vc[:, sl].astype(bf16), preferred_element_type=f32)
                    + jnp.dot(pn.astype(bf16), vn[:, sl].astype(bf16), preferred_element_type=f32))
    o_ref[...] = jnp.concatenate(outs, axis=-1)


def _attn_sample(z, k_cache, v_cache, rel_bias, row0, bs, ts):
    p_len = k_cache.shape[1]
    kc = k_cache.reshape(bs, p_len, A_WIDTH)
    vc = v_cache.reshape(bs, p_len, A_WIDTH)
    bias = _rel_bias_matrix(rel_bias, ts, p_len + ts, p_len)
    blk0 = row0 // ts

    def z_spec(col):
        return pl.BlockSpec((ts, A_WIDTH), lambda b: (blk0 + b, col))

    return pl.pallas_call(
        _attn_sample_kernel,
        grid=(bs,),
        in_specs=[z_spec(0), z_spec(1), z_spec(2),
                  pl.BlockSpec((1, p_len, A_WIDTH), lambda b: (b, 0, 0)),
                  pl.BlockSpec((1, p_len, A_WIDTH), lambda b: (b, 0, 0)),
                  pl.BlockSpec((HA, ts, p_len + ts), lambda b: (0, 0, 0))],
        out_specs=pl.BlockSpec((ts, A_WIDTH), lambda b: (b, 0)),
        out_shape=jax.ShapeDtypeStruct((bs * ts, A_WIDTH), f32),
        compiler_params=_cparams(("parallel",)),
        name="attn_sample",
    )(z, z, z, kc, vc, bias)


def _gelu_tanh(x):
    return 0.5 * x * (1.0 + jnp.tanh(math.sqrt(2.0 / math.pi) * (x + 0.044715 * (x * x * x))))


def _rglru_kernel(xb_ref, gb_ref, conv0_ref, h0_ref, cw_ref, cb_ref, wrg_ref, brg_ref,
                  wig_ref, big_ref, sp_ref, rec_ref, convn_ref, hl_ref, xp_ref, hc_ref):
    t = pl.program_id(1)
    tb = xb_ref.shape[0]
    pad = 8

    @pl.when(t == 0)
    def _():
        xp_ref[0:pad, :] = jnp.zeros((pad, B_WIDTH), f32)
        xp_ref[pad - (CONV_W - 1):pad, :] = conv0_ref[0]
        hc_ref[...] = h0_ref[0]

    xb = xb_ref[...]
    xp_ref[pad:pad + tb, :] = xb
    cw = cw_ref[...]
    u = cb_ref[...] + cw[CONV_W - 1:CONV_W, :] * xb
    for j in range(CONV_W - 1):
        sh = CONV_W - 1 - j
        u = u + cw[j:j + 1, :] * xp_ref[pad - sh:pad - sh + tb, :]
    convn_ref[0] = xp_ref[pad + tb - (CONV_W - 1):pad + tb, :]
    xp_ref[0:pad, :] = xp_ref[tb:tb + pad, :]

    ub = u.astype(bf16)
    r = jax.nn.sigmoid(jnp.dot(ub, wrg_ref[...], preferred_element_type=f32) + brg_ref[...])
    ig = jax.nn.sigmoid(jnp.dot(ub, wig_ref[...], preferred_element_type=f32) + big_ref[...])
    log_a = -LRU_C * r * sp_ref[...]
    a = jnp.exp(log_a)
    bt = jnp.sqrt(-jnp.tanh(log_a) * (a * a + 1.0)) * (ig * u)

    row = lax.broadcasted_iota(i32, (tb, 1), 0)
    s = 1
    while s < tb:
        keep = row >= s
        a_sh = pltpu.roll(a, s, axis=0)
        b_sh = pltpu.roll(bt, s, axis=0)
        bt = jnp.where(keep, a * b_sh + bt, bt)
        a = jnp.where(keep, a * a_sh, a)
        s *= 2
    h = a * hc_ref[...] + bt
    hc_ref[...] = h[tb - 1:tb, :]
    hl_ref[0] = h[tb - 1:tb, :]
    rec_ref[...] = h * _gelu_tanh(gb_ref[...])


def _rglru(z, conv0, h0, cw, cb, wrg, brg, wig, big, sp, row0, bsz, t_len, tb):
    nt = t_len // tb
    blk0 = row0 // tb

    def z_spec(col):
        return pl.BlockSpec((tb, B_WIDTH), lambda b, t: (blk0 + b * nt + t, col))

    def const(shape):
        return pl.BlockSpec(shape, lambda b, t: (0,) * len(shape))

    row = lambda v: v.reshape(1, B_WIDTH)
    return pl.pallas_call(
        _rglru_kernel,
        grid=(bsz, nt),
        in_specs=[z_spec(3), z_spec(4),
                  pl.BlockSpec((1, CONV_W - 1, B_WIDTH), lambda b, t: (b, 0, 0)),
                  pl.BlockSpec((1, 1, B_WIDTH), lambda b, t: (b, 0, 0)),
                  const((CONV_W, B_WIDTH)), const((1, B_WIDTH)),
                  const((B_WIDTH, B_WIDTH)), const((1, B_WIDTH)),
                  const((B_WIDTH, B_WIDTH)), const((1, B_WIDTH)), const((1, B_WIDTH))],
        out_specs=[pl.BlockSpec((tb, B_WIDTH), lambda b, t: (b * nt + t, 0)),
                   pl.BlockSpec((1, CONV_W - 1, B_WIDTH), lambda b, t: (b, 0, 0)),
                   pl.BlockSpec((1, 1, B_WIDTH), lambda b, t: (b, 0, 0))],
        out_shape=[jax.ShapeDtypeStruct((bsz * t_len, B_WIDTH), f32),
                   jax.ShapeDtypeStruct((bsz, CONV_W - 1, B_WIDTH), f32),
                   jax.ShapeDtypeStruct((bsz, 1, B_WIDTH), f32)],
        scratch_shapes=[pltpu.VMEM((tb + 8, B_WIDTH), f32), pltpu.VMEM((1, B_WIDTH), f32)],
        compiler_params=_cparams(("parallel", "arbitrary")),
        name="rglru",
    )(z, z, conv0, h0.reshape(bsz, 1, B_WIDTH), cw, row(cb), wrg, row(brg), wig, row(big), row(sp))


def _block_diag(w):
    hb, d, _ = w.shape
    eye = jnp.eye(hb, dtype=w.dtype)
    return (eye[:, None, :, None] * w[:, :, None, :]).reshape(hb * d, hb * d)


def _retention_kernel(x_ref, w_hbm, cos_ref, sin_ref, s0_hbm, gng_ref, gnb_ref,
                      y_ref, sout_hbm, w_ref, s_ref, z_ref, sem, *, chunk, nt, tb, zero_state, hoist):
    b = pl.program_id(0)
    t = pl.program_id(1)
    half = DKC // 2

    @pl.when((b == 0) & (t == 0))
    def _():
        cp = pltpu.make_async_copy(w_hbm, w_ref, sem.at[0])
        cp.start()
        cp.wait()
        if hoist:
            xa = x_ref[...].astype(bf16)
            for c in range(0, w_ref.shape[1], C_QK):
                z_ref[:, c:c + C_QK] = jnp.dot(xa, w_ref[:, c:c + C_QK], preferred_element_type=f32)

    @pl.when(t == 0)
    def _():
        if zero_state:
            s_ref[...] = jnp.zeros(s_ref.shape, f32)
        else:
            cp = pltpu.make_async_copy(s0_hbm.at[b], s_ref, sem.at[1])
            cp.start()
            cp.wait()

    cos, sin = cos_ref[...], sin_ref[...]
    n = lax.broadcasted_iota(i32, (tb, 1), 0)
    m = lax.broadcasted_iota(i32, (1, tb), 1)
    dist = n - m
    same = (n // chunk) == (m // chunk)
    expo = jnp.where(same, jnp.abs(dist), dist).astype(f32)
    visible = same | (dist > 0)
    nf = n.astype(f32)

    if hoist:
        row0 = pl.multiple_of(b * tb, tb)

        def proj(col, width):
            return z_ref[pl.ds(row0, tb), col:col + width]
    else:
        xb = x_ref[...].astype(bf16)

        def proj(col, width):
            return jnp.dot(xb, w_ref[:, col:col + width], preferred_element_type=f32)

    def rot(v, scale):
        x1, x2 = v[:, :half], v[:, half:]
        return jnp.concatenate([x1 * cos - x2 * sin, x2 * cos + x1 * sin], axis=-1) * scale

    for h in range(HC):
        lg = math.log(1.0 - 2.0 ** (-5.0 - h))
        qr = rot(proj(h * DKC, DKC), 1.0)
        kr = rot(proj(C_QK + h * DKC, DKC), DKC ** -0.5)
        vb = proj(2 * C_QK + h * DVC, DVC).astype(bf16)
        gate = proj(2 * C_QK + C_V + h * DVC, DVC)
        qb = qr.astype(bf16)
        dmat = jnp.where(visible, jnp.exp(lg * expo), 0.0)
        s = lax.dot_general(qb, kr.astype(bf16), (((1,), (1,)), ((), ())),
                            preferred_element_type=f32) * dmat
        s_old = s_ref[h]
        o = jnp.dot(s.astype(bf16), vb, preferred_element_type=f32)
        o = o + jnp.dot(qb, s_old.astype(bf16), preferred_element_type=f32) * jnp.exp(lg * (nf + 1.0))
        kd = (kr * jnp.exp(lg * (tb - 1.0 - nf))).astype(bf16)
        s_ref[h] = math.exp(lg * tb) * s_old + lax.dot_general(
            kd, vb, (((0,), (0,)), ((), ())), preferred_element_type=f32)

        mu = jnp.mean(o, axis=-1, keepdims=True)
        dlt = o - mu
        var = jnp.mean(dlt * dlt, axis=-1, keepdims=True)
        vs = slice(h * DVC, (h + 1) * DVC)
        yn = dlt * lax.rsqrt(var + GN_EPS) * gng_ref[:, vs] + gnb_ref[:, vs]
        y_ref[:, vs] = (jax.nn.silu(gate) * yn).astype(y_ref.dtype)

    @pl.when(t == nt - 1)
    def _():
        cp = pltpu.make_async_copy(s_ref, sout_hbm.at[b], sem.at[1])
        cp.start()
        cp.wait()


def _retention(x, w_in, s0, pos0, gn_g, gn_b, bsz, t_len, tb, chunk):
    nt = t_len // tb
    d = x.shape[1]
    half = DKC // 2
    inv = 1.0 / (ROPE_BASE ** (jnp.arange(half, dtype=f32) / half))
    ang = (pos0 + jnp.arange(t_len)).astype(f32)[:, None] * inv[None, :]
    cos, sin = jnp.cos(ang), jnp.sin(ang)
    zero_state = s0 is None
    if zero_state:
        s0 = jnp.zeros((1, 8, LANES), f32)
    hoist = nt == 1 and bsz > 1
    x_spec = (pl.BlockSpec(x.shape, lambda b, t: (0, 0)) if hoist
              else pl.BlockSpec((tb, d), lambda b, t: (b * nt + t, 0)))
    z_shape = (bsz * tb, w_in.shape[1]) if hoist else (8, LANES)
    return pl.pallas_call(
        functools.partial(_retention_kernel, chunk=chunk, nt=nt, tb=tb, zero_state=zero_state, hoist=hoist),
        grid=(bsz, nt),
        in_specs=[x_spec,
                  pl.BlockSpec(memory_space=pl.ANY),
                  pl.BlockSpec((tb, half), lambda b, t: (t, 0)),
                  pl.BlockSpec((tb, half), lambda b, t: (t, 0)),
                  pl.BlockSpec(memory_space=pl.ANY),
                  pl.BlockSpec((1, C_V), lambda b, t: (0, 0)),
                  pl.BlockSpec((1, C_V), lambda b, t: (0, 0))],
        out_specs=[pl.BlockSpec((tb, C_V), lambda b, t: (b * nt + t, 0)),
                   pl.BlockSpec(memory_space=pl.ANY)],
        out_shape=[jax.ShapeDtypeStruct((bsz * t_len, C_V), bf16),
                   jax.ShapeDtypeStruct((bsz, HC, DKC, DVC), f32)],
        scratch_shapes=[pltpu.VMEM(w_in.shape, bf16), pltpu.VMEM((HC, DKC, DVC), f32),
                        pltpu.VMEM(z_shape, f32), pltpu.SemaphoreType.DMA((2,))],
        compiler_params=_cparams(("arbitrary", "arbitrary")),
        name="retention",
    )(x, w_in, cos, sin, s0, gn_g.reshape(1, C_V), gn_b.reshape(1, C_V))


def _dispatch_kernel(ech_ref, est_ref, tab_ref, tabv_ref, ep_ref, es_ref, xp_ref, xs_ref,
                     xg_hbm, dest_ref, sbuf, zchunk, pending, sem, *, np_tiles, nt):
    i = pl.program_id(0)
    td = xp_ref.shape[0]
    ls = sbuf.shape[1]
    slot = i % 2
    nchunks = xg_hbm.shape[0]
    blk_chunks = MOE_BM // MOE_CH

    def chunk_wait(s):
        pltpu.make_async_copy(zchunk, xg_hbm.at[0], sem.at[s]).wait()

    def drain(s):
        def body(c, carry):
            chunk_wait(s)
            return carry
        lax.fori_loop(0, pending[s], body, 0)

    @pl.when(i == 0)
    def _():
        pending[0] = 0
        pending[1] = 0
        zchunk[...] = jnp.zeros(zchunk.shape, f32)

        def zero_chunks(lo, hi):
            def put(c, carry):
                pltpu.make_async_copy(zchunk, xg_hbm.at[c], sem.at[2]).start()
                return carry
            lax.fori_loop(lo, hi, put, 0)

            def done(c, carry):
                chunk_wait(2)
                return carry
            lax.fori_loop(lo, hi, done, 0)

        def per_expert(e, carry):
            used = est_ref[e] + ech_ref[e]
            zero_chunks(used, used + jnp.bitwise_and(-ech_ref[e], blk_chunks - 1))
            return carry
        lax.fori_loop(0, N_EXPERTS, per_expert, 0)
        last = N_EXPERTS - 1
        end = est_ref[last] + ech_ref[last] + jnp.bitwise_and(-ech_ref[last], blk_chunks - 1)
        zero_chunks(end, nchunks)

    def sort_and_copy(e_ref, x_ref):
        e = e_ref[...]
        lane = lax.broadcasted_iota(i32, (td, LANES), 1)
        member = jnp.zeros((td, LANES), f32)
        for k in range(TOP_K):
            member = member + (lane == e[:, k:k + 1]).astype(f32)
        r = lax.broadcasted_iota(i32, (td, td), 0)
        c = lax.broadcasted_iota(i32, (td, td), 1)
        before = jnp.dot((c < r).astype(bf16), member.astype(bf16), preferred_element_type=f32)
        in_tile = before + tabv_ref[0, 0:1, :]
        in_all = before + tabv_ref[0, 1:2, :]
        col = lax.broadcasted_iota(i32, (td, ls), 1)
        place = jnp.zeros((td, ls), f32)
        dest = jnp.zeros((td, LANES), i32)
        for k in range(TOP_K):
            sel = lane == e[:, k:k + 1]
            row_k = jnp.sum(jnp.where(sel, in_tile, 0.0), axis=-1, keepdims=True).astype(i32)
            dst_k = jnp.sum(jnp.where(sel, in_all, 0.0), axis=-1, keepdims=True).astype(i32)
            place = place + (col == row_k).astype(f32)
            dest = jnp.where(lane == k, dst_k, dest)
        dest_ref[...] = dest
        srt = lax.dot_general(place.astype(bf16), x_ref[...].astype(bf16), (((0,), (0,)), ((), ())),
                              preferred_element_type=f32)
        drain(slot)
        sbuf[slot] = srt

        total = 0
        for ei in range(N_EXPERTS):
            dst0 = tab_ref[0, 0, ei]
            n = tab_ref[0, 0, N_EXPERTS + ei]
            src0 = tab_ref[0, 0, 2 * N_EXPERTS + ei]

            def put(cc, carry):
                row = pl.multiple_of((src0 + cc) * MOE_CH, MOE_CH)
                pltpu.make_async_copy(sbuf.at[slot, pl.ds(row, MOE_CH)], xg_hbm.at[dst0 + cc],
                                      sem.at[slot]).start(priority=ei % 2)
                return carry
            lax.fori_loop(0, n, put, 0)
            total = total + n
        pending[slot] = total

    @pl.when(i < np_tiles)
    def _():
        sort_and_copy(ep_ref, xp_ref)

    @pl.when(i >= np_tiles)
    def _():
        sort_and_copy(es_ref, xs_ref)

    @pl.when(i == nt - 1)
    def _():
        drain(0)
        drain(1)


def _dispatch(xs, top_es, tab, tabv, ech, est, nblk, td):
    x_p, x_s = xs
    d = x_p.shape[1]
    np_tiles, ns_tiles = x_p.shape[0] // td, x_s.shape[0] // td
    nt = np_tiles + ns_tiles
    ls = td * TOP_K + N_EXPERTS * MOE_CH
    pspec = lambda w: pl.BlockSpec((td, w), lambda i, *_: (jnp.minimum(i, np_tiles - 1), 0))
    sspec = lambda w: pl.BlockSpec((td, w), lambda i, *_: (jnp.maximum(i - np_tiles, 0), 0))
    grid_spec = pltpu.PrefetchScalarGridSpec(
        num_scalar_prefetch=2,
        grid=(nt,),
        in_specs=[pl.BlockSpec((1, 1, LANES), lambda i, *_: (i, 0, 0), memory_space=pltpu.SMEM),
                  pl.BlockSpec((1, 8, LANES), lambda i, *_: (i, 0, 0)),
                  pspec(LANES), sspec(LANES), pspec(d), sspec(d)],
        out_specs=[pl.BlockSpec(memory_space=pl.ANY),
                   pl.BlockSpec((td, LANES), lambda i, *_: (i, 0))],
        scratch_shapes=[pltpu.VMEM((2, ls, d), f32), pltpu.VMEM((MOE_CH, d), f32),
                        pltpu.SMEM((2,), i32), pltpu.SemaphoreType.DMA((3,))],
    )
    xg, dest = pl.pallas_call(
        functools.partial(_dispatch_kernel, np_tiles=np_tiles, nt=nt),
        grid_spec=grid_spec,
        out_shape=[jax.ShapeDtypeStruct((nblk * MOE_BM // MOE_CH, MOE_CH, d), f32),
                   jax.ShapeDtypeStruct((nt * td, LANES), i32)],
        compiler_params=_cparams(("arbitrary",)),
        name="moe_dispatch",
    )(ech, est, tab, tabv, top_es[0], top_es[1], x_p, x_s)
    return xg.reshape(nblk * MOE_BM, d), dest


def _moe_kernel(blk_e_ref, nact_ref, first_ref, seg_ref, nxt_ref, x_ref, wup_hbm, bup_ref, wdn_hbm, bdn_ref,
                y_ref, wup_f, wdn_f, wup_bf, wdn_bf, sem, *, layer):
    b0 = pl.program_id(0) * MOE_PAIR
    nact = nact_ref[0]
    bm = MOE_BM

    def weight_copies(e, slot):
        return (pltpu.make_async_copy(wup_hbm.at[layer, e], wup_f.at[slot], sem.at[slot, 0]),
                pltpu.make_async_copy(wdn_hbm.at[layer, e], wdn_f.at[slot], sem.at[slot, 1]))

    @pl.when(b0 < nact)
    def _():
        for sub in range(MOE_PAIR):
            b = b0 + sub

            @pl.when(first_ref[b] == 1)
            def _():
                slot = seg_ref[b] % 2

                @pl.when(b == 0)
                def _():
                    for c in weight_copies(blk_e_ref[0], 0):
                        c.start()

                for c in weight_copies(blk_e_ref[b], slot):
                    c.wait()

                @pl.when(nxt_ref[b] >= 0)
                def _():
                    for c in weight_copies(nxt_ref[b], 1 - slot):
                        c.start()

                wup_bf[slot] = wup_f[slot].astype(bf16)
                wdn_bf[slot] = wdn_f[slot].astype(bf16)

        for sub in range(MOE_PAIR):
            b = b0 + sub
            slot = seg_ref[b] % 2
            e = layer * N_EXPERTS + blk_e_ref[b]
            rows = slice(sub * bm, (sub + 1) * bm)
            hdn = jnp.dot(x_ref[rows, :].astype(bf16), wup_bf[slot], preferred_element_type=f32)
            hdn = hdn + bup_ref[pl.ds(e, 1), :]
            glu = jnp.minimum(hdn[:, :D_FF], SWIGLU_LIMIT)
            lin = jnp.clip(hdn[:, D_FF:], -SWIGLU_LIMIT, SWIGLU_LIMIT)
            act = glu * jax.nn.sigmoid(SWIGLU_ALPHA * glu) * (lin + 1.0)
            y_ref[rows, :] = (jnp.dot(act.astype(bf16), wdn_bf[slot], preferred_element_type=f32)
                              + bdn_ref[pl.ds(e, 1), :])

    @pl.when(b0 >= nact)
    def _():
        y_ref[...] = jnp.zeros(y_ref.shape, f32)


def _moe_experts(xg, blk_e, nact, first, seg, nxt, l, w_up, b_up, w_down, b_down):
    r, d = xg.shape
    rows = MOE_BM * MOE_PAIR
    depth = w_up.shape[0]
    whole = lambda shape: pl.BlockSpec(shape, lambda s, *_: (0,) * len(shape))
    grid_spec = pltpu.PrefetchScalarGridSpec(
        num_scalar_prefetch=5,
        grid=(r // rows,),
        in_specs=[
            pl.BlockSpec((rows, d), lambda s, be, na, *_: (jnp.minimum(s, (na[0] - 1) // MOE_PAIR), 0)),
            pl.BlockSpec(memory_space=pl.ANY),
            whole((depth * N_EXPERTS, 2 * D_FF)),
            pl.BlockSpec(memory_space=pl.ANY),
            whole((depth * N_EXPERTS, d)),
        ],
        out_specs=pl.BlockSpec((rows, d), lambda s, *_: (s, 0)),
        scratch_shapes=[pltpu.VMEM((2, d, 2 * D_FF), f32), pltpu.VMEM((2, D_FF, d), f32),
                        pltpu.VMEM((2, d, 2 * D_FF), bf16), pltpu.VMEM((2, D_FF, d), bf16),
                        pltpu.SemaphoreType.DMA((2, 2))],
    )
    return pl.pallas_call(
        functools.partial(_moe_kernel, layer=l),
        grid_spec=grid_spec,
        out_shape=jax.ShapeDtypeStruct((r, d), f32),
        compiler_params=_cparams(("arbitrary",)),
        name="moe_experts",
    )(blk_e, nact, first, seg, nxt, xg, w_up, b_up.reshape(depth * N_EXPERTS, 2 * D_FF),
      w_down, b_down.reshape(depth * N_EXPERTS, d))


def _combine_ln_kernel(pos_cur_ref, pos_nxt_ref, y_hbm, gate_ref, x_ref, g_ref, b_ref, o_ref, gbuf, sem,
                       *, nt):
    i = pl.program_id(0)
    tm = x_ref.shape[0]

    def gather(pos_ref, slot):
        def body(r, carry):
            for k in range(TOP_K):
                p = pos_ref[0, 0, r * TOP_K + k]
                pltpu.make_async_copy(y_hbm.at[pl.ds(p, 1)], gbuf.at[slot, k, pl.ds(r, 1)],
                                      sem.at[slot]).start(priority=k % 2)
            return carry
        lax.fori_loop(0, tm, body, 0, unroll=4)

    @pl.when(i == 0)
    def _():
        gather(pos_cur_ref, 0)

    @pl.when(i + 1 < nt)
    def _():
        gather(pos_nxt_ref, (i + 1) % 2)

    slot = i % 2
    pltpu.make_async_copy(gbuf.at[slot], gbuf.at[slot], sem.at[slot]).wait()
    gates = gate_ref[...]
    acc = ALPHA * x_ref[...]
    for k in range(TOP_K):
        acc = acc + gates[:, k:k + 1] * gbuf[slot, k]
    o_ref[...] = _layernorm(acc, g_ref[...], b_ref[...])


def _combine_ln(y, pos3, blk0, gates, x, g, b, tm):
    rows, d = x.shape
    nt = rows // tm
    grid_spec = pltpu.PrefetchScalarGridSpec(
        num_scalar_prefetch=0,
        grid=(nt,),
        in_specs=[
            pl.BlockSpec((1, 1, tm * TOP_K), lambda i: (blk0 + i, 0, 0), memory_space=pltpu.SMEM),
            pl.BlockSpec((1, 1, tm * TOP_K), lambda i: (blk0 + jnp.minimum(i + 1, nt - 1), 0, 0),
                         memory_space=pltpu.SMEM),
            pl.BlockSpec(memory_space=pl.ANY),
            pl.BlockSpec((tm, LANES), lambda i: (i, 0)),
            pl.BlockSpec((tm, d), lambda i: (i, 0)),
            pl.BlockSpec((1, d), lambda i: (0, 0)),
            pl.BlockSpec((1, d), lambda i: (0, 0)),
        ],
        out_specs=pl.BlockSpec((tm, d), lambda i: (i, 0)),
        scratch_shapes=[pltpu.VMEM((2, TOP_K, tm, d), f32), pltpu.SemaphoreType.DMA((2,))],
    )
    return pl.pallas_call(
        functools.partial(_combine_ln_kernel, nt=nt),
        grid_spec=grid_spec,
        out_shape=jax.ShapeDtypeStruct((rows, d), f32),
        compiler_params=_cparams(("arbitrary",)),
        name="combine_ln",
    )(pos3, pos3, y, gates, x, g.reshape(1, d), b.reshape(1, d))


def _moe_ln(xs, top_es, gates, cnts, td, l, w_up, b_up, w_down, b_down, g, b):
    x_p, x_s = xs
    n_p, n_s = x_p.shape[0], x_s.shape[0]
    n = n_p + n_s
    bm = MOE_BM
    nt = n // td
    nblk = -(-(n * TOP_K + nt * N_EXPERTS * (MOE_CH - 1)) // bm) + N_EXPERTS
    nblk = -(-nblk // MOE_PAIR) * MOE_PAIR
    cnt = jnp.concatenate([c[:, 0, :N_EXPERTS] for c in cnts], axis=0).astype(i32)
    run = (cnt + MOE_CH - 1) // MOE_CH
    in_tile = jnp.cumsum(run, axis=1) - run
    ech = jnp.sum(run, axis=0)
    blk_chunks = bm // MOE_CH
    padded_ch = (ech + blk_chunks - 1) // blk_chunks * blk_chunks
    est = jnp.cumsum(padded_ch) - padded_ch
    in_all = est[None, :] + jnp.cumsum(run, axis=0) - run
    tab = jnp.concatenate([in_all, run, in_tile, jnp.zeros_like(run)], axis=1).reshape(nt, 1, LANES)
    tabv = jnp.zeros((nt, 8, LANES), f32)
    tabv = tabv.at[:, 0, :N_EXPERTS].set((in_tile * MOE_CH).astype(f32))
    tabv = tabv.at[:, 1, :N_EXPERTS].set((in_all * MOE_CH).astype(f32))
    xg, dest = _dispatch(xs, top_es, tab, tabv, ech, est, nblk, td)
    padded = padded_ch * MOE_CH
    ends = jnp.cumsum(padded)
    blk = jnp.arange(nblk, dtype=i32)
    blk_e = jnp.minimum(jnp.sum((ends[None, :] <= (blk * bm)[:, None]).astype(i32), axis=1), N_EXPERTS - 1)
    nact = (ends[-1] // bm).astype(i32).reshape(1)
    first = ((blk < nact[0]) & ((blk == 0) | (blk_e != jnp.roll(blk_e, 1)))).astype(i32)
    seg = jnp.cumsum(first) - 1
    eid = jnp.arange(N_EXPERTS, dtype=i32)
    later_nonempty = (padded[None, :] > 0) & (eid[None, :] > eid[:, None])
    nxt_of_e = jnp.min(jnp.where(later_nonempty, eid[None, :], N_EXPERTS), axis=1)
    nxt = jnp.where(nxt_of_e < N_EXPERTS, nxt_of_e, -1)[blk_e].astype(i32)
    dest3 = dest[:, :TOP_K].reshape(nt, 1, td * TOP_K)
    y = _moe_experts(xg, blk_e, nact, first, seg, nxt, l, w_up, b_up, w_down, b_down)
    return (_combine_ln(y, dest3, 0, gates[0], x_p, g, b, td),
            _combine_ln(y, dest3, n_p // td, gates[1], x_s, g, b, td))


def kernel(x_prompt, x_sample, cache_a_k, cache_a_v, state_b_conv, state_b_h, state_c_s, ab_w_in, ab_rel_bias, ab_conv_w, ab_conv_b, ab_w_rg, ab_b_rg, ab_w_ig, ab_b_ig, ab_lambda, ab_w_out, c_w_in, c_gn_g, c_gn_b, c_w_out, ln1_g, ln1_b, ln2_g, ln2_b, moe_w_router, moe_b_router, moe_w_up, moe_b_up, moe_w_down, moe_b_down):
    bp, tp, d = x_prompt.shape
    bs, ts, _ = x_sample.shape
    assert bp == 1 and tp % RET_TB == 0 and ts <= CHUNK
    n_p, n_s = bp * tp, bs * ts
    xs = (x_prompt.reshape(n_p, d), x_sample.reshape(n_s, d))
    tms = tuple(_pick(m, (512, 256, 128, 64)) for m in (n_p, n_s))
    td = _pick(math.gcd(n_p, n_s), (256, 128, 64, 32))
    moe = (moe_w_up, moe_b_up, moe_w_down, moe_b_down)
    router = (moe_w_router, moe_b_router)

    def per_group(fn, *groups):
        return tuple(zip(*[fn(*args) for args in zip(*groups)]))

    j = 0
    w_in = ab_w_in[j].astype(bf16)
    z_p, z_s = (_matmul(x, w_in, tm, w_in.shape[1]) for x, tm in zip(xs, tms))
    att = (_attn_prompt(z_p, ab_rel_bias[j], tp),
           _attn_sample(z_s, cache_a_k[j], cache_a_v[j], ab_rel_bias[j], 0, bs, ts))
    lru_w = (ab_conv_w[j], ab_conv_b[j], _block_diag(ab_w_rg[j]).astype(bf16), ab_b_rg[j],
             _block_diag(ab_w_ig[j]).astype(bf16), ab_b_ig[j], jax.nn.softplus(-ab_lambda[j]))
    rec_p, bc_p, bh_p = _rglru(z_p, jnp.zeros((bp, CONV_W - 1, B_WIDTH), f32), jnp.zeros((bp, B_WIDTH), f32),
                               *lru_w, 0, bp, tp, _pick(tp, (LRU_TB, 128, 64)))
    rec_s, bc_s, bh_s = _rglru(z_s, state_b_conv[j], state_b_h[j], *lru_w, 0, bs, ts, ts)
    w_out = ab_w_out[j].astype(bf16)
    w_out = [w_out[:A_WIDTH], w_out[A_WIDTH:]]
    xs, top_es, gates, cnts = per_group(
        lambda a, r, x, tm: _proj_ln([a, r], w_out, x, ln1_g[0], ln1_b[0], *router, 0, tm, td),
        att, (rec_p, rec_s), xs, tms)
    xs = _moe_ln(xs, top_es, gates, cnts, td, 0, *moe, ln2_g[0], ln2_b[0])

    keep = min(A_PAST_CHUNKS * CHUNK, tp)
    ak_p = z_p[n_p - keep:, A_WIDTH:2 * A_WIDTH].reshape(1, bp, keep, HA, DHA)
    av_p = z_p[n_p - keep:, 2 * A_WIDTH:3 * A_WIDTH].reshape(1, bp, keep, HA, DHA)
    ak_s = z_s[:, A_WIDTH:2 * A_WIDTH].reshape(1, bs, ts, HA, DHA)
    av_s = z_s[:, 2 * A_WIDTH:3 * A_WIDTH].reshape(1, bs, ts, HA, DHA)

    w_in = c_w_in[j].astype(bf16)
    y_p, cs_p = _retention(xs[0], w_in, None, 0, c_gn_g[j], c_gn_b[j], bp, tp, RET_TB, CHUNK)
    y_s, cs_s = _retention(xs[1], w_in, state_c_s[j], PAST_LEN, c_gn_g[j], c_gn_b[j], bs, ts, ts, ts)
    w_out = [c_w_out[j].astype(bf16)]
    xs, top_es, gates, cnts = per_group(
        lambda y, x, tm: _proj_ln([y], w_out, x, ln1_g[1], ln1_b[1], *router, 1, tm, td),
        (y_p, y_s), xs, tms)
    out_p, out_s = _moe_ln(xs, top_es, gates, cnts, td, 1, *moe, ln2_g[1], ln2_b[1])

    return (out_p.reshape(bp, tp, d), out_s.reshape(bs, ts, d),
            ak_p, av_p, bc_p[None], bh_p.reshape(1, bp, B_WIDTH), cs_p[None],
            ak_s, av_s, bc_s[None], bh_s.reshape(1, bs, B_WIDTH), cs_s[None])
```

```python
import functools
import math

import jax
import jax.numpy as jnp
from jax import lax
from jax.experimental import pallas as pl
from jax.experimental.pallas import tpu as pltpu

f32 = jnp.float32
bf16 = jnp.bfloat16
i32 = jnp.int32
u32 = jnp.uint32

DEPTH = 2
PAST_LEN = 1024
CHUNK = 64
A_PAST_CHUNKS = 8
REL_CLIP = 128
HA, DHA = 8, 64
A_WIDTH = HA * DHA
B_WIDTH = 512
HB = 8
CONV_W = 4
LRU_C = 8.0
HC, DKC, DVC = 4, 256, 512
C_QK, C_V = HC * DKC, HC * DVC
ROPE_BASE = 10000.0
GN_EPS = 1e-5
N_EXPERTS = 32
TOP_K = 4
D_FF = 1024
SWIGLU_LIMIT = 7.0
SWIGLU_ALPHA = 1.702
ALPHA = (2.0 * DEPTH) ** 0.25
LN_EPS = 1e-5
NEG = -1e30

LANES = 128
VMEM_LIMIT = 56 * 1024 * 1024
ATT_TQ = 256
MOE_BM = 256
MOE_BM_SHIFT = MOE_BM.bit_length() - 1
MOE_PAIR = 2
MOE_CH = 8
RET_TB = 512
LRU_TB = 256


def _pick(n, cands):
    for c in cands:
        if n % c == 0:
            return c
    raise ValueError(f"no tile for {n} in {cands}")


def _cparams(sem):
    return pltpu.CompilerParams(dimension_semantics=sem, vmem_limit_bytes=VMEM_LIMIT)


def _layernorm(v, g, b):
    mu = jnp.mean(v, axis=-1, keepdims=True)
    d = v - mu
    var = jnp.mean(d * d, axis=-1, keepdims=True)
    return d * lax.rsqrt(var + LN_EPS) * g + b


def _pack_bf16_pairs(v):
    w = v.shape[1] // 2
    hi = lax.bitcast_convert_type(v[:, :w], u32)
    lo = lax.bitcast_convert_type(v[:, w:], u32)
    return (hi & jnp.uint32(0xFFFF0000)) | (lo >> 16)


def _unpack_bf16_pairs(p):
    hi = lax.bitcast_convert_type(p & jnp.uint32(0xFFFF0000), f32)
    lo = lax.bitcast_convert_type(p << 16, f32)
    return jnp.concatenate([hi, lo], axis=1).astype(bf16)


def _round_up_bm(v):
    return lax.shift_left(lax.shift_right_logical(v + (MOE_BM - 1), MOE_BM_SHIFT), MOE_BM_SHIFT)


def _rel_bias_matrix(table, nq, nk, d0):
    p = nq + nk - 1
    j = jnp.arange(p)
    u = table[:, jnp.clip(d0 + nq - 1 - j, -REL_CLIP, REL_CLIP) + REL_CLIP]
    u = jnp.roll(u, -(nq - 1), axis=1)
    flat = jnp.tile(u, (1, nq))[:, :nq * (p - 1)]
    return flat.reshape(table.shape[0], nq, p - 1)[:, :, :nk].astype(f32)


def _mm_kernel(x_ref, w_ref, o_ref):
    o_ref[...] = jnp.dot(x_ref[...].astype(bf16), w_ref[...], preferred_element_type=f32)


def _matmul(x, w, tm, tn):
    m, k = x.shape
    n = w.shape[1]
    return pl.pallas_call(
        _mm_kernel,
        grid=(m // tm, n // tn),
        in_specs=[pl.BlockSpec((tm, k), lambda i, j: (i, 0)),
                  pl.BlockSpec((k, tn), lambda i, j: (0, j))],
        out_specs=pl.BlockSpec((tm, tn), lambda i, j: (i, j)),
        out_shape=jax.ShapeDtypeStruct((m, n), f32),
        compiler_params=_cparams(("parallel", "parallel")),
        name="dense_proj",
    )(x, w)


def _route_top_k(x, w, b):
    logits = jnp.dot(x.astype(bf16), w.astype(bf16), preferred_element_type=f32) + b
    tm = logits.shape[0]
    lane = lax.broadcasted_iota(i32, (tm, N_EXPERTS), 1)
    out_lane = lax.broadcasted_iota(i32, (tm, LANES), 1)
    e_out = jnp.zeros((tm, LANES), i32)
    g_out = jnp.zeros((tm, LANES), f32)
    v0 = None
    den = jnp.zeros((tm, 1), f32)
    for k in range(TOP_K):
        v = jnp.max(logits, axis=-1, keepdims=True)
        idx = jnp.min(jnp.where(logits == v, lane, N_EXPERTS), axis=-1, keepdims=True)
        logits = jnp.where(lane == idx, -jnp.inf, logits)
        if k == 0:
            v0 = v
        p = jnp.exp(v - v0)
        den = den + p
        e_out = jnp.where(out_lane == k, idx, e_out)
        g_out = jnp.where(out_lane == k, p, g_out)
    return e_out, g_out / den


def _proj_ln_kernel(*refs, n_in):
    a_refs = refs[:n_in]
    w_refs = refs[n_in:2 * n_in]
    x_ref, g_ref, b_ref, wr_ref, br_ref, o_ref, e_ref, gate_ref, cnt_ref = refs[2 * n_in:]
    acc = ALPHA * x_ref[...]
    for a_ref, w_ref in zip(a_refs, w_refs):
        acc = acc + jnp.dot(a_ref[...].astype(bf16), w_ref[...], preferred_element_type=f32)
    y = _layernorm(acc, g_ref[...], b_ref[...])
    o_ref[...] = y
    e_out, gates = _route_top_k(y, wr_ref[0], br_ref[0])
    e_ref[...] = e_out
    gate_ref[...] = gates
    nsub = cnt_ref.shape[0]
    td = e_out.shape[0] // nsub
    lane = lax.broadcasted_iota(i32, (td, LANES), 1)
    for u in range(nsub):
        eu = e_out[u * td:(u + 1) * td]
        cnt = jnp.zeros((1, LANES), f32)
        for k in range(TOP_K):
            cnt = cnt + jnp.sum((lane == eu[:, k:k + 1]).astype(f32), axis=0, keepdims=True)
        cnt_ref[u] = jnp.broadcast_to(cnt, cnt_ref.shape[1:])


def _proj_ln(acts, ws, x, g, b, w_router, b_router, l, tm, td):
    m, d = x.shape
    n_in = len(acts)
    depth = w_router.shape[0]
    nt = m // tm
    nsub = tm // td
    cur = lambda i: (i, 0)
    in_specs = ([pl.BlockSpec((tm, a.shape[1]), cur) for a in acts]
                + [pl.BlockSpec(w.shape, lambda i: (0, 0)) for w in ws]
                + [pl.BlockSpec((tm, d), cur),
                   pl.BlockSpec((1, d), lambda i: (0, 0)),
                   pl.BlockSpec((1, d), lambda i: (0, 0)),
                   pl.BlockSpec((1, d, N_EXPERTS), lambda i: (l, 0, 0)),
                   pl.BlockSpec((1, 1, N_EXPERTS), lambda i: (l, 0, 0))])
    return pl.pallas_call(
        functools.partial(_proj_ln_kernel, n_in=n_in),
        grid=(nt,),
        in_specs=in_specs,
        out_specs=[pl.BlockSpec((tm, d), cur),
                   pl.BlockSpec((tm, LANES), cur),
                   pl.BlockSpec((tm, LANES), cur),
                   pl.BlockSpec((nsub, 8, LANES), lambda i: (i, 0, 0))],
        out_shape=[jax.ShapeDtypeStruct((m, d), f32),
                   jax.ShapeDtypeStruct((m, LANES), i32),
                   jax.ShapeDtypeStruct((m, LANES), f32),
                   jax.ShapeDtypeStruct((m // td, 8, LANES), f32)],
        compiler_params=_cparams(("parallel",)),
        name="proj_ln",
    )(*acts, *ws, x, g.reshape(1, d), b.reshape(1, d), w_router, b_router.reshape(depth, 1, N_EXPERTS))


def _attn_prompt_kernel(q_ref, k0_ref, k1_ref, k2_ref, v0_ref, v1_ref, v2_ref, bias_ref, o_ref):
    i = pl.program_id(0)
    tq = q_ref.shape[0]
    q = q_ref[...]
    k = jnp.concatenate([k0_ref[...], k1_ref[...], k2_ref[...]], axis=0)
    v = jnp.concatenate([v0_ref[...], v1_ref[...], v2_ref[...]], axis=0)
    kcol = lax.broadcasted_iota(i32, (1, 3 * tq), 1)
    tile_ok = (kcol // tq + i) >= 2
    qb = (q * (DHA ** -0.5)).astype(bf16)
    kb = k.astype(bf16)
    vb = v.astype(bf16)
    outs = []
    for h in range(HA):
        sl = slice(h * DHA, (h + 1) * DHA)
        s = lax.dot_general(qb[:, sl], kb[:, sl], (((1,), (1,)), ((), ())), preferred_element_type=f32)
        s = jnp.where(tile_ok, s + bias_ref[h], NEG)
        p = jnp.exp(s - jnp.max(s, axis=-1, keepdims=True))
        o = jnp.dot(p.astype(bf16), vb[:, sl], preferred_element_type=f32)
        outs.append(o / jnp.sum(p, axis=-1, keepdims=True))
    o_ref[...] = jnp.concatenate(outs, axis=-1)


def _attn_prompt(z, rel_bias, tp):
    tq = ATT_TQ
    nq = tp // tq
    qq = jnp.arange(tq)[:, None]
    kk = jnp.arange(3 * tq)[None, :]
    band = (kk // CHUNK >= qq // CHUNK) & (kk // CHUNK <= qq // CHUNK + A_PAST_CHUNKS)
    bias = jnp.where(band[None], _rel_bias_matrix(rel_bias, tq, 3 * tq, 2 * tq), NEG)

    def kv_spec(back, col):
        return pl.BlockSpec((tq, A_WIDTH), lambda i: (jnp.maximum(i - back, 0), col))

    return pl.pallas_call(
        _attn_prompt_kernel,
        grid=(nq,),
        in_specs=[pl.BlockSpec((tq, A_WIDTH), lambda i: (i, 0)),
                  kv_spec(2, 1), kv_spec(1, 1), kv_spec(0, 1),
                  kv_spec(2, 2), kv_spec(1, 2), kv_spec(0, 2),
                  pl.BlockSpec((HA, tq, 3 * tq), lambda i: (0, 0, 0))],
        out_specs=pl.BlockSpec((tq, A_WIDTH), lambda i: (i, 0)),
        out_shape=jax.ShapeDtypeStruct((tp, A_WIDTH), f32),
        compiler_params=_cparams(("parallel",)),
        name="attn_prompt",
    )(z, z, z, z, z, z, z, bias)


def _attn_sample_kernel(q_ref, kn_ref, vn_ref, kc_ref, vc_ref, bias_ref, o_ref):
    p_len = kc_ref.shape[1]
    q, kn, vn = q_ref[...], kn_ref[...], vn_ref[...]
    kc, vc = kc_ref[0], vc_ref[0]
    outs = []
    for h in range(HA):
        sl = slice(h * DHA, (h + 1) * DHA)
        qh = q[:, sl].astype(bf16)
        dims = (((1,), (1,)), ((), ()))
        sc = lax.dot_general(qh, kc[:, sl].astype(bf16), dims, preferred_element_type=f32)
        sn = lax.dot_general(qh, kn[:, sl].astype(bf16), dims, preferred_element_type=f32)
        b = bias_ref[h]
        sc = sc * (DHA ** -0.5) + b[:, :p_len]
        sn = sn * (DHA ** -0.5) + b[:, p_len:]
        m = jnp.maximum(jnp.max(sc, axis=-1, keepdims=True), jnp.max(sn, axis=-1, keepdims=True))
        pc, pn = jnp.exp(sc - m), jnp.exp(sn - m)
        den = jnp.sum(pc, axis=-1, keepdims=True) + jnp.sum(pn, axis=-1, keepdims=True)
        pc, pn = pc / den, pn / den
        outs.append(jnp.dot(pc.astype(bf16), vc[:, sl].astype(bf16), preferred_element_type=f32)
                    + jnp.dot(pn.astype(bf16), vn[:, sl].astype(bf16), preferred_element_type=f32))
    o_ref[...] = jnp.concatenate(outs, axis=-1)


def _attn_sample(z, k_cache, v_cache, rel_bias, row0, bs, ts):
    p_len = k_cache.shape[1]
    kc = k_cache.reshape(bs, p_len, A_WIDTH)
    vc = v_cache.reshape(bs, p_len, A_WIDTH)
    bias = _rel_bias_matrix(rel_bias, ts, p_len + ts, p_len)
    blk0 = row0 // ts

    def z_spec(col):
        return pl.BlockSpec((ts, A_WIDTH), lambda b: (blk0 + b, col))

    return pl.pallas_call(
        _attn_sample_kernel,
        grid=(bs,),
        in_specs=[z_spec(0), z_spec(1), z_spec(2),
                  pl.BlockSpec((1, p_len, A_WIDTH), lambda b: (b, 0, 0)),
                  pl.BlockSpec((1, p_len, A_WIDTH), lambda b: (b, 0, 0)),
                  pl.BlockSpec((HA, ts, p_len + ts), lambda b: (0, 0, 0))],
        out_specs=pl.BlockSpec((ts, A_WIDTH), lambda b: (b, 0)),
        out_shape=jax.ShapeDtypeStruct((bs * ts, A_WIDTH), f32),
        compiler_params=_cparams(("parallel",)),
        name="attn_sample",
    )(z, z, z, kc, vc, bias)


def _gelu_tanh(x):
    return 0.5 * x * (1.0 + jnp.tanh(math.sqrt(2.0 / math.pi) * (x + 0.044715 * (x * x * x))))


def _rglru_kernel(xb_ref, gb_ref, conv0_ref, h0_ref, cw_ref, cb_ref, wrg_ref, brg_ref,
                  wig_ref, big_ref, sp_ref, rec_ref, convn_ref, hl_ref, xp_ref, hc_ref):
    t = pl.program_id(1)
    tb = xb_ref.shape[0]
    pad = 8

    @pl.when(t == 0)
    def _():
        xp_ref[0:pad, :] = jnp.zeros((pad, B_WIDTH), f32)
        xp_ref[pad - (CONV_W - 1):pad, :] = conv0_ref[0]
        hc_ref[...] = h0_ref[0]

    xb = xb_ref[...]
    xp_ref[pad:pad + tb, :] = xb
    cw = cw_ref[...]
    u = cb_ref[...] + cw[CONV_W - 1:CONV_W, :] * xb
    for j in range(CONV_W - 1):
        sh = CONV_W - 1 - j
        u = u + cw[j:j + 1, :] * xp_ref[pad - sh:pad - sh + tb, :]
    convn_ref[0] = xp_ref[pad + tb - (CONV_W - 1):pad + tb, :]
    xp_ref[0:pad, :] = xp_ref[tb:tb + pad, :]

    ub = u.astype(bf16)
    r = jax.nn.sigmoid(jnp.dot(ub, wrg_ref[...], preferred_element_type=f32) + brg_ref[...])
    ig = jax.nn.sigmoid(jnp.dot(ub, wig_ref[...], preferred_element_type=f32) + big_ref[...])
    log_a = -LRU_C * r * sp_ref[...]
    a = jnp.exp(log_a)
    bt = jnp.sqrt(-jnp.tanh(log_a) * (a * a + 1.0)) * (ig * u)

    row = lax.broadcasted_iota(i32, (tb, 1), 0)
    s = 1
    while s < tb:
        keep = row >= s
        a_sh = pltpu.roll(a, s, axis=0)
        b_sh = pltpu.roll(bt, s, axis=0)
        bt = jnp.where(keep, a * b_sh + bt, bt)
        a = jnp.where(keep, a * a_sh, a)
        s *= 2
    h = a * hc_ref[...] + bt
    hc_ref[...] = h[tb - 1:tb, :]
    hl_ref[0] = h[tb - 1:tb, :]
    rec_ref[...] = h * _gelu_tanh(gb_ref[...])


def _rglru(z, conv0, h0, cw, cb, wrg, brg, wig, big, sp, row0, bsz, t_len, tb):
    nt = t_len // tb
    blk0 = row0 // tb

    def z_spec(col):
        return pl.BlockSpec((tb, B_WIDTH), lambda b, t: (blk0 + b * nt + t, col))

    def const(shape):
        return pl.BlockSpec(shape, lambda b, t: (0,) * len(shape))

    row = lambda v: v.reshape(1, B_WIDTH)
    return pl.pallas_call(
        _rglru_kernel,
        grid=(bsz, nt),
        in_specs=[z_spec(3), z_spec(4),
                  pl.BlockSpec((1, CONV_W - 1, B_WIDTH), lambda b, t: (b, 0, 0)),
                  pl.BlockSpec((1, 1, B_WIDTH), lambda b, t: (b, 0, 0)),
                  const((CONV_W, B_WIDTH)), const((1, B_WIDTH)),
                  const((B_WIDTH, B_WIDTH)), const((1, B_WIDTH)),
                  const((B_WIDTH, B_WIDTH)), const((1, B_WIDTH)), const((1, B_WIDTH))],
        out_specs=[pl.BlockSpec((tb, B_WIDTH), lambda b, t: (b * nt + t, 0)),
                   pl.BlockSpec((1, CONV_W - 1, B_WIDTH), lambda b, t: (b, 0, 0)),
                   pl.BlockSpec((1, 1, B_WIDTH), lambda b, t: (b, 0, 0))],
        out_shape=[jax.ShapeDtypeStruct((bsz * t_len, B_WIDTH), f32),
                   jax.ShapeDtypeStruct((bsz, CONV_W - 1, B_WIDTH), f32),
                   jax.ShapeDtypeStruct((bsz, 1, B_WIDTH), f32)],
        scratch_shapes=[pltpu.VMEM((tb + 8, B_WIDTH), f32), pltpu.VMEM((1, B_WIDTH), f32)],
        compiler_params=_cparams(("parallel", "arbitrary")),
        name="rglru",
    )(z, z, conv0, h0.reshape(bsz, 1, B_WIDTH), cw, row(cb), wrg, row(brg), wig, row(big), row(sp))


def _block_diag(w):
    hb, d, _ = w.shape
    eye = jnp.eye(hb, dtype=w.dtype)
    return (eye[:, None, :, None] * w[:, :, None, :]).reshape(hb * d, hb * d)


def _retention_kernel(x_ref, w_hbm, cos_ref, sin_ref, s0_hbm, gng_ref, gnb_ref,
                      y_ref, sout_hbm, w_ref, s_ref, z_ref, sem, *, chunk, nt, tb, zero_state, hoist):
    b = pl.program_id(0)
    t = pl.program_id(1)
    half = DKC // 2

    @pl.when((b == 0) & (t == 0))
    def _():
        cp = pltpu.make_async_copy(w_hbm, w_ref, sem.at[0])
        cp.start()
        cp.wait()
        if hoist:
            xa = x_ref[...].astype(bf16)
            for c in range(0, w_ref.shape[1], C_QK):
                z_ref[:, c:c + C_QK] = jnp.dot(xa, w_ref[:, c:c + C_QK], preferred_element_type=f32)

    @pl.when(t == 0)
    def _():
        if zero_state:
            s_ref[...] = jnp.zeros(s_ref.shape, f32)
        else:
            cp = pltpu.make_async_copy(s0_hbm.at[b], s_ref, sem.at[1])
            cp.start()
            cp.wait()

    cos, sin = cos_ref[...], sin_ref[...]
    n = lax.broadcasted_iota(i32, (tb, 1), 0)
    m = lax.broadcasted_iota(i32, (1, tb), 1)
    dist = n - m
    same = (n // chunk) == (m // chunk)
    expo = jnp.where(same, jnp.abs(dist), dist).astype(f32)
    visible = same | (dist > 0)
    nf = n.astype(f32)

    if hoist:
        row0 = pl.multiple_of(b * tb, tb)

        def proj(col, width):
            return z_ref[pl.ds(row0, tb), col:col + width]
    else:
        xb = x_ref[...].astype(bf16)

        def proj(col, width):
            return jnp.dot(xb, w_ref[:, col:col + width], preferred_element_type=f32)

    def rot(v, scale):
        x1, x2 = v[:, :half], v[:, half:]
        return jnp.concatenate([x1 * cos - x2 * sin, x2 * cos + x1 * sin], axis=-1) * scale

    for h in range(HC):
        lg = math.log(1.0 - 2.0 ** (-5.0 - h))
        qr = rot(proj(h * DKC, DKC), 1.0)
        kr = rot(proj(C_QK + h * DKC, DKC), DKC ** -0.5)
        vb = proj(2 * C_QK + h * DVC, DVC).astype(bf16)
        gate = proj(2 * C_QK + C_V + h * DVC, DVC)
        qb = qr.astype(bf16)
        dmat = jnp.where(visible, jnp.exp(lg * expo), 0.0)
        s = lax.dot_general(qb, kr.astype(bf16), (((1,), (1,)), ((), ())),
                            preferred_element_type=f32) * dmat
        s_old = s_ref[h]
        o = jnp.dot(s.astype(bf16), vb, preferred_element_type=f32)
        o = o + jnp.dot(qb, s_old.astype(bf16), preferred_element_type=f32) * jnp.exp(lg * (nf + 1.0))
        kd = (kr * jnp.exp(lg * (tb - 1.0 - nf))).astype(bf16)
        s_ref[h] = math.exp(lg * tb) * s_old + lax.dot_general(
            kd, vb, (((0,), (0,)), ((), ())), preferred_element_type=f32)

        mu = jnp.mean(o, axis=-1, keepdims=True)
        dlt = o - mu
        var = jnp.mean(dlt * dlt, axis=-1, keepdims=True)
        vs = slice(h * DVC, (h + 1) * DVC)
        yn = dlt * lax.rsqrt(var + GN_EPS) * gng_ref[:, vs] + gnb_ref[:, vs]
        y_ref[:, vs] = (jax.nn.silu(gate) * yn).astype(y_ref.dtype)

    @pl.when(t == nt - 1)
    def _():
        cp = pltpu.make_async_copy(s_ref, sout_hbm.at[b], sem.at[1])
        cp.start()
        cp.wait()


def _retention(x, w_in, s0, pos0, gn_g, gn_b, bsz, t_len, tb, chunk):
    nt = t_len // tb
    d = x.shape[1]
    half = DKC // 2
    inv = 1.0 / (ROPE_BASE ** (jnp.arange(half, dtype=f32) / half))
    ang = (pos0 + jnp.arange(t_len)).astype(f32)[:, None] * inv[None, :]
    cos, sin = jnp.cos(ang), jnp.sin(ang)
    zero_state = s0 is None
    if zero_state:
        s0 = jnp.zeros((1, 8, LANES), f32)
    hoist = nt == 1 and bsz > 1
    x_spec = (pl.BlockSpec(x.shape, lambda b, t: (0, 0)) if hoist
              else pl.BlockSpec((tb, d), lambda b, t: (b * nt + t, 0)))
    z_shape = (bsz * tb, w_in.shape[1]) if hoist else (8, LANES)
    return pl.pallas_call(
        functools.partial(_retention_kernel, chunk=chunk, nt=nt, tb=tb, zero_state=zero_state, hoist=hoist),
        grid=(bsz, nt),
        in_specs=[x_spec,
                  pl.BlockSpec(memory_space=pl.ANY),
                  pl.BlockSpec((tb, half), lambda b, t: (t, 0)),
                  pl.BlockSpec((tb, half), lambda b, t: (t, 0)),
                  pl.BlockSpec(memory_space=pl.ANY),
                  pl.BlockSpec((1, C_V), lambda b, t: (0, 0)),
                  pl.BlockSpec((1, C_V), lambda b, t: (0, 0))],
        out_specs=[pl.BlockSpec((tb, C_V), lambda b, t: (b * nt + t, 0)),
                   pl.BlockSpec(memory_space=pl.ANY)],
        out_shape=[jax.ShapeDtypeStruct((bsz * t_len, C_V), bf16),
                   jax.ShapeDtypeStruct((bsz, HC, DKC, DVC), f32)],
        scratch_shapes=[pltpu.VMEM(w_in.shape, bf16), pltpu.VMEM((HC, DKC, DVC), f32),
                        pltpu.VMEM(z_shape, f32), pltpu.SemaphoreType.DMA((2,))],
        compiler_params=_cparams(("arbitrary", "arbitrary")),
        name="retention",
    )(x, w_in, cos, sin, s0, gn_g.reshape(1, C_V), gn_b.reshape(1, C_V))


def _dispatch_kernel(ech_ref, est_ref, tab_ref, tabv_ref, ep_ref, es_ref, xp_ref, xs_ref,
                     xg_hbm, dest_ref, sbuf, zchunk, pending, sem, *, np_tiles, nt):
    i = pl.program_id(0)
    td = xp_ref.shape[0]
    ls = sbuf.shape[1]
    slot = i % 2
    nchunks = xg_hbm.shape[0]
    blk_chunks = MOE_BM // MOE_CH

    def chunk_wait(s):
        pltpu.make_async_copy(zchunk, xg_hbm.at[0], sem.at[s]).wait()

    def drain(s):
        def body(c, carry):
            chunk_wait(s)
            return carry
        lax.fori_loop(0, pending[s], body, 0)

    @pl.when(i == 0)
    def _():
        pending[0] = 0
        pending[1] = 0
        zchunk[...] = jnp.zeros(zchunk.shape, zchunk.dtype)

        def zero_chunks(lo, hi):
            def put(c, carry):
                pltpu.make_async_copy(zchunk, xg_hbm.at[c], sem.at[2]).start()
                return carry
            lax.fori_loop(lo, hi, put, 0)

            def done(c, carry):
                chunk_wait(2)
                return carry
            lax.fori_loop(lo, hi, done, 0)

        def per_expert(e, carry):
            used = est_ref[e] + ech_ref[e]
            zero_chunks(used, used + jnp.bitwise_and(-ech_ref[e], blk_chunks - 1))
            return carry
        lax.fori_loop(0, N_EXPERTS, per_expert, 0)
        last = N_EXPERTS - 1
        end = est_ref[last] + ech_ref[last] + jnp.bitwise_and(-ech_ref[last], blk_chunks - 1)
        zero_chunks(end, nchunks)

    def sort_and_copy(e_ref, x_ref):
        e = e_ref[...]
        lane = lax.broadcasted_iota(i32, (td, LANES), 1)
        member = jnp.zeros((td, LANES), f32)
        for k in range(TOP_K):
            member = member + (lane == e[:, k:k + 1]).astype(f32)
        r = lax.broadcasted_iota(i32, (td, td), 0)
        c = lax.broadcasted_iota(i32, (td, td), 1)
        before = jnp.dot((c < r).astype(bf16), member.astype(bf16), preferred_element_type=f32)
        in_tile = before + tabv_ref[0, 0:1, :]
        in_all = before + tabv_ref[0, 1:2, :]
        col = lax.broadcasted_iota(i32, (td, ls), 1)
        place = jnp.zeros((td, ls), f32)
        dest = jnp.zeros((td, LANES), i32)
        for k in range(TOP_K):
            sel = lane == e[:, k:k + 1]
            row_k = jnp.sum(jnp.where(sel, in_tile, 0.0), axis=-1, keepdims=True).astype(i32)
            dst_k = jnp.sum(jnp.where(sel, in_all, 0.0), axis=-1, keepdims=True).astype(i32)
            place = place + (col == row_k).astype(f32)
            dest = jnp.where(lane == k, dst_k, dest)
        dest_ref[...] = dest
        srt = lax.dot_general(place.astype(bf16), x_ref[...].astype(bf16), (((0,), (0,)), ((), ())),
                              preferred_element_type=f32)
        drain(slot)
        sbuf[slot] = _pack_bf16_pairs(srt)

        total = 0
        for ei in range(N_EXPERTS):
            dst0 = tab_ref[0, 0, ei]
            n = tab_ref[0, 0, N_EXPERTS + ei]
            src0 = tab_ref[0, 0, 2 * N_EXPERTS + ei]

            def put(cc, carry):
                row = pl.multiple_of((src0 + cc) * MOE_CH, MOE_CH)
                pltpu.make_async_copy(sbuf.at[slot, pl.ds(row, MOE_CH)], xg_hbm.at[dst0 + cc],
                                      sem.at[slot]).start(priority=ei % 2)
                return carry
            lax.fori_loop(0, n, put, 0)
            total = total + n
        pending[slot] = total

    @pl.when(i < np_tiles)
    def _():
        sort_and_copy(ep_ref, xp_ref)

    @pl.when(i >= np_tiles)
    def _():
        sort_and_copy(es_ref, xs_ref)

    @pl.when(i == nt - 1)
    def _():
        drain(0)
        drain(1)


def _dispatch(xs, top_es, tab, tabv, ech, est, nblk, td):
    x_p, x_s = xs
    d = x_p.shape[1]
    np_tiles, ns_tiles = x_p.shape[0] // td, x_s.shape[0] // td
    nt = np_tiles + ns_tiles
    ls = td * TOP_K + N_EXPERTS * MOE_CH
    pspec = lambda w: pl.BlockSpec((td, w), lambda i, *_: (jnp.minimum(i, np_tiles - 1), 0))
    sspec = lambda w: pl.BlockSpec((td, w), lambda i, *_: (jnp.maximum(i - np_tiles, 0), 0))
    grid_spec = pltpu.PrefetchScalarGridSpec(
        num_scalar_prefetch=2,
        grid=(nt,),
        in_specs=[pl.BlockSpec((1, 1, LANES), lambda i, *_: (i, 0, 0), memory_space=pltpu.SMEM),
                  pl.BlockSpec((1, 8, LANES), lambda i, *_: (i, 0, 0)),
                  pspec(LANES), sspec(LANES), pspec(d), sspec(d)],
        out_specs=[pl.BlockSpec(memory_space=pl.ANY),
                   pl.BlockSpec((td, LANES), lambda i, *_: (i, 0))],
        scratch_shapes=[pltpu.VMEM((2, ls, d // 2), u32), pltpu.VMEM((MOE_CH, d // 2), u32),
                        pltpu.SMEM((2,), i32), pltpu.SemaphoreType.DMA((3,))],
    )
    xg, dest = pl.pallas_call(
        functools.partial(_dispatch_kernel, np_tiles=np_tiles, nt=nt),
        grid_spec=grid_spec,
        out_shape=[jax.ShapeDtypeStruct((nblk * MOE_BM // MOE_CH, MOE_CH, d // 2), u32),
                   jax.ShapeDtypeStruct((nt * td, LANES), i32)],
        compiler_params=_cparams(("arbitrary",)),
        name="moe_dispatch",
    )(ech, est, tab, tabv, top_es[0], top_es[1], x_p, x_s)
    return xg.reshape(nblk * MOE_BM, d // 2), dest


def _moe_kernel(blk_e_ref, nact_ref, first_ref, seg_ref, nxt_ref, x_ref, wup_hbm, bup_ref, wdn_hbm, bdn_ref,
                y_ref, wup_f, wdn_f, wup_bf, wdn_bf, sem, *, layer):
    b0 = pl.program_id(0) * MOE_PAIR
    nact = nact_ref[0]
    bm = MOE_BM

    def weight_copies(e, slot):
        return (pltpu.make_async_copy(wup_hbm.at[layer, e], wup_f.at[slot], sem.at[slot, 0]),
                pltpu.make_async_copy(wdn_hbm.at[layer, e], wdn_f.at[slot], sem.at[slot, 1]))

    @pl.when(b0 < nact)
    def _():
        for sub in range(MOE_PAIR):
            b = b0 + sub

            @pl.when(first_ref[b] == 1)
            def _():
                slot = seg_ref[b] % 2

                @pl.when(b == 0)
                def _():
                    for c in weight_copies(blk_e_ref[0], 0):
                        c.start()

                for c in weight_copies(blk_e_ref[b], slot):
                    c.wait()

                @pl.when(nxt_ref[b] >= 0)
                def _():
                    for c in weight_copies(nxt_ref[b], 1 - slot):
                        c.start()

                wup_bf[slot] = wup_f[slot].astype(bf16)
                wdn_bf[slot] = wdn_f[slot].astype(bf16)

        for sub in range(MOE_PAIR):
            b = b0 + sub
            slot = seg_ref[b] % 2
            e = layer * N_EXPERTS + blk_e_ref[b]
            rows = slice(sub * bm, (sub + 1) * bm)
            hdn = jnp.dot(_unpack_bf16_pairs(x_ref[rows, :]), wup_bf[slot], preferred_element_type=f32)
            hdn = hdn + bup_ref[pl.ds(e, 1), :]
            glu = jnp.minimum(hdn[:, :D_FF], SWIGLU_LIMIT)
            lin = jnp.clip(hdn[:, D_FF:], -SWIGLU_LIMIT, SWIGLU_LIMIT)
            act = glu * jax.nn.sigmoid(SWIGLU_ALPHA * glu) * (lin + 1.0)
            y_ref[rows, :] = (jnp.dot(act.astype(bf16), wdn_bf[slot], preferred_element_type=f32)
                              + bdn_ref[pl.ds(e, 1), :])

    @pl.when(b0 >= nact)
    def _():
        y_ref[...] = jnp.zeros(y_ref.shape, f32)


def _moe_experts(xg, blk_e, nact, first, seg, nxt, l, w_up, b_up, w_down, b_down):
    r, half = xg.shape
    d = 2 * half
    rows = MOE_BM * MOE_PAIR
    depth = w_up.shape[0]
    whole = lambda shape: pl.BlockSpec(shape, lambda s, *_: (0,) * len(shape))
    grid_spec = pltpu.PrefetchScalarGridSpec(
        num_scalar_prefetch=5,
        grid=(r // rows,),
        in_specs=[
            pl.BlockSpec((rows, half), lambda s, be, na, *_: (jnp.minimum(s, (na[0] - 1) // MOE_PAIR), 0)),
            pl.BlockSpec(memory_space=pl.ANY),
            whole((depth * N_EXPERTS, 2 * D_FF)),
            pl.BlockSpec(memory_space=pl.ANY),
            whole((depth * N_EXPERTS, d)),
        ],
        out_specs=pl.BlockSpec((rows, d), lambda s, *_: (s, 0)),
        scratch_shapes=[pltpu.VMEM((2, d, 2 * D_FF), f32), pltpu.VMEM((2, D_FF, d), f32),
                        pltpu.VMEM((2, d, 2 * D_FF), bf16), pltpu.VMEM((2, D_FF, d), bf16),
                        pltpu.SemaphoreType.DMA((2, 2))],
    )
    return pl.pallas_call(
        functools.partial(_moe_kernel, layer=l),
        grid_spec=grid_spec,
        out_shape=jax.ShapeDtypeStruct((r, d), f32),
        compiler_params=_cparams(("arbitrary",)),
        name="moe_experts",
    )(blk_e, nact, first, seg, nxt, xg, w_up, b_up.reshape(depth * N_EXPERTS, 2 * D_FF),
      w_down, b_down.reshape(depth * N_EXPERTS, d))


def _combine_ln_kernel(pos_cur_ref, pos_nxt_ref, y_hbm, gate_ref, x_ref, g_ref, b_ref, o_ref, gbuf, sem,
                       *, nt):
    i = pl.program_id(0)
    tm = x_ref.shape[0]

    def gather(pos_ref, slot):
        def body(r, carry):
            for k in range(TOP_K):
                p = pos_ref[0, 0, r * TOP_K + k]
                pltpu.make_async_copy(y_hbm.at[pl.ds(p, 1)], gbuf.at[slot, k, pl.ds(r, 1)],
                                      sem.at[slot]).start(priority=k % 2)
            return carry
        lax.fori_loop(0, tm, body, 0, unroll=4)

    @pl.when(i == 0)
    def _():
        gather(pos_cur_ref, 0)

    @pl.when(i + 1 < nt)
    def _():
        gather(pos_nxt_ref, (i + 1) % 2)

    slot = i % 2
    pltpu.make_async_copy(gbuf.at[slot], gbuf.at[slot], sem.at[slot]).wait()
    gates = gate_ref[...]
    acc = ALPHA * x_ref[...]
    for k in range(TOP_K):
        acc = acc + gates[:, k:k + 1] * gbuf[slot, k]
    o_ref[...] = _layernorm(acc, g_ref[...], b_ref[...])


def _combine_ln(y, pos3, blk0, gates, x, g, b, tm):
    rows, d = x.shape
    nt = rows // tm
    grid_spec = pltpu.PrefetchScalarGridSpec(
        num_scalar_prefetch=0,
        grid=(nt,),
        in_specs=[
            pl.BlockSpec((1, 1, tm * TOP_K), lambda i: (blk0 + i, 0, 0), memory_space=pltpu.SMEM),
            pl.BlockSpec((1, 1, tm * TOP_K), lambda i: (blk0 + jnp.minimum(i + 1, nt - 1), 0, 0),
                         memory_space=pltpu.SMEM),
            pl.BlockSpec(memory_space=pl.ANY),
            pl.BlockSpec((tm, LANES), lambda i: (i, 0)),
            pl.BlockSpec((tm, d), lambda i: (i, 0)),
            pl.BlockSpec((1, d), lambda i: (0, 0)),
            pl.BlockSpec((1, d), lambda i: (0, 0)),
        ],
        out_specs=pl.BlockSpec((tm, d), lambda i: (i, 0)),
        scratch_shapes=[pltpu.VMEM((2, TOP_K, tm, d), f32), pltpu.SemaphoreType.DMA((2,))],
    )
    return pl.pallas_call(
        functools.partial(_combine_ln_kernel, nt=nt),
        grid_spec=grid_spec,
        out_shape=jax.ShapeDtypeStruct((rows, d), f32),
        compiler_params=_cparams(("arbitrary",)),
        name="combine_ln",
    )(pos3, pos3, y, gates, x, g.reshape(1, d), b.reshape(1, d))


def _moe_ln(xs, top_es, gates, cnts, td, l, w_up, b_up, w_down, b_down, g, b):
    x_p, x_s = xs
    n_p, n_s = x_p.shape[0], x_s.shape[0]
    n = n_p + n_s
    bm = MOE_BM
    nt = n // td
    nblk = -(-(n * TOP_K + nt * N_EXPERTS * (MOE_CH - 1)) // bm) + N_EXPERTS
    nblk = -(-nblk // MOE_PAIR) * MOE_PAIR
    cnt = jnp.concatenate([c[:, 0, :N_EXPERTS] for c in cnts], axis=0).astype(i32)
    run = (cnt + MOE_CH - 1) // MOE_CH
    in_tile = jnp.cumsum(run, axis=1) - run
    ech = jnp.sum(run, axis=0)
    blk_chunks = bm // MOE_CH
    padded_ch = (ech + blk_chunks - 1) // blk_chunks * blk_chunks
    est = jnp.cumsum(padded_ch) - padded_ch
    in_all = est[None, :] + jnp.cumsum(run, axis=0) - run
    tab = jnp.concatenate([in_all, run, in_tile, jnp.zeros_like(run)], axis=1).reshape(nt, 1, LANES)
    tabv = jnp.zeros((nt, 8, LANES), f32)
    tabv = tabv.at[:, 0, :N_EXPERTS].set((in_tile * MOE_CH).astype(f32))
    tabv = tabv.at[:, 1, :N_EXPERTS].set((in_all * MOE_CH).astype(f32))
    xg, dest = _dispatch(xs, top_es, tab, tabv, ech, est, nblk, td)
    padded = padded_ch * MOE_CH
    ends = jnp.cumsum(padded)
    blk = jnp.arange(nblk, dtype=i32)
    blk_e = jnp.minimum(jnp.sum((ends[None, :] <= (blk * bm)[:, None]).astype(i32), axis=1), N_EXPERTS - 1)
    nact = (ends[-1] // bm).astype(i32).reshape(1)
    first = ((blk < nact[0]) & ((blk == 0) | (blk_e != jnp.roll(blk_e, 1)))).astype(i32)
    seg = jnp.cumsum(first) - 1
    eid = jnp.arange(N_EXPERTS, dtype=i32)
    later_nonempty = (padded[None, :] > 0) & (eid[None, :] > eid[:, None])
    nxt_of_e = jnp.min(jnp.where(later_nonempty, eid[None, :], N_EXPERTS), axis=1)
    nxt = jnp.where(nxt_of_e < N_EXPERTS, nxt_of_e, -1)[blk_e].astype(i32)
    dest3 = dest[:, :TOP_K].reshape(nt, 1, td * TOP_K)
    y = _moe_experts(xg, blk_e, nact, first, seg, nxt, l, w_up, b_up, w_down, b_down)
    return (_combine_ln(y, dest3, 0, gates[0], x_p, g, b, td),
            _combine_ln(y, dest3, n_p // td, gates[1], x_s, g, b, td))


def kernel(x_prompt, x_sample, cache_a_k, cache_a_v, state_b_conv, state_b_h, state_c_s, ab_w_in, ab_rel_bias, ab_conv_w, ab_conv_b, ab_w_rg, ab_b_rg, ab_w_ig, ab_b_ig, ab_lambda, ab_w_out, c_w_in, c_gn_g, c_gn_b, c_w_out, ln1_g, ln1_b, ln2_g, ln2_b, moe_w_router, moe_b_router, moe_w_up, moe_b_up, moe_w_down, moe_b_down):
    bp, tp, d = x_prompt.shape
    bs, ts, _ = x_sample.shape
    assert bp == 1 and tp % RET_TB == 0 and ts <= CHUNK
    n_p, n_s = bp * tp, bs * ts
    xs = (x_prompt.reshape(n_p, d), x_sample.reshape(n_s, d))
    tms = tuple(_pick(m, (512, 256, 128, 64)) for m in (n_p, n_s))
    td = _pick(math.gcd(n_p, n_s), (256, 128, 64, 32))
    moe = (moe_w_up, moe_b_up, moe_w_down, moe_b_down)
    router = (moe_w_router, moe_b_router)

    def per_group(fn, *groups):
        return tuple(zip(*[fn(*args) for args in zip(*groups)]))

    j = 0
    w_in = ab_w_in[j].astype(bf16)
    z_p, z_s = (_matmul(x, w_in, tm, w_in.shape[1]) for x, tm in zip(xs, tms))
    att = (_attn_prompt(z_p, ab_rel_bias[j], tp),
           _attn_sample(z_s, cache_a_k[j], cache_a_v[j], ab_rel_bias[j], 0, bs, ts))
    lru_w = (ab_conv_w[j], ab_conv_b[j], _block_diag(ab_w_rg[j]).astype(bf16), ab_b_rg[j],
             _block_diag(ab_w_ig[j]).astype(bf16), ab_b_ig[j], jax.nn.softplus(-ab_lambda[j]))
    rec_p, bc_p, bh_p = _rglru(z_p, jnp.zeros((bp, CONV_W - 1, B_WIDTH), f32), jnp.zeros((bp, B_WIDTH), f32),
                               *lru_w, 0, bp, tp, _pick(tp, (LRU_TB, 128, 64)))
    rec_s, bc_s, bh_s = _rglru(z_s, state_b_conv[j], state_b_h[j], *lru_w, 0, bs, ts, ts)
    w_out = ab_w_out[j].astype(bf16)
    w_out = [w_out[:A_WIDTH], w_out[A_WIDTH:]]
    xs, top_es, gates, cnts = per_group(
        lambda a, r, x, tm: _proj_ln([a, r], w_out, x, ln1_g[0], ln1_b[0], *router, 0, tm, td),
        att, (rec_p, rec_s), xs, tms)
    xs = _moe_ln(xs, top_es, gates, cnts, td, 0, *moe, ln2_g[0], ln2_b[0])

    keep = min(A_PAST_CHUNKS * CHUNK, tp)
    ak_p = z_p[n_p - keep:, A_WIDTH:2 * A_WIDTH].reshape(1, bp, keep, HA, DHA)
    av_p = z_p[n_p - keep:, 2 * A_WIDTH:3 * A_WIDTH].reshape(1, bp, keep, HA, DHA)
    ak_s = z_s[:, A_WIDTH:2 * A_WIDTH].reshape(1, bs, ts, HA, DHA)
    av_s = z_s[:, 2 * A_WIDTH:3 * A_WIDTH].reshape(1, bs, ts, HA, DHA)

    w_in = c_w_in[j].astype(bf16)
    y_p, cs_p = _retention(xs[0], w_in, None, 0, c_gn_g[j], c_gn_b[j], bp, tp, RET_TB, CHUNK)
    y_s, cs_s = _retention(xs[1], w_in, state_c_s[j], PAST_LEN, c_gn_g[j], c_gn_b[j], bs, ts, ts, ts)
    w_out = [c_w_out[j].astype(bf16)]
    xs, top_es, gates, cnts = per_group(
        lambda y, x, tm: _proj_ln([y], w_out, x, ln1_g[1], ln1_b[1], *router, 1, tm, td),
        (y_p, y_s), xs, tms)
    out_p, out_s = _moe_ln(xs, top_es, gates, cnts, td, 1, *moe, ln2_g[1], ln2_b[1])

    return (out_p.reshape(bp, tp, d), out_s.reshape(bs, ts, d),
            ak_p, av_p, bc_p[None], bh_p.reshape(1, bp, B_WIDTH), cs_p[None],
            ak_s, av_s, bc_s[None], bh_s.reshape(1, bs, B_WIDTH), cs_s[None])
```

```python
import functools
import math

import jax
import jax.numpy as jnp
from jax import lax
from jax.experimental import pallas as pl
from jax.experimental.pallas import tpu as pltpu

f32 = jnp.float32
bf16 = jnp.bfloat16
i32 = jnp.int32
u32 = jnp.uint32

DEPTH = 2
PAST_LEN = 1024
CHUNK = 64
A_PAST_CHUNKS = 8
REL_CLIP = 128
HA, DHA = 8, 64
A_WIDTH = HA * DHA
B_WIDTH = 512
HB = 8
CONV_W = 4
LRU_C = 8.0
HC, DKC, DVC = 4, 256, 512
C_QK, C_V = HC * DKC, HC * DVC
ROPE_BASE = 10000.0
GN_EPS = 1e-5
N_EXPERTS = 32
TOP_K = 4
D_FF = 1024
SWIGLU_LIMIT = 7.0
SWIGLU_ALPHA = 1.702
ALPHA = (2.0 * DEPTH) ** 0.25
LN_EPS = 1e-5
NEG = -1e30

LANES = 128
VMEM_LIMIT = 56 * 1024 * 1024
ATT_TQ = 256
MOE_BM = 256
MOE_BM_SHIFT = MOE_BM.bit_length() - 1
MOE_PAIR = 2
MOE_CH = 8
RET_TB = 512
LRU_TB = 256


def _pick(n, cands):
    for c in cands:
        if n % c == 0:
            return c
    raise ValueError(f"no tile for {n} in {cands}")


def _cparams(sem):
    return pltpu.CompilerParams(dimension_semantics=sem, vmem_limit_bytes=VMEM_LIMIT)


def _layernorm(v, g, b):
    mu = jnp.mean(v, axis=-1, keepdims=True)
    d = v - mu
    var = jnp.mean(d * d, axis=-1, keepdims=True)
    return d * lax.rsqrt(var + LN_EPS) * g + b


def _pack_bf16_pairs(v):
    w = v.shape[1] // 2
    hi = lax.bitcast_convert_type(v[:, :w], u32)
    lo = lax.bitcast_convert_type(v[:, w:], u32)
    return (hi & jnp.uint32(0xFFFF0000)) | (lo >> 16)


def _unpack_bf16_pairs(p):
    hi = lax.bitcast_convert_type(p & jnp.uint32(0xFFFF0000), f32)
    lo = lax.bitcast_convert_type(p << 16, f32)
    return jnp.concatenate([hi, lo], axis=1).astype(bf16)


def _round_up_bm(v):
    return lax.shift_left(lax.shift_right_logical(v + (MOE_BM - 1), MOE_BM_SHIFT), MOE_BM_SHIFT)


def _rel_bias_matrix(table, nq, nk, d0):
    p = nq + nk - 1
    j = jnp.arange(p)
    u = table[:, jnp.clip(d0 + nq - 1 - j, -REL_CLIP, REL_CLIP) + REL_CLIP]
    u = jnp.roll(u, -(nq - 1), axis=1)
    flat = jnp.tile(u, (1, nq))[:, :nq * (p - 1)]
    return flat.reshape(table.shape[0], nq, p - 1)[:, :, :nk].astype(f32)


def _mm_kernel(x_ref, w_ref, o_ref):
    o_ref[...] = jnp.dot(x_ref[...].astype(bf16), w_ref[...], preferred_element_type=f32)


def _matmul(x, w, tm, tn):
    m, k = x.shape
    n = w.shape[1]
    return pl.pallas_call(
        _mm_kernel,
        grid=(m // tm, n // tn),
        in_specs=[pl.BlockSpec((tm, k), lambda i, j: (i, 0)),
                  pl.BlockSpec((k, tn), lambda i, j: (0, j))],
        out_specs=pl.BlockSpec((tm, tn), lambda i, j: (i, j)),
        out_shape=jax.ShapeDtypeStruct((m, n), f32),
        compiler_params=_cparams(("parallel", "parallel")),
        name="dense_proj",
    )(x, w)


def _route_top_k(x, w, b):
    logits = jnp.dot(x.astype(bf16), w.astype(bf16), preferred_element_type=f32) + b
    tm = logits.shape[0]
    lane = lax.broadcasted_iota(i32, (tm, N_EXPERTS), 1)
    out_lane = lax.broadcasted_iota(i32, (tm, LANES), 1)
    e_out = jnp.zeros((tm, LANES), i32)
    g_out = jnp.zeros((tm, LANES), f32)
    v0 = None
    den = jnp.zeros((tm, 1), f32)
    for k in range(TOP_K):
        v = jnp.max(logits, axis=-1, keepdims=True)
        idx = jnp.min(jnp.where(logits == v, lane, N_EXPERTS), axis=-1, keepdims=True)
        logits = jnp.where(lane == idx, -jnp.inf, logits)
        if k == 0:
            v0 = v
        p = jnp.exp(v - v0)
        den = den + p
        e_out = jnp.where(out_lane == k, idx, e_out)
        g_out = jnp.where(out_lane == k, p, g_out)
    return e_out, g_out / den


def _proj_ln_kernel(*refs, n_in):
    a_refs = refs[:n_in]
    w_refs = refs[n_in:2 * n_in]
    x_ref, g_ref, b_ref, wr_ref, br_ref, o_ref, e_ref, gate_ref, cnt_ref = refs[2 * n_in:]
    acc = ALPHA * x_ref[...]
    for a_ref, w_ref in zip(a_refs, w_refs):
        acc = acc + jnp.dot(a_ref[...].astype(bf16), w_ref[...], preferred_element_type=f32)
    y = _layernorm(acc, g_ref[...], b_ref[...])
    o_ref[...] = y
    e_out, gates = _route_top_k(y, wr_ref[0], br_ref[0])
    e_ref[...] = e_out
    gate_ref[...] = gates
    nsub = cnt_ref.shape[0]
    td = e_out.shape[0] // nsub
    lane = lax.broadcasted_iota(i32, (td, LANES), 1)
    for u in range(nsub):
        eu = e_out[u * td:(u + 1) * td]
        cnt = jnp.zeros((1, LANES), f32)
        for k in range(TOP_K):
            cnt = cnt + jnp.sum((lane == eu[:, k:k + 1]).astype(f32), axis=0, keepdims=True)
        cnt_ref[u] = jnp.broadcast_to(cnt, cnt_ref.shape[1:])


def _proj_ln(acts, ws, x, g, b, w_router, b_router, l, tm, td):
    m, d = x.shape
    n_in = len(acts)
    depth = w_router.shape[0]
    nt = m // tm
    nsub = tm // td
    cur = lambda i: (i, 0)
    in_specs = ([pl.BlockSpec((tm, a.shape[1]), cur) for a in acts]
                + [pl.BlockSpec(w.shape, lambda i: (0, 0)) for w in ws]
                + [pl.BlockSpec((tm, d), cur),
                   pl.BlockSpec((1, d), lambda i: (0, 0)),
                   pl.BlockSpec((1, d), lambda i: (0, 0)),
                   pl.BlockSpec((1, d, N_EXPERTS), lambda i: (l, 0, 0)),
                   pl.BlockSpec((1, 1, N_EXPERTS), lambda i: (l, 0, 0))])
    return pl.pallas_call(
        functools.partial(_proj_ln_kernel, n_in=n_in),
        grid=(nt,),
        in_specs=in_specs,
        out_specs=[pl.BlockSpec((tm, d), cur),
                   pl.BlockSpec((tm, LANES), cur),
                   pl.BlockSpec((tm, LANES), cur),
                   pl.BlockSpec((nsub, 8, LANES), lambda i: (i, 0, 0))],
        out_shape=[jax.ShapeDtypeStruct((m, d), f32),
                   jax.ShapeDtypeStruct((m, LANES), i32),
                   jax.ShapeDtypeStruct((m, LANES), f32),
                   jax.ShapeDtypeStruct((m // td, 8, LANES), f32)],
        compiler_params=_cparams(("parallel",)),
        name="proj_ln",
    )(*acts, *ws, x, g.reshape(1, d), b.reshape(1, d), w_router, b_router.reshape(depth, 1, N_EXPERTS))


def _attn_prompt_kernel(q_ref, k0_ref, k1_ref, k2_ref, v0_ref, v1_ref, v2_ref, bias_ref, o_ref):
    i = pl.program_id(0)
    tq = q_ref.shape[0]
    q = q_ref[...]
    k = jnp.concatenate([k0_ref[...], k1_ref[...], k2_ref[...]], axis=0)
    v = jnp.concatenate([v0_ref[...], v1_ref[...], v2_ref[...]], axis=0)
    kcol = lax.broadcasted_iota(i32, (1, 3 * tq), 1)
    tile_ok = (kcol // tq + i) >= 2
    qb = (q * (DHA ** -0.5)).astype(bf16)
    kb = k.astype(bf16)
    vb = v.astype(bf16)
    outs = []
    for h in range(HA):
        sl = slice(h * DHA, (h + 1) * DHA)
        s = lax.dot_general(qb[:, sl], kb[:, sl], (((1,), (1,)), ((), ())), preferred_element_type=f32)
        s = jnp.where(tile_ok, s + bias_ref[h], NEG)
        p = jnp.exp(s - jnp.max(s, axis=-1, keepdims=True))
        o = jnp.dot(p.astype(bf16), vb[:, sl], preferred_element_type=f32)
        outs.append(o / jnp.sum(p, axis=-1, keepdims=True))
    o_ref[...] = jnp.concatenate(outs, axis=-1)


def _attn_prompt(z, rel_bias, tp):
    tq = ATT_TQ
    nq = tp // tq
    qq = jnp.arange(tq)[:, None]
    kk = jnp.arange(3 * tq)[None, :]
    band = (kk // CHUNK >= qq // CHUNK) & (kk // CHUNK <= qq // CHUNK + A_PAST_CHUNKS)
    bias = jnp.where(band[None], _rel_bias_matrix(rel_bias, tq, 3 * tq, 2 * tq), NEG)

    def kv_spec(back, col):
        return pl.BlockSpec((tq, A_WIDTH), lambda i: (jnp.maximum(i - back, 0), col))

    return pl.pallas_call(
        _attn_prompt_kernel,
        grid=(nq,),
        in_specs=[pl.BlockSpec((tq, A_WIDTH), lambda i: (i, 0)),
                  kv_spec(2, 1), kv_spec(1, 1), kv_spec(0, 1),
                  kv_spec(2, 2), kv_spec(1, 2), kv_spec(0, 2),
                  pl.BlockSpec((HA, tq, 3 * tq), lambda i: (0, 0, 0))],
        out_specs=pl.BlockSpec((tq, A_WIDTH), lambda i: (i, 0)),
        out_shape=jax.ShapeDtypeStruct((tp, A_WIDTH), f32),
        compiler_params=_cparams(("parallel",)),
        name="attn_prompt",
    )(z, z, z, z, z, z, z, bias)


def _attn_sample_kernel(q_ref, kn_ref, vn_ref, kc_ref, vc_ref, bias_ref, o_ref):
    p_len = kc_ref.shape[1]
    q, kn, vn = q_ref[...], kn_ref[...], vn_ref[...]
    kc, vc = kc_ref[0], vc_ref[0]
    outs = []
    for h in range(HA):
        sl = slice(h * DHA, (h + 1) * DHA)
        qh = q[:, sl].astype(bf16)
        dims = (((1,), (1,)), ((), ()))
        sc = lax.dot_general(qh, kc[:, sl].astype(bf16), dims, preferred_element_type=f32)
        sn = lax.dot_general(qh, kn[:, sl].astype(bf16), dims, preferred_element_type=f32)
        b = bias_ref[h]
        sc = sc * (DHA ** -0.5) + b[:, :p_len]
        sn = sn * (DHA ** -0.5) + b[:, p_len:]
        m = jnp.maximum(jnp.max(sc, axis=-1, keepdims=True), jnp.max(sn, axis=-1, keepdims=True))
        pc, pn = jnp.exp(sc - m), jnp.exp(sn - m)
        den = jnp.sum(pc, axis=-1, keepdims=True) + jnp.sum(pn, axis=-1, keepdims=True)
        pc, pn = pc / den, pn / den
        outs.append(jnp.dot(pc.astype(bf16), vc[:, sl].astype(bf16), preferred_element_type=f32)
                    + jnp.dot(pn.astype(bf16), vn[:, sl].astype(bf16), preferred_element_type=f32))
    o_ref[...] = jnp.concatenate(outs, axis=-1)


def _attn_sample(z, k_cache, v_cache, rel_bias, row0, bs, ts):
    p_len = k_cache.shape[1]
    kc = k_cache.reshape(bs, p_len, A_WIDTH)
    vc = v_cache.reshape(bs, p_len, A_WIDTH)
    bias = _rel_bias_matrix(rel_bias, ts, p_len + ts, p_len)
    blk0 = row0 // ts

    def z_spec(col):
        return pl.BlockSpec((ts, A_WIDTH), lambda b: (blk0 + b, col))

    return pl.pallas_call(
        _attn_sample_kernel,
        grid=(bs,),
        in_specs=[z_spec(0), z_spec(1), z_spec(2),
                  pl.BlockSpec((1, p_len, A_WIDTH), lambda b: (b, 0, 0)),
                  pl.BlockSpec((1, p_len, A_WIDTH), lambda b: (b, 0, 0)),
                  pl.BlockSpec((HA, ts, p_len + ts), lambda b: (0, 0, 0))],
        out_specs=pl.BlockSpec((ts, A_WIDTH), lambda b: (b, 0)),
        out_shape=jax.ShapeDtypeStruct((bs * ts, A_WIDTH), f32),
        compiler_params=_cparams(("parallel",)),
        name="attn_sample",
    )(z, z, z, kc, vc, bias)


def _gelu_tanh(x):
    return 0.5 * x * (1.0 + jnp.tanh(math.sqrt(2.0 / math.pi) * (x + 0.044715 * (x * x * x))))


def _rglru_kernel(xb_ref, gb_ref, conv0_ref, h0_ref, cw_ref, cb_ref, wrg_ref, brg_ref,
                  wig_ref, big_ref, sp_ref, rec_ref, convn_ref, hl_ref, xp_ref, hc_ref):
    t = pl.program_id(1)
    tb = xb_ref.shape[0]
    pad = 8

    @pl.when(t == 0)
    def _():
        xp_ref[0:pad, :] = jnp.zeros((pad, B_WIDTH), f32)
        xp_ref[pad - (CONV_W - 1):pad, :] = conv0_ref[0]
        hc_ref[...] = h0_ref[0]

    xb = xb_ref[...]
    xp_ref[pad:pad + tb, :] = xb
    cw = cw_ref[...]
    u = cb_ref[...] + cw[CONV_W - 1:CONV_W, :] * xb
    for j in range(CONV_W - 1):
        sh = CONV_W - 1 - j
        u = u + cw[j:j + 1, :] * xp_ref[pad - sh:pad - sh + tb, :]
    convn_ref[0] = xp_ref[pad + tb - (CONV_W - 1):pad + tb, :]
    xp_ref[0:pad, :] = xp_ref[tb:tb + pad, :]

    ub = u.astype(bf16)
    r = jax.nn.sigmoid(jnp.dot(ub, wrg_ref[...], preferred_element_type=f32) + brg_ref[...])
    ig = jax.nn.sigmoid(jnp.dot(ub, wig_ref[...], preferred_element_type=f32) + big_ref[...])
    log_a = -LRU_C * r * sp_ref[...]
    a = jnp.exp(log_a)
    bt = jnp.sqrt(-jnp.tanh(log_a) * (a * a + 1.0)) * (ig * u)

    row = lax.broadcasted_iota(i32, (tb, 1), 0)
    s = 1
    while s < tb:
        keep = row >= s
        a_sh = pltpu.roll(a, s, axis=0)
        b_sh = pltpu.roll(bt, s, axis=0)
        bt = jnp.where(keep, a * b_sh + bt, bt)
        a = jnp.where(keep, a * a_sh, a)
        s *= 2
    h = a * hc_ref[...] + bt
    hc_ref[...] = h[tb - 1:tb, :]
    hl_ref[0] = h[tb - 1:tb, :]
    rec_ref[...] = h * _gelu_tanh(gb_ref[...])


def _rglru(z, conv0, h0, cw, cb, wrg, brg, wig, big, sp, row0, bsz, t_len, tb):
    nt = t_len // tb
    blk0 = row0 // tb

    def z_spec(col):
        return pl.BlockSpec((tb, B_WIDTH), lambda b, t: (blk0 + b * nt + t, col))

    def const(shape):
        return pl.BlockSpec(shape, lambda b, t: (0,) * len(shape))

    row = lambda v: v.reshape(1, B_WIDTH)
    return pl.pallas_call(
        _rglru_kernel,
        grid=(bsz, nt),
        in_specs=[z_spec(3), z_spec(4),
                  pl.BlockSpec((1, CONV_W - 1, B_WIDTH), lambda b, t: (b, 0, 0)),
                  pl.BlockSpec((1, 1, B_WIDTH), lambda b, t: (b, 0, 0)),
                  const((CONV_W, B_WIDTH)), const((1, B_WIDTH)),
                  const((B_WIDTH, B_WIDTH)), const((1, B_WIDTH)),
                  const((B_WIDTH, B_WIDTH)), const((1, B_WIDTH)), const((1, B_WIDTH))],
        out_specs=[pl.BlockSpec((tb, B_WIDTH), lambda b, t: (b * nt + t, 0)),
                   pl.BlockSpec((1, CONV_W - 1, B_WIDTH), lambda b, t: (b, 0, 0)),
                   pl.BlockSpec((1, 1, B_WIDTH), lambda b, t: (b, 0, 0))],
        out_shape=[jax.ShapeDtypeStruct((bsz * t_len, B_WIDTH), f32),
                   jax.ShapeDtypeStruct((bsz, CONV_W - 1, B_WIDTH), f32),
                   jax.ShapeDtypeStruct((bsz, 1, B_WIDTH), f32)],
        scratch_shapes=[pltpu.VMEM((tb + 8, B_WIDTH), f32), pltpu.VMEM((1, B_WIDTH), f32)],
        compiler_params=_cparams(("parallel", "arbitrary")),
        name="rglru",
    )(z, z, conv0, h0.reshape(bsz, 1, B_WIDTH), cw, row(cb), wrg, row(brg), wig, row(big), row(sp))


def _block_diag(w):
    hb, d, _ = w.shape
    eye = jnp.eye(hb, dtype=w.dtype)
    return (eye[:, None, :, None] * w[:, :, None, :]).reshape(hb * d, hb * d)


def _retention_kernel(x_ref, w_hbm, cos_ref, sin_ref, s0_hbm, gng_ref, gnb_ref,
                      y_ref, sout_hbm, w_ref, s_ref, z_ref, sem, *, chunk, nt, tb, zero_state, hoist):
    b = pl.program_id(0)
    t = pl.program_id(1)
    half = DKC // 2

    @pl.when((b == 0) & (t == 0))
    def _():
        cp = pltpu.make_async_copy(w_hbm, w_ref, sem.at[0])
        cp.start()
        cp.wait()
        if hoist:
            xa = x_ref[...].astype(bf16)
            for c in range(0, w_ref.shape[1], C_QK):
                z_ref[:, c:c + C_QK] = jnp.dot(xa, w_ref[:, c:c + C_QK], preferred_element_type=f32)

    @pl.when(t == 0)
    def _():
        if zero_state:
            s_ref[...] = jnp.zeros(s_ref.shape, f32)
        else:
            cp = pltpu.make_async_copy(s0_hbm.at[b], s_ref, sem.at[1])
            cp.start()
            cp.wait()

    cos, sin = cos_ref[...], sin_ref[...]
    n = lax.broadcasted_iota(i32, (tb, 1), 0)
    m = lax.broadcasted_iota(i32, (1, tb), 1)
    dist = n - m
    same = (n // chunk) == (m // chunk)
    expo = jnp.where(same, jnp.abs(dist), dist).astype(f32)
    visible = same | (dist > 0)
    nf = n.astype(f32)

    if hoist:
        row0 = pl.multiple_of(b * tb, tb)

        def proj(col, width):
            return z_ref[pl.ds(row0, tb), col:col + width]
    else:
        xb = x_ref[...].astype(bf16)

        def proj(col, width):
            return jnp.dot(xb, w_ref[:, col:col + width], preferred_element_type=f32)

    def rot(v, scale):
        x1, x2 = v[:, :half], v[:, half:]
        return jnp.concatenate([x1 * cos - x2 * sin, x2 * cos + x1 * sin], axis=-1) * scale

    for h in range(HC):
        lg = math.log(1.0 - 2.0 ** (-5.0 - h))
        qr = rot(proj(h * DKC, DKC), 1.0)
        kr = rot(proj(C_QK + h * DKC, DKC), DKC ** -0.5)
        vb = proj(2 * C_QK + h * DVC, DVC).astype(bf16)
        gate = proj(2 * C_QK + C_V + h * DVC, DVC)
        qb = qr.astype(bf16)
        dmat = jnp.where(visible, jnp.exp(lg * expo), 0.0)
        s = lax.dot_general(qb, kr.astype(bf16), (((1,), (1,)), ((), ())),
                            preferred_element_type=f32) * dmat
        s_old = s_ref[h]
        o = jnp.dot(s.astype(bf16), vb, preferred_element_type=f32)
        o = o + jnp.dot(qb, s_old.astype(bf16), preferred_element_type=f32) * jnp.exp(lg * (nf + 1.0))
        kd = (kr * jnp.exp(lg * (tb - 1.0 - nf))).astype(bf16)
        s_ref[h] = math.exp(lg * tb) * s_old + lax.dot_general(
            kd, vb, (((0,), (0,)), ((), ())), preferred_element_type=f32)

        mu = jnp.mean(o, axis=-1, keepdims=True)
        dlt = o - mu
        var = jnp.mean(dlt * dlt, axis=-1, keepdims=True)
        vs = slice(h * DVC, (h + 1) * DVC)
        yn = dlt * lax.rsqrt(var + GN_EPS) * gng_ref[:, vs] + gnb_ref[:, vs]
        y_ref[:, vs] = (jax.nn.silu(gate) * yn).astype(y_ref.dtype)

    @pl.when(t == nt - 1)
    def _():
        cp = pltpu.make_async_copy(s_ref, sout_hbm.at[b], sem.at[1])
        cp.start()
        cp.wait()


def _retention(x, w_in, s0, pos0, gn_g, gn_b, bsz, t_len, tb, chunk):
    nt = t_len // tb
    d = x.shape[1]
    half = DKC // 2
    inv = 1.0 / (ROPE_BASE ** (jnp.arange(half, dtype=f32) / half))
    ang = (pos0 + jnp.arange(t_len)).astype(f32)[:, None] * inv[None, :]
    cos, sin = jnp.cos(ang), jnp.sin(ang)
    zero_state = s0 is None
    if zero_state:
        s0 = jnp.zeros((1, 8, LANES), f32)
    hoist = nt == 1 and bsz > 1
    x_spec = (pl.BlockSpec(x.shape, lambda b, t: (0, 0)) if hoist
              else pl.BlockSpec((tb, d), lambda b, t: (b * nt + t, 0)))
    z_shape = (bsz * tb, w_in.shape[1]) if hoist else (8, LANES)
    return pl.pallas_call(
        functools.partial(_retention_kernel, chunk=chunk, nt=nt, tb=tb, zero_state=zero_state, hoist=hoist),
        grid=(bsz, nt),
        in_specs=[x_spec,
                  pl.BlockSpec(memory_space=pl.ANY),
                  pl.BlockSpec((tb, half), lambda b, t: (t, 0)),
                  pl.BlockSpec((tb, half), lambda b, t: (t, 0)),
                  pl.BlockSpec(memory_space=pl.ANY),
                  pl.BlockSpec((1, C_V), lambda b, t: (0, 0)),
                  pl.BlockSpec((1, C_V), lambda b, t: (0, 0))],
        out_specs=[pl.BlockSpec((tb, C_V), lambda b, t: (b * nt + t, 0)),
                   pl.BlockSpec(memory_space=pl.ANY)],
        out_shape=[jax.ShapeDtypeStruct((bsz * t_len, C_V), bf16),
                   jax.ShapeDtypeStruct((bsz, HC, DKC, DVC), f32)],
        scratch_shapes=[pltpu.VMEM(w_in.shape, bf16), pltpu.VMEM((HC, DKC, DVC), f32),
                        pltpu.VMEM(z_shape, f32), pltpu.SemaphoreType.DMA((2,))],
        compiler_params=_cparams(("arbitrary", "arbitrary")),
        name="retention",
    )(x, w_in, cos, sin, s0, gn_g.reshape(1, C_V), gn_b.reshape(1, C_V))


def _dispatch_kernel(ech_ref, est_ref, tab_ref, tabv_ref, ep_ref, es_ref, xp_ref, xs_ref,
                     xg_hbm, dest_ref, sbuf, zchunk, pending, sem, *, np_tiles, nt):
    i = pl.program_id(0)
    td = xp_ref.shape[0]
    ls = sbuf.shape[1]
    slot = i % 2
    nchunks = xg_hbm.shape[0]
    blk_chunks = MOE_BM // MOE_CH

    def wait_chunks(s, count, max_bits):
        for b in reversed(range(max_bits)):
            @pl.when(jnp.bitwise_and(count, 1 << b) != 0)
            def _():
                span = xg_hbm.at[pl.ds(0, 1 << b)]
                pltpu.make_async_copy(span, span, sem.at[s]).wait()

    def drain(s):
        wait_chunks(s, pending[s], (ls // MOE_CH).bit_length())

    @pl.when(i == 0)
    def _():
        pending[0] = 0
        pending[1] = 0
        zchunk[...] = jnp.zeros(zchunk.shape, zchunk.dtype)

        def zero_chunks(lo, hi):
            def put(c, carry):
                pltpu.make_async_copy(zchunk, xg_hbm.at[c], sem.at[2]).start()
                return carry
            lax.fori_loop(lo, hi, put, 0)
            wait_chunks(2, jnp.maximum(hi - lo, 0), nchunks.bit_length())

        def per_expert(e, carry):
            used = est_ref[e] + ech_ref[e]
            zero_chunks(used, used + jnp.bitwise_and(-ech_ref[e], blk_chunks - 1))
            return carry
        lax.fori_loop(0, N_EXPERTS, per_expert, 0)
        last = N_EXPERTS - 1
        end = est_ref[last] + ech_ref[last] + jnp.bitwise_and(-ech_ref[last], blk_chunks - 1)
        zero_chunks(end, nchunks)

    def sort_and_copy(e_ref, x_ref):
        e = e_ref[...]
        lane = lax.broadcasted_iota(i32, (td, LANES), 1)
        member = jnp.zeros((td, LANES), f32)
        for k in range(TOP_K):
            member = member + (lane == e[:, k:k + 1]).astype(f32)
        r = lax.broadcasted_iota(i32, (td, td), 0)
        c = lax.broadcasted_iota(i32, (td, td), 1)
        before = jnp.dot((c < r).astype(bf16), member.astype(bf16), preferred_element_type=f32)
        in_tile = before + tabv_ref[0, 0:1, :]
        in_all = before + tabv_ref[0, 1:2, :]
        col = lax.broadcasted_iota(i32, (td, ls), 1)
        place = jnp.zeros((td, ls), f32)
        dest = jnp.zeros((td, LANES), i32)
        for k in range(TOP_K):
            sel = lane == e[:, k:k + 1]
            row_k = jnp.sum(jnp.where(sel, in_tile, 0.0), axis=-1, keepdims=True).astype(i32)
            dst_k = jnp.sum(jnp.where(sel, in_all, 0.0), axis=-1, keepdims=True).astype(i32)
            place = place + (col == row_k).astype(f32)
            dest = jnp.where(lane == k, dst_k, dest)
        dest_ref[...] = dest
        srt = lax.dot_general(place.astype(bf16), x_ref[...].astype(bf16), (((0,), (0,)), ((), ())),
                              preferred_element_type=f32)
        drain(slot)
        sbuf[slot] = _pack_bf16_pairs(srt)

        total = 0
        for ei in range(N_EXPERTS):
            dst0 = tab_ref[0, 0, ei]
            n = tab_ref[0, 0, N_EXPERTS + ei]
            src0 = tab_ref[0, 0, 2 * N_EXPERTS + ei]

            def put(cc, carry):
                row = pl.multiple_of((src0 + cc) * MOE_CH, MOE_CH)
                pltpu.make_async_copy(sbuf.at[slot, pl.ds(row, MOE_CH)], xg_hbm.at[dst0 + cc],
                                      sem.at[slot]).start(priority=ei % 2)
                return carry
            lax.fori_loop(0, n, put, 0)
            total = total + n
        pending[slot] = total

    @pl.when(i < np_tiles)
    def _():
        sort_and_copy(ep_ref, xp_ref)

    @pl.when(i >= np_tiles)
    def _():
        sort_and_copy(es_ref, xs_ref)

    @pl.when(i == nt - 1)
    def _():
        drain(0)
        drain(1)


def _dispatch(xs, top_es, tab, tabv, ech, est, nblk, td):
    x_p, x_s = xs
    d = x_p.shape[1]
    np_tiles, ns_tiles = x_p.shape[0] // td, x_s.shape[0] // td
    nt = np_tiles + ns_tiles
    ls = td * TOP_K + N_EXPERTS * MOE_CH
    pspec = lambda w: pl.BlockSpec((td, w), lambda i, *_: (jnp.minimum(i, np_tiles - 1), 0))
    sspec = lambda w: pl.BlockSpec((td, w), lambda i, *_: (jnp.maximum(i - np_tiles, 0), 0))
    grid_spec = pltpu.PrefetchScalarGridSpec(
        num_scalar_prefetch=2,
        grid=(nt,),
        in_specs=[pl.BlockSpec((1, 1, LANES), lambda i, *_: (i, 0, 0), memory_space=pltpu.SMEM),
                  pl.BlockSpec((1, 8, LANES), lambda i, *_: (i, 0, 0)),
                  pspec(LANES), sspec(LANES), pspec(d), sspec(d)],
        out_specs=[pl.BlockSpec(memory_space=pl.ANY),
                   pl.BlockSpec((td, LANES), lambda i, *_: (i, 0))],
        scratch_shapes=[pltpu.VMEM((2, ls, d // 2), u32), pltpu.VMEM((MOE_CH, d // 2), u32),
                        pltpu.SMEM((2,), i32), pltpu.SemaphoreType.DMA((3,))],
    )
    xg, dest = pl.pallas_call(
        functools.partial(_dispatch_kernel, np_tiles=np_tiles, nt=nt),
        grid_spec=grid_spec,
        out_shape=[jax.ShapeDtypeStruct((nblk * MOE_BM // MOE_CH, MOE_CH, d // 2), u32),
                   jax.ShapeDtypeStruct((nt * td, LANES), i32)],
        compiler_params=_cparams(("arbitrary",)),
        name="moe_dispatch",
    )(ech, est, tab, tabv, top_es[0], top_es[1], x_p, x_s)
    return xg.reshape(nblk * MOE_BM, d // 2), dest


def _moe_kernel(blk_e_ref, nact_ref, first_ref, seg_ref, nxt_ref, x_ref, wup_hbm, bup_ref, wdn_hbm, bdn_ref,
                y_ref, wup_f, wdn_f, wup_bf, wdn_bf, sem, *, layer):
    b0 = pl.program_id(0) * MOE_PAIR
    nact = nact_ref[0]
    bm = MOE_BM

    def weight_copies(e, slot):
        return (pltpu.make_async_copy(wup_hbm.at[layer, e], wup_f.at[slot], sem.at[slot, 0]),
                pltpu.make_async_copy(wdn_hbm.at[layer, e], wdn_f.at[slot], sem.at[slot, 1]))

    @pl.when(b0 < nact)
    def _():
        for sub in range(MOE_PAIR):
            b = b0 + sub

            @pl.when(first_ref[b] == 1)
            def _():
                slot = seg_ref[b] % 2

                @pl.when(b == 0)
                def _():
                    for c in weight_copies(blk_e_ref[0], 0):
                        c.start()

                for c in weight_copies(blk_e_ref[b], slot):
                    c.wait()

                @pl.when(nxt_ref[b] >= 0)
                def _():
                    for c in weight_copies(nxt_ref[b], 1 - slot):
                        c.start()

                wup_bf[slot] = wup_f[slot].astype(bf16)
                wdn_bf[slot] = wdn_f[slot].astype(bf16)

        for sub in range(MOE_PAIR):
            b = b0 + sub
            slot = seg_ref[b] % 2
            e = layer * N_EXPERTS + blk_e_ref[b]
            rows = slice(sub * bm, (sub + 1) * bm)
            hdn = jnp.dot(_unpack_bf16_pairs(x_ref[rows, :]), wup_bf[slot], preferred_element_type=f32)
            hdn = hdn + bup_ref[pl.ds(e, 1), :]
            glu = jnp.minimum(hdn[:, :D_FF], SWIGLU_LIMIT)
            lin = jnp.clip(hdn[:, D_FF:], -SWIGLU_LIMIT, SWIGLU_LIMIT)
            act = glu * jax.nn.sigmoid(SWIGLU_ALPHA * glu) * (lin + 1.0)
            y_ref[rows, :] = (jnp.dot(act.astype(bf16), wdn_bf[slot], preferred_element_type=f32)
                              + bdn_ref[pl.ds(e, 1), :])

    @pl.when(b0 >= nact)
    def _():
        y_ref[...] = jnp.zeros(y_ref.shape, f32)


def _moe_experts(xg, blk_e, nact, first, seg, nxt, l, w_up, b_up, w_down, b_down):
    r, half = xg.shape
    d = 2 * half
    rows = MOE_BM * MOE_PAIR
    depth = w_up.shape[0]
    whole = lambda shape: pl.BlockSpec(shape, lambda s, *_: (0,) * len(shape))
    grid_spec = pltpu.PrefetchScalarGridSpec(
        num_scalar_prefetch=5,
        grid=(r // rows,),
        in_specs=[
            pl.BlockSpec((rows, half), lambda s, be, na, *_: (jnp.minimum(s, (na[0] - 1) // MOE_PAIR), 0)),
            pl.BlockSpec(memory_space=pl.ANY),
            whole((depth * N_EXPERTS, 2 * D_FF)),
            pl.BlockSpec(memory_space=pl.ANY),
            whole((depth * N_EXPERTS, d)),
        ],
        out_specs=pl.BlockSpec((rows, d), lambda s, *_: (s, 0)),
        scratch_shapes=[pltpu.VMEM((2, d, 2 * D_FF), f32), pltpu.VMEM((2, D_FF, d), f32),
                        pltpu.VMEM((2, d, 2 * D_FF), bf16), pltpu.VMEM((2, D_FF, d), bf16),
                        pltpu.SemaphoreType.DMA((2, 2))],
    )
    return pl.pallas_call(
        functools.partial(_moe_kernel, layer=l),
        grid_spec=grid_spec,
        out_shape=jax.ShapeDtypeStruct((r, d), f32),
        compiler_params=_cparams(("arbitrary",)),
        name="moe_experts",
    )(blk_e, nact, first, seg, nxt, xg, w_up, b_up.reshape(depth * N_EXPERTS, 2 * D_FF),
      w_down, b_down.reshape(depth * N_EXPERTS, d))


def _combine_ln_kernel(pos_cur_ref, pos_nxt_ref, y_hbm, gate_ref, x_ref, g_ref, b_ref, o_ref, gbuf, sem,
                       *, nt):
    i = pl.program_id(0)
    tm = x_ref.shape[0]

    def gather(pos_ref, slot):
        def body(r, carry):
            for k in range(TOP_K):
                p = pos_ref[0, 0, r * TOP_K + k]
                pltpu.make_async_copy(y_hbm.at[pl.ds(p, 1)], gbuf.at[slot, k, pl.ds(r, 1)],
                                      sem.at[slot]).start(priority=k % 2)
            return carry
        lax.fori_loop(0, tm, body, 0, unroll=4)

    @pl.when(i == 0)
    def _():
        gather(pos_cur_ref, 0)

    @pl.when(i + 1 < nt)
    def _():
        gather(pos_nxt_ref, (i + 1) % 2)

    slot = i % 2
    pltpu.make_async_copy(gbuf.at[slot], gbuf.at[slot], sem.at[slot]).wait()
    gates = gate_ref[...]
    acc = ALPHA * x_ref[...]
    for k in range(TOP_K):
        acc = acc + gates[:, k:k + 1] * gbuf[slot, k]
    o_ref[...] = _layernorm(acc, g_ref[...], b_ref[...])


def _combine_ln(y, pos3, blk0, gates, x, g, b, tm):
    rows, d = x.shape
    nt = rows // tm
    grid_spec = pltpu.PrefetchScalarGridSpec(
        num_scalar_prefetch=0,
        grid=(nt,),
        in_specs=[
            pl.BlockSpec((1, 1, tm * TOP_K), lambda i: (blk0 + i, 0, 0), memory_space=pltpu.SMEM),
            pl.BlockSpec((1, 1, tm * TOP_K), lambda i: (blk0 + jnp.minimum(i + 1, nt - 1), 0, 0),
                         memory_space=pltpu.SMEM),
            pl.BlockSpec(memory_space=pl.ANY),
            pl.BlockSpec((tm, LANES), lambda i: (i, 0)),
            pl.BlockSpec((tm, d), lambda i: (i, 0)),
            pl.BlockSpec((1, d), lambda i: (0, 0)),
            pl.BlockSpec((1, d), lambda i: (0, 0)),
        ],
        out_specs=pl.BlockSpec((tm, d), lambda i: (i, 0)),
        scratch_shapes=[pltpu.VMEM((2, TOP_K, tm, d), f32), pltpu.SemaphoreType.DMA((2,))],
    )
    return pl.pallas_call(
        functools.partial(_combine_ln_kernel, nt=nt),
        grid_spec=grid_spec,
        out_shape=jax.ShapeDtypeStruct((rows, d), f32),
        compiler_params=_cparams(("arbitrary",)),
        name="combine_ln",
    )(pos3, pos3, y, gates, x, g.reshape(1, d), b.reshape(1, d))


def _moe_ln(xs, top_es, gates, cnts, td, l, w_up, b_up, w_down, b_down, g, b):
    x_p, x_s = xs
    n_p, n_s = x_p.shape[0], x_s.shape[0]
    n = n_p + n_s
    bm = MOE_BM
    nt = n // td
    nblk = -(-(n * TOP_K + nt * N_EXPERTS * (MOE_CH - 1)) // bm) + N_EXPERTS
    nblk = -(-nblk // MOE_PAIR) * MOE_PAIR
    cnt = jnp.concatenate([c[:, 0, :N_EXPERTS] for c in cnts], axis=0).astype(i32)
    run = (cnt + MOE_CH - 1) // MOE_CH
    in_tile = jnp.cumsum(run, axis=1) - run
    ech = jnp.sum(run, axis=0)
    blk_chunks = bm // MOE_CH
    padded_ch = (ech + blk_chunks - 1) // blk_chunks * blk_chunks
    est = jnp.cumsum(padded_ch) - padded_ch
    in_all = est[None, :] + jnp.cumsum(run, axis=0) - run
    tab = jnp.concatenate([in_all, run, in_tile, jnp.zeros_like(run)], axis=1).reshape(nt, 1, LANES)
    tabv = jnp.zeros((nt, 8, LANES), f32)
    tabv = tabv.at[:, 0, :N_EXPERTS].set((in_tile * MOE_CH).astype(f32))
    tabv = tabv.at[:, 1, :N_EXPERTS].set((in_all * MOE_CH).astype(f32))
    xg, dest = _dispatch(xs, top_es, tab, tabv, ech, est, nblk, td)
    padded = padded_ch * MOE_CH
    ends = jnp.cumsum(padded)
    blk = jnp.arange(nblk, dtype=i32)
    blk_e = jnp.minimum(jnp.sum((ends[None, :] <= (blk * bm)[:, None]).astype(i32), axis=1), N_EXPERTS - 1)
    nact = (ends[-1] // bm).astype(i32).reshape(1)
    first = ((blk < nact[0]) & ((blk == 0) | (blk_e != jnp.roll(blk_e, 1)))).astype(i32)
    seg = jnp.cumsum(first) - 1
    eid = jnp.arange(N_EXPERTS, dtype=i32)
    later_nonempty = (padded[None, :] > 0) & (eid[None, :] > eid[:, None])
    nxt_of_e = jnp.min(jnp.where(later_nonempty, eid[None, :], N_EXPERTS), axis=1)
    nxt = jnp.where(nxt_of_e < N_EXPERTS, nxt_of_e, -1)[blk_e].astype(i32)
    dest3 = dest[:, :TOP_K].reshape(nt, 1, td * TOP_K)
    y = _moe_experts(xg, blk_e, nact, first, seg, nxt, l, w_up, b_up, w_down, b_down)
    return (_combine_ln(y, dest3, 0, gates[0], x_p, g, b, td),
            _combine_ln(y, dest3, n_p // td, gates[1], x_s, g, b, td))


def kernel(x_prompt, x_sample, cache_a_k, cache_a_v, state_b_conv, state_b_h, state_c_s, ab_w_in, ab_rel_bias, ab_conv_w, ab_conv_b, ab_w_rg, ab_b_rg, ab_w_ig, ab_b_ig, ab_lambda, ab_w_out, c_w_in, c_gn_g, c_gn_b, c_w_out, ln1_g, ln1_b, ln2_g, ln2_b, moe_w_router, moe_b_router, moe_w_up, moe_b_up, moe_w_down, moe_b_down):
    bp, tp, d = x_prompt.shape
    bs, ts, _ = x_sample.shape
    assert bp == 1 and tp % RET_TB == 0 and ts <= CHUNK
    n_p, n_s = bp * tp, bs * ts
    xs = (x_prompt.reshape(n_p, d), x_sample.reshape(n_s, d))
    tms = tuple(_pick(m, (512, 256, 128, 64)) for m in (n_p, n_s))
    td = _pick(math.gcd(n_p, n_s), (256, 128, 64, 32))
    moe = (moe_w_up, moe_b_up, moe_w_down, moe_b_down)
    router = (moe_w_router, moe_b_router)

    def per_group(fn, *groups):
        return tuple(zip(*[fn(*args) for args in zip(*groups)]))

    j = 0
    w_in = ab_w_in[j].astype(bf16)
    z_p, z_s = (_matmul(x, w_in, tm, w_in.shape[1]) for x, tm in zip(xs, tms))
    att = (_attn_prompt(z_p, ab_rel_bias[j], tp),
           _attn_sample(z_s, cache_a_k[j], cache_a_v[j], ab_rel_bias[j], 0, bs, ts))
    lru_w = (ab_conv_w[j], ab_conv_b[j], _block_diag(ab_w_rg[j]).astype(bf16), ab_b_rg[j],
             _block_diag(ab_w_ig[j]).astype(bf16), ab_b_ig[j], jax.nn.softplus(-ab_lambda[j]))
    rec_p, bc_p, bh_p = _rglru(z_p, jnp.zeros((bp, CONV_W - 1, B_WIDTH), f32), jnp.zeros((bp, B_WIDTH), f32),
                               *lru_w, 0, bp, tp, _pick(tp, (LRU_TB, 128, 64)))
    rec_s, bc_s, bh_s = _rglru(z_s, state_b_conv[j], state_b_h[j], *lru_w, 0, bs, ts, ts)
    w_out = ab_w_out[j].astype(bf16)
    w_out = [w_out[:A_WIDTH], w_out[A_WIDTH:]]
    xs, top_es, gates, cnts = per_group(
        lambda a, r, x, tm: _proj_ln([a, r], w_out, x, ln1_g[0], ln1_b[0], *router, 0, tm, td),
        att, (rec_p, rec_s), xs, tms)
    xs = _moe_ln(xs, top_es, gates, cnts, td, 0, *moe, ln2_g[0], ln2_b[0])

    keep = min(A_PAST_CHUNKS * CHUNK, tp)
    ak_p = z_p[n_p - keep:, A_WIDTH:2 * A_WIDTH].reshape(1, bp, keep, HA, DHA)
    av_p = z_p[n_p - keep:, 2 * A_WIDTH:3 * A_WIDTH].reshape(1, bp, keep, HA, DHA)
    ak_s = z_s[:, A_WIDTH:2 * A_WIDTH].reshape(1, bs, ts, HA, DHA)
    av_s = z_s[:, 2 * A_WIDTH:3 * A_WIDTH].reshape(1, bs, ts, HA, DHA)

    w_in = c_w_in[j].astype(bf16)
    y_p, cs_p = _retention(xs[0], w_in, None, 0, c_gn_g[j], c_gn_b[j], bp, tp, RET_TB, CHUNK)
    y_s, cs_s = _retention(xs[1], w_in, state_c_s[j], PAST_LEN, c_gn_g[j], c_gn_b[j], bs, ts, ts, ts)
    w_out = [c_w_out[j].astype(bf16)]
    xs, top_es, gates, cnts = per_group(
        lambda y, x, tm: _proj_ln([y], w_out, x, ln1_g[1], ln1_b[1], *router, 1, tm, td),
        (y_p, y_s), xs, tms)
    out_p, out_s = _moe_ln(xs, top_es, gates, cnts, td, 1, *moe, ln2_g[1], ln2_b[1])

    return (out_p.reshape(bp, tp, d), out_s.reshape(bs, ts, d),
            ak_p, av_p, bc_p[None], bh_p.reshape(1, bp, B_WIDTH), cs_p[None],
            ak_s, av_s, bc_s[None], bh_s.reshape(1, bs, B_WIDTH), cs_s[None])
```

```python
import functools
import math

import jax
import jax.numpy as jnp
from jax import lax
from jax.experimental import pallas as pl
from jax.experimental.pallas import tpu as pltpu

f32 = jnp.float32
bf16 = jnp.bfloat16
i32 = jnp.int32
u32 = jnp.uint32

DEPTH = 2
PAST_LEN = 1024
CHUNK = 64
A_PAST_CHUNKS = 8
REL_CLIP = 128
HA, DHA = 8, 64
A_WIDTH = HA * DHA
B_WIDTH = 512
HB = 8
CONV_W = 4
LRU_C = 8.0
HC, DKC, DVC = 4, 256, 512
C_QK, C_V = HC * DKC, HC * DVC
ROPE_BASE = 10000.0
GN_EPS = 1e-5
N_EXPERTS = 32
TOP_K = 4
D_FF = 1024
SWIGLU_LIMIT = 7.0
SWIGLU_ALPHA = 1.702
ALPHA = (2.0 * DEPTH) ** 0.25
LN_EPS = 1e-5
NEG = -1e30

LANES = 128
VMEM_LIMIT = 56 * 1024 * 1024
ATT_TQ = 256
MOE_BM = 256
MOE_BM_SHIFT = MOE_BM.bit_length() - 1
MOE_PAIR = 2
MOE_CH = 8
RET_TB = 512
LRU_TB = 256


def _pick(n, cands):
    for c in cands:
        if n % c == 0:
            return c
    raise ValueError(f"no tile for {n} in {cands}")


def _cparams(sem):
    return pltpu.CompilerParams(dimension_semantics=sem, vmem_limit_bytes=VMEM_LIMIT)


def _layernorm(v, g, b):
    mu = jnp.mean(v, axis=-1, keepdims=True)
    d = v - mu
    var = jnp.mean(d * d, axis=-1, keepdims=True)
    return d * lax.rsqrt(var + LN_EPS) * g + b


def _pack_bf16_pairs(v):
    w = v.shape[1] // 2
    hi = lax.bitcast_convert_type(v[:, :w], u32)
    lo = lax.bitcast_convert_type(v[:, w:], u32)
    return (hi & jnp.uint32(0xFFFF0000)) | (lo >> 16)


def _unpack_bf16_pairs(p):
    hi = lax.bitcast_convert_type(p & jnp.uint32(0xFFFF0000), f32)
    lo = lax.bitcast_convert_type(p << 16, f32)
    return jnp.concatenate([hi, lo], axis=1).astype(bf16)


def _round_up_bm(v):
    return lax.shift_left(lax.shift_right_logical(v + (MOE_BM - 1), MOE_BM_SHIFT), MOE_BM_SHIFT)


def _rel_bias_matrix(table, nq, nk, d0):
    p = nq + nk - 1
    j = jnp.arange(p)
    u = table[:, jnp.clip(d0 + nq - 1 - j, -REL_CLIP, REL_CLIP) + REL_CLIP]
    u = jnp.roll(u, -(nq - 1), axis=1)
    flat = jnp.tile(u, (1, nq))[:, :nq * (p - 1)]
    return flat.reshape(table.shape[0], nq, p - 1)[:, :, :nk].astype(f32)


def _mm_kernel(x_ref, w_ref, o_ref):
    o_ref[...] = jnp.dot(x_ref[...].astype(bf16), w_ref[...], preferred_element_type=f32)


def _matmul(x, w, tm, tn):
    m, k = x.shape
    n = w.shape[1]
    return pl.pallas_call(
        _mm_kernel,
        grid=(m // tm, n // tn),
        in_specs=[pl.BlockSpec((tm, k), lambda i, j: (i, 0)),
                  pl.BlockSpec((k, tn), lambda i, j: (0, j))],
        out_specs=pl.BlockSpec((tm, tn), lambda i, j: (i, j)),
        out_shape=jax.ShapeDtypeStruct((m, n), f32),
        compiler_params=_cparams(("parallel", "parallel")),
        name="dense_proj",
    )(x, w)


def _route_top_k(x, w, b):
    logits = jnp.dot(x.astype(bf16), w.astype(bf16), preferred_element_type=f32) + b
    tm = logits.shape[0]
    lane = lax.broadcasted_iota(i32, (tm, N_EXPERTS), 1)
    out_lane = lax.broadcasted_iota(i32, (tm, LANES), 1)
    e_out = jnp.zeros((tm, LANES), i32)
    g_out = jnp.zeros((tm, LANES), f32)
    v0 = None
    den = jnp.zeros((tm, 1), f32)
    for k in range(TOP_K):
        v = jnp.max(logits, axis=-1, keepdims=True)
        idx = jnp.min(jnp.where(logits == v, lane, N_EXPERTS), axis=-1, keepdims=True)
        logits = jnp.where(lane == idx, -jnp.inf, logits)
        if k == 0:
            v0 = v
        p = jnp.exp(v - v0)
        den = den + p
        e_out = jnp.where(out_lane == k, idx, e_out)
        g_out = jnp.where(out_lane == k, p, g_out)
    return e_out, g_out / den


def _proj_ln_kernel(*refs, n_in):
    a_refs = refs[:n_in]
    w_refs = refs[n_in:2 * n_in]
    x_ref, g_ref, b_ref, wr_ref, br_ref, o_ref, e_ref, gate_ref, cnt_ref = refs[2 * n_in:]
    acc = ALPHA * x_ref[...]
    for a_ref, w_ref in zip(a_refs, w_refs):
        acc = acc + jnp.dot(a_ref[...].astype(bf16), w_ref[...], preferred_element_type=f32)
    y = _layernorm(acc, g_ref[...], b_ref[...])
    o_ref[...] = y
    e_out, gates = _route_top_k(y, wr_ref[0], br_ref[0])
    e_ref[...] = e_out
    gate_ref[...] = gates
    nsub = cnt_ref.shape[0]
    td = e_out.shape[0] // nsub
    lane = lax.broadcasted_iota(i32, (td, LANES), 1)
    for u in range(nsub):
        eu = e_out[u * td:(u + 1) * td]
        cnt = jnp.zeros((1, LANES), f32)
        for k in range(TOP_K):
            cnt = cnt + jnp.sum((lane == eu[:, k:k + 1]).astype(f32), axis=0, keepdims=True)
        cnt_ref[u] = jnp.broadcast_to(cnt, cnt_ref.shape[1:])


def _proj_ln(acts, ws, x, g, b, w_router, b_router, l, tm, td):
    m, d = x.shape
    n_in = len(acts)
    depth = w_router.shape[0]
    nt = m // tm
    nsub = tm // td
    cur = lambda i: (i, 0)
    in_specs = ([pl.BlockSpec((tm, a.shape[1]), cur) for a in acts]
                + [pl.BlockSpec(w.shape, lambda i: (0, 0)) for w in ws]
                + [pl.BlockSpec((tm, d), cur),
                   pl.BlockSpec((1, d), lambda i: (0, 0)),
                   pl.BlockSpec((1, d), lambda i: (0, 0)),
                   pl.BlockSpec((1, d, N_EXPERTS), lambda i: (l, 0, 0)),
                   pl.BlockSpec((1, 1, N_EXPERTS), lambda i: (l, 0, 0))])
    return pl.pallas_call(
        functools.partial(_proj_ln_kernel, n_in=n_in),
        grid=(nt,),
        in_specs=in_specs,
        out_specs=[pl.BlockSpec((tm, d), cur),
                   pl.BlockSpec((tm, LANES), cur),
                   pl.BlockSpec((tm, LANES), cur),
                   pl.BlockSpec((nsub, 8, LANES), lambda i: (i, 0, 0))],
        out_shape=[jax.ShapeDtypeStruct((m, d), f32),
                   jax.ShapeDtypeStruct((m, LANES), i32),
                   jax.ShapeDtypeStruct((m, LANES), f32),
                   jax.ShapeDtypeStruct((m // td, 8, LANES), f32)],
        compiler_params=_cparams(("parallel",)),
        name="proj_ln",
    )(*acts, *ws, x, g.reshape(1, d), b.reshape(1, d), w_router, b_router.reshape(depth, 1, N_EXPERTS))


def _attn_prompt_kernel(q_ref, k0_ref, k1_ref, k2_ref, v0_ref, v1_ref, v2_ref, bias_ref, o_ref):
    i = pl.program_id(0)
    tq = q_ref.shape[0]
    q = q_ref[...]
    k = jnp.concatenate([k0_ref[...], k1_ref[...], k2_ref[...]], axis=0)
    v = jnp.concatenate([v0_ref[...], v1_ref[...], v2_ref[...]], axis=0)
    kcol = lax.broadcasted_iota(i32, (1, 3 * tq), 1)
    tile_ok = (kcol // tq + i) >= 2
    qb = (q * (DHA ** -0.5)).astype(bf16)
    kb = k.astype(bf16)
    vb = v.astype(bf16)
    outs = []
    for h in range(HA):
        sl = slice(h * DHA, (h + 1) * DHA)
        s = lax.dot_general(qb[:, sl], kb[:, sl], (((1,), (1,)), ((), ())), preferred_element_type=f32)
        s = jnp.where(tile_ok, s + bias_ref[h], NEG)
        p = jnp.exp(s - jnp.max(s, axis=-1, keepdims=True))
        o = jnp.dot(p.astype(bf16), vb[:, sl], preferred_element_type=f32)
        outs.append(o / jnp.sum(p, axis=-1, keepdims=True))
    o_ref[...] = jnp.concatenate(outs, axis=-1)


def _attn_prompt(z, rel_bias, tp):
    tq = ATT_TQ
    nq = tp // tq
    qq = jnp.arange(tq)[:, None]
    kk = jnp.arange(3 * tq)[None, :]
    band = (kk // CHUNK >= qq // CHUNK) & (kk // CHUNK <= qq // CHUNK + A_PAST_CHUNKS)
    bias = jnp.where(band[None], _rel_bias_matrix(rel_bias, tq, 3 * tq, 2 * tq), NEG)

    def kv_spec(back, col):
        return pl.BlockSpec((tq, A_WIDTH), lambda i: (jnp.maximum(i - back, 0), col))

    return pl.pallas_call(
        _attn_prompt_kernel,
        grid=(nq,),
        in_specs=[pl.BlockSpec((tq, A_WIDTH), lambda i: (i, 0)),
                  kv_spec(2, 1), kv_spec(1, 1), kv_spec(0, 1),
                  kv_spec(2, 2), kv_spec(1, 2), kv_spec(0, 2),
                  pl.BlockSpec((HA, tq, 3 * tq), lambda i: (0, 0, 0))],
        out_specs=pl.BlockSpec((tq, A_WIDTH), lambda i: (i, 0)),
        out_shape=jax.ShapeDtypeStruct((tp, A_WIDTH), f32),
        compiler_params=_cparams(("parallel",)),
        name="attn_prompt",
    )(z, z, z, z, z, z, z, bias)


def _attn_sample_kernel(q_ref, kn_ref, vn_ref, kc_ref, vc_ref, bias_ref, o_ref):
    p_len = kc_ref.shape[1]
    q, kn, vn = q_ref[...], kn_ref[...], vn_ref[...]
    kc, vc = kc_ref[0], vc_ref[0]
    outs = []
    for h in range(HA):
        sl = slice(h * DHA, (h + 1) * DHA)
        qh = q[:, sl].astype(bf16)
        dims = (((1,), (1,)), ((), ()))
        sc = lax.dot_general(qh, kc[:, sl].astype(bf16), dims, preferred_element_type=f32)
        sn = lax.dot_general(qh, kn[:, sl].astype(bf16), dims, preferred_element_type=f32)
        b = bias_ref[h]
        sc = sc * (DHA ** -0.5) + b[:, :p_len]
        sn = sn * (DHA ** -0.5) + b[:, p_len:]
        m = jnp.maximum(jnp.max(sc, axis=-1, keepdims=True), jnp.max(sn, axis=-1, keepdims=True))
        pc, pn = jnp.exp(sc - m), jnp.exp(sn - m)
        den = jnp.sum(pc, axis=-1, keepdims=True) + jnp.sum(pn, axis=-1, keepdims=True)
        pc, pn = pc / den, pn / den
        outs.append(jnp.dot(pc.astype(bf16), vc[:, sl].astype(bf16), preferred_element_type=f32)
                    + jnp.dot(pn.astype(bf16), vn[:, sl].astype(bf16), preferred_element_type=f32))
    o_ref[...] = jnp.concatenate(outs, axis=-1)


def _attn_sample(z, k_cache, v_cache, rel_bias, row0, bs, ts):
    p_len = k_cache.shape[1]
    kc = k_cache.reshape(bs, p_len, A_WIDTH)
    vc = v_cache.reshape(bs, p_len, A_WIDTH)
    bias = _rel_bias_matrix(rel_bias, ts, p_len + ts, p_len)
    blk0 = row0 // ts

    def z_spec(col):
        return pl.BlockSpec((ts, A_WIDTH), lambda b: (blk0 + b, col))

    return pl.pallas_call(
        _attn_sample_kernel,
        grid=(bs,),
        in_specs=[z_spec(0), z_spec(1), z_spec(2),
                  pl.BlockSpec((1, p_len, A_WIDTH), lambda b: (b, 0, 0)),
                  pl.BlockSpec((1, p_len, A_WIDTH), lambda b: (b, 0, 0)),
                  pl.BlockSpec((HA, ts, p_len + ts), lambda b: (0, 0, 0))],
        out_specs=pl.BlockSpec((ts, A_WIDTH), lambda b: (b, 0)),
        out_shape=jax.ShapeDtypeStruct((bs * ts, A_WIDTH), f32),
        compiler_params=_cparams(("parallel",)),
        name="attn_sample",
    )(z, z, z, kc, vc, bias)


def _gelu_tanh(x):
    return 0.5 * x * (1.0 + jnp.tanh(math.sqrt(2.0 / math.pi) * (x + 0.044715 * (x * x * x))))


def _rglru_kernel(xb_ref, gb_ref, conv0_ref, h0_ref, cw_ref, cb_ref, wrg_ref, brg_ref,
                  wig_ref, big_ref, sp_ref, rec_ref, convn_ref, hl_ref, xp_ref, hc_ref):
    t = pl.program_id(1)
    tb = xb_ref.shape[0]
    pad = 8

    @pl.when(t == 0)
    def _():
        xp_ref[0:pad, :] = jnp.zeros((pad, B_WIDTH), f32)
        xp_ref[pad - (CONV_W - 1):pad, :] = conv0_ref[0]
        hc_ref[...] = h0_ref[0]

    xb = xb_ref[...]
    xp_ref[pad:pad + tb, :] = xb
    cw = cw_ref[...]
    u = cb_ref[...] + cw[CONV_W - 1:CONV_W, :] * xb
    for j in range(CONV_W - 1):
        sh = CONV_W - 1 - j
        u = u + cw[j:j + 1, :] * xp_ref[pad - sh:pad - sh + tb, :]
    convn_ref[0] = xp_ref[pad + tb - (CONV_W - 1):pad + tb, :]
    xp_ref[0:pad, :] = xp_ref[tb:tb + pad, :]

    ub = u.astype(bf16)
    r = jax.nn.sigmoid(jnp.dot(ub, wrg_ref[...], preferred_element_type=f32) + brg_ref[...])
    ig = jax.nn.sigmoid(jnp.dot(ub, wig_ref[...], preferred_element_type=f32) + big_ref[...])
    log_a = -LRU_C * r * sp_ref[...]
    a = jnp.exp(log_a)
    bt = jnp.sqrt(-jnp.tanh(log_a) * (a * a + 1.0)) * (ig * u)

    row = lax.broadcasted_iota(i32, (tb, 1), 0)
    s = 1
    while s < tb:
        keep = row >= s
        a_sh = pltpu.roll(a, s, axis=0)
        b_sh = pltpu.roll(bt, s, axis=0)
        bt = jnp.where(keep, a * b_sh + bt, bt)
        a = jnp.where(keep, a * a_sh, a)
        s *= 2
    h = a * hc_ref[...] + bt
    hc_ref[...] = h[tb - 1:tb, :]
    hl_ref[0] = h[tb - 1:tb, :]
    rec_ref[...] = h * _gelu_tanh(gb_ref[...])


def _rglru(z, conv0, h0, cw, cb, wrg, brg, wig, big, sp, row0, bsz, t_len, tb):
    nt = t_len // tb
    blk0 = row0 // tb

    def z_spec(col):
        return pl.BlockSpec((tb, B_WIDTH), lambda b, t: (blk0 + b * nt + t, col))

    def const(shape):
        return pl.BlockSpec(shape, lambda b, t: (0,) * len(shape))

    row = lambda v: v.reshape(1, B_WIDTH)
    return pl.pallas_call(
        _rglru_kernel,
        grid=(bsz, nt),
        in_specs=[z_spec(3), z_spec(4),
                  pl.BlockSpec((1, CONV_W - 1, B_WIDTH), lambda b, t: (b, 0, 0)),
                  pl.BlockSpec((1, 1, B_WIDTH), lambda b, t: (b, 0, 0)),
                  const((CONV_W, B_WIDTH)), const((1, B_WIDTH)),
                  const((B_WIDTH, B_WIDTH)), const((1, B_WIDTH)),
                  const((B_WIDTH, B_WIDTH)), const((1, B_WIDTH)), const((1, B_WIDTH))],
        out_specs=[pl.BlockSpec((tb, B_WIDTH), lambda b, t: (b * nt + t, 0)),
                   pl.BlockSpec((1, CONV_W - 1, B_WIDTH), lambda b, t: (b, 0, 0)),
                   pl.BlockSpec((1, 1, B_WIDTH), lambda b, t: (b, 0, 0))],
        out_shape=[jax.ShapeDtypeStruct((bsz * t_len, B_WIDTH), f32),
                   jax.ShapeDtypeStruct((bsz, CONV_W - 1, B_WIDTH), f32),
                   jax.ShapeDtypeStruct((bsz, 1, B_WIDTH), f32)],
        scratch_shapes=[pltpu.VMEM((tb + 8, B_WIDTH), f32), pltpu.VMEM((1, B_WIDTH), f32)],
        compiler_params=_cparams(("parallel", "arbitrary")),
        name="rglru",
    )(z, z, conv0, h0.reshape(bsz, 1, B_WIDTH), cw, row(cb), wrg, row(brg), wig, row(big), row(sp))


def _block_diag(w):
    hb, d, _ = w.shape
    eye = jnp.eye(hb, dtype=w.dtype)
    return (eye[:, None, :, None] * w[:, :, None, :]).reshape(hb * d, hb * d)


def _retention_kernel(x_ref, w_hbm, cos_ref, sin_ref, s0_hbm, gng_ref, gnb_ref,
                      y_ref, sout_hbm, w_ref, s_ref, z_ref, sem, *, chunk, nt, tb, zero_state, hoist):
    b = pl.program_id(0)
    t = pl.program_id(1)
    half = DKC // 2

    @pl.when((b == 0) & (t == 0))
    def _():
        cp = pltpu.make_async_copy(w_hbm, w_ref, sem.at[0])
        cp.start()
        cp.wait()
        if hoist:
            xa = x_ref[...].astype(bf16)
            for c in range(0, w_ref.shape[1], C_QK):
                z_ref[:, c:c + C_QK] = jnp.dot(xa, w_ref[:, c:c + C_QK], preferred_element_type=f32)

    @pl.when(t == 0)
    def _():
        if zero_state:
            s_ref[...] = jnp.zeros(s_ref.shape, f32)
        else:
            cp = pltpu.make_async_copy(s0_hbm.at[b], s_ref, sem.at[1])
            cp.start()
            cp.wait()

    cos, sin = cos_ref[...], sin_ref[...]
    n = lax.broadcasted_iota(i32, (tb, 1), 0)
    m = lax.broadcasted_iota(i32, (1, tb), 1)
    dist = n - m
    same = (n // chunk) == (m // chunk)
    expo = jnp.where(same, jnp.abs(dist), dist).astype(f32)
    visible = same | (dist > 0)
    nf = n.astype(f32)

    if hoist:
        row0 = pl.multiple_of(b * tb, tb)

        def proj(col, width):
            return z_ref[pl.ds(row0, tb), col:col + width]
    else:
        xb = x_ref[...].astype(bf16)

        def proj(col, width):
            return jnp.dot(xb, w_ref[:, col:col + width], preferred_element_type=f32)

    def rot(v, scale):
        x1, x2 = v[:, :half], v[:, half:]
        return jnp.concatenate([x1 * cos - x2 * sin, x2 * cos + x1 * sin], axis=-1) * scale

    for h in range(HC):
        lg = math.log(1.0 - 2.0 ** (-5.0 - h))
        qr = rot(proj(h * DKC, DKC), 1.0)
        kr = rot(proj(C_QK + h * DKC, DKC), DKC ** -0.5)
        vb = proj(2 * C_QK + h * DVC, DVC).astype(bf16)
        gate = proj(2 * C_QK + C_V + h * DVC, DVC)
        qb = qr.astype(bf16)
        dmat = jnp.where(visible, jnp.exp(lg * expo), 0.0)
        s = lax.dot_general(qb, kr.astype(bf16), (((1,), (1,)), ((), ())),
                            preferred_element_type=f32) * dmat
        s_old = s_ref[h]
        o = jnp.dot(s.astype(bf16), vb, preferred_element_type=f32)
        o = o + jnp.dot(qb, s_old.astype(bf16), preferred_element_type=f32) * jnp.exp(lg * (nf + 1.0))
        kd = (kr * jnp.exp(lg * (tb - 1.0 - nf))).astype(bf16)
        s_ref[h] = math.exp(lg * tb) * s_old + lax.dot_general(
            kd, vb, (((0,), (0,)), ((), ())), preferred_element_type=f32)

        mu = jnp.mean(o, axis=-1, keepdims=True)
        dlt = o - mu
        var = jnp.mean(dlt * dlt, axis=-1, keepdims=True)
        vs = slice(h * DVC, (h + 1) * DVC)
        yn = dlt * lax.rsqrt(var + GN_EPS) * gng_ref[:, vs] + gnb_ref[:, vs]
        y_ref[:, vs] = (jax.nn.silu(gate) * yn).astype(y_ref.dtype)

    @pl.when(t == nt - 1)
    def _():
        cp = pltpu.make_async_copy(s_ref, sout_hbm.at[b], sem.at[1])
        cp.start()
        cp.wait()


def _retention(x, w_in, s0, pos0, gn_g, gn_b, bsz, t_len, tb, chunk):
    nt = t_len // tb
    d = x.shape[1]
    half = DKC // 2
    inv = 1.0 / (ROPE_BASE ** (jnp.arange(half, dtype=f32) / half))
    ang = (pos0 + jnp.arange(t_len)).astype(f32)[:, None] * inv[None, :]
    cos, sin = jnp.cos(ang), jnp.sin(ang)
    zero_state = s0 is None
    if zero_state:
        s0 = jnp.zeros((1, 8, LANES), f32)
    hoist = nt == 1 and bsz > 1
    x_spec = (pl.BlockSpec(x.shape, lambda b, t: (0, 0)) if hoist
              else pl.BlockSpec((tb, d), lambda b, t: (b * nt + t, 0)))
    z_shape = (bsz * tb, w_in.shape[1]) if hoist else (8, LANES)
    return pl.pallas_call(
        functools.partial(_retention_kernel, chunk=chunk, nt=nt, tb=tb, zero_state=zero_state, hoist=hoist),
        grid=(bsz, nt),
        in_specs=[x_spec,
                  pl.BlockSpec(memory_space=pl.ANY),
                  pl.BlockSpec((tb, half), lambda b, t: (t, 0)),
                  pl.BlockSpec((tb, half), lambda b, t: (t, 0)),
                  pl.BlockSpec(memory_space=pl.ANY),
                  pl.BlockSpec((1, C_V), lambda b, t: (0, 0)),
                  pl.BlockSpec((1, C_V), lambda b, t: (0, 0))],
        out_specs=[pl.BlockSpec((tb, C_V), lambda b, t: (b * nt + t, 0)),
                   pl.BlockSpec(memory_space=pl.ANY)],
        out_shape=[jax.ShapeDtypeStruct((bsz * t_len, C_V), bf16),
                   jax.ShapeDtypeStruct((bsz, HC, DKC, DVC), f32)],
        scratch_shapes=[pltpu.VMEM(w_in.shape, bf16), pltpu.VMEM((HC, DKC, DVC), f32),
                        pltpu.VMEM(z_shape, f32), pltpu.SemaphoreType.DMA((2,))],
        compiler_params=_cparams(("arbitrary", "arbitrary")),
        name="retention",
    )(x, w_in, cos, sin, s0, gn_g.reshape(1, C_V), gn_b.reshape(1, C_V))


def _dispatch_kernel(ech_ref, est_ref, tab_ref, tabv_ref, ep_ref, es_ref, xp_ref, xs_ref,
                     xg_hbm, lrow_ref, sbuf, zchunk, pending, sem, *, np_tiles, nt):
    i = pl.program_id(0)
    td = xp_ref.shape[0]
    ls = sbuf.shape[1]
    slot = i % 2
    nchunks = xg_hbm.shape[0]
    blk_chunks = MOE_BM // MOE_CH

    def wait_chunks(s, count, max_bits):
        for b in reversed(range(max_bits)):
            @pl.when(jnp.bitwise_and(count, 1 << b) != 0)
            def _():
                span = xg_hbm.at[pl.ds(0, 1 << b)]
                pltpu.make_async_copy(span, span, sem.at[s]).wait()

    def drain(s):
        wait_chunks(s, pending[s], (ls // MOE_CH).bit_length())

    @pl.when(i == 0)
    def _():
        pending[0] = 0
        pending[1] = 0
        zchunk[...] = jnp.zeros(zchunk.shape, zchunk.dtype)

        def zero_chunks(lo, hi):
            def put(c, carry):
                pltpu.make_async_copy(zchunk, xg_hbm.at[c], sem.at[2]).start()
                return carry
            lax.fori_loop(lo, hi, put, 0)
            wait_chunks(2, jnp.maximum(hi - lo, 0), nchunks.bit_length())

        def per_expert(e, carry):
            used = est_ref[e] + ech_ref[e]
            zero_chunks(used, used + jnp.bitwise_and(-ech_ref[e], blk_chunks - 1))
            return carry
        lax.fori_loop(0, N_EXPERTS, per_expert, 0)
        last = N_EXPERTS - 1
        end = est_ref[last] + ech_ref[last] + jnp.bitwise_and(-ech_ref[last], blk_chunks - 1)
        zero_chunks(end, nchunks)

    def sort_and_copy(e_ref, x_ref):
        e = e_ref[...]
        lane = lax.broadcasted_iota(i32, (td, LANES), 1)
        member = jnp.zeros((td, LANES), f32)
        for k in range(TOP_K):
            member = member + (lane == e[:, k:k + 1]).astype(f32)
        r = lax.broadcasted_iota(i32, (td, td), 0)
        c = lax.broadcasted_iota(i32, (td, td), 1)
        before = jnp.dot((c < r).astype(bf16), member.astype(bf16), preferred_element_type=f32)
        in_tile = before + tabv_ref[0, 0:1, :]
        col = lax.broadcasted_iota(i32, (td, ls), 1)
        place = jnp.zeros((td, ls), f32)
        lrow = jnp.zeros((td, LANES), i32)
        for k in range(TOP_K):
            sel = lane == e[:, k:k + 1]
            row_k = jnp.sum(jnp.where(sel, in_tile, 0.0), axis=-1, keepdims=True).astype(i32)
            place = place + (col == row_k).astype(f32)
            lrow = jnp.where(lane == k, row_k, lrow)
        lrow_ref[...] = lrow
        srt = lax.dot_general(place.astype(bf16), x_ref[...].astype(bf16), (((0,), (0,)), ((), ())),
                              preferred_element_type=f32)
        drain(slot)
        sbuf[slot] = _pack_bf16_pairs(srt)

        total = 0
        for ei in range(N_EXPERTS):
            dst0 = tab_ref[0, 0, ei]
            n = tab_ref[0, 0, N_EXPERTS + ei]
            src0 = tab_ref[0, 0, 2 * N_EXPERTS + ei]

            def put(cc, carry):
                row = pl.multiple_of((src0 + cc) * MOE_CH, MOE_CH)
                pltpu.make_async_copy(sbuf.at[slot, pl.ds(row, MOE_CH)], xg_hbm.at[dst0 + cc],
                                      sem.at[slot]).start(priority=ei % 2)
                return carry
            lax.fori_loop(0, n, put, 0)
            total = total + n
        pending[slot] = total

    @pl.when(i < np_tiles)
    def _():
        sort_and_copy(ep_ref, xp_ref)

    @pl.when(i >= np_tiles)
    def _():
        sort_and_copy(es_ref, xs_ref)

    @pl.when(i == nt - 1)
    def _():
        drain(0)
        drain(1)


def _dispatch(xs, top_es, tab, tabv, ech, est, nblk, td):
    x_p, x_s = xs
    d = x_p.shape[1]
    np_tiles, ns_tiles = x_p.shape[0] // td, x_s.shape[0] // td
    nt = np_tiles + ns_tiles
    ls = td * TOP_K + N_EXPERTS * MOE_CH
    pspec = lambda w: pl.BlockSpec((td, w), lambda i, *_: (jnp.minimum(i, np_tiles - 1), 0))
    sspec = lambda w: pl.BlockSpec((td, w), lambda i, *_: (jnp.maximum(i - np_tiles, 0), 0))
    grid_spec = pltpu.PrefetchScalarGridSpec(
        num_scalar_prefetch=2,
        grid=(nt,),
        in_specs=[pl.BlockSpec((1, 1, LANES), lambda i, *_: (i, 0, 0), memory_space=pltpu.SMEM),
                  pl.BlockSpec((1, 8, LANES), lambda i, *_: (i, 0, 0)),
                  pspec(LANES), sspec(LANES), pspec(d), sspec(d)],
        out_specs=[pl.BlockSpec(memory_space=pl.ANY),
                   pl.BlockSpec((td, LANES), lambda i, *_: (i, 0))],
        scratch_shapes=[pltpu.VMEM((2, ls, d // 2), u32), pltpu.VMEM((MOE_CH, d // 2), u32),
                        pltpu.SMEM((2,), i32), pltpu.SemaphoreType.DMA((3,))],
    )
    xg, lrow = pl.pallas_call(
        functools.partial(_dispatch_kernel, np_tiles=np_tiles, nt=nt),
        grid_spec=grid_spec,
        out_shape=[jax.ShapeDtypeStruct((nblk * MOE_BM // MOE_CH, MOE_CH, d // 2), u32),
                   jax.ShapeDtypeStruct((nt * td, LANES), i32)],
        compiler_params=_cparams(("arbitrary",)),
        name="moe_dispatch",
    )(ech, est, tab, tabv, top_es[0], top_es[1], x_p, x_s)
    return xg.reshape(nblk * MOE_BM, d // 2), lrow


def _moe_kernel(blk_e_ref, nact_ref, first_ref, seg_ref, nxt_ref, x_ref, wup_hbm, bup_ref, wdn_hbm, bdn_ref,
                y_ref, wup_f, wdn_f, wup_bf, wdn_bf, sem, *, layer):
    b0 = pl.program_id(0) * MOE_PAIR
    nact = nact_ref[0]
    bm = MOE_BM

    def weight_copies(e, slot):
        return (pltpu.make_async_copy(wup_hbm.at[layer, e], wup_f.at[slot], sem.at[slot, 0]),
                pltpu.make_async_copy(wdn_hbm.at[layer, e], wdn_f.at[slot], sem.at[slot, 1]))

    @pl.when(b0 < nact)
    def _():
        for sub in range(MOE_PAIR):
            b = b0 + sub

            @pl.when(first_ref[b] == 1)
            def _():
                slot = seg_ref[b] % 2

                @pl.when(b == 0)
                def _():
                    for c in weight_copies(blk_e_ref[0], 0):
                        c.start()

                for c in weight_copies(blk_e_ref[b], slot):
                    c.wait()

                @pl.when(nxt_ref[b] >= 0)
                def _():
                    for c in weight_copies(nxt_ref[b], 1 - slot):
                        c.start()

                wup_bf[slot] = wup_f[slot].astype(bf16)
                wdn_bf[slot] = wdn_f[slot].astype(bf16)

        for sub in range(MOE_PAIR):
            b = b0 + sub
            slot = seg_ref[b] % 2
            e = layer * N_EXPERTS + blk_e_ref[b]
            rows = slice(sub * bm, (sub + 1) * bm)
            hdn = jnp.dot(_unpack_bf16_pairs(x_ref[rows, :]), wup_bf[slot], preferred_element_type=f32)
            hdn = hdn + bup_ref[pl.ds(e, 1), :]
            glu = jnp.minimum(hdn[:, :D_FF], SWIGLU_LIMIT)
            lin = jnp.clip(hdn[:, D_FF:], -SWIGLU_LIMIT, SWIGLU_LIMIT)
            act = glu * jax.nn.sigmoid(SWIGLU_ALPHA * glu) * (lin + 1.0)
            y_ref[rows, :] = (jnp.dot(act.astype(bf16), wdn_bf[slot], preferred_element_type=f32)
                              + bdn_ref[pl.ds(e, 1), :])

    @pl.when(b0 >= nact)
    def _():
        y_ref[...] = jnp.zeros(y_ref.shape, f32)


def _moe_experts(xg, blk_e, nact, first, seg, nxt, l, w_up, b_up, w_down, b_down):
    r, half = xg.shape
    d = 2 * half
    rows = MOE_BM * MOE_PAIR
    depth = w_up.shape[0]
    whole = lambda shape: pl.BlockSpec(shape, lambda s, *_: (0,) * len(shape))
    grid_spec = pltpu.PrefetchScalarGridSpec(
        num_scalar_prefetch=5,
        grid=(r // rows,),
        in_specs=[
            pl.BlockSpec((rows, half), lambda s, be, na, *_: (jnp.minimum(s, (na[0] - 1) // MOE_PAIR), 0)),
            pl.BlockSpec(memory_space=pl.ANY),
            whole((depth * N_EXPERTS, 2 * D_FF)),
            pl.BlockSpec(memory_space=pl.ANY),
            whole((depth * N_EXPERTS, d)),
        ],
        out_specs=pl.BlockSpec((rows, d), lambda s, *_: (s, 0)),
        scratch_shapes=[pltpu.VMEM((2, d, 2 * D_FF), f32), pltpu.VMEM((2, D_FF, d), f32),
                        pltpu.VMEM((2, d, 2 * D_FF), bf16), pltpu.VMEM((2, D_FF, d), bf16),
                        pltpu.SemaphoreType.DMA((2, 2))],
    )
    return pl.pallas_call(
        functools.partial(_moe_kernel, layer=l),
        grid_spec=grid_spec,
        out_shape=jax.ShapeDtypeStruct((r, d), f32),
        compiler_params=_cparams(("arbitrary",)),
        name="moe_experts",
    )(blk_e, nact, first, seg, nxt, xg, w_up, b_up.reshape(depth * N_EXPERTS, 2 * D_FF),
      w_down, b_down.reshape(depth * N_EXPERTS, d))


def _split_bf16(v):
    hi = v.astype(bf16)
    return hi, (v - hi.astype(f32)).astype(bf16)


def _combine_ln_kernel(tab_cur_ref, tab_nxt_ref, y_hbm, lrow_ref, gate_ref, x_ref, g_ref, b_ref, o_ref,
                       ybuf, pending, sem, *, nt):
    i = pl.program_id(0)
    td = x_ref.shape[0]
    ls = ybuf.shape[1]
    slot = i % 2

    def gather(tab_ref, s):
        total = 0
        for ei in range(N_EXPERTS):
            src0 = tab_ref[0, 0, ei]
            n = tab_ref[0, 0, N_EXPERTS + ei]
            dst0 = tab_ref[0, 0, 2 * N_EXPERTS + ei]

            def get(cc, carry):
                row = pl.multiple_of((dst0 + cc) * MOE_CH, MOE_CH)
                pltpu.make_async_copy(y_hbm.at[src0 + cc], ybuf.at[s, pl.ds(row, MOE_CH)],
                                      sem.at[s]).start(priority=ei % 2)
                return carry
            lax.fori_loop(0, n, get, 0)
            total = total + n
        pending[s] = total

    @pl.when(i == 0)
    def _():
        ybuf[...] = jnp.zeros(ybuf.shape, f32)
        gather(tab_cur_ref, 0)

    @pl.when(i + 1 < nt)
    def _():
        gather(tab_nxt_ref, (i + 1) % 2)

    count = pending[slot]
    for bit in reversed(range((ls // MOE_CH).bit_length())):
        @pl.when(jnp.bitwise_and(count, 1 << bit) != 0)
        def _():
            span = y_hbm.at[pl.ds(0, 1 << bit)]
            pltpu.make_async_copy(span, span, sem.at[slot]).wait()

    lrow = lrow_ref[...]
    gates = gate_ref[...]
    col = lax.broadcasted_iota(i32, (td, ls), 1)
    weight = jnp.zeros((td, ls), f32)
    for k in range(TOP_K):
        weight = weight + jnp.where(col == lrow[:, k:k + 1], gates[:, k:k + 1], 0.0)
    wh, wl = _split_bf16(weight)
    yh, yl = _split_bf16(ybuf[slot])
    moe = (jnp.dot(wh, yh, preferred_element_type=f32) + jnp.dot(wl, yh, preferred_element_type=f32)
           + jnp.dot(wh, yl, preferred_element_type=f32))
    o_ref[...] = _layernorm(ALPHA * x_ref[...] + moe, g_ref[...], b_ref[...])


def _combine_ln(y, tab, lrow, blk0, gates, x, g, b, td):
    rows, d = x.shape
    nt = rows // td
    ls = td * TOP_K + N_EXPERTS * MOE_CH
    y3 = y.reshape(y.shape[0] // MOE_CH, MOE_CH, d)
    grid_spec = pltpu.PrefetchScalarGridSpec(
        num_scalar_prefetch=0,
        grid=(nt,),
        in_specs=[
            pl.BlockSpec((1, 1, LANES), lambda i: (blk0 + i, 0, 0), memory_space=pltpu.SMEM),
            pl.BlockSpec((1, 1, LANES), lambda i: (blk0 + jnp.minimum(i + 1, nt - 1), 0, 0),
                         memory_space=pltpu.SMEM),
            pl.BlockSpec(memory_space=pl.ANY),
            pl.BlockSpec((td, LANES), lambda i: (blk0 + i, 0)),
            pl.BlockSpec((td, LANES), lambda i: (i, 0)),
            pl.BlockSpec((td, d), lambda i: (i, 0)),
            pl.BlockSpec((1, d), lambda i: (0, 0)),
            pl.BlockSpec((1, d), lambda i: (0, 0)),
        ],
        out_specs=pl.BlockSpec((td, d), lambda i: (i, 0)),
        scratch_shapes=[pltpu.VMEM((2, ls, d), f32), pltpu.SMEM((2,), i32), pltpu.SemaphoreType.DMA((2,))],
    )
    return pl.pallas_call(
        functools.partial(_combine_ln_kernel, nt=nt),
        grid_spec=grid_spec,
        out_shape=jax.ShapeDtypeStruct((rows, d), f32),
        compiler_params=_cparams(("arbitrary",)),
        name="combine_ln",
    )(tab, tab, y3, lrow, gates, x, g.reshape(1, d), b.reshape(1, d))


def _moe_ln(xs, top_es, gates, cnts, td, l, w_up, b_up, w_down, b_down, g, b):
    x_p, x_s = xs
    n_p, n_s = x_p.shape[0], x_s.shape[0]
    n = n_p + n_s
    bm = MOE_BM
    nt = n // td
    nblk = -(-(n * TOP_K + nt * N_EXPERTS * (MOE_CH - 1)) // bm) + N_EXPERTS
    nblk = -(-nblk // MOE_PAIR) * MOE_PAIR
    cnt = jnp.concatenate([c[:, 0, :N_EXPERTS] for c in cnts], axis=0).astype(i32)
    run = (cnt + MOE_CH - 1) // MOE_CH
    in_tile = jnp.cumsum(run, axis=1) - run
    ech = jnp.sum(run, axis=0)
    blk_chunks = bm // MOE_CH
    padded_ch = (ech + blk_chunks - 1) // blk_chunks * blk_chunks
    est = jnp.cumsum(padded_ch) - padded_ch
    in_all = est[None, :] + jnp.cumsum(run, axis=0) - run
    tab = jnp.concatenate([in_all, run, in_tile, jnp.zeros_like(run)], axis=1).reshape(nt, 1, LANES)
    tabv = jnp.zeros((nt, 8, LANES), f32).at[:, 0, :N_EXPERTS].set((in_tile * MOE_CH).astype(f32))
    xg, lrow = _dispatch(xs, top_es, tab, tabv, ech, est, nblk, td)
    padded = padded_ch * MOE_CH
    ends = jnp.cumsum(padded)
    blk = jnp.arange(nblk, dtype=i32)
    blk_e = jnp.minimum(jnp.sum((ends[None, :] <= (blk * bm)[:, None]).astype(i32), axis=1), N_EXPERTS - 1)
    nact = (ends[-1] // bm).astype(i32).reshape(1)
    first = ((blk < nact[0]) & ((blk == 0) | (blk_e != jnp.roll(blk_e, 1)))).astype(i32)
    seg = jnp.cumsum(first) - 1
    eid = jnp.arange(N_EXPERTS, dtype=i32)
    later_nonempty = (padded[None, :] > 0) & (eid[None, :] > eid[:, None])
    nxt_of_e = jnp.min(jnp.where(later_nonempty, eid[None, :], N_EXPERTS), axis=1)
    nxt = jnp.where(nxt_of_e < N_EXPERTS, nxt_of_e, -1)[blk_e].astype(i32)
    y = _moe_experts(xg, blk_e, nact, first, seg, nxt, l, w_up, b_up, w_down, b_down)
    return (_combine_ln(y, tab, lrow, 0, gates[0], x_p, g, b, td),
            _combine_ln(y, tab, lrow, n_p // td, gates[1], x_s, g, b, td))


def kernel(x_prompt, x_sample, cache_a_k, cache_a_v, state_b_conv, state_b_h, state_c_s, ab_w_in, ab_rel_bias, ab_conv_w, ab_conv_b, ab_w_rg, ab_b_rg, ab_w_ig, ab_b_ig, ab_lambda, ab_w_out, c_w_in, c_gn_g, c_gn_b, c_w_out, ln1_g, ln1_b, ln2_g, ln2_b, moe_w_router, moe_b_router, moe_w_up, moe_b_up, moe_w_down, moe_b_down):
    bp, tp, d = x_prompt.shape
    bs, ts, _ = x_sample.shape
    assert bp == 1 and tp % RET_TB == 0 and ts <= CHUNK
    n_p, n_s = bp * tp, bs * ts
    xs = (x_prompt.reshape(n_p, d), x_sample.reshape(n_s, d))
    tms = tuple(_pick(m, (512, 256, 128, 64)) for m in (n_p, n_s))
    td = _pick(math.gcd(n_p, n_s), (256, 128, 64, 32))
    moe = (moe_w_up, moe_b_up, moe_w_down, moe_b_down)
    router = (moe_w_router, moe_b_router)

    def per_group(fn, *groups):
        return tuple(zip(*[fn(*args) for args in zip(*groups)]))

    j = 0
    w_in = ab_w_in[j].astype(bf16)
    z_p, z_s = (_matmul(x, w_in, tm, w_in.shape[1]) for x, tm in zip(xs, tms))
    att = (_attn_prompt(z_p, ab_rel_bias[j], tp),
           _attn_sample(z_s, cache_a_k[j], cache_a_v[j], ab_rel_bias[j], 0, bs, ts))
    lru_w = (ab_conv_w[j], ab_conv_b[j], _block_diag(ab_w_rg[j]).astype(bf16), ab_b_rg[j],
             _block_diag(ab_w_ig[j]).astype(bf16), ab_b_ig[j], jax.nn.softplus(-ab_lambda[j]))
    rec_p, bc_p, bh_p = _rglru(z_p, jnp.zeros((bp, CONV_W - 1, B_WIDTH), f32), jnp.zeros((bp, B_WIDTH), f32),
                               *lru_w, 0, bp, tp, _pick(tp, (LRU_TB, 128, 64)))
    rec_s, bc_s, bh_s = _rglru(z_s, state_b_conv[j], state_b_h[j], *lru_w, 0, bs, ts, ts)
    w_out = ab_w_out[j].astype(bf16)
    w_out = [w_out[:A_WIDTH], w_out[A_WIDTH:]]
    xs, top_es, gates, cnts = per_group(
        lambda a, r, x, tm: _proj_ln([a, r], w_out, x, ln1_g[0], ln1_b[0], *router, 0, tm, td),
        att, (rec_p, rec_s), xs, tms)
    xs = _moe_ln(xs, top_es, gates, cnts, td, 0, *moe, ln2_g[0], ln2_b[0])

    keep = min(A_PAST_CHUNKS * CHUNK, tp)
    ak_p = z_p[n_p - keep:, A_WIDTH:2 * A_WIDTH].reshape(1, bp, keep, HA, DHA)
    av_p = z_p[n_p - keep:, 2 * A_WIDTH:3 * A_WIDTH].reshape(1, bp, keep, HA, DHA)
    ak_s = z_s[:, A_WIDTH:2 * A_WIDTH].reshape(1, bs, ts, HA, DHA)
    av_s = z_s[:, 2 * A_WIDTH:3 * A_WIDTH].reshape(1, bs, ts, HA, DHA)

    w_in = c_w_in[j].astype(bf16)
    y_p, cs_p = _retention(xs[0], w_in, None, 0, c_gn_g[j], c_gn_b[j], bp, tp, RET_TB, CHUNK)
    y_s, cs_s = _retention(xs[1], w_in, state_c_s[j], PAST_LEN, c_gn_g[j], c_gn_b[j], bs, ts, ts, ts)
    w_out = [c_w_out[j].astype(bf16)]
    xs, top_es, gates, cnts = per_group(
        lambda y, x, tm: _proj_ln([y], w_out, x, ln1_g[1], ln1_b[1], *router, 1, tm, td),
        (y_p, y_s), xs, tms)
    out_p, out_s = _moe_ln(xs, top_es, gates, cnts, td, 1, *moe, ln2_g[1], ln2_b[1])

    return (out_p.reshape(bp, tp, d), out_s.reshape(bs, ts, d),
            ak_p, av_p, bc_p[None], bh_p.reshape(1, bp, B_WIDTH), cs_p[None],
            ak_s, av_s, bc_s[None], bh_s.reshape(1, bs, B_WIDTH), cs_s[None])
```

```python
import functools
import math

import jax
import jax.numpy as jnp
from jax import lax
from jax.experimental import pallas as pl
from jax.experimental.pallas import tpu as pltpu

f32 = jnp.float32
bf16 = jnp.bfloat16
i32 = jnp.int32
u32 = jnp.uint32

DEPTH = 2
PAST_LEN = 1024
CHUNK = 64
A_PAST_CHUNKS = 8
REL_CLIP = 128
HA, DHA = 8, 64
A_WIDTH = HA * DHA
B_WIDTH = 512
HB = 8
CONV_W = 4
LRU_C = 8.0
HC, DKC, DVC = 4, 256, 512
C_QK, C_V = HC * DKC, HC * DVC
ROPE_BASE = 10000.0
GN_EPS = 1e-5
N_EXPERTS = 32
TOP_K = 4
D_FF = 1024
SWIGLU_LIMIT = 7.0
SWIGLU_ALPHA = 1.702
ALPHA = (2.0 * DEPTH) ** 0.25
LN_EPS = 1e-5
NEG = -1e30

LANES = 128
VMEM_LIMIT = 56 * 1024 * 1024
ATT_TQ = 256
MOE_BM = 256
MOE_BM_SHIFT = MOE_BM.bit_length() - 1
MOE_PAIR = 2
MOE_CH = 8
RET_TB = 512
LRU_TB = 256


def _pick(n, cands):
    for c in cands:
        if n % c == 0:
            return c
    raise ValueError(f"no tile for {n} in {cands}")


def _cparams(sem):
    return pltpu.CompilerParams(dimension_semantics=sem, vmem_limit_bytes=VMEM_LIMIT)


def _layernorm(v, g, b):
    mu = jnp.mean(v, axis=-1, keepdims=True)
    d = v - mu
    var = jnp.mean(d * d, axis=-1, keepdims=True)
    return d * lax.rsqrt(var + LN_EPS) * g + b


def _pack_bf16_pairs(v):
    w = v.shape[1] // 2
    hi = lax.bitcast_convert_type(v[:, :w], u32)
    lo = lax.bitcast_convert_type(v[:, w:], u32)
    return (hi & jnp.uint32(0xFFFF0000)) | (lo >> 16)


def _unpack_bf16_pairs(p):
    hi = lax.bitcast_convert_type(p & jnp.uint32(0xFFFF0000), f32)
    lo = lax.bitcast_convert_type(p << 16, f32)
    return jnp.concatenate([hi, lo], axis=1).astype(bf16)


def _round_up_bm(v):
    return lax.shift_left(lax.shift_right_logical(v + (MOE_BM - 1), MOE_BM_SHIFT), MOE_BM_SHIFT)


def _rel_bias_matrix(table, nq, nk, d0):
    p = nq + nk - 1
    j = jnp.arange(p)
    u = table[:, jnp.clip(d0 + nq - 1 - j, -REL_CLIP, REL_CLIP) + REL_CLIP]
    u = jnp.roll(u, -(nq - 1), axis=1)
    flat = jnp.tile(u, (1, nq))[:, :nq * (p - 1)]
    return flat.reshape(table.shape[0], nq, p - 1)[:, :, :nk].astype(f32)


def _mm_kernel(x_ref, w_ref, o_ref):
    o_ref[...] = jnp.dot(x_ref[...].astype(bf16), w_ref[...], preferred_element_type=f32)


def _matmul(x, w, tm, tn):
    m, k = x.shape
    n = w.shape[1]
    return pl.pallas_call(
        _mm_kernel,
        grid=(m // tm, n // tn),
        in_specs=[pl.BlockSpec((tm, k), lambda i, j: (i, 0)),
                  pl.BlockSpec((k, tn), lambda i, j: (0, j))],
        out_specs=pl.BlockSpec((tm, tn), lambda i, j: (i, j)),
        out_shape=jax.ShapeDtypeStruct((m, n), f32),
        compiler_params=_cparams(("parallel", "parallel")),
        name="dense_proj",
    )(x, w)


def _route_top_k(x, w, b):
    logits = jnp.dot(x.astype(bf16), w.astype(bf16), preferred_element_type=f32) + b
    tm = logits.shape[0]
    lane = lax.broadcasted_iota(i32, (tm, N_EXPERTS), 1)
    out_lane = lax.broadcasted_iota(i32, (tm, LANES), 1)
    e_out = jnp.zeros((tm, LANES), i32)
    g_out = jnp.zeros((tm, LANES), f32)
    member = jnp.zeros((tm, N_EXPERTS), f32)
    v0 = None
    den = jnp.zeros((tm, 1), f32)
    for k in range(TOP_K):
        v = jnp.max(logits, axis=-1, keepdims=True)
        idx = jnp.min(jnp.where(logits == v, lane, N_EXPERTS), axis=-1, keepdims=True)
        chosen = lane == idx
        logits = jnp.where(chosen, -jnp.inf, logits)
        member = member + chosen.astype(f32)
        if k == 0:
            v0 = v
        p = jnp.exp(v - v0)
        den = den + p
        e_out = jnp.where(out_lane == k, idx, e_out)
        g_out = jnp.where(out_lane == k, p, g_out)
    return e_out, g_out / den, member


def _proj_ln_kernel(*refs, n_in):
    a_refs = refs[:n_in]
    w_refs = refs[n_in:2 * n_in]
    x_ref, g_ref, b_ref, wr_ref, br_ref, o_ref, e_ref, gate_ref, cnt_ref = refs[2 * n_in:]
    acc = ALPHA * x_ref[...]
    for a_ref, w_ref in zip(a_refs, w_refs):
        acc = acc + jnp.dot(a_ref[...].astype(bf16), w_ref[...], preferred_element_type=f32)
    y = _layernorm(acc, g_ref[...], b_ref[...])
    o_ref[...] = y
    e_out, gates, member = _route_top_k(y, wr_ref[0], br_ref[0])
    e_ref[...] = e_out
    gate_ref[...] = gates
    nsub = cnt_ref.shape[0]
    td = e_out.shape[0] // nsub
    for u in range(nsub):
        cnt = jnp.sum(member[u * td:(u + 1) * td], axis=0, keepdims=True)
        cnt = jnp.concatenate([cnt, jnp.zeros((1, LANES - N_EXPERTS), f32)], axis=1)
        cnt_ref[u] = jnp.broadcast_to(cnt, cnt_ref.shape[1:])


def _proj_ln(acts, ws, x, g, b, w_router, b_router, l, tm, td):
    m, d = x.shape
    n_in = len(acts)
    depth = w_router.shape[0]
    nt = m // tm
    nsub = tm // td
    cur = lambda i: (i, 0)
    in_specs = ([pl.BlockSpec((tm, a.shape[1]), cur) for a in acts]
                + [pl.BlockSpec(w.shape, lambda i: (0, 0)) for w in ws]
                + [pl.BlockSpec((tm, d), cur),
                   pl.BlockSpec((1, d), lambda i: (0, 0)),
                   pl.BlockSpec((1, d), lambda i: (0, 0)),
                   pl.BlockSpec((1, d, N_EXPERTS), lambda i: (l, 0, 0)),
                   pl.BlockSpec((1, 1, N_EXPERTS), lambda i: (l, 0, 0))])
    return pl.pallas_call(
        functools.partial(_proj_ln_kernel, n_in=n_in),
        grid=(nt,),
        in_specs=in_specs,
        out_specs=[pl.BlockSpec((tm, d), cur),
                   pl.BlockSpec((tm, LANES), cur),
                   pl.BlockSpec((tm, LANES), cur),
                   pl.BlockSpec((nsub, 8, LANES), lambda i: (i, 0, 0))],
        out_shape=[jax.ShapeDtypeStruct((m, d), f32),
                   jax.ShapeDtypeStruct((m, LANES), i32),
                   jax.ShapeDtypeStruct((m, LANES), f32),
                   jax.ShapeDtypeStruct((m // td, 8, LANES), f32)],
        compiler_params=_cparams(("parallel",)),
        name="proj_ln",
    )(*acts, *ws, x, g.reshape(1, d), b.reshape(1, d), w_router, b_router.reshape(depth, 1, N_EXPERTS))


def _attn_prompt_kernel(q_ref, k0_ref, k1_ref, k2_ref, v0_ref, v1_ref, v2_ref, bias_ref, o_ref):
    i = pl.program_id(0)
    tq = q_ref.shape[0]
    q = q_ref[...]
    k = jnp.concatenate([k0_ref[...], k1_ref[...], k2_ref[...]], axis=0)
    v = jnp.concatenate([v0_ref[...], v1_ref[...], v2_ref[...]], axis=0)
    kcol = lax.broadcasted_iota(i32, (1, 3 * tq), 1)
    tile_ok = (kcol // tq + i) >= 2
    qb = (q * (DHA ** -0.5)).astype(bf16)
    kb = k.astype(bf16)
    vb = v.astype(bf16)
    outs = []
    for h in range(HA):
        sl = slice(h * DHA, (h + 1) * DHA)
        s = lax.dot_general(qb[:, sl], kb[:, sl], (((1,), (1,)), ((), ())), preferred_element_type=f32)
        s = jnp.where(tile_ok, s + bias_ref[h], NEG)
        p = jnp.exp(s - jnp.max(s, axis=-1, keepdims=True))
        o = jnp.dot(p.astype(bf16), vb[:, sl], preferred_element_type=f32)
        outs.append(o / jnp.sum(p, axis=-1, keepdims=True))
    o_ref[...] = jnp.concatenate(outs, axis=-1)


def _attn_prompt(z, rel_bias, tp):
    tq = ATT_TQ
    nq = tp // tq
    qq = jnp.arange(tq)[:, None]
    kk = jnp.arange(3 * tq)[None, :]
    band = (kk // CHUNK >= qq // CHUNK) & (kk // CHUNK <= qq // CHUNK + A_PAST_CHUNKS)
    bias = jnp.where(band[None], _rel_bias_matrix(rel_bias, tq, 3 * tq, 2 * tq), NEG)

    def kv_spec(back, col):
        return pl.BlockSpec((tq, A_WIDTH), lambda i: (jnp.maximum(i - back, 0), col))

    return pl.pallas_call(
        _attn_prompt_kernel,
        grid=(nq,),
        in_specs=[pl.BlockSpec((tq, A_WIDTH), lambda i: (i, 0)),
                  kv_spec(2, 1), kv_spec(1, 1), kv_spec(0, 1),
                  kv_spec(2, 2), kv_spec(1, 2), kv_spec(0, 2),
                  pl.BlockSpec((HA, tq, 3 * tq), lambda i: (0, 0, 0))],
        out_specs=pl.BlockSpec((tq, A_WIDTH), lambda i: (i, 0)),
        out_shape=jax.ShapeDtypeStruct((tp, A_WIDTH), f32),
        compiler_params=_cparams(("parallel",)),
        name="attn_prompt",
    )(z, z, z, z, z, z, z, bias)


def _attn_sample_kernel(q_ref, kn_ref, vn_ref, kc_ref, vc_ref, bias_ref, o_ref):
    p_len = kc_ref.shape[1]
    q, kn, vn = q_ref[...], kn_ref[...], vn_ref[...]
    kc, vc = kc_ref[0], vc_ref[0]
    outs = []
    for h in range(HA):
        sl = slice(h * DHA, (h + 1) * DHA)
        qh = q[:, sl].astype(bf16)
        dims = (((1,), (1,)), ((), ()))
        sc = lax.dot_general(qh, kc[:, sl].astype(bf16), dims, preferred_element_type=f32)
        sn = lax.dot_general(qh, kn[:, sl].astype(bf16), dims, preferred_element_type=f32)
        b = bias_ref[h]
        sc = sc * (DHA ** -0.5) + b[:, :p_len]
        sn = sn * (DHA ** -0.5) + b[:, p_len:]
        m = jnp.maximum(jnp.max(sc, axis=-1, keepdims=True), jnp.max(sn, axis=-1, keepdims=True))
        pc, pn = jnp.exp(sc - m), jnp.exp(sn - m)
        den = jnp.sum(pc, axis=-1, keepdims=True) + jnp.sum(pn, axis=-1, keepdims=True)
        pc, pn = pc / den, pn / den
        outs.append(jnp.dot(pc.astype(bf16), vc[:, sl].astype(bf16), preferred_element_type=f32)
                    + jnp.dot(pn.astype(bf16), vn[:, sl].astype(bf16), preferred_element_type=f32))
    o_ref[...] = jnp.concatenate(outs, axis=-1)


def _attn_sample(z, k_cache, v_cache, rel_bias, row0, bs, ts):
    p_len = k_cache.shape[1]
    kc = k_cache.reshape(bs, p_len, A_WIDTH)
    vc = v_cache.reshape(bs, p_len, A_WIDTH)
    bias = _rel_bias_matrix(rel_bias, ts, p_len + ts, p_len)
    blk0 = row0 // ts

    def z_spec(col):
        return pl.BlockSpec((ts, A_WIDTH), lambda b: (blk0 + b, col))

    return pl.pallas_call(
        _attn_sample_kernel,
        grid=(bs,),
        in_specs=[z_spec(0), z_spec(1), z_spec(2),
                  pl.BlockSpec((1, p_len, A_WIDTH), lambda b: (b, 0, 0)),
                  pl.BlockSpec((1, p_len, A_WIDTH), lambda b: (b, 0, 0)),
                  pl.BlockSpec((HA, ts, p_len + ts), lambda b: (0, 0, 0))],
        out_specs=pl.BlockSpec((ts, A_WIDTH), lambda b: (b, 0)),
        out_shape=jax.ShapeDtypeStruct((bs * ts, A_WIDTH), f32),
        compiler_params=_cparams(("parallel",)),
        name="attn_sample",
    )(z, z, z, kc, vc, bias)


def _gelu_tanh(x):
    return 0.5 * x * (1.0 + jnp.tanh(math.sqrt(2.0 / math.pi) * (x + 0.044715 * (x * x * x))))


def _rglru_kernel(xb_ref, gb_ref, conv0_ref, h0_ref, cw_ref, cb_ref, wrg_ref, brg_ref,
                  wig_ref, big_ref, sp_ref, rec_ref, convn_ref, hl_ref, xp_ref, hc_ref):
    t = pl.program_id(1)
    tb = xb_ref.shape[0]
    pad = 8

    @pl.when(t == 0)
    def _():
        xp_ref[0:pad, :] = jnp.zeros((pad, B_WIDTH), f32)
        xp_ref[pad - (CONV_W - 1):pad, :] = conv0_ref[0]
        hc_ref[...] = h0_ref[0]

    xb = xb_ref[...]
    xp_ref[pad:pad + tb, :] = xb
    cw = cw_ref[...]
    u = cb_ref[...] + cw[CONV_W - 1:CONV_W, :] * xb
    for j in range(CONV_W - 1):
        sh = CONV_W - 1 - j
        u = u + cw[j:j + 1, :] * xp_ref[pad - sh:pad - sh + tb, :]
    convn_ref[0] = xp_ref[pad + tb - (CONV_W - 1):pad + tb, :]
    xp_ref[0:pad, :] = xp_ref[tb:tb + pad, :]

    ub = u.astype(bf16)
    r = jax.nn.sigmoid(jnp.dot(ub, wrg_ref[...], preferred_element_type=f32) + brg_ref[...])
    ig = jax.nn.sigmoid(jnp.dot(ub, wig_ref[...], preferred_element_type=f32) + big_ref[...])
    log_a = -LRU_C * r * sp_ref[...]
    a = jnp.exp(log_a)
    bt = jnp.sqrt(-jnp.tanh(log_a) * (a * a + 1.0)) * (ig * u)

    row = lax.broadcasted_iota(i32, (tb, 1), 0)
    s = 1
    while s < tb:
        keep = row >= s
        a_sh = pltpu.roll(a, s, axis=0)
        b_sh = pltpu.roll(bt, s, axis=0)
        bt = jnp.where(keep, a * b_sh + bt, bt)
        a = jnp.where(keep, a * a_sh, a)
        s *= 2
    h = a * hc_ref[...] + bt
    hc_ref[...] = h[tb - 1:tb, :]
    hl_ref[0] = h[tb - 1:tb, :]
    rec_ref[...] = h * _gelu_tanh(gb_ref[...])


def _rglru(z, conv0, h0, cw, cb, wrg, brg, wig, big, sp, row0, bsz, t_len, tb):
    nt = t_len // tb
    blk0 = row0 // tb

    def z_spec(col):
        return pl.BlockSpec((tb, B_WIDTH), lambda b, t: (blk0 + b * nt + t, col))

    def const(shape):
        return pl.BlockSpec(shape, lambda b, t: (0,) * len(shape))

    row = lambda v: v.reshape(1, B_WIDTH)
    return pl.pallas_call(
        _rglru_kernel,
        grid=(bsz, nt),
        in_specs=[z_spec(3), z_spec(4),
                  pl.BlockSpec((1, CONV_W - 1, B_WIDTH), lambda b, t: (b, 0, 0)),
                  pl.BlockSpec((1, 1, B_WIDTH), lambda b, t: (b, 0, 0)),
                  const((CONV_W, B_WIDTH)), const((1, B_WIDTH)),
                  const((B_WIDTH, B_WIDTH)), const((1, B_WIDTH)),
                  const((B_WIDTH, B_WIDTH)), const((1, B_WIDTH)), const((1, B_WIDTH))],
        out_specs=[pl.BlockSpec((tb, B_WIDTH), lambda b, t: (b * nt + t, 0)),
                   pl.BlockSpec((1, CONV_W - 1, B_WIDTH), lambda b, t: (b, 0, 0)),
                   pl.BlockSpec((1, 1, B_WIDTH), lambda b, t: (b, 0, 0))],
        out_shape=[jax.ShapeDtypeStruct((bsz * t_len, B_WIDTH), f32),
                   jax.ShapeDtypeStruct((bsz, CONV_W - 1, B_WIDTH), f32),
                   jax.ShapeDtypeStruct((bsz, 1, B_WIDTH), f32)],
        scratch_shapes=[pltpu.VMEM((tb + 8, B_WIDTH), f32), pltpu.VMEM((1, B_WIDTH), f32)],
        compiler_params=_cparams(("parallel", "arbitrary")),
        name="rglru",
    )(z, z, conv0, h0.reshape(bsz, 1, B_WIDTH), cw, row(cb), wrg, row(brg), wig, row(big), row(sp))


def _block_diag(w):
    hb, d, _ = w.shape
    eye = jnp.eye(hb, dtype=w.dtype)
    return (eye[:, None, :, None] * w[:, :, None, :]).reshape(hb * d, hb * d)


def _retention_kernel(x_ref, w_hbm, cos_ref, sin_ref, s0_hbm, gng_ref, gnb_ref,
                      y_ref, sout_hbm, w_ref, s_ref, z_ref, sem, *, chunk, nt, tb, zero_state, hoist):
    b = pl.program_id(0)
    t = pl.program_id(1)
    half = DKC // 2

    @pl.when((b == 0) & (t == 0))
    def _():
        cp = pltpu.make_async_copy(w_hbm, w_ref, sem.at[0])
        cp.start()
        cp.wait()
        if hoist:
            xa = x_ref[...].astype(bf16)
            for c in range(0, w_ref.shape[1], C_QK):
                z_ref[:, c:c + C_QK] = jnp.dot(xa, w_ref[:, c:c + C_QK], preferred_element_type=f32)

    @pl.when(t == 0)
    def _():
        if zero_state:
            s_ref[...] = jnp.zeros(s_ref.shape, f32)
        else:
            cp = pltpu.make_async_copy(s0_hbm.at[b], s_ref, sem.at[1])
            cp.start()
            cp.wait()

    cos, sin = cos_ref[...], sin_ref[...]
    n = lax.broadcasted_iota(i32, (tb, 1), 0)
    m = lax.broadcasted_iota(i32, (1, tb), 1)
    dist = n - m
    same = (n // chunk) == (m // chunk)
    expo = jnp.where(same, jnp.abs(dist), dist).astype(f32)
    visible = same | (dist > 0)
    nf = n.astype(f32)

    if hoist:
        row0 = pl.multiple_of(b * tb, tb)

        def proj(col, width):
            return z_ref[pl.ds(row0, tb), col:col + width]
    else:
        xb = x_ref[...].astype(bf16)

        def proj(col, width):
            return jnp.dot(xb, w_ref[:, col:col + width], preferred_element_type=f32)

    def rot(v, scale):
        x1, x2 = v[:, :half], v[:, half:]
        return jnp.concatenate([x1 * cos - x2 * sin, x2 * cos + x1 * sin], axis=-1) * scale

    for h in range(HC):
        lg = math.log(1.0 - 2.0 ** (-5.0 - h))
        qr = rot(proj(h * DKC, DKC), 1.0)
        kr = rot(proj(C_QK + h * DKC, DKC), DKC ** -0.5)
        vb = proj(2 * C_QK + h * DVC, DVC).astype(bf16)
        gate = proj(2 * C_QK + C_V + h * DVC, DVC)
        qb = qr.astype(bf16)
        dmat = jnp.where(visible, jnp.exp(lg * expo), 0.0)
        s = lax.dot_general(qb, kr.astype(bf16), (((1,), (1,)), ((), ())),
                            preferred_element_type=f32) * dmat
        s_old = s_ref[h]
        o = jnp.dot(s.astype(bf16), vb, preferred_element_type=f32)
        o = o + jnp.dot(qb, s_old.astype(bf16), preferred_element_type=f32) * jnp.exp(lg * (nf + 1.0))
        kd = (kr * jnp.exp(lg * (tb - 1.0 - nf))).astype(bf16)
        s_ref[h] = math.exp(lg * tb) * s_old + lax.dot_general(
            kd, vb, (((0,), (0,)), ((), ())), preferred_element_type=f32)

        mu = jnp.mean(o, axis=-1, keepdims=True)
        dlt = o - mu
        var = jnp.mean(dlt * dlt, axis=-1, keepdims=True)
        vs = slice(h * DVC, (h + 1) * DVC)
        yn = dlt * lax.rsqrt(var + GN_EPS) * gng_ref[:, vs] + gnb_ref[:, vs]
        y_ref[:, vs] = (jax.nn.silu(gate) * yn).astype(y_ref.dtype)

    @pl.when(t == nt - 1)
    def _():
        cp = pltpu.make_async_copy(s_ref, sout_hbm.at[b], sem.at[1])
        cp.start()
        cp.wait()


def _retention(x, w_in, s0, pos0, gn_g, gn_b, bsz, t_len, tb, chunk):
    nt = t_len // tb
    d = x.shape[1]
    half = DKC // 2
    inv = 1.0 / (ROPE_BASE ** (jnp.arange(half, dtype=f32) / half))
    ang = (pos0 + jnp.arange(t_len)).astype(f32)[:, None] * inv[None, :]
    cos, sin = jnp.cos(ang), jnp.sin(ang)
    zero_state = s0 is None
    if zero_state:
        s0 = jnp.zeros((1, 8, LANES), f32)
    hoist = nt == 1 and bsz > 1
    x_spec = (pl.BlockSpec(x.shape, lambda b, t: (0, 0)) if hoist
              else pl.BlockSpec((tb, d), lambda b, t: (b * nt + t, 0)))
    z_shape = (bsz * tb, w_in.shape[1]) if hoist else (8, LANES)
    return pl.pallas_call(
        functools.partial(_retention_kernel, chunk=chunk, nt=nt, tb=tb, zero_state=zero_state, hoist=hoist),
        grid=(bsz, nt),
        in_specs=[x_spec,
                  pl.BlockSpec(memory_space=pl.ANY),
                  pl.BlockSpec((tb, half), lambda b, t: (t, 0)),
                  pl.BlockSpec((tb, half), lambda b, t: (t, 0)),
                  pl.BlockSpec(memory_space=pl.ANY),
                  pl.BlockSpec((1, C_V), lambda b, t: (0, 0)),
                  pl.BlockSpec((1, C_V), lambda b, t: (0, 0))],
        out_specs=[pl.BlockSpec((tb, C_V), lambda b, t: (b * nt + t, 0)),
                   pl.BlockSpec(memory_space=pl.ANY)],
        out_shape=[jax.ShapeDtypeStruct((bsz * t_len, C_V), bf16),
                   jax.ShapeDtypeStruct((bsz, HC, DKC, DVC), f32)],
        scratch_shapes=[pltpu.VMEM(w_in.shape, bf16), pltpu.VMEM((HC, DKC, DVC), f32),
                        pltpu.VMEM(z_shape, f32), pltpu.SemaphoreType.DMA((2,))],
        compiler_params=_cparams(("arbitrary", "arbitrary")),
        name="retention",
    )(x, w_in, cos, sin, s0, gn_g.reshape(1, C_V), gn_b.reshape(1, C_V))


def _dispatch_kernel(ech_ref, est_ref, tab_ref, tabv_ref, ep_ref, es_ref, xp_ref, xs_ref,
                     xg_hbm, lrow_ref, sbuf, zchunk, pending, sem, *, np_tiles, nt):
    i = pl.program_id(0)
    td = xp_ref.shape[0]
    ls = sbuf.shape[1]
    slot = i % 2
    nchunks = xg_hbm.shape[0]
    blk_chunks = MOE_BM // MOE_CH

    def wait_chunks(s, count, max_bits):
        for b in reversed(range(max_bits)):
            @pl.when(jnp.bitwise_and(count, 1 << b) != 0)
            def _():
                span = xg_hbm.at[pl.ds(0, 1 << b)]
                pltpu.make_async_copy(span, span, sem.at[s]).wait()

    def drain(s):
        wait_chunks(s, pending[s], (ls // MOE_CH).bit_length())

    @pl.when(i == 0)
    def _():
        pending[0] = 0
        pending[1] = 0
        zchunk[...] = jnp.zeros(zchunk.shape, zchunk.dtype)

        def zero_chunks(lo, hi):
            def put(c, carry):
                pltpu.make_async_copy(zchunk, xg_hbm.at[c], sem.at[2]).start()
                return carry
            lax.fori_loop(lo, hi, put, 0)
            wait_chunks(2, jnp.maximum(hi - lo, 0), nchunks.bit_length())

        def per_expert(e, carry):
            used = est_ref[e] + ech_ref[e]
            zero_chunks(used, used + jnp.bitwise_and(-ech_ref[e], blk_chunks - 1))
            return carry
        lax.fori_loop(0, N_EXPERTS, per_expert, 0)
        last = N_EXPERTS - 1
        end = est_ref[last] + ech_ref[last] + jnp.bitwise_and(-ech_ref[last], blk_chunks - 1)
        zero_chunks(end, nchunks)

    def sort_and_copy(e_ref, x_ref):
        e = e_ref[...]
        lane = lax.broadcasted_iota(i32, (td, LANES), 1)
        member = jnp.zeros((td, LANES), f32)
        for k in range(TOP_K):
            member = member + (lane == e[:, k:k + 1]).astype(f32)
        r = lax.broadcasted_iota(i32, (td, td), 0)
        c = lax.broadcasted_iota(i32, (td, td), 1)
        before = jnp.dot((c < r).astype(bf16), member.astype(bf16), preferred_element_type=f32)
        in_tile = before + tabv_ref[0, 0:1, :]
        col = lax.broadcasted_iota(i32, (td, ls), 1)
        place = None
        lrow = jnp.zeros((td, LANES), i32)
        for k in range(TOP_K):
            sel = lane == e[:, k:k + 1]
            row_k = jnp.sum(jnp.where(sel, in_tile, 0.0), axis=-1, keepdims=True).astype(i32)
            place = (col == row_k) if place is None else place | (col == row_k)
            lrow = jnp.where(lane == k, row_k, lrow)
        lrow_ref[...] = lrow
        place = jnp.where(place, 1.0, 0.0).astype(bf16)
        srt = lax.dot_general(place, x_ref[...].astype(bf16), (((0,), (0,)), ((), ())),
                              preferred_element_type=f32)
        drain(slot)
        sbuf[slot] = _pack_bf16_pairs(srt)

        total = 0
        for ei in range(N_EXPERTS):
            dst0 = tab_ref[0, 0, ei]
            n = tab_ref[0, 0, N_EXPERTS + ei]
            src0 = tab_ref[0, 0, 2 * N_EXPERTS + ei]

            def put(cc, carry):
                row = pl.multiple_of((src0 + cc) * MOE_CH, MOE_CH)
                pltpu.make_async_copy(sbuf.at[slot, pl.ds(row, MOE_CH)], xg_hbm.at[dst0 + cc],
                                      sem.at[slot]).start(priority=ei % 2)
                return carry
            lax.fori_loop(0, n, put, 0)
            total = total + n
        pending[slot] = total

    @pl.when(i < np_tiles)
    def _():
        sort_and_copy(ep_ref, xp_ref)

    @pl.when(i >= np_tiles)
    def _():
        sort_and_copy(es_ref, xs_ref)

    @pl.when(i == nt - 1)
    def _():
        drain(0)
        drain(1)


def _dispatch(xs, top_es, tab, tabv, ech, est, nblk, td):
    x_p, x_s = xs
    d = x_p.shape[1]
    np_tiles, ns_tiles = x_p.shape[0] // td, x_s.shape[0] // td
    nt = np_tiles + ns_tiles
    ls = td * TOP_K + N_EXPERTS * MOE_CH
    pspec = lambda w: pl.BlockSpec((td, w), lambda i, *_: (jnp.minimum(i, np_tiles - 1), 0))
    sspec = lambda w: pl.BlockSpec((td, w), lambda i, *_: (jnp.maximum(i - np_tiles, 0), 0))
    grid_spec = pltpu.PrefetchScalarGridSpec(
        num_scalar_prefetch=2,
        grid=(nt,),
        in_specs=[pl.BlockSpec((1, 1, LANES), lambda i, *_: (i, 0, 0), memory_space=pltpu.SMEM),
                  pl.BlockSpec((1, 8, LANES), lambda i, *_: (i, 0, 0)),
                  pspec(LANES), sspec(LANES), pspec(d), sspec(d)],
        out_specs=[pl.BlockSpec(memory_space=pl.ANY),
                   pl.BlockSpec((td, LANES), lambda i, *_: (i, 0))],
        scratch_shapes=[pltpu.VMEM((2, ls, d // 2), u32), pltpu.VMEM((MOE_CH, d // 2), u32),
                        pltpu.SMEM((2,), i32), pltpu.SemaphoreType.DMA((3,))],
    )
    xg, lrow = pl.pallas_call(
        functools.partial(_dispatch_kernel, np_tiles=np_tiles, nt=nt),
        grid_spec=grid_spec,
        out_shape=[jax.ShapeDtypeStruct((nblk * MOE_BM // MOE_CH, MOE_CH, d // 2), u32),
                   jax.ShapeDtypeStruct((nt * td, LANES), i32)],
        compiler_params=_cparams(("arbitrary",)),
        name="moe_dispatch",
    )(ech, est, tab, tabv, top_es[0], top_es[1], x_p, x_s)
    return xg.reshape(nblk * MOE_BM, d // 2), lrow


def _moe_kernel(blk_e_ref, nact_ref, first_ref, seg_ref, nxt_ref, x_ref, wup_hbm, bup_ref, wdn_hbm, bdn_ref,
                y_ref, wup_f, wdn_f, wup_bf, wdn_bf, sem, *, layer):
    b0 = pl.program_id(0) * MOE_PAIR
    nact = nact_ref[0]
    bm = MOE_BM

    def weight_copies(e, slot):
        return (pltpu.make_async_copy(wup_hbm.at[layer, e], wup_f.at[slot], sem.at[slot, 0]),
                pltpu.make_async_copy(wdn_hbm.at[layer, e], wdn_f.at[slot], sem.at[slot, 1]))

    @pl.when(b0 < nact)
    def _():
        for sub in range(MOE_PAIR):
            b = b0 + sub

            @pl.when(first_ref[b] == 1)
            def _():
                slot = seg_ref[b] % 2

                @pl.when(b == 0)
                def _():
                    for c in weight_copies(blk_e_ref[0], 0):
                        c.start()

                for c in weight_copies(blk_e_ref[b], slot):
                    c.wait()

                @pl.when(nxt_ref[b] >= 0)
                def _():
                    for c in weight_copies(nxt_ref[b], 1 - slot):
                        c.start()

                wup_bf[slot] = wup_f[slot].astype(bf16)
                wdn_bf[slot] = wdn_f[slot].astype(bf16)

        for sub in range(MOE_PAIR):
            b = b0 + sub
            slot = seg_ref[b] % 2
            e = layer * N_EXPERTS + blk_e_ref[b]
            rows = slice(sub * bm, (sub + 1) * bm)
            hdn = jnp.dot(_unpack_bf16_pairs(x_ref[rows, :]), wup_bf[slot], preferred_element_type=f32)
            hdn = hdn + bup_ref[pl.ds(e, 1), :]
            glu = jnp.minimum(hdn[:, :D_FF], SWIGLU_LIMIT)
            lin = jnp.clip(hdn[:, D_FF:], -SWIGLU_LIMIT, SWIGLU_LIMIT)
            act = glu * jax.nn.sigmoid(SWIGLU_ALPHA * glu) * (lin + 1.0)
            y_ref[rows, :] = (jnp.dot(act.astype(bf16), wdn_bf[slot], preferred_element_type=f32)
                              + bdn_ref[pl.ds(e, 1), :])

    @pl.when(b0 >= nact)
    def _():
        y_ref[...] = jnp.zeros(y_ref.shape, f32)


def _moe_experts(xg, blk_e, nact, first, seg, nxt, l, w_up, b_up, w_down, b_down):
    r, half = xg.shape
    d = 2 * half
    rows = MOE_BM * MOE_PAIR
    depth = w_up.shape[0]
    whole = lambda shape: pl.BlockSpec(shape, lambda s, *_: (0,) * len(shape))
    grid_spec = pltpu.PrefetchScalarGridSpec(
        num_scalar_prefetch=5,
        grid=(r // rows,),
        in_specs=[
            pl.BlockSpec((rows, half), lambda s, be, na, *_: (jnp.minimum(s, (na[0] - 1) // MOE_PAIR), 0)),
            pl.BlockSpec(memory_space=pl.ANY),
            whole((depth * N_EXPERTS, 2 * D_FF)),
            pl.BlockSpec(memory_space=pl.ANY),
            whole((depth * N_EXPERTS, d)),
        ],
        out_specs=pl.BlockSpec((rows, d), lambda s, *_: (s, 0)),
        scratch_shapes=[pltpu.VMEM((2, d, 2 * D_FF), f32), pltpu.VMEM((2, D_FF, d), f32),
                        pltpu.VMEM((2, d, 2 * D_FF), bf16), pltpu.VMEM((2, D_FF, d), bf16),
                        pltpu.SemaphoreType.DMA((2, 2))],
    )
    return pl.pallas_call(
        functools.partial(_moe_kernel, layer=l),
        grid_spec=grid_spec,
        out_shape=jax.ShapeDtypeStruct((r, d), f32),
        compiler_params=_cparams(("arbitrary",)),
        name="moe_experts",
    )(blk_e, nact, first, seg, nxt, xg, w_up, b_up.reshape(depth * N_EXPERTS, 2 * D_FF),
      w_down, b_down.reshape(depth * N_EXPERTS, d))


def _split_bf16(v):
    hi = v.astype(bf16)
    return hi, (v - hi.astype(f32)).astype(bf16)


def _combine_ln_kernel(tab_cur_ref, tab_nxt_ref, y_hbm, lrow_ref, gate_ref, x_ref, g_ref, b_ref, o_ref,
                       ybuf, pending, sem, *, nt):
    i = pl.program_id(0)
    td = x_ref.shape[0]
    ls = ybuf.shape[1]
    slot = i % 2

    def gather(tab_ref, s):
        total = 0
        for ei in range(N_EXPERTS):
            src0 = tab_ref[0, 0, ei]
            n = tab_ref[0, 0, N_EXPERTS + ei]
            dst0 = tab_ref[0, 0, 2 * N_EXPERTS + ei]

            def get(cc, carry):
                row = pl.multiple_of((dst0 + cc) * MOE_CH, MOE_CH)
                pltpu.make_async_copy(y_hbm.at[src0 + cc], ybuf.at[s, pl.ds(row, MOE_CH)],
                                      sem.at[s]).start(priority=ei % 2)
                return carry
            lax.fori_loop(0, n, get, 0)
            total = total + n
        pending[s] = total

    @pl.when(i == 0)
    def _():
        ybuf[...] = jnp.zeros(ybuf.shape, f32)
        gather(tab_cur_ref, 0)

    @pl.when(i + 1 < nt)
    def _():
        gather(tab_nxt_ref, (i + 1) % 2)

    count = pending[slot]
    for bit in reversed(range((ls // MOE_CH).bit_length())):
        @pl.when(jnp.bitwise_and(count, 1 << bit) != 0)
        def _():
            span = y_hbm.at[pl.ds(0, 1 << bit)]
            pltpu.make_async_copy(span, span, sem.at[slot]).wait()

    lrow = lrow_ref[...]
    gates = gate_ref[...]
    col = lax.broadcasted_iota(i32, (td, ls), 1)
    weight = jnp.zeros((td, ls), f32)
    for k in range(TOP_K):
        weight = weight + jnp.where(col == lrow[:, k:k + 1], gates[:, k:k + 1], 0.0)
    wh, wl = _split_bf16(weight)
    yb = ybuf[slot].astype(bf16)
    moe = jnp.dot(wh, yb, preferred_element_type=f32) + jnp.dot(wl, yb, preferred_element_type=f32)
    o_ref[...] = _layernorm(ALPHA * x_ref[...] + moe, g_ref[...], b_ref[...])


def _combine_ln(y, tab, lrow, blk0, gates, x, g, b, td):
    rows, d = x.shape
    nt = rows // td
    ls = td * TOP_K + N_EXPERTS * MOE_CH
    y3 = y.reshape(y.shape[0] // MOE_CH, MOE_CH, d)
    grid_spec = pltpu.PrefetchScalarGridSpec(
        num_scalar_prefetch=0,
        grid=(nt,),
        in_specs=[
            pl.BlockSpec((1, 1, LANES), lambda i: (blk0 + i, 0, 0), memory_space=pltpu.SMEM),
            pl.BlockSpec((1, 1, LANES), lambda i: (blk0 + jnp.minimum(i + 1, nt - 1), 0, 0),
                         memory_space=pltpu.SMEM),
            pl.BlockSpec(memory_space=pl.ANY),
            pl.BlockSpec((td, LANES), lambda i: (blk0 + i, 0)),
            pl.BlockSpec((td, LANES), lambda i: (i, 0)),
            pl.BlockSpec((td, d), lambda i: (i, 0)),
            pl.BlockSpec((1, d), lambda i: (0, 0)),
            pl.BlockSpec((1, d), lambda i: (0, 0)),
        ],
        out_specs=pl.BlockSpec((td, d), lambda i: (i, 0)),
        scratch_shapes=[pltpu.VMEM((2, ls, d), f32), pltpu.SMEM((2,), i32), pltpu.SemaphoreType.DMA((2,))],
    )
    return pl.pallas_call(
        functools.partial(_combine_ln_kernel, nt=nt),
        grid_spec=grid_spec,
        out_shape=jax.ShapeDtypeStruct((rows, d), f32),
        compiler_params=_cparams(("arbitrary",)),
        name="combine_ln",
    )(tab, tab, y3, lrow, gates, x, g.reshape(1, d), b.reshape(1, d))


def _moe_ln(xs, top_es, gates, cnts, td, l, w_up, b_up, w_down, b_down, g, b):
    x_p, x_s = xs
    n_p, n_s = x_p.shape[0], x_s.shape[0]
    n = n_p + n_s
    bm = MOE_BM
    nt = n // td
    nblk = -(-(n * TOP_K + nt * N_EXPERTS * (MOE_CH - 1)) // bm) + N_EXPERTS
    nblk = -(-nblk // MOE_PAIR) * MOE_PAIR
    cnt = jnp.concatenate([c[:, 0, :N_EXPERTS] for c in cnts], axis=0).astype(i32)
    run = (cnt + MOE_CH - 1) // MOE_CH
    in_tile = jnp.cumsum(run, axis=1) - run
    ech = jnp.sum(run, axis=0)
    blk_chunks = bm // MOE_CH
    padded_ch = (ech + blk_chunks - 1) // blk_chunks * blk_chunks
    est = jnp.cumsum(padded_ch) - padded_ch
    in_all = est[None, :] + jnp.cumsum(run, axis=0) - run
    tab = jnp.concatenate([in_all, run, in_tile, jnp.zeros_like(run)], axis=1).reshape(nt, 1, LANES)
    tabv = jnp.zeros((nt, 8, LANES), f32).at[:, 0, :N_EXPERTS].set((in_tile * MOE_CH).astype(f32))
    xg, lrow = _dispatch(xs, top_es, tab, tabv, ech, est, nblk, td)
    padded = padded_ch * MOE_CH
    ends = jnp.cumsum(padded)
    blk = jnp.arange(nblk, dtype=i32)
    blk_e = jnp.minimum(jnp.sum((ends[None, :] <= (blk * bm)[:, None]).astype(i32), axis=1), N_EXPERTS - 1)
    nact = (ends[-1] // bm).astype(i32).reshape(1)
    first = ((blk < nact[0]) & ((blk == 0) | (blk_e != jnp.roll(blk_e, 1)))).astype(i32)
    seg = jnp.cumsum(first) - 1
    eid = jnp.arange(N_EXPERTS, dtype=i32)
    later_nonempty = (padded[None, :] > 0) & (eid[None, :] > eid[:, None])
    nxt_of_e = jnp.min(jnp.where(later_nonempty, eid[None, :], N_EXPERTS), axis=1)
    nxt = jnp.where(nxt_of_e < N_EXPERTS, nxt_of_e, -1)[blk_e].astype(i32)
    y = _moe_experts(xg, blk_e, nact, first, seg, nxt, l, w_up, b_up, w_down, b_down)
    return (_combine_ln(y, tab, lrow, 0, gates[0], x_p, g, b, td),
            _combine_ln(y, tab, lrow, n_p // td, gates[1], x_s, g, b, td))


def kernel(x_prompt, x_sample, cache_a_k, cache_a_v, state_b_conv, state_b_h, state_c_s, ab_w_in, ab_rel_bias, ab_conv_w, ab_conv_b, ab_w_rg, ab_b_rg, ab_w_ig, ab_b_ig, ab_lambda, ab_w_out, c_w_in, c_gn_g, c_gn_b, c_w_out, ln1_g, ln1_b, ln2_g, ln2_b, moe_w_router, moe_b_router, moe_w_up, moe_b_up, moe_w_down, moe_b_down):
    bp, tp, d = x_prompt.shape
    bs, ts, _ = x_sample.shape
    assert bp == 1 and tp % RET_TB == 0 and ts <= CHUNK
    n_p, n_s = bp * tp, bs * ts
    xs = (x_prompt.reshape(n_p, d), x_sample.reshape(n_s, d))
    tms = tuple(_pick(m, (512, 256, 128, 64)) for m in (n_p, n_s))
    td = _pick(math.gcd(n_p, n_s), (256, 128, 64, 32))
    moe = (moe_w_up, moe_b_up, moe_w_down, moe_b_down)
    router = (moe_w_router, moe_b_router)

    def per_group(fn, *groups):
        return tuple(zip(*[fn(*args) for args in zip(*groups)]))

    j = 0
    w_in = ab_w_in[j].astype(bf16)
    z_p, z_s = (_matmul(x, w_in, tm, w_in.shape[1]) for x, tm in zip(xs, tms))
    att = (_attn_prompt(z_p, ab_rel_bias[j], tp),
           _attn_sample(z_s, cache_a_k[j], cache_a_v[j], ab_rel_bias[j], 0, bs, ts))
    lru_w = (ab_conv_w[j], ab_conv_b[j], _block_diag(ab_w_rg[j]).astype(bf16), ab_b_rg[j],
             _block_diag(ab_w_ig[j]).astype(bf16), ab_b_ig[j], jax.nn.softplus(-ab_lambda[j]))
    rec_p, bc_p, bh_p = _rglru(z_p, jnp.zeros((bp, CONV_W - 1, B_WIDTH), f32), jnp.zeros((bp, B_WIDTH), f32),
                               *lru_w, 0, bp, tp, _pick(tp, (LRU_TB, 128, 64)))
    rec_s, bc_s, bh_s = _rglru(z_s, state_b_conv[j], state_b_h[j], *lru_w, 0, bs, ts, ts)
    w_out = ab_w_out[j].astype(bf16)
    w_out = [w_out[:A_WIDTH], w_out[A_WIDTH:]]
    xs, top_es, gates, cnts = per_group(
        lambda a, r, x, tm: _proj_ln([a, r], w_out, x, ln1_g[0], ln1_b[0], *router, 0, tm, td),
        att, (rec_p, rec_s), xs, tms)
    xs = _moe_ln(xs, top_es, gates, cnts, td, 0, *moe, ln2_g[0], ln2_b[0])

    keep = min(A_PAST_CHUNKS * CHUNK, tp)
    ak_p = z_p[n_p - keep:, A_WIDTH:2 * A_WIDTH].reshape(1, bp, keep, HA, DHA)
    av_p = z_p[n_p - keep:, 2 * A_WIDTH:3 * A_WIDTH].reshape(1, bp, keep, HA, DHA)
    ak_s = z_s[:, A_WIDTH:2 * A_WIDTH].reshape(1, bs, ts, HA, DHA)
    av_s = z_s[:, 2 * A_WIDTH:3 * A_WIDTH].reshape(1, bs, ts, HA, DHA)

    w_in = c_w_in[j].astype(bf16)
    y_p, cs_p = _retention(xs[0], w_in, None, 0, c_gn_g[j], c_gn_b[j], bp, tp, RET_TB, CHUNK)
    y_s, cs_s = _retention(xs[1], w_in, state_c_s[j], PAST_LEN, c_gn_g[j], c_gn_b[j], bs, ts, ts, ts)
    w_out = [c_w_out[j].astype(bf16)]
    xs, top_es, gates, cnts = per_group(
        lambda y, x, tm: _proj_ln([y], w_out, x, ln1_g[1], ln1_b[1], *router, 1, tm, td),
        (y_p, y_s), xs, tms)
    out_p, out_s = _moe_ln(xs, top_es, gates, cnts, td, 1, *moe, ln2_g[1], ln2_b[1])

    return (out_p.reshape(bp, tp, d), out_s.reshape(bs, ts, d),
            ak_p, av_p, bc_p[None], bh_p.reshape(1, bp, B_WIDTH), cs_p[None],
            ak_s, av_s, bc_s[None], bh_s.reshape(1, bs, B_WIDTH), cs_s[None])
```

```python
import functools
import math

import jax
import jax.numpy as jnp
from jax import lax
from jax.experimental import pallas as pl
from jax.experimental.pallas import tpu as pltpu

f32 = jnp.float32
bf16 = jnp.bfloat16
i32 = jnp.int32
u32 = jnp.uint32

DEPTH = 2
PAST_LEN = 1024
CHUNK = 64
A_PAST_CHUNKS = 8
REL_CLIP = 128
HA, DHA = 8, 64
A_WIDTH = HA * DHA
B_WIDTH = 512
HB = 8
CONV_W = 4
LRU_C = 8.0
HC, DKC, DVC = 4, 256, 512
C_QK, C_V = HC * DKC, HC * DVC
ROPE_BASE = 10000.0
GN_EPS = 1e-5
N_EXPERTS = 32
TOP_K = 4
D_FF = 1024
SWIGLU_LIMIT = 7.0
SWIGLU_ALPHA = 1.702
ALPHA = (2.0 * DEPTH) ** 0.25
LN_EPS = 1e-5
NEG = -1e30

LANES = 128
VMEM_LIMIT = 56 * 1024 * 1024
ATT_TQ = 256
MOE_BM = 256
MOE_BM_SHIFT = MOE_BM.bit_length() - 1
MOE_PAIR = 2
MOE_CH = 8
MOE_BIG = 4
RET_TB = 512
LRU_TB = 256


def _pick(n, cands):
    for c in cands:
        if n % c == 0:
            return c
    raise ValueError(f"no tile for {n} in {cands}")


def _cparams(sem):
    return pltpu.CompilerParams(dimension_semantics=sem, vmem_limit_bytes=VMEM_LIMIT)


def _layernorm(v, g, b):
    mu = jnp.mean(v, axis=-1, keepdims=True)
    d = v - mu
    var = jnp.mean(d * d, axis=-1, keepdims=True)
    return d * lax.rsqrt(var + LN_EPS) * g + b


def _pack_bf16_pairs(v):
    w = v.shape[1] // 2
    hi = lax.bitcast_convert_type(v[:, :w], u32)
    lo = lax.bitcast_convert_type(v[:, w:], u32)
    return (hi & jnp.uint32(0xFFFF0000)) | (lo >> 16)


def _unpack_bf16_pairs(p):
    hi = lax.bitcast_convert_type(p & jnp.uint32(0xFFFF0000), f32)
    lo = lax.bitcast_convert_type(p << 16, f32)
    return jnp.concatenate([hi, lo], axis=1).astype(bf16)


def _rows(ref, chunk, nchunks):
    return ref.at[pl.ds(pl.multiple_of(chunk * MOE_CH, MOE_CH), nchunks * MOE_CH)]


def _copy_run(src, src_chunk, dst, dst_chunk, n, sem, priority):
    nbig = lax.shift_right_logical(n, MOE_BIG.bit_length() - 1)

    def big(c, carry):
        pltpu.make_async_copy(_rows(src, src_chunk + c * MOE_BIG, MOE_BIG),
                              _rows(dst, dst_chunk + c * MOE_BIG, MOE_BIG), sem).start(priority=priority)
        return carry
    lax.fori_loop(0, nbig, big, 0)

    def small(c, carry):
        pltpu.make_async_copy(_rows(src, src_chunk + c, 1), _rows(dst, dst_chunk + c, 1),
                              sem).start(priority=priority)
        return carry
    lax.fori_loop(nbig * MOE_BIG, n, small, 0)


def _wait_chunks(hbm, sem, count, max_bits):
    for b in reversed(range(max_bits)):
        @pl.when(jnp.bitwise_and(count, 1 << b) != 0)
        def _():
            span = _rows(hbm, 0, 1 << b)
            pltpu.make_async_copy(span, span, sem).wait()


def _round_up_bm(v):
    return lax.shift_left(lax.shift_right_logical(v + (MOE_BM - 1), MOE_BM_SHIFT), MOE_BM_SHIFT)


def _rel_bias_matrix(table, nq, nk, d0):
    p = nq + nk - 1
    j = jnp.arange(p)
    u = table[:, jnp.clip(d0 + nq - 1 - j, -REL_CLIP, REL_CLIP) + REL_CLIP]
    u = jnp.roll(u, -(nq - 1), axis=1)
    flat = jnp.tile(u, (1, nq))[:, :nq * (p - 1)]
    return flat.reshape(table.shape[0], nq, p - 1)[:, :, :nk].astype(f32)


def _mm_kernel(x_ref, w_ref, o_ref):
    o_ref[...] = jnp.dot(x_ref[...].astype(bf16), w_ref[...], preferred_element_type=f32)


def _matmul(x, w, tm, tn):
    m, k = x.shape
    n = w.shape[1]
    return pl.pallas_call(
        _mm_kernel,
        grid=(m // tm, n // tn),
        in_specs=[pl.BlockSpec((tm, k), lambda i, j: (i, 0)),
                  pl.BlockSpec((k, tn), lambda i, j: (0, j))],
        out_specs=pl.BlockSpec((tm, tn), lambda i, j: (i, j)),
        out_shape=jax.ShapeDtypeStruct((m, n), f32),
        compiler_params=_cparams(("parallel", "parallel")),
        name="dense_proj",
    )(x, w)


def _route_top_k(x, w, b):
    logits = jnp.dot(x.astype(bf16), w.astype(bf16), preferred_element_type=f32) + b
    tm = logits.shape[0]
    lane = lax.broadcasted_iota(i32, (tm, N_EXPERTS), 1)
    out_lane = lax.broadcasted_iota(i32, (tm, LANES), 1)
    e_out = jnp.zeros((tm, LANES), i32)
    g_out = jnp.zeros((tm, LANES), f32)
    member = jnp.zeros((tm, N_EXPERTS), f32)
    v0 = None
    den = jnp.zeros((tm, 1), f32)
    for k in range(TOP_K):
        v = jnp.max(logits, axis=-1, keepdims=True)
        idx = jnp.min(jnp.where(logits == v, lane, N_EXPERTS), axis=-1, keepdims=True)
        chosen = lane == idx
        logits = jnp.where(chosen, -jnp.inf, logits)
        member = member + chosen.astype(f32)
        if k == 0:
            v0 = v
        p = jnp.exp(v - v0)
        den = den + p
        e_out = jnp.where(out_lane == k, idx, e_out)
        g_out = jnp.where(out_lane == k, p, g_out)
    return e_out, g_out / den, member


def _proj_ln_kernel(*refs, n_in):
    a_refs = refs[:n_in]
    w_refs = refs[n_in:2 * n_in]
    x_ref, g_ref, b_ref, wr_ref, br_ref, o_ref, e_ref, gate_ref, cnt_ref = refs[2 * n_in:]
    acc = ALPHA * x_ref[...]
    for a_ref, w_ref in zip(a_refs, w_refs):
        acc = acc + jnp.dot(a_ref[...].astype(bf16), w_ref[...], preferred_element_type=f32)
    y = _layernorm(acc, g_ref[...], b_ref[...])
    o_ref[...] = y
    e_out, gates, member = _route_top_k(y, wr_ref[0], br_ref[0])
    e_ref[...] = e_out
    gate_ref[...] = gates
    nsub = cnt_ref.shape[0]
    td = e_out.shape[0] // nsub
    for u in range(nsub):
        cnt = jnp.sum(member[u * td:(u + 1) * td], axis=0, keepdims=True)
        cnt = jnp.concatenate([cnt, jnp.zeros((1, LANES - N_EXPERTS), f32)], axis=1)
        cnt_ref[u] = jnp.broadcast_to(cnt, cnt_ref.shape[1:])


def _proj_ln(acts, ws, x, g, b, w_router, b_router, l, tm, td):
    m, d = x.shape
    n_in = len(acts)
    depth = w_router.shape[0]
    nt = m // tm
    nsub = tm // td
    cur = lambda i: (i, 0)
    in_specs = ([pl.BlockSpec((tm, a.shape[1]), cur) for a in acts]
                + [pl.BlockSpec(w.shape, lambda i: (0, 0)) for w in ws]
                + [pl.BlockSpec((tm, d), cur),
                   pl.BlockSpec((1, d), lambda i: (0, 0)),
                   pl.BlockSpec((1, d), lambda i: (0, 0)),
                   pl.BlockSpec((1, d, N_EXPERTS), lambda i: (l, 0, 0)),
                   pl.BlockSpec((1, 1, N_EXPERTS), lambda i: (l, 0, 0))])
    return pl.pallas_call(
        functools.partial(_proj_ln_kernel, n_in=n_in),
        grid=(nt,),
        in_specs=in_specs,
        out_specs=[pl.BlockSpec((tm, d), cur),
                   pl.BlockSpec((tm, LANES), cur),
                   pl.BlockSpec((tm, LANES), cur),
                   pl.BlockSpec((nsub, 8, LANES), lambda i: (i, 0, 0))],
        out_shape=[jax.ShapeDtypeStruct((m, d), f32),
                   jax.ShapeDtypeStruct((m, LANES), i32),
                   jax.ShapeDtypeStruct((m, LANES), f32),
                   jax.ShapeDtypeStruct((m // td, 8, LANES), f32)],
        compiler_params=_cparams(("parallel",)),
        name="proj_ln",
    )(*acts, *ws, x, g.reshape(1, d), b.reshape(1, d), w_router, b_router.reshape(depth, 1, N_EXPERTS))


def _attn_prompt_kernel(q_ref, k0_ref, k1_ref, k2_ref, v0_ref, v1_ref, v2_ref, bias_ref, o_ref):
    i = pl.program_id(0)
    tq = q_ref.shape[0]
    q = q_ref[...]
    k = jnp.concatenate([k0_ref[...], k1_ref[...], k2_ref[...]], axis=0)
    v = jnp.concatenate([v0_ref[...], v1_ref[...], v2_ref[...]], axis=0)
    kcol = lax.broadcasted_iota(i32, (1, 3 * tq), 1)
    tile_ok = (kcol // tq + i) >= 2
    qb = (q * (DHA ** -0.5)).astype(bf16)
    kb = k.astype(bf16)
    vb = v.astype(bf16)
    outs = []
    for h in range(HA):
        sl = slice(h * DHA, (h + 1) * DHA)
        s = lax.dot_general(qb[:, sl], kb[:, sl], (((1,), (1,)), ((), ())), preferred_element_type=f32)
        s = jnp.where(tile_ok, s + bias_ref[h], NEG)
        p = jnp.exp(s - jnp.max(s, axis=-1, keepdims=True))
        o = jnp.dot(p.astype(bf16), vb[:, sl], preferred_element_type=f32)
        outs.append(o / jnp.sum(p, axis=-1, keepdims=True))
    o_ref[...] = jnp.concatenate(outs, axis=-1)


def _attn_prompt(z, rel_bias, tp):
    tq = ATT_TQ
    nq = tp // tq
    qq = jnp.arange(tq)[:, None]
    kk = jnp.arange(3 * tq)[None, :]
    band = (kk // CHUNK >= qq // CHUNK) & (kk // CHUNK <= qq // CHUNK + A_PAST_CHUNKS)
    bias = jnp.where(band[None], _rel_bias_matrix(rel_bias, tq, 3 * tq, 2 * tq), NEG)

    def kv_spec(back, col):
        return pl.BlockSpec((tq, A_WIDTH), lambda i: (jnp.maximum(i - back, 0), col))

    return pl.pallas_call(
        _attn_prompt_kernel,
        grid=(nq,),
        in_specs=[pl.BlockSpec((tq, A_WIDTH), lambda i: (i, 0)),
                  kv_spec(2, 1), kv_spec(1, 1), kv_spec(0, 1),
                  kv_spec(2, 2), kv_spec(1, 2), kv_spec(0, 2),
                  pl.BlockSpec((HA, tq, 3 * tq), lambda i: (0, 0, 0))],
        out_specs=pl.BlockSpec((tq, A_WIDTH), lambda i: (i, 0)),
        out_shape=jax.ShapeDtypeStruct((tp, A_WIDTH), f32),
        compiler_params=_cparams(("parallel",)),
        name="attn_prompt",
    )(z, z, z, z, z, z, z, bias)


def _attn_sample_kernel(q_ref, kn_ref, vn_ref, kc_ref, vc_ref, bias_ref, o_ref):
    p_len = kc_ref.shape[1]
    q, kn, vn = q_ref[...], kn_ref[...], vn_ref[...]
    kc, vc = kc_ref[0], vc_ref[0]
    outs = []
    for h in range(HA):
        sl = slice(h * DHA, (h + 1) * DHA)
        qh = q[:, sl].astype(bf16)
        dims = (((1,), (1,)), ((), ()))
        sc = lax.dot_general(qh, kc[:, sl].astype(bf16), dims, preferred_element_type=f32)
        sn = lax.dot_general(qh, kn[:, sl].astype(bf16), dims, preferred_element_type=f32)
        b = bias_ref[h]
        sc = sc * (DHA ** -0.5) + b[:, :p_len]
        sn = sn * (DHA ** -0.5) + b[:, p_len:]
        m = jnp.maximum(jnp.max(sc, axis=-1, keepdims=True), jnp.max(sn, axis=-1, keepdims=True))
        pc, pn = jnp.exp(sc - m), jnp.exp(sn - m)
        den = jnp.sum(pc, axis=-1, keepdims=True) + jnp.sum(pn, axis=-1, keepdims=True)
        pc, pn = pc / den, pn / den
        outs.append(jnp.dot(pc.astype(bf16), vc[:, sl].astype(bf16), preferred_element_type=f32)
                    + jnp.dot(pn.astype(bf16), vn[:, sl].astype(bf16), preferred_element_type=f32))
    o_ref[...] = jnp.concatenate(outs, axis=-1)


def _attn_sample(z, k_cache, v_cache, rel_bias, row0, bs, ts):
    p_len = k_cache.shape[1]
    kc = k_cache.reshape(bs, p_len, A_WIDTH)
    vc = v_cache.reshape(bs, p_len, A_WIDTH)
    bias = _rel_bias_matrix(rel_bias, ts, p_len + ts, p_len)
    blk0 = row0 // ts

    def z_spec(col):
        return pl.BlockSpec((ts, A_WIDTH), lambda b: (blk0 + b, col))

    return pl.pallas_call(
        _attn_sample_kernel,
        grid=(bs,),
        in_specs=[z_spec(0), z_spec(1), z_spec(2),
                  pl.BlockSpec((1, p_len, A_WIDTH), lambda b: (b, 0, 0)),
                  pl.BlockSpec((1, p_len, A_WIDTH), lambda b: (b, 0, 0)),
                  pl.BlockSpec((HA, ts, p_len + ts), lambda b: (0, 0, 0))],
        out_specs=pl.BlockSpec((ts, A_WIDTH), lambda b: (b, 0)),
        out_shape=jax.ShapeDtypeStruct((bs * ts, A_WIDTH), f32),
        compiler_params=_cparams(("parallel",)),
        name="attn_sample",
    )(z, z, z, kc, vc, bias)


def _gelu_tanh(x):
    return 0.5 * x * (1.0 + jnp.tanh(math.sqrt(2.0 / math.pi) * (x + 0.044715 * (x * x * x))))


def _rglru_kernel(xb_ref, gb_ref, conv0_ref, h0_ref, cw_ref, cb_ref, wrg_ref, brg_ref,
                  wig_ref, big_ref, sp_ref, rec_ref, convn_ref, hl_ref, xp_ref, hc_ref):
    t = pl.program_id(1)
    tb = xb_ref.shape[0]
    pad = 8

    @pl.when(t == 0)
    def _():
        xp_ref[0:pad, :] = jnp.zeros((pad, B_WIDTH), f32)
        xp_ref[pad - (CONV_W - 1):pad, :] = conv0_ref[0]
        hc_ref[...] = h0_ref[0]

    xb = xb_ref[...]
    xp_ref[pad:pad + tb, :] = xb
    cw = cw_ref[...]
    u = cb_ref[...] + cw[CONV_W - 1:CONV_W, :] * xb
    for j in range(CONV_W - 1):
        sh = CONV_W - 1 - j
        u = u + cw[j:j + 1, :] * xp_ref[pad - sh:pad - sh + tb, :]
    convn_ref[0] = xp_ref[pad + tb - (CONV_W - 1):pad + tb, :]
    xp_ref[0:pad, :] = xp_ref[tb:tb + pad, :]

    ub = u.astype(bf16)
    r = jax.nn.sigmoid(jnp.dot(ub, wrg_ref[...], preferred_element_type=f32) + brg_ref[...])
    ig = jax.nn.sigmoid(jnp.dot(ub, wig_ref[...], preferred_element_type=f32) + big_ref[...])
    log_a = -LRU_C * r * sp_ref[...]
    a = jnp.exp(log_a)
    bt = jnp.sqrt(-jnp.tanh(log_a) * (a * a + 1.0)) * (ig * u)

    row = lax.broadcasted_iota(i32, (tb, 1), 0)
    s = 1
    while s < tb:
        keep = row >= s
        a_sh = pltpu.roll(a, s, axis=0)
        b_sh = pltpu.roll(bt, s, axis=0)
        bt = jnp.where(keep, a * b_sh + bt, bt)
        a = jnp.where(keep, a * a_sh, a)
        s *= 2
    h = a * hc_ref[...] + bt
    hc_ref[...] = h[tb - 1:tb, :]
    hl_ref[0] = h[tb - 1:tb, :]
    rec_ref[...] = h * _gelu_tanh(gb_ref[...])


def _rglru(z, conv0, h0, cw, cb, wrg, brg, wig, big, sp, row0, bsz, t_len, tb):
    nt = t_len // tb
    blk0 = row0 // tb

    def z_spec(col):
        return pl.BlockSpec((tb, B_WIDTH), lambda b, t: (blk0 + b * nt + t, col))

    def const(shape):
        return pl.BlockSpec(shape, lambda b, t: (0,) * len(shape))

    row = lambda v: v.reshape(1, B_WIDTH)
    return pl.pallas_call(
        _rglru_kernel,
        grid=(bsz, nt),
        in_specs=[z_spec(3), z_spec(4),
                  pl.BlockSpec((1, CONV_W - 1, B_WIDTH), lambda b, t: (b, 0, 0)),
                  pl.BlockSpec((1, 1, B_WIDTH), lambda b, t: (b, 0, 0)),
                  const((CONV_W, B_WIDTH)), const((1, B_WIDTH)),
                  const((B_WIDTH, B_WIDTH)), const((1, B_WIDTH)),
                  const((B_WIDTH, B_WIDTH)), const((1, B_WIDTH)), const((1, B_WIDTH))],
        out_specs=[pl.BlockSpec((tb, B_WIDTH), lambda b, t: (b * nt + t, 0)),
                   pl.BlockSpec((1, CONV_W - 1, B_WIDTH), lambda b, t: (b, 0, 0)),
                   pl.BlockSpec((1, 1, B_WIDTH), lambda b, t: (b, 0, 0))],
        out_shape=[jax.ShapeDtypeStruct((bsz * t_len, B_WIDTH), f32),
                   jax.ShapeDtypeStruct((bsz, CONV_W - 1, B_WIDTH), f32),
                   jax.ShapeDtypeStruct((bsz, 1, B_WIDTH), f32)],
        scratch_shapes=[pltpu.VMEM((tb + 8, B_WIDTH), f32), pltpu.VMEM((1, B_WIDTH), f32)],
        compiler_params=_cparams(("parallel", "arbitrary")),
        name="rglru",
    )(z, z, conv0, h0.reshape(bsz, 1, B_WIDTH), cw, row(cb), wrg, row(brg), wig, row(big), row(sp))


def _block_diag(w):
    hb, d, _ = w.shape
    eye = jnp.eye(hb, dtype=w.dtype)
    return (eye[:, None, :, None] * w[:, :, None, :]).reshape(hb * d, hb * d)


def _retention_kernel(x_ref, w_hbm, cos_ref, sin_ref, s0_hbm, gng_ref, gnb_ref,
                      y_ref, sout_hbm, w_ref, s_ref, z_ref, sem, *, chunk, nt, tb, zero_state, hoist):
    b = pl.program_id(0)
    t = pl.program_id(1)
    half = DKC // 2

    @pl.when((b == 0) & (t == 0))
    def _():
        cp = pltpu.make_async_copy(w_hbm, w_ref, sem.at[0])
        cp.start()
        cp.wait()
        if hoist:
            xa = x_ref[...].astype(bf16)
            for c in range(0, w_ref.shape[1], C_QK):
                z_ref[:, c:c + C_QK] = jnp.dot(xa, w_ref[:, c:c + C_QK], preferred_element_type=f32)

    @pl.when(t == 0)
    def _():
        if zero_state:
            s_ref[...] = jnp.zeros(s_ref.shape, f32)
        else:
            cp = pltpu.make_async_copy(s0_hbm.at[b], s_ref, sem.at[1])
            cp.start()
            cp.wait()

    cos, sin = cos_ref[...], sin_ref[...]
    n = lax.broadcasted_iota(i32, (tb, 1), 0)
    m = lax.broadcasted_iota(i32, (1, tb), 1)
    dist = n - m
    same = (n // chunk) == (m // chunk)
    expo = jnp.where(same, jnp.abs(dist), dist).astype(f32)
    visible = same | (dist > 0)
    nf = n.astype(f32)

    if hoist:
        row0 = pl.multiple_of(b * tb, tb)

        def proj(col, width):
            return z_ref[pl.ds(row0, tb), col:col + width]
    else:
        xb = x_ref[...].astype(bf16)

        def proj(col, width):
            return jnp.dot(xb, w_ref[:, col:col + width], preferred_element_type=f32)

    def rot(v, scale):
        x1, x2 = v[:, :half], v[:, half:]
        return jnp.concatenate([x1 * cos - x2 * sin, x2 * cos + x1 * sin], axis=-1) * scale

    for h in range(HC):
        lg = math.log(1.0 - 2.0 ** (-5.0 - h))
        qr = rot(proj(h * DKC, DKC), 1.0)
        kr = rot(proj(C_QK + h * DKC, DKC), DKC ** -0.5)
        vb = proj(2 * C_QK + h * DVC, DVC).astype(bf16)
        gate = proj(2 * C_QK + C_V + h * DVC, DVC)
        qb = qr.astype(bf16)
        dmat = jnp.where(visible, jnp.exp(lg * expo), 0.0)
        s = lax.dot_general(qb, kr.astype(bf16), (((1,), (1,)), ((), ())),
                            preferred_element_type=f32) * dmat
        s_old = s_ref[h]
        o = jnp.dot(s.astype(bf16), vb, preferred_element_type=f32)
        o = o + jnp.dot(qb, s_old.astype(bf16), preferred_element_type=f32) * jnp.exp(lg * (nf + 1.0))
        kd = (kr * jnp.exp(lg * (tb - 1.0 - nf))).astype(bf16)
        s_ref[h] = math.exp(lg * tb) * s_old + lax.dot_general(
            kd, vb, (((0,), (0,)), ((), ())), preferred_element_type=f32)

        mu = jnp.mean(o, axis=-1, keepdims=True)
        dlt = o - mu
        var = jnp.mean(dlt * dlt, axis=-1, keepdims=True)
        vs = slice(h * DVC, (h + 1) * DVC)
        yn = dlt * lax.rsqrt(var + GN_EPS) * gng_ref[:, vs] + gnb_ref[:, vs]
        y_ref[:, vs] = (jax.nn.silu(gate) * yn).astype(y_ref.dtype)

    @pl.when(t == nt - 1)
    def _():
        cp = pltpu.make_async_copy(s_ref, sout_hbm.at[b], sem.at[1])
        cp.start()
        cp.wait()


def _retention(x, w_in, s0, pos0, gn_g, gn_b, bsz, t_len, tb, chunk):
    nt = t_len // tb
    d = x.shape[1]
    half = DKC // 2
    inv = 1.0 / (ROPE_BASE ** (jnp.arange(half, dtype=f32) / half))
    ang = (pos0 + jnp.arange(t_len)).astype(f32)[:, None] * inv[None, :]
    cos, sin = jnp.cos(ang), jnp.sin(ang)
    zero_state = s0 is None
    if zero_state:
        s0 = jnp.zeros((1, 8, LANES), f32)
    hoist = nt == 1 and bsz > 1
    x_spec = (pl.BlockSpec(x.shape, lambda b, t: (0, 0)) if hoist
              else pl.BlockSpec((tb, d), lambda b, t: (b * nt + t, 0)))
    z_shape = (bsz * tb, w_in.shape[1]) if hoist else (8, LANES)
    return pl.pallas_call(
        functools.partial(_retention_kernel, chunk=chunk, nt=nt, tb=tb, zero_state=zero_state, hoist=hoist),
        grid=(bsz, nt),
        in_specs=[x_spec,
                  pl.BlockSpec(memory_space=pl.ANY),
                  pl.BlockSpec((tb, half), lambda b, t: (t, 0)),
                  pl.BlockSpec((tb, half), lambda b, t: (t, 0)),
                  pl.BlockSpec(memory_space=pl.ANY),
                  pl.BlockSpec((1, C_V), lambda b, t: (0, 0)),
                  pl.BlockSpec((1, C_V), lambda b, t: (0, 0))],
        out_specs=[pl.BlockSpec((tb, C_V), lambda b, t: (b * nt + t, 0)),
                   pl.BlockSpec(memory_space=pl.ANY)],
        out_shape=[jax.ShapeDtypeStruct((bsz * t_len, C_V), bf16),
                   jax.ShapeDtypeStruct((bsz, HC, DKC, DVC), f32)],
        scratch_shapes=[pltpu.VMEM(w_in.shape, bf16), pltpu.VMEM((HC, DKC, DVC), f32),
                        pltpu.VMEM(z_shape, f32), pltpu.SemaphoreType.DMA((2,))],
        compiler_params=_cparams(("arbitrary", "arbitrary")),
        name="retention",
    )(x, w_in, cos, sin, s0, gn_g.reshape(1, C_V), gn_b.reshape(1, C_V))


def _dispatch_kernel(ech_ref, est_ref, tab_ref, tabv_ref, ep_ref, es_ref, xp_ref, xs_ref,
                     xg_hbm, lrow_ref, sbuf, zchunk, pending, sem, *, np_tiles, nt):
    i = pl.program_id(0)
    td = xp_ref.shape[0]
    ls = sbuf.shape[1]
    slot = i % 2
    nchunks = xg_hbm.shape[0] // MOE_CH
    blk_chunks = MOE_BM // MOE_CH

    def drain(s):
        _wait_chunks(xg_hbm, sem.at[s], pending[s], (ls // MOE_CH).bit_length())

    @pl.when(i == 0)
    def _():
        pending[0] = 0
        pending[1] = 0
        zchunk[...] = jnp.zeros(zchunk.shape, zchunk.dtype)

        def zero_chunks(lo, hi):
            def put(c, carry):
                pltpu.make_async_copy(zchunk, _rows(xg_hbm, c, 1), sem.at[2]).start()
                return carry
            lax.fori_loop(lo, hi, put, 0)
            _wait_chunks(xg_hbm, sem.at[2], jnp.maximum(hi - lo, 0), nchunks.bit_length())

        def per_expert(e, carry):
            used = est_ref[e] + ech_ref[e]
            zero_chunks(used, used + jnp.bitwise_and(-ech_ref[e], blk_chunks - 1))
            return carry
        lax.fori_loop(0, N_EXPERTS, per_expert, 0)
        last = N_EXPERTS - 1
        end = est_ref[last] + ech_ref[last] + jnp.bitwise_and(-ech_ref[last], blk_chunks - 1)
        zero_chunks(end, nchunks)

    def sort_and_copy(e_ref, x_ref):
        e = e_ref[...]
        lane = lax.broadcasted_iota(i32, (td, LANES), 1)
        member = jnp.zeros((td, LANES), f32)
        for k in range(TOP_K):
            member = member + (lane == e[:, k:k + 1]).astype(f32)
        r = lax.broadcasted_iota(i32, (td, td), 0)
        c = lax.broadcasted_iota(i32, (td, td), 1)
        before = jnp.dot((c < r).astype(bf16), member.astype(bf16), preferred_element_type=f32)
        in_tile = before + tabv_ref[0, 0:1, :]
        col = lax.broadcasted_iota(i32, (td, ls), 1)
        place = None
        lrow = jnp.zeros((td, LANES), i32)
        for k in range(TOP_K):
            sel = lane == e[:, k:k + 1]
            row_k = jnp.sum(jnp.where(sel, in_tile, 0.0), axis=-1, keepdims=True).astype(i32)
            place = (col == row_k) if place is None else place | (col == row_k)
            lrow = jnp.where(lane == k, row_k, lrow)
        lrow_ref[...] = lrow
        place = jnp.where(place, 1.0, 0.0).astype(bf16)
        srt = lax.dot_general(place, x_ref[...].astype(bf16), (((0,), (0,)), ((), ())),
                              preferred_element_type=f32)
        drain(slot)
        sbuf[slot] = _pack_bf16_pairs(srt)

        total = 0
        for ei in range(N_EXPERTS):
            dst0 = tab_ref[0, 0, ei]
            n = tab_ref[0, 0, N_EXPERTS + ei]
            src0 = tab_ref[0, 0, 2 * N_EXPERTS + ei]
            _copy_run(sbuf.at[slot], src0, xg_hbm, dst0, n, sem.at[slot], ei % 2)
            total = total + n
        pending[slot] = total

    @pl.when(i < np_tiles)
    def _():
        sort_and_copy(ep_ref, xp_ref)

    @pl.when(i >= np_tiles)
    def _():
        sort_and_copy(es_ref, xs_ref)

    @pl.when(i == nt - 1)
    def _():
        drain(0)
        drain(1)


def _dispatch(xs, top_es, tab, tabv, ech, est, nblk, td):
    x_p, x_s = xs
    d = x_p.shape[1]
    np_tiles, ns_tiles = x_p.shape[0] // td, x_s.shape[0] // td
    nt = np_tiles + ns_tiles
    ls = td * TOP_K + N_EXPERTS * MOE_CH
    pspec = lambda w: pl.BlockSpec((td, w), lambda i, *_: (jnp.minimum(i, np_tiles - 1), 0))
    sspec = lambda w: pl.BlockSpec((td, w), lambda i, *_: (jnp.maximum(i - np_tiles, 0), 0))
    grid_spec = pltpu.PrefetchScalarGridSpec(
        num_scalar_prefetch=2,
        grid=(nt,),
        in_specs=[pl.BlockSpec((1, 1, LANES), lambda i, *_: (i, 0, 0), memory_space=pltpu.SMEM),
                  pl.BlockSpec((1, 8, LANES), lambda i, *_: (i, 0, 0)),
                  pspec(LANES), sspec(LANES), pspec(d), sspec(d)],
        out_specs=[pl.BlockSpec(memory_space=pl.ANY),
                   pl.BlockSpec((td, LANES), lambda i, *_: (i, 0))],
        scratch_shapes=[pltpu.VMEM((2, ls, d // 2), u32), pltpu.VMEM((MOE_CH, d // 2), u32),
                        pltpu.SMEM((2,), i32), pltpu.SemaphoreType.DMA((3,))],
    )
    xg, lrow = pl.pallas_call(
        functools.partial(_dispatch_kernel, np_tiles=np_tiles, nt=nt),
        grid_spec=grid_spec,
        out_shape=[jax.ShapeDtypeStruct((nblk * MOE_BM, d // 2), u32),
                   jax.ShapeDtypeStruct((nt * td, LANES), i32)],
        compiler_params=_cparams(("arbitrary",)),
        name="moe_dispatch",
    )(ech, est, tab, tabv, top_es[0], top_es[1], x_p, x_s)
    return xg, lrow


def _moe_kernel(blk_e_ref, nact_ref, first_ref, seg_ref, nxt_ref, x_ref, wup_hbm, bup_ref, wdn_hbm, bdn_ref,
                y_ref, wup_f, wdn_f, wup_bf, wdn_bf, sem, *, layer):
    b0 = pl.program_id(0) * MOE_PAIR
    nact = nact_ref[0]
    bm = MOE_BM

    def weight_copies(e, slot):
        return (pltpu.make_async_copy(wup_hbm.at[layer, e], wup_f.at[slot], sem.at[slot, 0]),
                pltpu.make_async_copy(wdn_hbm.at[layer, e], wdn_f.at[slot], sem.at[slot, 1]))

    @pl.when(b0 < nact)
    def _():
        for sub in range(MOE_PAIR):
            b = b0 + sub

            @pl.when(first_ref[b] == 1)
            def _():
                slot = seg_ref[b] % 2

                @pl.when(b == 0)
                def _():
                    for c in weight_copies(blk_e_ref[0], 0):
                        c.start()

                for c in weight_copies(blk_e_ref[b], slot):
                    c.wait()

                @pl.when(nxt_ref[b] >= 0)
                def _():
                    for c in weight_copies(nxt_ref[b], 1 - slot):
                        c.start()

                wup_bf[slot] = wup_f[slot].astype(bf16)
                wdn_bf[slot] = wdn_f[slot].astype(bf16)

        for sub in range(MOE_PAIR):
            b = b0 + sub
            slot = seg_ref[b] % 2
            e = layer * N_EXPERTS + blk_e_ref[b]
            rows = slice(sub * bm, (sub + 1) * bm)
            hdn = jnp.dot(_unpack_bf16_pairs(x_ref[rows, :]), wup_bf[slot], preferred_element_type=f32)
            hdn = hdn + bup_ref[pl.ds(e, 1), :]
            glu = jnp.minimum(hdn[:, :D_FF], SWIGLU_LIMIT)
            lin = jnp.clip(hdn[:, D_FF:], -SWIGLU_LIMIT, SWIGLU_LIMIT)
            act = glu * jax.nn.sigmoid(SWIGLU_ALPHA * glu) * (lin + 1.0)
            y_ref[rows, :] = (jnp.dot(act.astype(bf16), wdn_bf[slot], preferred_element_type=f32)
                              + bdn_ref[pl.ds(e, 1), :])

    @pl.when(b0 >= nact)
    def _():
        y_ref[...] = jnp.zeros(y_ref.shape, f32)


def _moe_experts(xg, blk_e, nact, first, seg, nxt, l, w_up, b_up, w_down, b_down):
    r, half = xg.shape
    d = 2 * half
    rows = MOE_BM * MOE_PAIR
    depth = w_up.shape[0]
    whole = lambda shape: pl.BlockSpec(shape, lambda s, *_: (0,) * len(shape))
    grid_spec = pltpu.PrefetchScalarGridSpec(
        num_scalar_prefetch=5,
        grid=(r // rows,),
        in_specs=[
            pl.BlockSpec((rows, half), lambda s, be, na, *_: (jnp.minimum(s, (na[0] - 1) // MOE_PAIR), 0)),
            pl.BlockSpec(memory_space=pl.ANY),
            whole((depth * N_EXPERTS, 2 * D_FF)),
            pl.BlockSpec(memory_space=pl.ANY),
            whole((depth * N_EXPERTS, d)),
        ],
        out_specs=pl.BlockSpec((rows, d), lambda s, *_: (s, 0)),
        scratch_shapes=[pltpu.VMEM((2, d, 2 * D_FF), f32), pltpu.VMEM((2, D_FF, d), f32),
                        pltpu.VMEM((2, d, 2 * D_FF), bf16), pltpu.VMEM((2, D_FF, d), bf16),
                        pltpu.SemaphoreType.DMA((2, 2))],
    )
    return pl.pallas_call(
        functools.partial(_moe_kernel, layer=l),
        grid_spec=grid_spec,
        out_shape=jax.ShapeDtypeStruct((r, d), f32),
        compiler_params=_cparams(("arbitrary",)),
        name="moe_experts",
    )(blk_e, nact, first, seg, nxt, xg, w_up, b_up.reshape(depth * N_EXPERTS, 2 * D_FF),
      w_down, b_down.reshape(depth * N_EXPERTS, d))


def _split_bf16(v):
    hi = v.astype(bf16)
    return hi, (v - hi.astype(f32)).astype(bf16)


def _combine_ln_kernel(tab_cur_ref, tab_nxt_ref, y_hbm, lrow_ref, gate_ref, x_ref, g_ref, b_ref, o_ref,
                       ybuf, pending, sem, *, nt):
    i = pl.program_id(0)
    td = x_ref.shape[0]
    ls = ybuf.shape[1]
    slot = i % 2

    def gather(tab_ref, s):
        total = 0
        for ei in range(N_EXPERTS):
            src0 = tab_ref[0, 0, ei]
            n = tab_ref[0, 0, N_EXPERTS + ei]
            dst0 = tab_ref[0, 0, 2 * N_EXPERTS + ei]
            _copy_run(y_hbm, src0, ybuf.at[s], dst0, n, sem.at[s], ei % 2)
            total = total + n
        pending[s] = total

    @pl.when(i == 0)
    def _():
        ybuf[...] = jnp.zeros(ybuf.shape, f32)
        gather(tab_cur_ref, 0)

    @pl.when(i + 1 < nt)
    def _():
        gather(tab_nxt_ref, (i + 1) % 2)

    _wait_chunks(y_hbm, sem.at[slot], pending[slot], (ls // MOE_CH).bit_length())

    lrow = lrow_ref[...]
    gates = gate_ref[...]
    col = lax.broadcasted_iota(i32, (td, ls), 1)
    weight = jnp.zeros((td, ls), f32)
    for k in range(TOP_K):
        weight = weight + jnp.where(col == lrow[:, k:k + 1], gates[:, k:k + 1], 0.0)
    wh, wl = _split_bf16(weight)
    yb = ybuf[slot].astype(bf16)
    moe = jnp.dot(wh, yb, preferred_element_type=f32) + jnp.dot(wl, yb, preferred_element_type=f32)
    o_ref[...] = _layernorm(ALPHA * x_ref[...] + moe, g_ref[...], b_ref[...])


def _combine_ln(y, tab, lrow, blk0, gates, x, g, b, td):
    rows, d = x.shape
    nt = rows // td
    ls = td * TOP_K + N_EXPERTS * MOE_CH
    grid_spec = pltpu.PrefetchScalarGridSpec(
        num_scalar_prefetch=0,
        grid=(nt,),
        in_specs=[
            pl.BlockSpec((1, 1, LANES), lambda i: (blk0 + i, 0, 0), memory_space=pltpu.SMEM),
            pl.BlockSpec((1, 1, LANES), lambda i: (blk0 + jnp.minimum(i + 1, nt - 1), 0, 0),
                         memory_space=pltpu.SMEM),
            pl.BlockSpec(memory_space=pl.ANY),
            pl.BlockSpec((td, LANES), lambda i: (blk0 + i, 0)),
            pl.BlockSpec((td, LANES), lambda i: (i, 0)),
            pl.BlockSpec((td, d), lambda i: (i, 0)),
            pl.BlockSpec((1, d), lambda i: (0, 0)),
            pl.BlockSpec((1, d), lambda i: (0, 0)),
        ],
        out_specs=pl.BlockSpec((td, d), lambda i: (i, 0)),
        scratch_shapes=[pltpu.VMEM((2, ls, d), f32), pltpu.SMEM((2,), i32), pltpu.SemaphoreType.DMA((2,))],
    )
    return pl.pallas_call(
        functools.partial(_combine_ln_kernel, nt=nt),
        grid_spec=grid_spec,
        out_shape=jax.ShapeDtypeStruct((rows, d), f32),
        compiler_params=_cparams(("arbitrary",)),
        name="combine_ln",
    )(tab, tab, y, lrow, gates, x, g.reshape(1, d), b.reshape(1, d))


def _moe_ln(xs, top_es, gates, cnts, td, l, w_up, b_up, w_down, b_down, g, b):
    x_p, x_s = xs
    n_p, n_s = x_p.shape[0], x_s.shape[0]
    n = n_p + n_s
    bm = MOE_BM
    nt = n // td
    nblk = -(-(n * TOP_K + nt * N_EXPERTS * (MOE_CH - 1)) // bm) + N_EXPERTS
    nblk = -(-nblk // MOE_PAIR) * MOE_PAIR
    cnt = jnp.concatenate([c[:, 0, :N_EXPERTS] for c in cnts], axis=0).astype(i32)
    run = (cnt + MOE_CH - 1) // MOE_CH
    in_tile = jnp.cumsum(run, axis=1) - run
    ech = jnp.sum(run, axis=0)
    blk_chunks = bm // MOE_CH
    padded_ch = (ech + blk_chunks - 1) // blk_chunks * blk_chunks
    est = jnp.cumsum(padded_ch) - padded_ch
    in_all = est[None, :] + jnp.cumsum(run, axis=0) - run
    tab = jnp.concatenate([in_all, run, in_tile, jnp.zeros_like(run)], axis=1).reshape(nt, 1, LANES)
    tabv = jnp.zeros((nt, 8, LANES), f32).at[:, 0, :N_EXPERTS].set((in_tile * MOE_CH).astype(f32))
    xg, lrow = _dispatch(xs, top_es, tab, tabv, ech, est, nblk, td)
    padded = padded_ch * MOE_CH
    ends = jnp.cumsum(padded)
    blk = jnp.arange(nblk, dtype=i32)
    blk_e = jnp.minimum(jnp.sum((ends[None, :] <= (blk * bm)[:, None]).astype(i32), axis=1), N_EXPERTS - 1)
    nact = (ends[-1] // bm).astype(i32).reshape(1)
    first = ((blk < nact[0]) & ((blk == 0) | (blk_e != jnp.roll(blk_e, 1)))).astype(i32)
    seg = jnp.cumsum(first) - 1
    eid = jnp.arange(N_EXPERTS, dtype=i32)
    later_nonempty = (padded[None, :] > 0) & (eid[None, :] > eid[:, None])
    nxt_of_e = jnp.min(jnp.where(later_nonempty, eid[None, :], N_EXPERTS), axis=1)
    nxt = jnp.where(nxt_of_e < N_EXPERTS, nxt_of_e, -1)[blk_e].astype(i32)
    y = _moe_experts(xg, blk_e, nact, first, seg, nxt, l, w_up, b_up, w_down, b_down)
    return (_combine_ln(y, tab, lrow, 0, gates[0], x_p, g, b, td),
            _combine_ln(y, tab, lrow, n_p // td, gates[1], x_s, g, b, td))


def kernel(x_prompt, x_sample, cache_a_k, cache_a_v, state_b_conv, state_b_h, state_c_s, ab_w_in, ab_rel_bias, ab_conv_w, ab_conv_b, ab_w_rg, ab_b_rg, ab_w_ig, ab_b_ig, ab_lambda, ab_w_out, c_w_in, c_gn_g, c_gn_b, c_w_out, ln1_g, ln1_b, ln2_g, ln2_b, moe_w_router, moe_b_router, moe_w_up, moe_b_up, moe_w_down, moe_b_down):
    bp, tp, d = x_prompt.shape
    bs, ts, _ = x_sample.shape
    assert bp == 1 and tp % RET_TB == 0 and ts <= CHUNK
    n_p, n_s = bp * tp, bs * ts
    xs = (x_prompt.reshape(n_p, d), x_sample.reshape(n_s, d))
    tms = tuple(_pick(m, (512, 256, 128, 64)) for m in (n_p, n_s))
    td = _pick(math.gcd(n_p, n_s), (256, 128, 64, 32))
    moe = (moe_w_up, moe_b_up, moe_w_down, moe_b_down)
    router = (moe_w_router, moe_b_router)

    def per_group(fn, *groups):
        return tuple(zip(*[fn(*args) for args in zip(*groups)]))

    j = 0
    w_in = ab_w_in[j].astype(bf16)
    z_p, z_s = (_matmul(x, w_in, tm, w_in.shape[1]) for x, tm in zip(xs, tms))
    att = (_attn_prompt(z_p, ab_rel_bias[j], tp),
           _attn_sample(z_s, cache_a_k[j], cache_a_v[j], ab_rel_bias[j], 0, bs, ts))
    lru_w = (ab_conv_w[j], ab_conv_b[j], _block_diag(ab_w_rg[j]).astype(bf16), ab_b_rg[j],
             _block_diag(ab_w_ig[j]).astype(bf16), ab_b_ig[j], jax.nn.softplus(-ab_lambda[j]))
    rec_p, bc_p, bh_p = _rglru(z_p, jnp.zeros((bp, CONV_W - 1, B_WIDTH), f32), jnp.zeros((bp, B_WIDTH), f32),
                               *lru_w, 0, bp, tp, _pick(tp, (LRU_TB, 128, 64)))
    rec_s, bc_s, bh_s = _rglru(z_s, state_b_conv[j], state_b_h[j], *lru_w, 0, bs, ts, ts)
    w_out = ab_w_out[j].astype(bf16)
    w_out = [w_out[:A_WIDTH], w_out[A_WIDTH:]]
    xs, top_es, gates, cnts = per_group(
        lambda a, r, x, tm: _proj_ln([a, r], w_out, x, ln1_g[0], ln1_b[0], *router, 0, tm, td),
        att, (rec_p, rec_s), xs, tms)
    xs = _moe_ln(xs, top_es, gates, cnts, td, 0, *moe, ln2_g[0], ln2_b[0])

    keep = min(A_PAST_CHUNKS * CHUNK, tp)
    ak_p = z_p[n_p - keep:, A_WIDTH:2 * A_WIDTH].reshape(1, bp, keep, HA, DHA)
    av_p = z_p[n_p - keep:, 2 * A_WIDTH:3 * A_WIDTH].reshape(1, bp, keep, HA, DHA)
    ak_s = z_s[:, A_WIDTH:2 * A_WIDTH].reshape(1, bs, ts, HA, DHA)
    av_s = z_s[:, 2 * A_WIDTH:3 * A_WIDTH].reshape(1, bs, ts, HA, DHA)

    w_in = c_w_in[j].astype(bf16)
    y_p, cs_p = _retention(xs[0], w_in, None, 0, c_gn_g[j], c_gn_b[j], bp, tp, RET_TB, CHUNK)
    y_s, cs_s = _retention(xs[1], w_in, state_c_s[j], PAST_LEN, c_gn_g[j], c_gn_b[j], bs, ts, ts, ts)
    w_out = [c_w_out[j].astype(bf16)]
    xs, top_es, gates, cnts = per_group(
        lambda y, x, tm: _proj_ln([y], w_out, x, ln1_g[1], ln1_b[1], *router, 1, tm, td),
        (y_p, y_s), xs, tms)
    out_p, out_s = _moe_ln(xs, top_es, gates, cnts, td, 1, *moe, ln2_g[1], ln2_b[1])

    return (out_p.reshape(bp, tp, d), out_s.reshape(bs, ts, d),
            ak_p, av_p, bc_p[None], bh_p.reshape(1, bp, B_WIDTH), cs_p[None],
            ak_s, av_s, bc_s[None], bh_s.reshape(1, bs, B_WIDTH), cs_s[None])
```

```python
import functools
import math

import jax
import jax.numpy as jnp
from jax import lax
from jax.experimental import pallas as pl
from jax.experimental.pallas import tpu as pltpu

f32 = jnp.float32
bf16 = jnp.bfloat16
i32 = jnp.int32
u32 = jnp.uint32

DEPTH = 2
PAST_LEN = 1024
CHUNK = 64
A_PAST_CHUNKS = 8
REL_CLIP = 128
HA, DHA = 8, 64
A_WIDTH = HA * DHA
B_WIDTH = 512
HB = 8
CONV_W = 4
LRU_C = 8.0
HC, DKC, DVC = 4, 256, 512
C_QK, C_V = HC * DKC, HC * DVC
ROPE_BASE = 10000.0
GN_EPS = 1e-5
N_EXPERTS = 32
TOP_K = 4
D_FF = 1024
SWIGLU_LIMIT = 7.0
SWIGLU_ALPHA = 1.702
ALPHA = (2.0 * DEPTH) ** 0.25
LN_EPS = 1e-5
NEG = -1e30

LANES = 128
VMEM_LIMIT = 56 * 1024 * 1024
ATT_TQ = 256
MOE_BM = 256
MOE_PAIR = 2
MOE_CH = 8
MOE_BIG = 4
RET_TB = 256
LRU_TB = 256


def _pick(n, cands):
    for c in cands:
        if n % c == 0:
            return c
    raise ValueError(f"no tile for {n} in {cands}")


def _cparams(sem):
    return pltpu.CompilerParams(dimension_semantics=sem, vmem_limit_bytes=VMEM_LIMIT)


def _layernorm(v, g, b):
    mu = jnp.mean(v, axis=-1, keepdims=True)
    d = v - mu
    var = jnp.mean(d * d, axis=-1, keepdims=True)
    return d * lax.rsqrt(var + LN_EPS) * g + b


def _pack_bf16_pairs(v):
    w = v.shape[1] // 2
    hi = lax.bitcast_convert_type(v[:, :w], u32)
    lo = lax.bitcast_convert_type(v[:, w:], u32)
    return (hi & jnp.uint32(0xFFFF0000)) | (lo >> 16)


def _unpack_bf16_pairs(p):
    hi = lax.bitcast_convert_type(p & jnp.uint32(0xFFFF0000), f32)
    lo = lax.bitcast_convert_type(p << 16, f32)
    return jnp.concatenate([hi, lo], axis=1).astype(bf16)


def _rows(ref, chunk, nchunks):
    return ref.at[pl.ds(pl.multiple_of(chunk * MOE_CH, MOE_CH), nchunks * MOE_CH)]


def _copy_run(src, src_chunk, dst, dst_chunk, n, sem, priority):
    nbig = lax.shift_right_logical(n, MOE_BIG.bit_length() - 1)

    def big(c, carry):
        pltpu.make_async_copy(_rows(src, src_chunk + c * MOE_BIG, MOE_BIG),
                              _rows(dst, dst_chunk + c * MOE_BIG, MOE_BIG), sem).start(priority=priority)
        return carry
    lax.fori_loop(0, nbig, big, 0)

    def small(c, carry):
        pltpu.make_async_copy(_rows(src, src_chunk + c, 1), _rows(dst, dst_chunk + c, 1),
                              sem).start(priority=priority)
        return carry
    lax.fori_loop(nbig * MOE_BIG, n, small, 0)


def _wait_chunks(hbm, sem, count, max_bits):
    for b in reversed(range(max_bits)):
        @pl.when(jnp.bitwise_and(count, 1 << b) != 0)
        def _():
            span = _rows(hbm, 0, 1 << b)
            pltpu.make_async_copy(span, span, sem).wait()


def _rel_bias_matrix(table, nq, nk, d0):
    p = nq + nk - 1
    j = jnp.arange(p)
    u = table[:, jnp.clip(d0 + nq - 1 - j, -REL_CLIP, REL_CLIP) + REL_CLIP]
    u = jnp.roll(u, -(nq - 1), axis=1)
    flat = jnp.tile(u, (1, nq))[:, :nq * (p - 1)]
    return flat.reshape(table.shape[0], nq, p - 1)[:, :, :nk].astype(f32)


def _mm_kernel(x_ref, w_ref, o_ref):
    o_ref[...] = jnp.dot(x_ref[...].astype(bf16), w_ref[...], preferred_element_type=f32)


def _matmul(x, w, tm, tn):
    m, k = x.shape
    n = w.shape[1]
    return pl.pallas_call(
        _mm_kernel,
        grid=(m // tm, n // tn),
        in_specs=[pl.BlockSpec((tm, k), lambda i, j: (i, 0)),
                  pl.BlockSpec((k, tn), lambda i, j: (0, j))],
        out_specs=pl.BlockSpec((tm, tn), lambda i, j: (i, j)),
        out_shape=jax.ShapeDtypeStruct((m, n), f32),
        compiler_params=_cparams(("parallel", "parallel")),
        name="dense_proj",
    )(x, w)


def _route_top_k(x, w, b):
    logits = jnp.dot(x.astype(bf16), w.astype(bf16), preferred_element_type=f32) + b
    tm = logits.shape[0]
    lane = lax.broadcasted_iota(i32, (tm, N_EXPERTS), 1)
    out_lane = lax.broadcasted_iota(i32, (tm, LANES), 1)
    e_out = jnp.zeros((tm, LANES), i32)
    g_out = jnp.zeros((tm, LANES), f32)
    member = jnp.zeros((tm, N_EXPERTS), f32)
    v0 = None
    den = jnp.zeros((tm, 1), f32)
    for k in range(TOP_K):
        v = jnp.max(logits, axis=-1, keepdims=True)
        idx = jnp.min(jnp.where(logits == v, lane, N_EXPERTS), axis=-1, keepdims=True)
        chosen = lane == idx
        logits = jnp.where(chosen, -jnp.inf, logits)
        member = member + chosen.astype(f32)
        if k == 0:
            v0 = v
        p = jnp.exp(v - v0)
        den = den + p
        e_out = jnp.where(out_lane == k, idx, e_out)
        g_out = jnp.where(out_lane == k, p, g_out)
    return e_out, g_out / den, member


def _proj_ln_kernel(*refs, n_in):
    a_refs = refs[:n_in]
    w_refs = refs[n_in:2 * n_in]
    x_ref, g_ref, b_ref, wr_ref, br_ref, o_ref, e_ref, gate_ref, cnt_ref = refs[2 * n_in:]
    acc = ALPHA * x_ref[...]
    for a_ref, w_ref in zip(a_refs, w_refs):
        acc = acc + jnp.dot(a_ref[...].astype(bf16), w_ref[...], preferred_element_type=f32)
    y = _layernorm(acc, g_ref[...], b_ref[...])
    o_ref[...] = y
    e_out, gates, member = _route_top_k(y, wr_ref[0], br_ref[0])
    e_ref[...] = e_out
    gate_ref[...] = gates
    nsub = cnt_ref.shape[0]
    td = e_out.shape[0] // nsub
    for u in range(nsub):
        cnt = jnp.sum(member[u * td:(u + 1) * td], axis=0, keepdims=True)
        cnt = jnp.concatenate([cnt, jnp.zeros((1, LANES - N_EXPERTS), f32)], axis=1)
        cnt_ref[u] = jnp.broadcast_to(cnt, cnt_ref.shape[1:])


def _proj_ln(acts, ws, x, g, b, w_router, b_router, l, tm, td):
    m, d = x.shape
    n_in = len(acts)
    depth = w_router.shape[0]
    nt = m // tm
    nsub = tm // td
    cur = lambda i: (i, 0)
    in_specs = ([pl.BlockSpec((tm, a.shape[1]), cur) for a in acts]
                + [pl.BlockSpec(w.shape, lambda i: (0, 0)) for w in ws]
                + [pl.BlockSpec((tm, d), cur),
                   pl.BlockSpec((1, d), lambda i: (0, 0)),
                   pl.BlockSpec((1, d), lambda i: (0, 0)),
                   pl.BlockSpec((1, d, N_EXPERTS), lambda i: (l, 0, 0)),
                   pl.BlockSpec((1, 1, N_EXPERTS), lambda i: (l, 0, 0))])
    return pl.pallas_call(
        functools.partial(_proj_ln_kernel, n_in=n_in),
        grid=(nt,),
        in_specs=in_specs,
        out_specs=[pl.BlockSpec((tm, d), cur),
                   pl.BlockSpec((tm, LANES), cur),
                   pl.BlockSpec((tm, LANES), cur),
                   pl.BlockSpec((nsub, 8, LANES), lambda i: (i, 0, 0))],
        out_shape=[jax.ShapeDtypeStruct((m, d), f32),
                   jax.ShapeDtypeStruct((m, LANES), i32),
                   jax.ShapeDtypeStruct((m, LANES), f32),
                   jax.ShapeDtypeStruct((m // td, 8, LANES), f32)],
        compiler_params=_cparams(("parallel",)),
        name="proj_ln",
    )(*acts, *ws, x, g.reshape(1, d), b.reshape(1, d), w_router, b_router.reshape(depth, 1, N_EXPERTS))


def _attn_prompt_kernel(q_ref, k0_ref, k1_ref, k2_ref, v0_ref, v1_ref, v2_ref, bias_ref, o_ref):
    i = pl.program_id(0)
    tq = q_ref.shape[0]
    q = q_ref[...]
    k = jnp.concatenate([k0_ref[...], k1_ref[...], k2_ref[...]], axis=0)
    v = jnp.concatenate([v0_ref[...], v1_ref[...], v2_ref[...]], axis=0)
    kcol = lax.broadcasted_iota(i32, (1, 3 * tq), 1)
    tile_ok = (kcol // tq + i) >= 2
    qb = (q * (DHA ** -0.5)).astype(bf16)
    kb = k.astype(bf16)
    vb = v.astype(bf16)
    outs = []
    for h in range(HA):
        sl = slice(h * DHA, (h + 1) * DHA)
        s = lax.dot_general(qb[:, sl], kb[:, sl], (((1,), (1,)), ((), ())), preferred_element_type=f32)
        s = jnp.where(tile_ok, s + bias_ref[h], NEG)
        p = jnp.exp(s - jnp.max(s, axis=-1, keepdims=True))
        o = jnp.dot(p.astype(bf16), vb[:, sl], preferred_element_type=f32)
        outs.append(o / jnp.sum(p, axis=-1, keepdims=True))
    o_ref[...] = jnp.concatenate(outs, axis=-1)


def _attn_prompt(z, rel_bias, tp):
    tq = ATT_TQ
    nq = tp // tq
    qq = jnp.arange(tq)[:, None]
    kk = jnp.arange(3 * tq)[None, :]
    band = (kk // CHUNK >= qq // CHUNK) & (kk // CHUNK <= qq // CHUNK + A_PAST_CHUNKS)
    bias = jnp.where(band[None], _rel_bias_matrix(rel_bias, tq, 3 * tq, 2 * tq), NEG)

    def kv_spec(back, col):
        return pl.BlockSpec((tq, A_WIDTH), lambda i: (jnp.maximum(i - back, 0), col))

    return pl.pallas_call(
        _attn_prompt_kernel,
        grid=(nq,),
        in_specs=[pl.BlockSpec((tq, A_WIDTH), lambda i: (i, 0)),
                  kv_spec(2, 1), kv_spec(1, 1), kv_spec(0, 1),
                  kv_spec(2, 2), kv_spec(1, 2), kv_spec(0, 2),
                  pl.BlockSpec((HA, tq, 3 * tq), lambda i: (0, 0, 0))],
        out_specs=pl.BlockSpec((tq, A_WIDTH), lambda i: (i, 0)),
        out_shape=jax.ShapeDtypeStruct((tp, A_WIDTH), f32),
        compiler_params=_cparams(("parallel",)),
        name="attn_prompt",
    )(z, z, z, z, z, z, z, bias)


def _attn_sample_kernel(q_ref, kn_ref, vn_ref, kc_ref, vc_ref, bias_ref, o_ref):
    p_len = kc_ref.shape[1]
    q, kn, vn = q_ref[...], kn_ref[...], vn_ref[...]
    kc, vc = kc_ref[0], vc_ref[0]
    outs = []
    for h in range(HA):
        sl = slice(h * DHA, (h + 1) * DHA)
        qh = q[:, sl].astype(bf16)
        dims = (((1,), (1,)), ((), ()))
        sc = lax.dot_general(qh, kc[:, sl].astype(bf16), dims, preferred_element_type=f32)
        sn = lax.dot_general(qh, kn[:, sl].astype(bf16), dims, preferred_element_type=f32)
        b = bias_ref[h]
        sc = sc * (DHA ** -0.5) + b[:, :p_len]
        sn = sn * (DHA ** -0.5) + b[:, p_len:]
        m = jnp.maximum(jnp.max(sc, axis=-1, keepdims=True), jnp.max(sn, axis=-1, keepdims=True))
        pc, pn = jnp.exp(sc - m), jnp.exp(sn - m)
        den = jnp.sum(pc, axis=-1, keepdims=True) + jnp.sum(pn, axis=-1, keepdims=True)
        pc, pn = pc / den, pn / den
        outs.append(jnp.dot(pc.astype(bf16), vc[:, sl].astype(bf16), preferred_element_type=f32)
                    + jnp.dot(pn.astype(bf16), vn[:, sl].astype(bf16), preferred_element_type=f32))
    o_ref[...] = jnp.concatenate(outs, axis=-1)


def _attn_sample(z, k_cache, v_cache, rel_bias, row0, bs, ts):
    p_len = k_cache.shape[1]
    kc = k_cache.reshape(bs, p_len, A_WIDTH)
    vc = v_cache.reshape(bs, p_len, A_WIDTH)
    bias = _rel_bias_matrix(rel_bias, ts, p_len + ts, p_len)
    blk0 = row0 // ts

    def z_spec(col):
        return pl.BlockSpec((ts, A_WIDTH), lambda b: (blk0 + b, col))

    return pl.pallas_call(
        _attn_sample_kernel,
        grid=(bs,),
        in_specs=[z_spec(0), z_spec(1), z_spec(2),
                  pl.BlockSpec((1, p_len, A_WIDTH), lambda b: (b, 0, 0)),
                  pl.BlockSpec((1, p_len, A_WIDTH), lambda b: (b, 0, 0)),
                  pl.BlockSpec((HA, ts, p_len + ts), lambda b: (0, 0, 0))],
        out_specs=pl.BlockSpec((ts, A_WIDTH), lambda b: (b, 0)),
        out_shape=jax.ShapeDtypeStruct((bs * ts, A_WIDTH), f32),
        compiler_params=_cparams(("parallel",)),
        name="attn_sample",
    )(z, z, z, kc, vc, bias)


def _gelu_tanh(x):
    return 0.5 * x * (1.0 + jnp.tanh(math.sqrt(2.0 / math.pi) * (x + 0.044715 * (x * x * x))))


def _rglru_kernel(xb_ref, gb_ref, conv0_ref, h0_ref, cw_ref, cb_ref, wrg_ref, brg_ref,
                  wig_ref, big_ref, sp_ref, rec_ref, convn_ref, hl_ref, xp_ref, hc_ref):
    t = pl.program_id(1)
    tb = xb_ref.shape[0]
    pad = 8

    @pl.when(t == 0)
    def _():
        xp_ref[0:pad, :] = jnp.zeros((pad, B_WIDTH), f32)
        xp_ref[pad - (CONV_W - 1):pad, :] = conv0_ref[0]
        hc_ref[...] = h0_ref[0]

    xb = xb_ref[...]
    xp_ref[pad:pad + tb, :] = xb
    cw = cw_ref[...]
    u = cb_ref[...] + cw[CONV_W - 1:CONV_W, :] * xb
    for j in range(CONV_W - 1):
        sh = CONV_W - 1 - j
        u = u + cw[j:j + 1, :] * xp_ref[pad - sh:pad - sh + tb, :]
    convn_ref[0] = xp_ref[pad + tb - (CONV_W - 1):pad + tb, :]
    xp_ref[0:pad, :] = xp_ref[tb:tb + pad, :]

    ub = u.astype(bf16)
    r = jax.nn.sigmoid(jnp.dot(ub, wrg_ref[...], preferred_element_type=f32) + brg_ref[...])
    ig = jax.nn.sigmoid(jnp.dot(ub, wig_ref[...], preferred_element_type=f32) + big_ref[...])
    log_a = -LRU_C * r * sp_ref[...]
    a = jnp.exp(log_a)
    bt = jnp.sqrt(-jnp.tanh(log_a) * (a * a + 1.0)) * (ig * u)

    row = lax.broadcasted_iota(i32, (tb, 1), 0)
    s = 1
    while s < tb:
        keep = row >= s
        a_sh = pltpu.roll(a, s, axis=0)
        b_sh = pltpu.roll(bt, s, axis=0)
        bt = jnp.where(keep, a * b_sh + bt, bt)
        a = jnp.where(keep, a * a_sh, a)
        s *= 2
    h = a * hc_ref[...] + bt
    hc_ref[...] = h[tb - 1:tb, :]
    hl_ref[0] = h[tb - 1:tb, :]
    rec_ref[...] = h * _gelu_tanh(gb_ref[...])


def _rglru(z, conv0, h0, cw, cb, wrg, brg, wig, big, sp, row0, bsz, t_len, tb):
    nt = t_len // tb
    blk0 = row0 // tb

    def z_spec(col):
        return pl.BlockSpec((tb, B_WIDTH), lambda b, t: (blk0 + b * nt + t, col))

    def const(shape):
        return pl.BlockSpec(shape, lambda b, t: (0,) * len(shape))

    row = lambda v: v.reshape(1, B_WIDTH)
    return pl.pallas_call(
        _rglru_kernel,
        grid=(bsz, nt),
        in_specs=[z_spec(3), z_spec(4),
                  pl.BlockSpec((1, CONV_W - 1, B_WIDTH), lambda b, t: (b, 0, 0)),
                  pl.BlockSpec((1, 1, B_WIDTH), lambda b, t: (b, 0, 0)),
                  const((CONV_W, B_WIDTH)), const((1, B_WIDTH)),
                  const((B_WIDTH, B_WIDTH)), const((1, B_WIDTH)),
                  const((B_WIDTH, B_WIDTH)), const((1, B_WIDTH)), const((1, B_WIDTH))],
        out_specs=[pl.BlockSpec((tb, B_WIDTH), lambda b, t: (b * nt + t, 0)),
                   pl.BlockSpec((1, CONV_W - 1, B_WIDTH), lambda b, t: (b, 0, 0)),
                   pl.BlockSpec((1, 1, B_WIDTH), lambda b, t: (b, 0, 0))],
        out_shape=[jax.ShapeDtypeStruct((bsz * t_len, B_WIDTH), f32),
                   jax.ShapeDtypeStruct((bsz, CONV_W - 1, B_WIDTH), f32),
                   jax.ShapeDtypeStruct((bsz, 1, B_WIDTH), f32)],
        scratch_shapes=[pltpu.VMEM((tb + 8, B_WIDTH), f32), pltpu.VMEM((1, B_WIDTH), f32)],
        compiler_params=_cparams(("parallel", "arbitrary")),
        name="rglru",
    )(z, z, conv0, h0.reshape(bsz, 1, B_WIDTH), cw, row(cb), wrg, row(brg), wig, row(big), row(sp))


def _block_diag(w):
    hb, d, _ = w.shape
    eye = jnp.eye(hb, dtype=w.dtype)
    return (eye[:, None, :, None] * w[:, :, None, :]).reshape(hb * d, hb * d)


def _retention_kernel(x_ref, w_hbm, cos_ref, sin_ref, s0_hbm, gng_ref, gnb_ref,
                      y_ref, sout_hbm, w_ref, s_ref, z_ref, sem, *, chunk, nt, tb, zero_state, hoist):
    b = pl.program_id(0)
    t = pl.program_id(1)
    half = DKC // 2

    @pl.when((b == 0) & (t == 0))
    def _():
        cp = pltpu.make_async_copy(w_hbm, w_ref, sem.at[0])
        cp.start()
        cp.wait()
        if hoist:
            xa = x_ref[...].astype(bf16)
            for c in range(0, w_ref.shape[1], C_QK):
                z_ref[:, c:c + C_QK] = jnp.dot(xa, w_ref[:, c:c + C_QK], preferred_element_type=f32)

    @pl.when(t == 0)
    def _():
        if zero_state:
            s_ref[...] = jnp.zeros(s_ref.shape, f32)
        else:
            cp = pltpu.make_async_copy(s0_hbm.at[b], s_ref, sem.at[1])
            cp.start()
            cp.wait()

    cos, sin = cos_ref[...], sin_ref[...]
    n = lax.broadcasted_iota(i32, (tb, 1), 0)
    m = lax.broadcasted_iota(i32, (1, tb), 1)
    dist = n - m
    same = (n // chunk) == (m // chunk)
    expo = jnp.where(same, jnp.abs(dist), dist).astype(f32)
    visible = same | (dist > 0)
    nf = n.astype(f32)

    if hoist:
        row0 = pl.multiple_of(b * tb, tb)

        def proj(col, width):
            return z_ref[pl.ds(row0, tb), col:col + width]
    else:
        xb = x_ref[...].astype(bf16)

        def proj(col, width):
            return jnp.dot(xb, w_ref[:, col:col + width], preferred_element_type=f32)

    def rot(v, scale):
        x1, x2 = v[:, :half], v[:, half:]
        return jnp.concatenate([x1 * cos - x2 * sin, x2 * cos + x1 * sin], axis=-1) * scale

    for h in range(HC):
        lg = math.log(1.0 - 2.0 ** (-5.0 - h))
        qr = rot(proj(h * DKC, DKC), 1.0)
        kr = rot(proj(C_QK + h * DKC, DKC), DKC ** -0.5)
        vb = proj(2 * C_QK + h * DVC, DVC).astype(bf16)
        gate = proj(2 * C_QK + C_V + h * DVC, DVC)
        qb = qr.astype(bf16)
        dmat = jnp.where(visible, jnp.exp(lg * expo), 0.0)
        s = lax.dot_general(qb, kr.astype(bf16), (((1,), (1,)), ((), ())),
                            preferred_element_type=f32) * dmat
        s_old = s_ref[h]
        o = jnp.dot(s.astype(bf16), vb, preferred_element_type=f32)
        o = o + jnp.dot(qb, s_old.astype(bf16), preferred_element_type=f32) * jnp.exp(lg * (nf + 1.0))
        kd = (kr * jnp.exp(lg * (tb - 1.0 - nf))).astype(bf16)
        s_ref[h] = math.exp(lg * tb) * s_old + lax.dot_general(
            kd, vb, (((0,), (0,)), ((), ())), preferred_element_type=f32)

        mu = jnp.mean(o, axis=-1, keepdims=True)
        dlt = o - mu
        var = jnp.mean(dlt * dlt, axis=-1, keepdims=True)
        vs = slice(h * DVC, (h + 1) * DVC)
        yn = dlt * lax.rsqrt(var + GN_EPS) * gng_ref[:, vs] + gnb_ref[:, vs]
        y_ref[:, vs] = (jax.nn.silu(gate) * yn).astype(y_ref.dtype)

    @pl.when(t == nt - 1)
    def _():
        cp = pltpu.make_async_copy(s_ref, sout_hbm.at[b], sem.at[1])
        cp.start()
        cp.wait()


def _retention(x, w_in, s0, pos0, gn_g, gn_b, bsz, t_len, tb, chunk):
    nt = t_len // tb
    d = x.shape[1]
    half = DKC // 2
    inv = 1.0 / (ROPE_BASE ** (jnp.arange(half, dtype=f32) / half))
    ang = (pos0 + jnp.arange(t_len)).astype(f32)[:, None] * inv[None, :]
    cos, sin = jnp.cos(ang), jnp.sin(ang)
    zero_state = s0 is None
    if zero_state:
        s0 = jnp.zeros((1, 8, LANES), f32)
    hoist = nt == 1 and bsz > 1
    x_spec = (pl.BlockSpec(x.shape, lambda b, t: (0, 0)) if hoist
              else pl.BlockSpec((tb, d), lambda b, t: (b * nt + t, 0)))
    z_shape = (bsz * tb, w_in.shape[1]) if hoist else (8, LANES)
    return pl.pallas_call(
        functools.partial(_retention_kernel, chunk=chunk, nt=nt, tb=tb, zero_state=zero_state, hoist=hoist),
        grid=(bsz, nt),
        in_specs=[x_spec,
                  pl.BlockSpec(memory_space=pl.ANY),
                  pl.BlockSpec((tb, half), lambda b, t: (t, 0)),
                  pl.BlockSpec((tb, half), lambda b, t: (t, 0)),
                  pl.BlockSpec(memory_space=pl.ANY),
                  pl.BlockSpec((1, C_V), lambda b, t: (0, 0)),
                  pl.BlockSpec((1, C_V), lambda b, t: (0, 0))],
        out_specs=[pl.BlockSpec((tb, C_V), lambda b, t: (b * nt + t, 0)),
                   pl.BlockSpec(memory_space=pl.ANY)],
        out_shape=[jax.ShapeDtypeStruct((bsz * t_len, C_V), bf16),
                   jax.ShapeDtypeStruct((bsz, HC, DKC, DVC), f32)],
        scratch_shapes=[pltpu.VMEM(w_in.shape, bf16), pltpu.VMEM((HC, DKC, DVC), f32),
                        pltpu.VMEM(z_shape, f32), pltpu.SemaphoreType.DMA((2,))],
        compiler_params=_cparams(("arbitrary", "arbitrary")),
        name="retention",
    )(x, w_in, cos, sin, s0, gn_g.reshape(1, C_V), gn_b.reshape(1, C_V))


def _dispatch_kernel(ech_ref, est_ref, tab_ref, tabv_ref, ep_ref, es_ref, xp_ref, xs_ref,
                     xg_hbm, lrow_ref, sbuf, zchunk, pending, sem, *, np_tiles, nt):
    i = pl.program_id(0)
    td = xp_ref.shape[0]
    ls = sbuf.shape[1]
    slot = i % 2
    nchunks = xg_hbm.shape[0] // MOE_CH
    blk_chunks = MOE_BM // MOE_CH

    def drain(s):
        _wait_chunks(xg_hbm, sem.at[s], pending[s], (ls // MOE_CH).bit_length())

    @pl.when(i == 0)
    def _():
        pending[0] = 0
        pending[1] = 0
        zchunk[...] = jnp.zeros(zchunk.shape, zchunk.dtype)

        def zero_chunks(lo, hi):
            def put(c, carry):
                pltpu.make_async_copy(zchunk, _rows(xg_hbm, c, 1), sem.at[2]).start()
                return carry
            lax.fori_loop(lo, hi, put, 0)
            _wait_chunks(xg_hbm, sem.at[2], jnp.maximum(hi - lo, 0), nchunks.bit_length())

        def per_expert(e, carry):
            used = est_ref[e] + ech_ref[e]
            zero_chunks(used, used + jnp.bitwise_and(-ech_ref[e], blk_chunks - 1))
            return carry
        lax.fori_loop(0, N_EXPERTS, per_expert, 0)
        last = N_EXPERTS - 1
        end = est_ref[last] + ech_ref[last] + jnp.bitwise_and(-ech_ref[last], blk_chunks - 1)
        zero_chunks(end, nchunks)

    def sort_and_copy(e_ref, x_ref):
        e = e_ref[...]
        lane = lax.broadcasted_iota(i32, (td, LANES), 1)
        member = jnp.zeros((td, LANES), f32)
        for k in range(TOP_K):
            member = member + (lane == e[:, k:k + 1]).astype(f32)
        r = lax.broadcasted_iota(i32, (td, td), 0)
        c = lax.broadcasted_iota(i32, (td, td), 1)
        before = jnp.dot((c < r).astype(bf16), member.astype(bf16), preferred_element_type=f32)
        in_tile = before + tabv_ref[0, 0:1, :]
        col = lax.broadcasted_iota(i32, (td, ls), 1)
        place = None
        lrow = jnp.zeros((td, LANES), i32)
        for k in range(TOP_K):
            sel = lane == e[:, k:k + 1]
            row_k = jnp.sum(jnp.where(sel, in_tile, 0.0), axis=-1, keepdims=True).astype(i32)
            place = (col == row_k) if place is None else place | (col == row_k)
            lrow = jnp.where(lane == k, row_k, lrow)
        lrow_ref[...] = lrow
        place = jnp.where(place, 1.0, 0.0).astype(bf16)
        srt = lax.dot_general(place, x_ref[...].astype(bf16), (((0,), (0,)), ((), ())),
                              preferred_element_type=f32)
        drain(slot)
        sbuf[slot] = _pack_bf16_pairs(srt)

        total = 0
        for ei in range(N_EXPERTS):
            dst0 = tab_ref[0, 0, ei]
            n = tab_ref[0, 0, N_EXPERTS + ei]
            src0 = tab_ref[0, 0, 2 * N_EXPERTS + ei]
            _copy_run(sbuf.at[slot], src0, xg_hbm, dst0, n, sem.at[slot], ei % 2)
            total = total + n
        pending[slot] = total

    @pl.when(i < np_tiles)
    def _():
        sort_and_copy(ep_ref, xp_ref)

    @pl.when(i >= np_tiles)
    def _():
        sort_and_copy(es_ref, xs_ref)

    @pl.when(i == nt - 1)
    def _():
        drain(0)
        drain(1)


def _dispatch(xs, top_es, tab, tabv, ech, est, nblk, td):
    x_p, x_s = xs
    d = x_p.shape[1]
    np_tiles, ns_tiles = x_p.shape[0] // td, x_s.shape[0] // td
    nt = np_tiles + ns_tiles
    ls = td * TOP_K + N_EXPERTS * MOE_CH
    pspec = lambda w: pl.BlockSpec((td, w), lambda i, *_: (jnp.minimum(i, np_tiles - 1), 0))
    sspec = lambda w: pl.BlockSpec((td, w), lambda i, *_: (jnp.maximum(i - np_tiles, 0), 0))
    grid_spec = pltpu.PrefetchScalarGridSpec(
        num_scalar_prefetch=2,
        grid=(nt,),
        in_specs=[pl.BlockSpec((1, 1, LANES), lambda i, *_: (i, 0, 0), memory_space=pltpu.SMEM),
                  pl.BlockSpec((1, 8, LANES), lambda i, *_: (i, 0, 0)),
                  pspec(LANES), sspec(LANES), pspec(d), sspec(d)],
        out_specs=[pl.BlockSpec(memory_space=pl.ANY),
                   pl.BlockSpec((td, LANES), lambda i, *_: (i, 0))],
        scratch_shapes=[pltpu.VMEM((2, ls, d // 2), u32), pltpu.VMEM((MOE_CH, d // 2), u32),
                        pltpu.SMEM((2,), i32), pltpu.SemaphoreType.DMA((3,))],
    )
    xg, lrow = pl.pallas_call(
        functools.partial(_dispatch_kernel, np_tiles=np_tiles, nt=nt),
        grid_spec=grid_spec,
        out_shape=[jax.ShapeDtypeStruct((nblk * MOE_BM, d // 2), u32),
                   jax.ShapeDtypeStruct((nt * td, LANES), i32)],
        compiler_params=_cparams(("arbitrary",)),
        name="moe_dispatch",
    )(ech, est, tab, tabv, top_es[0], top_es[1], x_p, x_s)
    return xg, lrow


def _moe_kernel(blk_e_ref, nact_ref, first_ref, seg_ref, nxt_ref, x_ref, wup_hbm, bup_ref, wdn_hbm, bdn_ref,
                y_ref, wup_f, wdn_f, wup_bf, wdn_bf, sem, *, layer):
    b0 = pl.program_id(0) * MOE_PAIR
    nact = nact_ref[0]
    bm = MOE_BM

    def weight_copies(e, slot):
        return (pltpu.make_async_copy(wup_hbm.at[layer, e], wup_f.at[slot], sem.at[slot, 0]),
                pltpu.make_async_copy(wdn_hbm.at[layer, e], wdn_f.at[slot], sem.at[slot, 1]))

    @pl.when(b0 < nact)
    def _():
        for sub in range(MOE_PAIR):
            b = b0 + sub

            @pl.when(first_ref[b] == 1)
            def _():
                slot = seg_ref[b] % 2

                @pl.when(b == 0)
                def _():
                    for c in weight_copies(blk_e_ref[0], 0):
                        c.start()

                for c in weight_copies(blk_e_ref[b], slot):
                    c.wait()

                @pl.when(nxt_ref[b] >= 0)
                def _():
                    for c in weight_copies(nxt_ref[b], 1 - slot):
                        c.start()

                wup_bf[slot] = wup_f[slot].astype(bf16)
                wdn_bf[slot] = wdn_f[slot].astype(bf16)

        for sub in range(MOE_PAIR):
            b = b0 + sub
            slot = seg_ref[b] % 2
            e = layer * N_EXPERTS + blk_e_ref[b]
            rows = slice(sub * bm, (sub + 1) * bm)
            hdn = jnp.dot(_unpack_bf16_pairs(x_ref[rows, :]), wup_bf[slot], preferred_element_type=f32)
            hdn = hdn + bup_ref[pl.ds(e, 1), :]
            glu = jnp.minimum(hdn[:, :D_FF], SWIGLU_LIMIT)
            lin = jnp.clip(hdn[:, D_FF:], -SWIGLU_LIMIT, SWIGLU_LIMIT)
            act = glu * jax.nn.sigmoid(SWIGLU_ALPHA * glu) * (lin + 1.0)
            y_ref[rows, :] = (jnp.dot(act.astype(bf16), wdn_bf[slot], preferred_element_type=f32)
                              + bdn_ref[pl.ds(e, 1), :])

    @pl.when(b0 >= nact)
    def _():
        y_ref[...] = jnp.zeros(y_ref.shape, f32)


def _moe_experts(xg, blk_e, nact, first, seg, nxt, l, w_up, b_up, w_down, b_down):
    r, half = xg.shape
    d = 2 * half
    rows = MOE_BM * MOE_PAIR
    depth = w_up.shape[0]
    whole = lambda shape: pl.BlockSpec(shape, lambda s, *_: (0,) * len(shape))
    grid_spec = pltpu.PrefetchScalarGridSpec(
        num_scalar_prefetch=5,
        grid=(r // rows,),
        in_specs=[
            pl.BlockSpec((rows, half), lambda s, be, na, *_: (jnp.minimum(s, (na[0] - 1) // MOE_PAIR), 0)),
            pl.BlockSpec(memory_space=pl.ANY),
            whole((depth * N_EXPERTS, 2 * D_FF)),
            pl.BlockSpec(memory_space=pl.ANY),
            whole((depth * N_EXPERTS, d)),
        ],
        out_specs=pl.BlockSpec((rows, d), lambda s, *_: (s, 0)),
        scratch_shapes=[pltpu.VMEM((2, d, 2 * D_FF), f32), pltpu.VMEM((2, D_FF, d), f32),
                        pltpu.VMEM((2, d, 2 * D_FF), bf16), pltpu.VMEM((2, D_FF, d), bf16),
                        pltpu.SemaphoreType.DMA((2, 2))],
    )
    return pl.pallas_call(
        functools.partial(_moe_kernel, layer=l),
        grid_spec=grid_spec,
        out_shape=jax.ShapeDtypeStruct((r, d), f32),
        compiler_params=_cparams(("arbitrary",)),
        name="moe_experts",
    )(blk_e, nact, first, seg, nxt, xg, w_up, b_up.reshape(depth * N_EXPERTS, 2 * D_FF),
      w_down, b_down.reshape(depth * N_EXPERTS, d))


def _split_bf16(v):
    hi = v.astype(bf16)
    return hi, (v - hi.astype(f32)).astype(bf16)


def _combine_ln_kernel(tab_cur_ref, tab_nxt_ref, y_hbm, lrow_ref, gate_ref, x_ref, g_ref, b_ref, o_ref,
                       ybuf, pending, sem, *, nt):
    i = pl.program_id(0)
    td = x_ref.shape[0]
    ls = ybuf.shape[1]
    slot = i % 2

    def gather(tab_ref, s):
        total = 0
        for ei in range(N_EXPERTS):
            src0 = tab_ref[0, 0, ei]
            n = tab_ref[0, 0, N_EXPERTS + ei]
            dst0 = tab_ref[0, 0, 2 * N_EXPERTS + ei]
            _copy_run(y_hbm, src0, ybuf.at[s], dst0, n, sem.at[s], ei % 2)
            total = total + n
        pending[s] = total

    @pl.when(i == 0)
    def _():
        ybuf[...] = jnp.zeros(ybuf.shape, f32)
        gather(tab_cur_ref, 0)

    @pl.when(i + 1 < nt)
    def _():
        gather(tab_nxt_ref, (i + 1) % 2)

    _wait_chunks(y_hbm, sem.at[slot], pending[slot], (ls // MOE_CH).bit_length())

    lrow = lrow_ref[...]
    gates = gate_ref[...]
    col = lax.broadcasted_iota(i32, (td, ls), 1)
    weight = jnp.zeros((td, ls), f32)
    for k in range(TOP_K):
        weight = weight + jnp.where(col == lrow[:, k:k + 1], gates[:, k:k + 1], 0.0)
    wh, wl = _split_bf16(weight)
    yb = ybuf[slot].astype(bf16)
    moe = jnp.dot(wh, yb, preferred_element_type=f32) + jnp.dot(wl, yb, preferred_element_type=f32)
    o_ref[...] = _layernorm(ALPHA * x_ref[...] + moe, g_ref[...], b_ref[...])


def _combine_ln(y, tab, lrow, blk0, gates, x, g, b, td):
    rows, d = x.shape
    nt = rows // td
    ls = td * TOP_K + N_EXPERTS * MOE_CH
    grid_spec = pltpu.PrefetchScalarGridSpec(
        num_scalar_prefetch=0,
        grid=(nt,),
        in_specs=[
            pl.BlockSpec((1, 1, LANES), lambda i: (blk0 + i, 0, 0), memory_space=pltpu.SMEM),
            pl.BlockSpec((1, 1, LANES), lambda i: (blk0 + jnp.minimum(i + 1, nt - 1), 0, 0),
                         memory_space=pltpu.SMEM),
            pl.BlockSpec(memory_space=pl.ANY),
            pl.BlockSpec((td, LANES), lambda i: (blk0 + i, 0)),
            pl.BlockSpec((td, LANES), lambda i: (i, 0)),
            pl.BlockSpec((td, d), lambda i: (i, 0)),
            pl.BlockSpec((1, d), lambda i: (0, 0)),
            pl.BlockSpec((1, d), lambda i: (0, 0)),
        ],
        out_specs=pl.BlockSpec((td, d), lambda i: (i, 0)),
        scratch_shapes=[pltpu.VMEM((2, ls, d), f32), pltpu.SMEM((2,), i32), pltpu.SemaphoreType.DMA((2,))],
    )
    return pl.pallas_call(
        functools.partial(_combine_ln_kernel, nt=nt),
        grid_spec=grid_spec,
        out_shape=jax.ShapeDtypeStruct((rows, d), f32),
        compiler_params=_cparams(("arbitrary",)),
        name="combine_ln",
    )(tab, tab, y, lrow, gates, x, g.reshape(1, d), b.reshape(1, d))


def _moe_ln(xs, top_es, gates, cnts, td, l, w_up, b_up, w_down, b_down, g, b):
    x_p, x_s = xs
    n_p, n_s = x_p.shape[0], x_s.shape[0]
    n = n_p + n_s
    bm = MOE_BM
    nt = n // td
    nblk = -(-(n * TOP_K + nt * N_EXPERTS * (MOE_CH - 1)) // bm) + N_EXPERTS
    nblk = -(-nblk // MOE_PAIR) * MOE_PAIR
    cnt = jnp.concatenate([c[:, 0, :N_EXPERTS] for c in cnts], axis=0).astype(i32)
    run = (cnt + MOE_CH - 1) // MOE_CH
    in_tile = jnp.cumsum(run, axis=1) - run
    ech = jnp.sum(run, axis=0)
    blk_chunks = bm // MOE_CH
    padded_ch = (ech + blk_chunks - 1) // blk_chunks * blk_chunks
    est = jnp.cumsum(padded_ch) - padded_ch
    in_all = est[None, :] + jnp.cumsum(run, axis=0) - run
    tab = jnp.concatenate([in_all, run, in_tile, jnp.zeros_like(run)], axis=1).reshape(nt, 1, LANES)
    tabv = jnp.zeros((nt, 8, LANES), f32).at[:, 0, :N_EXPERTS].set((in_tile * MOE_CH).astype(f32))
    xg, lrow = _dispatch(xs, top_es, tab, tabv, ech, est, nblk, td)
    padded = padded_ch * MOE_CH
    ends = jnp.cumsum(padded)
    blk = jnp.arange(nblk, dtype=i32)
    blk_e = jnp.minimum(jnp.sum((ends[None, :] <= (blk * bm)[:, None]).astype(i32), axis=1), N_EXPERTS - 1)
    nact = (ends[-1] // bm).astype(i32).reshape(1)
    first = ((blk < nact[0]) & ((blk == 0) | (blk_e != jnp.roll(blk_e, 1)))).astype(i32)
    seg = jnp.cumsum(first) - 1
    eid = jnp.arange(N_EXPERTS, dtype=i32)
    later_nonempty = (padded[None, :] > 0) & (eid[None, :] > eid[:, None])
    nxt_of_e = jnp.min(jnp.where(later_nonempty, eid[None, :], N_EXPERTS), axis=1)
    nxt = jnp.where(nxt_of_e < N_EXPERTS, nxt_of_e, -1)[blk_e].astype(i32)
    y = _moe_experts(xg, blk_e, nact, first, seg, nxt, l, w_up, b_up, w_down, b_down)
    return (_combine_ln(y, tab, lrow, 0, gates[0], x_p, g, b, td),
            _combine_ln(y, tab, lrow, n_p // td, gates[1], x_s, g, b, td))


def kernel(x_prompt, x_sample, cache_a_k, cache_a_v, state_b_conv, state_b_h, state_c_s, ab_w_in, ab_rel_bias, ab_conv_w, ab_conv_b, ab_w_rg, ab_b_rg, ab_w_ig, ab_b_ig, ab_lambda, ab_w_out, c_w_in, c_gn_g, c_gn_b, c_w_out, ln1_g, ln1_b, ln2_g, ln2_b, moe_w_router, moe_b_router, moe_w_up, moe_b_up, moe_w_down, moe_b_down):
    bp, tp, d = x_prompt.shape
    bs, ts, _ = x_sample.shape
    assert bp == 1 and tp % RET_TB == 0 and ts <= CHUNK
    n_p, n_s = bp * tp, bs * ts
    xs = (x_prompt.reshape(n_p, d), x_sample.reshape(n_s, d))
    tms = tuple(_pick(m, (512, 256, 128, 64)) for m in (n_p, n_s))
    td = _pick(math.gcd(n_p, n_s), (256, 128, 64, 32))
    moe = (moe_w_up, moe_b_up, moe_w_down, moe_b_down)
    router = (moe_w_router, moe_b_router)

    def per_group(fn, *groups):
        return tuple(zip(*[fn(*args) for args in zip(*groups)]))

    j = 0
    w_in = ab_w_in[j].astype(bf16)
    z_p, z_s = (_matmul(x, w_in, tm, w_in.shape[1]) for x, tm in zip(xs, tms))
    att = (_attn_prompt(z_p, ab_rel_bias[j], tp),
           _attn_sample(z_s, cache_a_k[j], cache_a_v[j], ab_rel_bias[j], 0, bs, ts))
    lru_w = (ab_conv_w[j], ab_conv_b[j], _block_diag(ab_w_rg[j]).astype(bf16), ab_b_rg[j],
             _block_diag(ab_w_ig[j]).astype(bf16), ab_b_ig[j], jax.nn.softplus(-ab_lambda[j]))
    rec_p, bc_p, bh_p = _rglru(z_p, jnp.zeros((bp, CONV_W - 1, B_WIDTH), f32), jnp.zeros((bp, B_WIDTH), f32),
                               *lru_w, 0, bp, tp, _pick(tp, (LRU_TB, 128, 64)))
    rec_s, bc_s, bh_s = _rglru(z_s, state_b_conv[j], state_b_h[j], *lru_w, 0, bs, ts, ts)
    w_out = ab_w_out[j].astype(bf16)
    w_out = [w_out[:A_WIDTH], w_out[A_WIDTH:]]
    xs, top_es, gates, cnts = per_group(
        lambda a, r, x, tm: _proj_ln([a, r], w_out, x, ln1_g[0], ln1_b[0], *router, 0, tm, td),
        att, (rec_p, rec_s), xs, tms)
    xs = _moe_ln(xs, top_es, gates, cnts, td, 0, *moe, ln2_g[0], ln2_b[0])

    keep = min(A_PAST_CHUNKS * CHUNK, tp)
    ak_p = z_p[n_p - keep:, A_WIDTH:2 * A_WIDTH].reshape(1, bp, keep, HA, DHA)
    av_p = z_p[n_p - keep:, 2 * A_WIDTH:3 * A_WIDTH].reshape(1, bp, keep, HA, DHA)
    ak_s = z_s[:, A_WIDTH:2 * A_WIDTH].reshape(1, bs, ts, HA, DHA)
    av_s = z_s[:, 2 * A_WIDTH:3 * A_WIDTH].reshape(1, bs, ts, HA, DHA)

    w_in = c_w_in[j].astype(bf16)
    y_p, cs_p = _retention(xs[0], w_in, None, 0, c_gn_g[j], c_gn_b[j], bp, tp, RET_TB, CHUNK)
    y_s, cs_s = _retention(xs[1], w_in, state_c_s[j], PAST_LEN, c_gn_g[j], c_gn_b[j], bs, ts, ts, ts)
    w_out = [c_w_out[j].astype(bf16)]
    xs, top_es, gates, cnts = per_group(
        lambda y, x, tm: _proj_ln([y], w_out, x, ln1_g[1], ln1_b[1], *router, 1, tm, td),
        (y_p, y_s), xs, tms)
    out_p, out_s = _moe_ln(xs, top_es, gates, cnts, td, 1, *moe, ln2_g[1], ln2_b[1])

    return (out_p.reshape(bp, tp, d), out_s.reshape(bs, ts, d),
            ak_p, av_p, bc_p[None], bh_p.reshape(1, bp, B_WIDTH), cs_p[None],
            ak_s, av_s, bc_s[None], bh_s.reshape(1, bs, B_WIDTH), cs_s[None])
```

```python
import functools
import math

import jax
import jax.numpy as jnp
from jax import lax
from jax.experimental import pallas as pl
from jax.experimental.pallas import tpu as pltpu

f32 = jnp.float32
bf16 = jnp.bfloat16
i32 = jnp.int32
u32 = jnp.uint32

DEPTH = 2
PAST_LEN = 1024
CHUNK = 64
A_PAST_CHUNKS = 8
REL_CLIP = 128
HA, DHA = 8, 64
A_WIDTH = HA * DHA
B_WIDTH = 512
HB = 8
CONV_W = 4
LRU_C = 8.0
HC, DKC, DVC = 4, 256, 512
C_QK, C_V = HC * DKC, HC * DVC
ROPE_BASE = 10000.0
GN_EPS = 1e-5
N_EXPERTS = 32
TOP_K = 4
D_FF = 1024
SWIGLU_LIMIT = 7.0
SWIGLU_ALPHA = 1.702
ALPHA = (2.0 * DEPTH) ** 0.25
LN_EPS = 1e-5
NEG = -1e30

LANES = 128
VMEM_LIMIT = 56 * 1024 * 1024
ATT_TQ = 256
MOE_BM = 256
MOE_PAIR = 2
MOE_CH = 8
MOE_BIG = 4
RET_TB = 256
LRU_TB = 256


def _pick(n, cands):
    for c in cands:
        if n % c == 0:
            return c
    raise ValueError(f"no tile for {n} in {cands}")


def _cparams(sem):
    return pltpu.CompilerParams(dimension_semantics=sem, vmem_limit_bytes=VMEM_LIMIT)


def _layernorm(v, g, b):
    mu = jnp.mean(v, axis=-1, keepdims=True)
    d = v - mu
    var = jnp.mean(d * d, axis=-1, keepdims=True)
    return d * lax.rsqrt(var + LN_EPS) * g + b


def _pack_bf16_pairs(v):
    w = v.shape[1] // 2
    hi = lax.bitcast_convert_type(v[:, :w], u32)
    lo = lax.bitcast_convert_type(v[:, w:], u32)
    return (hi & jnp.uint32(0xFFFF0000)) | (lo >> 16)


def _unpack_bf16_pairs(p):
    hi = lax.bitcast_convert_type(p & jnp.uint32(0xFFFF0000), f32)
    lo = lax.bitcast_convert_type(p << 16, f32)
    return jnp.concatenate([hi, lo], axis=1).astype(bf16)


def _rows(ref, chunk, nchunks):
    return ref.at[pl.ds(pl.multiple_of(chunk * MOE_CH, MOE_CH), nchunks * MOE_CH)]


def _copy_run(src, src_chunk, dst, dst_chunk, n, sem, priority):
    nbig = lax.shift_right_logical(n, MOE_BIG.bit_length() - 1)

    def big(c, carry):
        pltpu.make_async_copy(_rows(src, src_chunk + c * MOE_BIG, MOE_BIG),
                              _rows(dst, dst_chunk + c * MOE_BIG, MOE_BIG), sem).start(priority=priority)
        return carry
    lax.fori_loop(0, nbig, big, 0)

    def small(c, carry):
        pltpu.make_async_copy(_rows(src, src_chunk + c, 1), _rows(dst, dst_chunk + c, 1),
                              sem).start(priority=priority)
        return carry
    lax.fori_loop(nbig * MOE_BIG, n, small, 0)


def _wait_chunks(hbm, sem, count, max_bits):
    for b in reversed(range(max_bits)):
        @pl.when(jnp.bitwise_and(count, 1 << b) != 0)
        def _():
            span = _rows(hbm, 0, 1 << b)
            pltpu.make_async_copy(span, span, sem).wait()


def _rel_bias_matrix(table, nq, nk, d0):
    p = nq + nk - 1
    j = jnp.arange(p)
    u = table[:, jnp.clip(d0 + nq - 1 - j, -REL_CLIP, REL_CLIP) + REL_CLIP]
    u = jnp.roll(u, -(nq - 1), axis=1)
    flat = jnp.tile(u, (1, nq))[:, :nq * (p - 1)]
    return flat.reshape(table.shape[0], nq, p - 1)[:, :, :nk].astype(f32)


def _mm_kernel(x_ref, w_ref, o_ref):
    o_ref[...] = jnp.dot(x_ref[...].astype(bf16), w_ref[...], preferred_element_type=f32)


def _matmul(x, w, tm, tn):
    m, k = x.shape
    n = w.shape[1]
    return pl.pallas_call(
        _mm_kernel,
        grid=(m // tm, n // tn),
        in_specs=[pl.BlockSpec((tm, k), lambda i, j: (i, 0)),
                  pl.BlockSpec((k, tn), lambda i, j: (0, j))],
        out_specs=pl.BlockSpec((tm, tn), lambda i, j: (i, j)),
        out_shape=jax.ShapeDtypeStruct((m, n), f32),
        compiler_params=_cparams(("parallel", "parallel")),
        name="dense_proj",
    )(x, w)


def _route_top_k(x, w_t, b_col):
    logits = lax.dot_general(w_t.astype(bf16), x.astype(bf16), (((1,), (1,)), ((), ())),
                             preferred_element_type=f32) + b_col
    tm = logits.shape[1]
    row = lax.broadcasted_iota(i32, (N_EXPERTS, tm), 0)
    member = jnp.zeros((N_EXPERTS, tm), f32)
    ids, ps = [], []
    den = jnp.zeros((1, tm), f32)
    for k in range(TOP_K):
        v = jnp.max(logits, axis=0, keepdims=True)
        idx = jnp.min(jnp.where(logits == v, row, N_EXPERTS), axis=0, keepdims=True)
        chosen = row == idx
        logits = jnp.where(chosen, -jnp.inf, logits)
        member = member + chosen.astype(f32)
        p = jnp.exp(v - ps[0][1]) if k else jnp.ones((1, tm), f32)
        ps.append((p, v))
        den = den + p
        ids.append(idx.astype(f32))
    out_row = lax.broadcasted_iota(i32, (LANES, tm), 0)
    e_rows = jnp.zeros((LANES, tm), f32)
    g_rows = jnp.zeros((LANES, tm), f32)
    for k in range(TOP_K):
        e_rows = jnp.where(out_row == k, ids[k], e_rows)
        g_rows = jnp.where(out_row == k, ps[k][0] / den, g_rows)
    return e_rows.T.astype(i32), g_rows.T, member


def _proj_ln_kernel(*refs, n_in):
    a_refs = refs[:n_in]
    w_refs = refs[n_in:2 * n_in]
    x_ref, g_ref, b_ref, wr_ref, br_ref, o_ref, e_ref, gate_ref, cnt_ref = refs[2 * n_in:]
    acc = ALPHA * x_ref[...]
    for a_ref, w_ref in zip(a_refs, w_refs):
        acc = acc + jnp.dot(a_ref[...].astype(bf16), w_ref[...], preferred_element_type=f32)
    y = _layernorm(acc, g_ref[...], b_ref[...])
    o_ref[...] = y
    e_out, gates, member = _route_top_k(y, wr_ref[0], br_ref[0])
    e_ref[...] = e_out
    gate_ref[...] = gates
    nsub = cnt_ref.shape[0]
    td = e_out.shape[0] // nsub
    ones = jnp.ones((8, td), bf16)
    for u in range(nsub):
        cnt = lax.dot_general(ones, member[:, u * td:(u + 1) * td].astype(bf16), (((1,), (1,)), ((), ())),
                              preferred_element_type=f32)
        cnt_ref[u] = jnp.concatenate([cnt, jnp.zeros((8, LANES - N_EXPERTS), f32)], axis=1)


def _proj_ln(acts, ws, x, g, b, w_router, b_router, l, tm, td):
    m, d = x.shape
    n_in = len(acts)
    depth = w_router.shape[0]
    nt = m // tm
    nsub = tm // td
    cur = lambda i: (i, 0)
    in_specs = ([pl.BlockSpec((tm, a.shape[1]), cur) for a in acts]
                + [pl.BlockSpec(w.shape, lambda i: (0, 0)) for w in ws]
                + [pl.BlockSpec((tm, d), cur),
                   pl.BlockSpec((1, d), lambda i: (0, 0)),
                   pl.BlockSpec((1, d), lambda i: (0, 0)),
                   pl.BlockSpec((1, N_EXPERTS, d), lambda i: (l, 0, 0)),
                   pl.BlockSpec((1, N_EXPERTS, 1), lambda i: (l, 0, 0))])
    return pl.pallas_call(
        functools.partial(_proj_ln_kernel, n_in=n_in),
        grid=(nt,),
        in_specs=in_specs,
        out_specs=[pl.BlockSpec((tm, d), cur),
                   pl.BlockSpec((tm, LANES), cur),
                   pl.BlockSpec((tm, LANES), cur),
                   pl.BlockSpec((nsub, 8, LANES), lambda i: (i, 0, 0))],
        out_shape=[jax.ShapeDtypeStruct((m, d), f32),
                   jax.ShapeDtypeStruct((m, LANES), i32),
                   jax.ShapeDtypeStruct((m, LANES), f32),
                   jax.ShapeDtypeStruct((m // td, 8, LANES), f32)],
        compiler_params=_cparams(("parallel",)),
        name="proj_ln",
    )(*acts, *ws, x, g.reshape(1, d), b.reshape(1, d), jnp.swapaxes(w_router, 1, 2),
      b_router.reshape(depth, N_EXPERTS, 1))


def _attn_prompt_kernel(q_ref, k0_ref, k1_ref, k2_ref, v0_ref, v1_ref, v2_ref, bias_ref, o_ref):
    i = pl.program_id(0)
    tq = q_ref.shape[0]
    q = q_ref[...]
    k = jnp.concatenate([k0_ref[...], k1_ref[...], k2_ref[...]], axis=0)
    v = jnp.concatenate([v0_ref[...], v1_ref[...], v2_ref[...]], axis=0)
    kcol = lax.broadcasted_iota(i32, (1, 3 * tq), 1)
    tile_ok = (kcol // tq + i) >= 2
    qb = (q * (DHA ** -0.5)).astype(bf16)
    kb = k.astype(bf16)
    vb = v.astype(bf16)
    outs = []
    for h in range(HA):
        sl = slice(h * DHA, (h + 1) * DHA)
        s = lax.dot_general(qb[:, sl], kb[:, sl], (((1,), (1,)), ((), ())), preferred_element_type=f32)
        s = jnp.where(tile_ok, s + bias_ref[h], NEG)
        p = jnp.exp(s - jnp.max(s, axis=-1, keepdims=True))
        o = jnp.dot(p.astype(bf16), vb[:, sl], preferred_element_type=f32)
        outs.append(o / jnp.sum(p, axis=-1, keepdims=True))
    o_ref[...] = jnp.concatenate(outs, axis=-1)


def _attn_prompt(z, rel_bias, tp):
    tq = ATT_TQ
    nq = tp // tq
    qq = jnp.arange(tq)[:, None]
    kk = jnp.arange(3 * tq)[None, :]
    band = (kk // CHUNK >= qq // CHUNK) & (kk // CHUNK <= qq // CHUNK + A_PAST_CHUNKS)
    bias = jnp.where(band[None], _rel_bias_matrix(rel_bias, tq, 3 * tq, 2 * tq), NEG)

    def kv_spec(back, col):
        return pl.BlockSpec((tq, A_WIDTH), lambda i: (jnp.maximum(i - back, 0), col))

    return pl.pallas_call(
        _attn_prompt_kernel,
        grid=(nq,),
        in_specs=[pl.BlockSpec((tq, A_WIDTH), lambda i: (i, 0)),
                  kv_spec(2, 1), kv_spec(1, 1), kv_spec(0, 1),
                  kv_spec(2, 2), kv_spec(1, 2), kv_spec(0, 2),
                  pl.BlockSpec((HA, tq, 3 * tq), lambda i: (0, 0, 0))],
        out_specs=pl.BlockSpec((tq, A_WIDTH), lambda i: (i, 0)),
        out_shape=jax.ShapeDtypeStruct((tp, A_WIDTH), f32),
        compiler_params=_cparams(("parallel",)),
        name="attn_prompt",
    )(z, z, z, z, z, z, z, bias)


def _attn_sample_kernel(q_ref, kn_ref, vn_ref, kc_ref, vc_ref, bias_ref, o_ref):
    p_len = kc_ref.shape[1]
    q, kn, vn = q_ref[...], kn_ref[...], vn_ref[...]
    kc, vc = kc_ref[0], vc_ref[0]
    outs = []
    for h in range(HA):
        sl = slice(h * DHA, (h + 1) * DHA)
        qh = q[:, sl].astype(bf16)
        dims = (((1,), (1,)), ((), ()))
        sc = lax.dot_general(qh, kc[:, sl].astype(bf16), dims, preferred_element_type=f32)
        sn = lax.dot_general(qh, kn[:, sl].astype(bf16), dims, preferred_element_type=f32)
        b = bias_ref[h]
        sc = sc * (DHA ** -0.5) + b[:, :p_len]
        sn = sn * (DHA ** -0.5) + b[:, p_len:]
        m = jnp.maximum(jnp.max(sc, axis=-1, keepdims=True), jnp.max(sn, axis=-1, keepdims=True))
        pc, pn = jnp.exp(sc - m), jnp.exp(sn - m)
        den = jnp.sum(pc, axis=-1, keepdims=True) + jnp.sum(pn, axis=-1, keepdims=True)
        pc, pn = pc / den, pn / den
        outs.append(jnp.dot(pc.astype(bf16), vc[:, sl].astype(bf16), preferred_element_type=f32)
                    + jnp.dot(pn.astype(bf16), vn[:, sl].astype(bf16), preferred_element_type=f32))
    o_ref[...] = jnp.concatenate(outs, axis=-1)


def _attn_sample(z, k_cache, v_cache, rel_bias, row0, bs, ts):
    p_len = k_cache.shape[1]
    kc = k_cache.reshape(bs, p_len, A_WIDTH)
    vc = v_cache.reshape(bs, p_len, A_WIDTH)
    bias = _rel_bias_matrix(rel_bias, ts, p_len + ts, p_len)
    blk0 = row0 // ts

    def z_spec(col):
        return pl.BlockSpec((ts, A_WIDTH), lambda b: (blk0 + b, col))

    return pl.pallas_call(
        _attn_sample_kernel,
        grid=(bs,),
        in_specs=[z_spec(0), z_spec(1), z_spec(2),
                  pl.BlockSpec((1, p_len, A_WIDTH), lambda b: (b, 0, 0)),
                  pl.BlockSpec((1, p_len, A_WIDTH), lambda b: (b, 0, 0)),
                  pl.BlockSpec((HA, ts, p_len + ts), lambda b: (0, 0, 0))],
        out_specs=pl.BlockSpec((ts, A_WIDTH), lambda b: (b, 0)),
        out_shape=jax.ShapeDtypeStruct((bs * ts, A_WIDTH), f32),
        compiler_params=_cparams(("parallel",)),
        name="attn_sample",
    )(z, z, z, kc, vc, bias)


def _gelu_tanh(x):
    return 0.5 * x * (1.0 + jnp.tanh(math.sqrt(2.0 / math.pi) * (x + 0.044715 * (x * x * x))))


def _rglru_kernel(xb_ref, gb_ref, conv0_ref, h0_ref, cw_ref, cb_ref, wrg_ref, brg_ref,
                  wig_ref, big_ref, sp_ref, rec_ref, convn_ref, hl_ref, xp_ref, hc_ref):
    t = pl.program_id(1)
    tb = xb_ref.shape[0]
    pad = 8

    @pl.when(t == 0)
    def _():
        xp_ref[0:pad, :] = jnp.zeros((pad, B_WIDTH), f32)
        xp_ref[pad - (CONV_W - 1):pad, :] = conv0_ref[0]
        hc_ref[...] = h0_ref[0]

    xb = xb_ref[...]
    xp_ref[pad:pad + tb, :] = xb
    cw = cw_ref[...]
    u = cb_ref[...] + cw[CONV_W - 1:CONV_W, :] * xb
    for j in range(CONV_W - 1):
        sh = CONV_W - 1 - j
        u = u + cw[j:j + 1, :] * xp_ref[pad - sh:pad - sh + tb, :]
    convn_ref[0] = xp_ref[pad + tb - (CONV_W - 1):pad + tb, :]
    xp_ref[0:pad, :] = xp_ref[tb:tb + pad, :]

    ub = u.astype(bf16)
    r = jax.nn.sigmoid(jnp.dot(ub, wrg_ref[...], preferred_element_type=f32) + brg_ref[...])
    ig = jax.nn.sigmoid(jnp.dot(ub, wig_ref[...], preferred_element_type=f32) + big_ref[...])
    log_a = -LRU_C * r * sp_ref[...]
    a = jnp.exp(log_a)
    bt = jnp.sqrt(-jnp.tanh(log_a) * (a * a + 1.0)) * (ig * u)

    row = lax.broadcasted_iota(i32, (tb, 1), 0)
    s = 1
    while s < tb:
        keep = row >= s
        a_sh = pltpu.roll(a, s, axis=0)
        b_sh = pltpu.roll(bt, s, axis=0)
        bt = jnp.where(keep, a * b_sh + bt, bt)
        a = jnp.where(keep, a * a_sh, a)
        s *= 2
    h = a * hc_ref[...] + bt
    hc_ref[...] = h[tb - 1:tb, :]
    hl_ref[0] = h[tb - 1:tb, :]
    rec_ref[...] = h * _gelu_tanh(gb_ref[...])


def _rglru(z, conv0, h0, cw, cb, wrg, brg, wig, big, sp, row0, bsz, t_len, tb):
    nt = t_len // tb
    blk0 = row0 // tb

    def z_spec(col):
        return pl.BlockSpec((tb, B_WIDTH), lambda b, t: (blk0 + b * nt + t, col))

    def const(shape):
        return pl.BlockSpec(shape, lambda b, t: (0,) * len(shape))

    row = lambda v: v.reshape(1, B_WIDTH)
    return pl.pallas_call(
        _rglru_kernel,
        grid=(bsz, nt),
        in_specs=[z_spec(3), z_spec(4),
                  pl.BlockSpec((1, CONV_W - 1, B_WIDTH), lambda b, t: (b, 0, 0)),
                  pl.BlockSpec((1, 1, B_WIDTH), lambda b, t: (b, 0, 0)),
                  const((CONV_W, B_WIDTH)), const((1, B_WIDTH)),
                  const((B_WIDTH, B_WIDTH)), const((1, B_WIDTH)),
                  const((B_WIDTH, B_WIDTH)), const((1, B_WIDTH)), const((1, B_WIDTH))],
        out_specs=[pl.BlockSpec((tb, B_WIDTH), lambda b, t: (b * nt + t, 0)),
                   pl.BlockSpec((1, CONV_W - 1, B_WIDTH), lambda b, t: (b, 0, 0)),
                   pl.BlockSpec((1, 1, B_WIDTH), lambda b, t: (b, 0, 0))],
        out_shape=[jax.ShapeDtypeStruct((bsz * t_len, B_WIDTH), f32),
                   jax.ShapeDtypeStruct((bsz, CONV_W - 1, B_WIDTH), f32),
                   jax.ShapeDtypeStruct((bsz, 1, B_WIDTH), f32)],
        scratch_shapes=[pltpu.VMEM((tb + 8, B_WIDTH), f32), pltpu.VMEM((1, B_WIDTH), f32)],
        compiler_params=_cparams(("parallel", "arbitrary")),
        name="rglru",
    )(z, z, conv0, h0.reshape(bsz, 1, B_WIDTH), cw, row(cb), wrg, row(brg), wig, row(big), row(sp))


def _block_diag(w):
    hb, d, _ = w.shape
    eye = jnp.eye(hb, dtype=w.dtype)
    return (eye[:, None, :, None] * w[:, :, None, :]).reshape(hb * d, hb * d)


def _retention_kernel(x_ref, w_hbm, cos_ref, sin_ref, s0_hbm, gng_ref, gnb_ref,
                      y_ref, sout_hbm, w_ref, s_ref, z_ref, sem, *, chunk, nt, tb, zero_state, hoist):
    b = pl.program_id(0)
    t = pl.program_id(1)
    half = DKC // 2

    @pl.when((b == 0) & (t == 0))
    def _():
        cp = pltpu.make_async_copy(w_hbm, w_ref, sem.at[0])
        cp.start()
        cp.wait()
        if hoist:
            xa = x_ref[...].astype(bf16)
            for c in range(0, w_ref.shape[1], C_QK):
                z_ref[:, c:c + C_QK] = jnp.dot(xa, w_ref[:, c:c + C_QK], preferred_element_type=f32)

    @pl.when(t == 0)
    def _():
        if zero_state:
            s_ref[...] = jnp.zeros(s_ref.shape, f32)
        else:
            cp = pltpu.make_async_copy(s0_hbm.at[b], s_ref, sem.at[1])
            cp.start()
            cp.wait()

    cos, sin = cos_ref[...], sin_ref[...]
    n = lax.broadcasted_iota(i32, (tb, 1), 0)
    m = lax.broadcasted_iota(i32, (1, tb), 1)
    dist = n - m
    same = (n // chunk) == (m // chunk)
    expo = jnp.where(same, jnp.abs(dist), dist).astype(f32)
    visible = same | (dist > 0)
    nf = n.astype(f32)

    if hoist:
        row0 = pl.multiple_of(b * tb, tb)

        def proj(col, width):
            return z_ref[pl.ds(row0, tb), col:col + width]
    else:
        xb = x_ref[...].astype(bf16)

        def proj(col, width):
            return jnp.dot(xb, w_ref[:, col:col + width], preferred_element_type=f32)

    def rot(v, scale):
        x1, x2 = v[:, :half], v[:, half:]
        return jnp.concatenate([x1 * cos - x2 * sin, x2 * cos + x1 * sin], axis=-1) * scale

    for h in range(HC):
        lg = math.log(1.0 - 2.0 ** (-5.0 - h))
        qr = rot(proj(h * DKC, DKC), 1.0)
        kr = rot(proj(C_QK + h * DKC, DKC), DKC ** -0.5)
        vb = proj(2 * C_QK + h * DVC, DVC).astype(bf16)
        gate = proj(2 * C_QK + C_V + h * DVC, DVC)
        qb = qr.astype(bf16)
        dmat = jnp.where(visible, jnp.exp(lg * expo), 0.0)
        s = lax.dot_general(qb, kr.astype(bf16), (((1,), (1,)), ((), ())),
                            preferred_element_type=f32) * dmat
        s_old = s_ref[h]
        o = jnp.dot(s.astype(bf16), vb, preferred_element_type=f32)
        o = o + jnp.dot(qb, s_old.astype(bf16), preferred_element_type=f32) * jnp.exp(lg * (nf + 1.0))
        kd = (kr * jnp.exp(lg * (tb - 1.0 - nf))).astype(bf16)
        s_ref[h] = math.exp(lg * tb) * s_old + lax.dot_general(
            kd, vb, (((0,), (0,)), ((), ())), preferred_element_type=f32)

        mu = jnp.mean(o, axis=-1, keepdims=True)
        dlt = o - mu
        var = jnp.mean(dlt * dlt, axis=-1, keepdims=True)
        vs = slice(h * DVC, (h + 1) * DVC)
        yn = dlt * lax.rsqrt(var + GN_EPS) * gng_ref[:, vs] + gnb_ref[:, vs]
        y_ref[:, vs] = (jax.nn.silu(gate) * yn).astype(y_ref.dtype)

    @pl.when(t == nt - 1)
    def _():
        cp = pltpu.make_async_copy(s_ref, sout_hbm.at[b], sem.at[1])
        cp.start()
        cp.wait()


def _retention(x, w_in, s0, pos0, gn_g, gn_b, bsz, t_len, tb, chunk):
    nt = t_len // tb
    d = x.shape[1]
    half = DKC // 2
    inv = 1.0 / (ROPE_BASE ** (jnp.arange(half, dtype=f32) / half))
    ang = (pos0 + jnp.arange(t_len)).astype(f32)[:, None] * inv[None, :]
    cos, sin = jnp.cos(ang), jnp.sin(ang)
    zero_state = s0 is None
    if zero_state:
        s0 = jnp.zeros((1, 8, LANES), f32)
    hoist = nt == 1 and bsz > 1
    x_spec = (pl.BlockSpec(x.shape, lambda b, t: (0, 0)) if hoist
              else pl.BlockSpec((tb, d), lambda b, t: (b * nt + t, 0)))
    z_shape = (bsz * tb, w_in.shape[1]) if hoist else (8, LANES)
    return pl.pallas_call(
        functools.partial(_retention_kernel, chunk=chunk, nt=nt, tb=tb, zero_state=zero_state, hoist=hoist),
        grid=(bsz, nt),
        in_specs=[x_spec,
                  pl.BlockSpec(memory_space=pl.ANY),
                  pl.BlockSpec((tb, half), lambda b, t: (t, 0)),
                  pl.BlockSpec((tb, half), lambda b, t: (t, 0)),
                  pl.BlockSpec(memory_space=pl.ANY),
                  pl.BlockSpec((1, C_V), lambda b, t: (0, 0)),
                  pl.BlockSpec((1, C_V), lambda b, t: (0, 0))],
        out_specs=[pl.BlockSpec((tb, C_V), lambda b, t: (b * nt + t, 0)),
                   pl.BlockSpec(memory_space=pl.ANY)],
        out_shape=[jax.ShapeDtypeStruct((bsz * t_len, C_V), bf16),
                   jax.ShapeDtypeStruct((bsz, HC, DKC, DVC), f32)],
        scratch_shapes=[pltpu.VMEM(w_in.shape, bf16), pltpu.VMEM((HC, DKC, DVC), f32),
                        pltpu.VMEM(z_shape, f32), pltpu.SemaphoreType.DMA((2,))],
        compiler_params=_cparams(("arbitrary", "arbitrary")),
        name="retention",
    )(x, w_in, cos, sin, s0, gn_g.reshape(1, C_V), gn_b.reshape(1, C_V))


def _dispatch_kernel(ech_ref, est_ref, tab_ref, tabv_ref, ep_ref, es_ref, xp_ref, xs_ref,
                     xg_hbm, lrow_ref, sbuf, zchunk, pending, sem, *, np_tiles, nt):
    i = pl.program_id(0)
    td = xp_ref.shape[0]
    ls = sbuf.shape[1]
    slot = i % 2
    nchunks = xg_hbm.shape[0] // MOE_CH
    blk_chunks = MOE_BM // MOE_CH

    def drain(s):
        _wait_chunks(xg_hbm, sem.at[s], pending[s], (ls // MOE_CH).bit_length())

    @pl.when(i == 0)
    def _():
        pending[0] = 0
        pending[1] = 0
        zchunk[...] = jnp.zeros(zchunk.shape, zchunk.dtype)

        def zero_chunks(lo, hi):
            def put(c, carry):
                pltpu.make_async_copy(zchunk, _rows(xg_hbm, c, 1), sem.at[2]).start()
                return carry
            lax.fori_loop(lo, hi, put, 0)
            _wait_chunks(xg_hbm, sem.at[2], jnp.maximum(hi - lo, 0), nchunks.bit_length())

        def per_expert(e, carry):
            used = est_ref[e] + ech_ref[e]
            zero_chunks(used, used + jnp.bitwise_and(-ech_ref[e], blk_chunks - 1))
            return carry
        lax.fori_loop(0, N_EXPERTS, per_expert, 0)
        last = N_EXPERTS - 1
        end = est_ref[last] + ech_ref[last] + jnp.bitwise_and(-ech_ref[last], blk_chunks - 1)
        zero_chunks(end, nchunks)

    def sort_and_copy(e_ref, x_ref):
        e = e_ref[...]
        lane = lax.broadcasted_iota(i32, (td, LANES), 1)
        member = jnp.zeros((td, LANES), f32)
        for k in range(TOP_K):
            member = member + (lane == e[:, k:k + 1]).astype(f32)
        r = lax.broadcasted_iota(i32, (td, td), 0)
        c = lax.broadcasted_iota(i32, (td, td), 1)
        before = jnp.dot((c < r).astype(bf16), member.astype(bf16), preferred_element_type=f32)
        in_tile = before + tabv_ref[0, 0:1, :]
        col = lax.broadcasted_iota(i32, (td, ls), 1)
        place = None
        lrow = jnp.zeros((td, LANES), i32)
        for k in range(TOP_K):
            sel = lane == e[:, k:k + 1]
            row_k = jnp.sum(jnp.where(sel, in_tile, 0.0), axis=-1, keepdims=True).astype(i32)
            place = (col == row_k) if place is None else place | (col == row_k)
            lrow = jnp.where(lane == k, row_k, lrow)
        lrow_ref[...] = lrow
        place = jnp.where(place, 1.0, 0.0).astype(bf16)
        srt = lax.dot_general(place, x_ref[...].astype(bf16), (((0,), (0,)), ((), ())),
                              preferred_element_type=f32)
        drain(slot)
        sbuf[slot] = _pack_bf16_pairs(srt)

        total = 0
        for ei in range(N_EXPERTS):
            dst0 = tab_ref[0, 0, ei]
            n = tab_ref[0, 0, N_EXPERTS + ei]
            src0 = tab_ref[0, 0, 2 * N_EXPERTS + ei]
            _copy_run(sbuf.at[slot], src0, xg_hbm, dst0, n, sem.at[slot], ei % 2)
            total = total + n
        pending[slot] = total

    @pl.when(i < np_tiles)
    def _():
        sort_and_copy(ep_ref, xp_ref)

    @pl.when(i >= np_tiles)
    def _():
        sort_and_copy(es_ref, xs_ref)

    @pl.when(i == nt - 1)
    def _():
        drain(0)
        drain(1)


def _dispatch(xs, top_es, tab, tabv, ech, est, nblk, td):
    x_p, x_s = xs
    d = x_p.shape[1]
    np_tiles, ns_tiles = x_p.shape[0] // td, x_s.shape[0] // td
    nt = np_tiles + ns_tiles
    ls = td * TOP_K + N_EXPERTS * MOE_CH
    pspec = lambda w: pl.BlockSpec((td, w), lambda i, *_: (jnp.minimum(i, np_tiles - 1), 0))
    sspec = lambda w: pl.BlockSpec((td, w), lambda i, *_: (jnp.maximum(i - np_tiles, 0), 0))
    grid_spec = pltpu.PrefetchScalarGridSpec(
        num_scalar_prefetch=2,
        grid=(nt,),
        in_specs=[pl.BlockSpec((1, 1, LANES), lambda i, *_: (i, 0, 0), memory_space=pltpu.SMEM),
                  pl.BlockSpec((1, 8, LANES), lambda i, *_: (i, 0, 0)),
                  pspec(LANES), sspec(LANES), pspec(d), sspec(d)],
        out_specs=[pl.BlockSpec(memory_space=pl.ANY),
                   pl.BlockSpec((td, LANES), lambda i, *_: (i, 0))],
        scratch_shapes=[pltpu.VMEM((2, ls, d // 2), u32), pltpu.VMEM((MOE_CH, d // 2), u32),
                        pltpu.SMEM((2,), i32), pltpu.SemaphoreType.DMA((3,))],
    )
    xg, lrow = pl.pallas_call(
        functools.partial(_dispatch_kernel, np_tiles=np_tiles, nt=nt),
        grid_spec=grid_spec,
        out_shape=[jax.ShapeDtypeStruct((nblk * MOE_BM, d // 2), u32),
                   jax.ShapeDtypeStruct((nt * td, LANES), i32)],
        compiler_params=_cparams(("arbitrary",)),
        name="moe_dispatch",
    )(ech, est, tab, tabv, top_es[0], top_es[1], x_p, x_s)
    return xg, lrow


def _moe_kernel(blk_e_ref, nact_ref, first_ref, seg_ref, nxt_ref, x_ref, wup_hbm, bup_ref, wdn_hbm, bdn_ref,
                y_ref, wup_f, wdn_f, wup_bf, wdn_bf, sem, *, layer):
    b0 = pl.program_id(0) * MOE_PAIR
    nact = nact_ref[0]
    bm = MOE_BM

    def weight_copies(e, slot):
        return (pltpu.make_async_copy(wup_hbm.at[layer, e], wup_f.at[slot], sem.at[slot, 0]),
                pltpu.make_async_copy(wdn_hbm.at[layer, e], wdn_f.at[slot], sem.at[slot, 1]))

    @pl.when(b0 < nact)
    def _():
        for sub in range(MOE_PAIR):
            b = b0 + sub

            @pl.when(first_ref[b] == 1)
            def _():
                slot = seg_ref[b] % 2

                @pl.when(b == 0)
                def _():
                    for c in weight_copies(blk_e_ref[0], 0):
                        c.start()

                for c in weight_copies(blk_e_ref[b], slot):
                    c.wait()

                @pl.when(nxt_ref[b] >= 0)
                def _():
                    for c in weight_copies(nxt_ref[b], 1 - slot):
                        c.start()

                wup_bf[slot] = wup_f[slot].astype(bf16)
                wdn_bf[slot] = wdn_f[slot].astype(bf16)

        for sub in range(MOE_PAIR):
            b = b0 + sub
            slot = seg_ref[b] % 2
            e = layer * N_EXPERTS + blk_e_ref[b]
            rows = slice(sub * bm, (sub + 1) * bm)
            hdn = jnp.dot(_unpack_bf16_pairs(x_ref[rows, :]), wup_bf[slot], preferred_element_type=f32)
            hdn = hdn + bup_ref[pl.ds(e, 1), :]
            glu = jnp.minimum(hdn[:, :D_FF], SWIGLU_LIMIT)
            lin = jnp.clip(hdn[:, D_FF:], -SWIGLU_LIMIT, SWIGLU_LIMIT)
            act = glu * jax.nn.sigmoid(SWIGLU_ALPHA * glu) * (lin + 1.0)
            y_ref[rows, :] = (jnp.dot(act.astype(bf16), wdn_bf[slot], preferred_element_type=f32)
                              + bdn_ref[pl.ds(e, 1), :])

    @pl.when(b0 >= nact)
    def _():
        y_ref[...] = jnp.zeros(y_ref.shape, f32)


def _moe_experts(xg, blk_e, nact, first, seg, nxt, l, w_up, b_up, w_down, b_down):
    r, half = xg.shape
    d = 2 * half
    rows = MOE_BM * MOE_PAIR
    depth = w_up.shape[0]
    whole = lambda shape: pl.BlockSpec(shape, lambda s, *_: (0,) * len(shape))
    grid_spec = pltpu.PrefetchScalarGridSpec(
        num_scalar_prefetch=5,
        grid=(r // rows,),
        in_specs=[
            pl.BlockSpec((rows, half), lambda s, be, na, *_: (jnp.minimum(s, (na[0] - 1) // MOE_PAIR), 0)),
            pl.BlockSpec(memory_space=pl.ANY),
            whole((depth * N_EXPERTS, 2 * D_FF)),
            pl.BlockSpec(memory_space=pl.ANY),
            whole((depth * N_EXPERTS, d)),
        ],
        out_specs=pl.BlockSpec((rows, d), lambda s, *_: (s, 0)),
        scratch_shapes=[pltpu.VMEM((2, d, 2 * D_FF), f32), pltpu.VMEM((2, D_FF, d), f32),
                        pltpu.VMEM((2, d, 2 * D_FF), bf16), pltpu.VMEM((2, D_FF, d), bf16),
                        pltpu.SemaphoreType.DMA((2, 2))],
    )
    return pl.pallas_call(
        functools.partial(_moe_kernel, layer=l),
        grid_spec=grid_spec,
        out_shape=jax.ShapeDtypeStruct((r, d), f32),
        compiler_params=_cparams(("arbitrary",)),
        name="moe_experts",
    )(blk_e, nact, first, seg, nxt, xg, w_up, b_up.reshape(depth * N_EXPERTS, 2 * D_FF),
      w_down, b_down.reshape(depth * N_EXPERTS, d))


def _split_bf16(v):
    hi = v.astype(bf16)
    return hi, (v - hi.astype(f32)).astype(bf16)


def _combine_ln_kernel(tab_cur_ref, tab_nxt_ref, y_hbm, lrow_ref, gate_ref, x_ref, g_ref, b_ref, o_ref,
                       ybuf, pending, sem, *, nt):
    i = pl.program_id(0)
    td = x_ref.shape[0]
    ls = ybuf.shape[1]
    slot = i % 2

    def gather(tab_ref, s):
        total = 0
        for ei in range(N_EXPERTS):
            src0 = tab_ref[0, 0, ei]
            n = tab_ref[0, 0, N_EXPERTS + ei]
            dst0 = tab_ref[0, 0, 2 * N_EXPERTS + ei]
            _copy_run(y_hbm, src0, ybuf.at[s], dst0, n, sem.at[s], ei % 2)
            total = total + n
        pending[s] = total

    @pl.when(i == 0)
    def _():
        ybuf[...] = jnp.zeros(ybuf.shape, f32)
        gather(tab_cur_ref, 0)

    @pl.when(i + 1 < nt)
    def _():
        gather(tab_nxt_ref, (i + 1) % 2)

    _wait_chunks(y_hbm, sem.at[slot], pending[slot], (ls // MOE_CH).bit_length())

    lrow = lrow_ref[...]
    gates = gate_ref[...]
    col = lax.broadcasted_iota(i32, (td, ls), 1)
    weight = jnp.zeros((td, ls), f32)
    for k in range(TOP_K):
        weight = weight + jnp.where(col == lrow[:, k:k + 1], gates[:, k:k + 1], 0.0)
    wh, wl = _split_bf16(weight)
    yb = ybuf[slot].astype(bf16)
    moe = jnp.dot(wh, yb, preferred_element_type=f32) + jnp.dot(wl, yb, preferred_element_type=f32)
    o_ref[...] = _layernorm(ALPHA * x_ref[...] + moe, g_ref[...], b_ref[...])


def _combine_ln(y, tab, lrow, blk0, gates, x, g, b, td):
    rows, d = x.shape
    nt = rows // td
    ls = td * TOP_K + N_EXPERTS * MOE_CH
    grid_spec = pltpu.PrefetchScalarGridSpec(
        num_scalar_prefetch=0,
        grid=(nt,),
        in_specs=[
            pl.BlockSpec((1, 1, LANES), lambda i: (blk0 + i, 0, 0), memory_space=pltpu.SMEM),
            pl.BlockSpec((1, 1, LANES), lambda i: (blk0 + jnp.minimum(i + 1, nt - 1), 0, 0),
                         memory_space=pltpu.SMEM),
            pl.BlockSpec(memory_space=pl.ANY),
            pl.BlockSpec((td, LANES), lambda i: (blk0 + i, 0)),
            pl.BlockSpec((td, LANES), lambda i: (i, 0)),
            pl.BlockSpec((td, d), lambda i: (i, 0)),
            pl.BlockSpec((1, d), lambda i: (0, 0)),
            pl.BlockSpec((1, d), lambda i: (0, 0)),
        ],
        out_specs=pl.BlockSpec((td, d), lambda i: (i, 0)),
        scratch_shapes=[pltpu.VMEM((2, ls, d), f32), pltpu.SMEM((2,), i32), pltpu.SemaphoreType.DMA((2,))],
    )
    return pl.pallas_call(
        functools.partial(_combine_ln_kernel, nt=nt),
        grid_spec=grid_spec,
        out_shape=jax.ShapeDtypeStruct((rows, d), f32),
        compiler_params=_cparams(("arbitrary",)),
        name="combine_ln",
    )(tab, tab, y, lrow, gates, x, g.reshape(1, d), b.reshape(1, d))


def _moe_ln(xs, top_es, gates, cnts, td, l, w_up, b_up, w_down, b_down, g, b):
    x_p, x_s = xs
    n_p, n_s = x_p.shape[0], x_s.shape[0]
    n = n_p + n_s
    bm = MOE_BM
    nt = n // td
    nblk = -(-(n * TOP_K + nt * N_EXPERTS * (MOE_CH - 1)) // bm) + N_EXPERTS
    nblk = -(-nblk // MOE_PAIR) * MOE_PAIR
    cnt = jnp.concatenate([c[:, 0, :N_EXPERTS] for c in cnts], axis=0).astype(i32)
    run = (cnt + MOE_CH - 1) // MOE_CH
    in_tile = jnp.cumsum(run, axis=1) - run
    ech = jnp.sum(run, axis=0)
    blk_chunks = bm // MOE_CH
    padded_ch = (ech + blk_chunks - 1) // blk_chunks * blk_chunks
    est = jnp.cumsum(padded_ch) - padded_ch
    in_all = est[None, :] + jnp.cumsum(run, axis=0) - run
    tab = jnp.concatenate([in_all, run, in_tile, jnp.zeros_like(run)], axis=1).reshape(nt, 1, LANES)
    tabv = jnp.zeros((nt, 8, LANES), f32).at[:, 0, :N_EXPERTS].set((in_tile * MOE_CH).astype(f32))
    xg, lrow = _dispatch(xs, top_es, tab, tabv, ech, est, nblk, td)
    padded = padded_ch * MOE_CH
    ends = jnp.cumsum(padded)
    blk = jnp.arange(nblk, dtype=i32)
    blk_e = jnp.minimum(jnp.sum((ends[None, :] <= (blk * bm)[:, None]).astype(i32), axis=1), N_EXPERTS - 1)
    nact = (ends[-1] // bm).astype(i32).reshape(1)
    first = ((blk < nact[0]) & ((blk == 0) | (blk_e != jnp.roll(blk_e, 1)))).astype(i32)
    seg = jnp.cumsum(first) - 1
    eid = jnp.arange(N_EXPERTS, dtype=i32)
    later_nonempty = (padded[None, :] > 0) & (eid[None, :] > eid[:, None])
    nxt_of_e = jnp.min(jnp.where(later_nonempty, eid[None, :], N_EXPERTS), axis=1)
    nxt = jnp.where(nxt_of_e < N_EXPERTS, nxt_of_e, -1)[blk_e].astype(i32)
    y = _moe_experts(xg, blk_e, nact, first, seg, nxt, l, w_up, b_up, w_down, b_down)
    return (_combine_ln(y, tab, lrow, 0, gates[0], x_p, g, b, td),
            _combine_ln(y, tab, lrow, n_p // td, gates[1], x_s, g, b, td))


def kernel(x_prompt, x_sample, cache_a_k, cache_a_v, state_b_conv, state_b_h, state_c_s, ab_w_in, ab_rel_bias, ab_conv_w, ab_conv_b, ab_w_rg, ab_b_rg, ab_w_ig, ab_b_ig, ab_lambda, ab_w_out, c_w_in, c_gn_g, c_gn_b, c_w_out, ln1_g, ln1_b, ln2_g, ln2_b, moe_w_router, moe_b_router, moe_w_up, moe_b_up, moe_w_down, moe_b_down):
    bp, tp, d = x_prompt.shape
    bs, ts, _ = x_sample.shape
    assert bp == 1 and tp % RET_TB == 0 and ts <= CHUNK
    n_p, n_s = bp * tp, bs * ts
    xs = (x_prompt.reshape(n_p, d), x_sample.reshape(n_s, d))
    tms = tuple(_pick(m, (512, 256, 128, 64)) for m in (n_p, n_s))
    td = _pick(math.gcd(n_p, n_s), (256, 128, 64, 32))
    moe = (moe_w_up, moe_b_up, moe_w_down, moe_b_down)
    router = (moe_w_router, moe_b_router)

    def per_group(fn, *groups):
        return tuple(zip(*[fn(*args) for args in zip(*groups)]))

    j = 0
    w_in = ab_w_in[j].astype(bf16)
    z_p, z_s = (_matmul(x, w_in, tm, w_in.shape[1]) for x, tm in zip(xs, tms))
    att = (_attn_prompt(z_p, ab_rel_bias[j], tp),
           _attn_sample(z_s, cache_a_k[j], cache_a_v[j], ab_rel_bias[j], 0, bs, ts))
    lru_w = (ab_conv_w[j], ab_conv_b[j], _block_diag(ab_w_rg[j]).astype(bf16), ab_b_rg[j],
             _block_diag(ab_w_ig[j]).astype(bf16), ab_b_ig[j], jax.nn.softplus(-ab_lambda[j]))
    rec_p, bc_p, bh_p = _rglru(z_p, jnp.zeros((bp, CONV_W - 1, B_WIDTH), f32), jnp.zeros((bp, B_WIDTH), f32),
                               *lru_w, 0, bp, tp, _pick(tp, (LRU_TB, 128, 64)))
    rec_s, bc_s, bh_s = _rglru(z_s, state_b_conv[j], state_b_h[j], *lru_w, 0, bs, ts, ts)
    w_out = ab_w_out[j].astype(bf16)
    w_out = [w_out[:A_WIDTH], w_out[A_WIDTH:]]
    xs, top_es, gates, cnts = per_group(
        lambda a, r, x, tm: _proj_ln([a, r], w_out, x, ln1_g[0], ln1_b[0], *router, 0, tm, td),
        att, (rec_p, rec_s), xs, tms)
    xs = _moe_ln(xs, top_es, gates, cnts, td, 0, *moe, ln2_g[0], ln2_b[0])

    keep = min(A_PAST_CHUNKS * CHUNK, tp)
    ak_p = z_p[n_p - keep:, A_WIDTH:2 * A_WIDTH].reshape(1, bp, keep, HA, DHA)
    av_p = z_p[n_p - keep:, 2 * A_WIDTH:3 * A_WIDTH].reshape(1, bp, keep, HA, DHA)
    ak_s = z_s[:, A_WIDTH:2 * A_WIDTH].reshape(1, bs, ts, HA, DHA)
    av_s = z_s[:, 2 * A_WIDTH:3 * A_WIDTH].reshape(1, bs, ts, HA, DHA)

    w_in = c_w_in[j].astype(bf16)
    y_p, cs_p = _retention(xs[0], w_in, None, 0, c_gn_g[j], c_gn_b[j], bp, tp, RET_TB, CHUNK)
    y_s, cs_s = _retention(xs[1], w_in, state_c_s[j], PAST_LEN, c_gn_g[j], c_gn_b[j], bs, ts, ts, ts)
    w_out = [c_w_out[j].astype(bf16)]
    xs, top_es, gates, cnts = per_group(
        lambda y, x, tm: _proj_ln([y], w_out, x, ln1_g[1], ln1_b[1], *router, 1, tm, td),
        (y_p, y_s), xs, tms)
    out_p, out_s = _moe_ln(xs, top_es, gates, cnts, td, 1, *moe, ln2_g[1], ln2_b[1])

    return (out_p.reshape(bp, tp, d), out_s.reshape(bs, ts, d),
            ak_p, av_p, bc_p[None], bh_p.reshape(1, bp, B_WIDTH), cs_p[None],
            ak_s, av_s, bc_s[None], bh_s.reshape(1, bs, B_WIDTH), cs_s[None])
```

```python
import functools
import math

import jax
import jax.numpy as jnp
from jax import lax
from jax.experimental import pallas as pl
from jax.experimental.pallas import tpu as pltpu

f32 = jnp.float32
bf16 = jnp.bfloat16
i32 = jnp.int32
u32 = jnp.uint32

DEPTH = 2
PAST_LEN = 1024
CHUNK = 64
A_PAST_CHUNKS = 8
REL_CLIP = 128
HA, DHA = 8, 64
A_WIDTH = HA * DHA
B_WIDTH = 512
HB = 8
CONV_W = 4
LRU_C = 8.0
HC, DKC, DVC = 4, 256, 512
C_QK, C_V = HC * DKC, HC * DVC
ROPE_BASE = 10000.0
GN_EPS = 1e-5
N_EXPERTS = 32
TOP_K = 4
D_FF = 1024
SWIGLU_LIMIT = 7.0
SWIGLU_ALPHA = 1.702
ALPHA = (2.0 * DEPTH) ** 0.25
LN_EPS = 1e-5
NEG = -1e30

LANES = 128
VMEM_LIMIT = 56 * 1024 * 1024
ATT_TQ = 256
MOE_BM = 256
MOE_PAIR = 2
MOE_CH = 8
MOE_BIG = 4
RET_TB = 256
LRU_TB = 256


def _pick(n, cands):
    for c in cands:
        if n % c == 0:
            return c
    raise ValueError(f"no tile for {n} in {cands}")


def _cparams(sem):
    return pltpu.CompilerParams(dimension_semantics=sem, vmem_limit_bytes=VMEM_LIMIT)


def _layernorm(v, g, b):
    mu = jnp.mean(v, axis=-1, keepdims=True)
    d = v - mu
    var = jnp.mean(d * d, axis=-1, keepdims=True)
    return d * lax.rsqrt(var + LN_EPS) * g + b


def _pack_bf16_pairs(v):
    w = v.shape[1] // 2
    hi = lax.bitcast_convert_type(v[:, :w], u32)
    lo = lax.bitcast_convert_type(v[:, w:], u32)
    return (hi & jnp.uint32(0xFFFF0000)) | (lo >> 16)


def _unpack_bf16_pairs(p):
    hi = lax.bitcast_convert_type(p & jnp.uint32(0xFFFF0000), f32)
    lo = lax.bitcast_convert_type(p << 16, f32)
    return jnp.concatenate([hi, lo], axis=1).astype(bf16)


def _rows(ref, chunk, nchunks):
    return ref.at[pl.ds(pl.multiple_of(chunk * MOE_CH, MOE_CH), nchunks * MOE_CH)]


def _copy_run(src, src_chunk, dst, dst_chunk, n, sem, priority):
    nbig = lax.shift_right_logical(n, MOE_BIG.bit_length() - 1)

    def big(c, carry):
        pltpu.make_async_copy(_rows(src, src_chunk + c * MOE_BIG, MOE_BIG),
                              _rows(dst, dst_chunk + c * MOE_BIG, MOE_BIG), sem).start(priority=priority)
        return carry
    lax.fori_loop(0, nbig, big, 0)

    def small(c, carry):
        pltpu.make_async_copy(_rows(src, src_chunk + c, 1), _rows(dst, dst_chunk + c, 1),
                              sem).start(priority=priority)
        return carry
    lax.fori_loop(nbig * MOE_BIG, n, small, 0)


def _wait_chunks(hbm, sem, count, max_bits):
    for b in reversed(range(max_bits)):
        @pl.when(jnp.bitwise_and(count, 1 << b) != 0)
        def _():
            span = _rows(hbm, 0, 1 << b)
            pltpu.make_async_copy(span, span, sem).wait()


def _rel_bias_matrix(table, nq, nk, d0):
    p = nq + nk - 1
    j = jnp.arange(p)
    u = table[:, jnp.clip(d0 + nq - 1 - j, -REL_CLIP, REL_CLIP) + REL_CLIP]
    u = jnp.roll(u, -(nq - 1), axis=1)
    flat = jnp.tile(u, (1, nq))[:, :nq * (p - 1)]
    return flat.reshape(table.shape[0], nq, p - 1)[:, :, :nk].astype(f32)


def _mm_kernel(x_ref, w_ref, o_ref):
    o_ref[...] = jnp.dot(x_ref[...].astype(bf16), w_ref[...], preferred_element_type=f32)


def _matmul(x, w, tm, tn):
    m, k = x.shape
    n = w.shape[1]
    return pl.pallas_call(
        _mm_kernel,
        grid=(m // tm, n // tn),
        in_specs=[pl.BlockSpec((tm, k), lambda i, j: (i, 0)),
                  pl.BlockSpec((k, tn), lambda i, j: (0, j))],
        out_specs=pl.BlockSpec((tm, tn), lambda i, j: (i, j)),
        out_shape=jax.ShapeDtypeStruct((m, n), f32),
        compiler_params=_cparams(("parallel", "parallel")),
        name="dense_proj",
    )(x, w)


def _route_top_k(x, w_t, b_col):
    logits = lax.dot_general(w_t.astype(bf16), x.astype(bf16), (((1,), (1,)), ((), ())),
                             preferred_element_type=f32) + b_col
    tm = logits.shape[1]
    row = lax.broadcasted_iota(i32, (N_EXPERTS, tm), 0)
    member = jnp.zeros((N_EXPERTS, tm), f32)
    ids, ps = [], []
    den = jnp.zeros((1, tm), f32)
    for k in range(TOP_K):
        v = jnp.max(logits, axis=0, keepdims=True)
        idx = jnp.min(jnp.where(logits == v, row, N_EXPERTS), axis=0, keepdims=True)
        chosen = row == idx
        logits = jnp.where(chosen, -jnp.inf, logits)
        member = member + chosen.astype(f32)
        p = jnp.exp(v - ps[0][1]) if k else jnp.ones((1, tm), f32)
        ps.append((p, v))
        den = den + p
        ids.append(idx.astype(f32))
    out_row = lax.broadcasted_iota(i32, (LANES, tm), 0)
    e_rows = jnp.zeros((LANES, tm), f32)
    g_rows = jnp.zeros((LANES, tm), f32)
    for k in range(TOP_K):
        e_rows = jnp.where(out_row == k, ids[k], e_rows)
        g_rows = jnp.where(out_row == k, ps[k][0] / den, g_rows)
    return e_rows.T.astype(i32), g_rows.T, member


def _proj_ln_kernel(*refs, n_in):
    a_refs = refs[:n_in]
    w_refs = refs[n_in:2 * n_in]
    x_ref, g_ref, b_ref, wr_ref, br_ref, o_ref, e_ref, gate_ref, cnt_ref = refs[2 * n_in:]
    acc = ALPHA * x_ref[...]
    for a_ref, w_ref in zip(a_refs, w_refs):
        acc = acc + jnp.dot(a_ref[...].astype(bf16), w_ref[...], preferred_element_type=f32)
    y = _layernorm(acc, g_ref[...], b_ref[...])
    o_ref[...] = y
    e_out, gates, member = _route_top_k(y, wr_ref[0], br_ref[0])
    e_ref[...] = e_out
    gate_ref[...] = gates
    nsub = cnt_ref.shape[0]
    td = e_out.shape[0] // nsub
    ones = jnp.ones((8, td), bf16)
    for u in range(nsub):
        cnt = lax.dot_general(ones, member[:, u * td:(u + 1) * td].astype(bf16), (((1,), (1,)), ((), ())),
                              preferred_element_type=f32)
        cnt_ref[u] = jnp.concatenate([cnt, jnp.zeros((8, LANES - N_EXPERTS), f32)], axis=1)


def _proj_ln(acts, ws, x, g, b, w_router, b_router, l, tm, td):
    m, d = x.shape
    n_in = len(acts)
    depth = w_router.shape[0]
    nt = m // tm
    nsub = tm // td
    cur = lambda i: (i, 0)
    in_specs = ([pl.BlockSpec((tm, a.shape[1]), cur) for a in acts]
                + [pl.BlockSpec(w.shape, lambda i: (0, 0)) for w in ws]
                + [pl.BlockSpec((tm, d), cur),
                   pl.BlockSpec((1, d), lambda i: (0, 0)),
                   pl.BlockSpec((1, d), lambda i: (0, 0)),
                   pl.BlockSpec((1, N_EXPERTS, d), lambda i: (l, 0, 0)),
                   pl.BlockSpec((1, N_EXPERTS, 1), lambda i: (l, 0, 0))])
    return pl.pallas_call(
        functools.partial(_proj_ln_kernel, n_in=n_in),
        grid=(nt,),
        in_specs=in_specs,
        out_specs=[pl.BlockSpec((tm, d), cur),
                   pl.BlockSpec((tm, LANES), cur),
                   pl.BlockSpec((tm, LANES), cur),
                   pl.BlockSpec((nsub, 8, LANES), lambda i: (i, 0, 0))],
        out_shape=[jax.ShapeDtypeStruct((m, d), f32),
                   jax.ShapeDtypeStruct((m, LANES), i32),
                   jax.ShapeDtypeStruct((m, LANES), f32),
                   jax.ShapeDtypeStruct((m // td, 8, LANES), f32)],
        compiler_params=_cparams(("parallel",)),
        name="proj_ln",
    )(*acts, *ws, x, g.reshape(1, d), b.reshape(1, d), jnp.swapaxes(w_router, 1, 2),
      b_router.reshape(depth, N_EXPERTS, 1))


def _attn_prompt_kernel(q_ref, k0_ref, k1_ref, k2_ref, v0_ref, v1_ref, v2_ref, bias_ref, o_ref):
    i = pl.program_id(0)
    tq = q_ref.shape[0]
    q = q_ref[...]
    k = jnp.concatenate([k0_ref[...], k1_ref[...], k2_ref[...]], axis=0)
    v = jnp.concatenate([v0_ref[...], v1_ref[...], v2_ref[...]], axis=0)
    kcol = lax.broadcasted_iota(i32, (1, 3 * tq), 1)
    tile_ok = (kcol // tq + i) >= 2
    qb = (q * (DHA ** -0.5)).astype(bf16)
    kb = k.astype(bf16)
    vb = v.astype(bf16)
    outs = []
    for h in range(HA):
        sl = slice(h * DHA, (h + 1) * DHA)
        s = lax.dot_general(qb[:, sl], kb[:, sl], (((1,), (1,)), ((), ())), preferred_element_type=f32)
        s = jnp.where(tile_ok, s + bias_ref[h], NEG)
        p = jnp.exp(s - jnp.max(s, axis=-1, keepdims=True))
        o = jnp.dot(p.astype(bf16), vb[:, sl], preferred_element_type=f32)
        outs.append(o / jnp.sum(p, axis=-1, keepdims=True))
    o_ref[...] = jnp.concatenate(outs, axis=-1)


def _attn_prompt(z, rel_bias, tp):
    tq = ATT_TQ
    nq = tp // tq
    qq = jnp.arange(tq)[:, None]
    kk = jnp.arange(3 * tq)[None, :]
    band = (kk // CHUNK >= qq // CHUNK) & (kk // CHUNK <= qq // CHUNK + A_PAST_CHUNKS)
    bias = jnp.where(band[None], _rel_bias_matrix(rel_bias, tq, 3 * tq, 2 * tq), NEG)

    def kv_spec(back, col):
        return pl.BlockSpec((tq, A_WIDTH), lambda i: (jnp.maximum(i - back, 0), col))

    return pl.pallas_call(
        _attn_prompt_kernel,
        grid=(nq,),
        in_specs=[pl.BlockSpec((tq, A_WIDTH), lambda i: (i, 0)),
                  kv_spec(2, 1), kv_spec(1, 1), kv_spec(0, 1),
                  kv_spec(2, 2), kv_spec(1, 2), kv_spec(0, 2),
                  pl.BlockSpec((HA, tq, 3 * tq), lambda i: (0, 0, 0))],
        out_specs=pl.BlockSpec((tq, A_WIDTH), lambda i: (i, 0)),
        out_shape=jax.ShapeDtypeStruct((tp, A_WIDTH), f32),
        compiler_params=_cparams(("parallel",)),
        name="attn_prompt",
    )(z, z, z, z, z, z, z, bias)


def _attn_sample_kernel(q_ref, kn_ref, vn_ref, kc_ref, vc_ref, bias_ref, o_ref):
    p_len = kc_ref.shape[1]
    q, kn, vn = q_ref[...], kn_ref[...], vn_ref[...]
    kc, vc = kc_ref[0], vc_ref[0]
    outs = []
    for h in range(HA):
        sl = slice(h * DHA, (h + 1) * DHA)
        qh = q[:, sl].astype(bf16)
        dims = (((1,), (1,)), ((), ()))
        sc = lax.dot_general(qh, kc[:, sl].astype(bf16), dims, preferred_element_type=f32)
        sn = lax.dot_general(qh, kn[:, sl].astype(bf16), dims, preferred_element_type=f32)
        b = bias_ref[h]
        sc = sc * (DHA ** -0.5) + b[:, :p_len]
        sn = sn * (DHA ** -0.5) + b[:, p_len:]
        m = jnp.maximum(jnp.max(sc, axis=-1, keepdims=True), jnp.max(sn, axis=-1, keepdims=True))
        pc, pn = jnp.exp(sc - m), jnp.exp(sn - m)
        den = jnp.sum(pc, axis=-1, keepdims=True) + jnp.sum(pn, axis=-1, keepdims=True)
        pc, pn = pc / den, pn / den
        outs.append(jnp.dot(pc.astype(bf16), vc[:, sl].astype(bf16), preferred_element_type=f32)
                    + jnp.dot(pn.astype(bf16), vn[:, sl].astype(bf16), preferred_element_type=f32))
    o_ref[...] = jnp.concatenate(outs, axis=-1)


def _attn_sample(z, k_cache, v_cache, rel_bias, row0, bs, ts):
    p_len = k_cache.shape[1]
    kc = k_cache.reshape(bs, p_len, A_WIDTH)
    vc = v_cache.reshape(bs, p_len, A_WIDTH)
    bias = _rel_bias_matrix(rel_bias, ts, p_len + ts, p_len)
    blk0 = row0 // ts

    def z_spec(col):
        return pl.BlockSpec((ts, A_WIDTH), lambda b: (blk0 + b, col))

    return pl.pallas_call(
        _attn_sample_kernel,
        grid=(bs,),
        in_specs=[z_spec(0), z_spec(1), z_spec(2),
                  pl.BlockSpec((1, p_len, A_WIDTH), lambda b: (b, 0, 0)),
                  pl.BlockSpec((1, p_len, A_WIDTH), lambda b: (b, 0, 0)),
                  pl.BlockSpec((HA, ts, p_len + ts), lambda b: (0, 0, 0))],
        out_specs=pl.BlockSpec((ts, A_WIDTH), lambda b: (b, 0)),
        out_shape=jax.ShapeDtypeStruct((bs * ts, A_WIDTH), f32),
        compiler_params=_cparams(("parallel",)),
        name="attn_sample",
    )(z, z, z, kc, vc, bias)


def _gelu_tanh(x):
    return 0.5 * x * (1.0 + jnp.tanh(math.sqrt(2.0 / math.pi) * (x + 0.044715 * (x * x * x))))


def _rglru_kernel(xb_ref, gb_ref, conv0_ref, h0_ref, cw_ref, cb_ref, wrg_ref, brg_ref,
                  wig_ref, big_ref, sp_ref, rec_ref, convn_ref, hl_ref, xp_ref, hc_ref):
    t = pl.program_id(1)
    tb = xb_ref.shape[0]
    pad = 8

    @pl.when(t == 0)
    def _():
        xp_ref[0:pad, :] = jnp.zeros((pad, B_WIDTH), f32)
        xp_ref[pad - (CONV_W - 1):pad, :] = conv0_ref[0]
        hc_ref[...] = h0_ref[0]

    xb = xb_ref[...]
    xp_ref[pad:pad + tb, :] = xb
    cw = cw_ref[...]
    u = cb_ref[...] + cw[CONV_W - 1:CONV_W, :] * xb
    for j in range(CONV_W - 1):
        sh = CONV_W - 1 - j
        u = u + cw[j:j + 1, :] * xp_ref[pad - sh:pad - sh + tb, :]
    convn_ref[0] = xp_ref[pad + tb - (CONV_W - 1):pad + tb, :]
    xp_ref[0:pad, :] = xp_ref[tb:tb + pad, :]

    ub = u.astype(bf16)
    r = jax.nn.sigmoid(jnp.dot(ub, wrg_ref[...], preferred_element_type=f32) + brg_ref[...])
    ig = jax.nn.sigmoid(jnp.dot(ub, wig_ref[...], preferred_element_type=f32) + big_ref[...])
    log_a = -LRU_C * r * sp_ref[...]
    a = jnp.exp(log_a)
    bt = jnp.sqrt(-jnp.tanh(log_a) * (a * a + 1.0)) * (ig * u)

    row = lax.broadcasted_iota(i32, (tb, 1), 0)
    s = 1
    while s < tb:
        keep = row >= s
        a_sh = pltpu.roll(a, s, axis=0)
        b_sh = pltpu.roll(bt, s, axis=0)
        bt = jnp.where(keep, a * b_sh + bt, bt)
        a = jnp.where(keep, a * a_sh, a)
        s *= 2
    h = a * hc_ref[...] + bt
    hc_ref[...] = h[tb - 1:tb, :]
    hl_ref[0] = h[tb - 1:tb, :]
    rec_ref[...] = h * _gelu_tanh(gb_ref[...])


def _rglru(z, conv0, h0, cw, cb, wrg, brg, wig, big, sp, row0, bsz, t_len, tb):
    nt = t_len // tb
    blk0 = row0 // tb

    def z_spec(col):
        return pl.BlockSpec((tb, B_WIDTH), lambda b, t: (blk0 + b * nt + t, col))

    def const(shape):
        return pl.BlockSpec(shape, lambda b, t: (0,) * len(shape))

    row = lambda v: v.reshape(1, B_WIDTH)
    return pl.pallas_call(
        _rglru_kernel,
        grid=(bsz, nt),
        in_specs=[z_spec(3), z_spec(4),
                  pl.BlockSpec((1, CONV_W - 1, B_WIDTH), lambda b, t: (b, 0, 0)),
                  pl.BlockSpec((1, 1, B_WIDTH), lambda b, t: (b, 0, 0)),
                  const((CONV_W, B_WIDTH)), const((1, B_WIDTH)),
                  const((B_WIDTH, B_WIDTH)), const((1, B_WIDTH)),
                  const((B_WIDTH, B_WIDTH)), const((1, B_WIDTH)), const((1, B_WIDTH))],
        out_specs=[pl.BlockSpec((tb, B_WIDTH), lambda b, t: (b * nt + t, 0)),
                   pl.BlockSpec((1, CONV_W - 1, B_WIDTH), lambda b, t: (b, 0, 0)),
                   pl.BlockSpec((1, 1, B_WIDTH), lambda b, t: (b, 0, 0))],
        out_shape=[jax.ShapeDtypeStruct((bsz * t_len, B_WIDTH), f32),
                   jax.ShapeDtypeStruct((bsz, CONV_W - 1, B_WIDTH), f32),
                   jax.ShapeDtypeStruct((bsz, 1, B_WIDTH), f32)],
        scratch_shapes=[pltpu.VMEM((tb + 8, B_WIDTH), f32), pltpu.VMEM((1, B_WIDTH), f32)],
        compiler_params=_cparams(("parallel", "arbitrary")),
        name="rglru",
    )(z, z, conv0, h0.reshape(bsz, 1, B_WIDTH), cw, row(cb), wrg, row(brg), wig, row(big), row(sp))


def _block_diag(w):
    hb, d, _ = w.shape
    eye = jnp.eye(hb, dtype=w.dtype)
    return (eye[:, None, :, None] * w[:, :, None, :]).reshape(hb * d, hb * d)


def _retention_kernel(x_ref, w_hbm, cos_ref, sin_ref, s0_hbm, gng_ref, gnb_ref,
                      y_ref, sout_hbm, w_ref, s_ref, z_ref, sem, *, chunk, nt, tb, zero_state, hoist):
    b = pl.program_id(0)
    t = pl.program_id(1)
    half = DKC // 2

    @pl.when((b == 0) & (t == 0))
    def _():
        cp = pltpu.make_async_copy(w_hbm, w_ref, sem.at[0])
        cp.start()
        cp.wait()
        if hoist:
            xa = x_ref[...].astype(bf16)
            for c in range(0, w_ref.shape[1], C_QK):
                z_ref[:, c:c + C_QK] = jnp.dot(xa, w_ref[:, c:c + C_QK], preferred_element_type=f32)

    @pl.when(t == 0)
    def _():
        if zero_state:
            s_ref[...] = jnp.zeros(s_ref.shape, f32)
        else:
            cp = pltpu.make_async_copy(s0_hbm.at[b], s_ref, sem.at[1])
            cp.start()
            cp.wait()

    cos, sin = cos_ref[...], sin_ref[...]
    n = lax.broadcasted_iota(i32, (tb, 1), 0)
    m = lax.broadcasted_iota(i32, (1, tb), 1)
    dist = n - m
    same = (n // chunk) == (m // chunk)
    expo = jnp.where(same, jnp.abs(dist), dist).astype(f32)
    visible = same | (dist > 0)
    nf = n.astype(f32)

    if hoist:
        row0 = pl.multiple_of(b * tb, tb)

        def proj(col, width):
            return z_ref[pl.ds(row0, tb), col:col + width]
    else:
        xb = x_ref[...].astype(bf16)

        def proj(col, width):
            return jnp.dot(xb, w_ref[:, col:col + width], preferred_element_type=f32)

    def rot(v, scale):
        x1, x2 = v[:, :half], v[:, half:]
        return jnp.concatenate([x1 * cos - x2 * sin, x2 * cos + x1 * sin], axis=-1) * scale

    for h in range(HC):
        lg = math.log(1.0 - 2.0 ** (-5.0 - h))
        qr = rot(proj(h * DKC, DKC), 1.0)
        kr = rot(proj(C_QK + h * DKC, DKC), DKC ** -0.5)
        vb = proj(2 * C_QK + h * DVC, DVC).astype(bf16)
        gate = proj(2 * C_QK + C_V + h * DVC, DVC)
        qb = qr.astype(bf16)
        dmat = jnp.where(visible, jnp.exp(lg * expo), 0.0)
        s = lax.dot_general(qb, kr.astype(bf16), (((1,), (1,)), ((), ())),
                            preferred_element_type=f32) * dmat
        s_old = s_ref[h]
        o = jnp.dot(s.astype(bf16), vb, preferred_element_type=f32)
        o = o + jnp.dot(qb, s_old.astype(bf16), preferred_element_type=f32) * jnp.exp(lg * (nf + 1.0))
        kd = (kr * jnp.exp(lg * (tb - 1.0 - nf))).astype(bf16)
        s_ref[h] = math.exp(lg * tb) * s_old + lax.dot_general(
            kd, vb, (((0,), (0,)), ((), ())), preferred_element_type=f32)

        mu = jnp.mean(o, axis=-1, keepdims=True)
        dlt = o - mu
        var = jnp.mean(dlt * dlt, axis=-1, keepdims=True)
        vs = slice(h * DVC, (h + 1) * DVC)
        yn = dlt * lax.rsqrt(var + GN_EPS) * gng_ref[:, vs] + gnb_ref[:, vs]
        y_ref[:, vs] = (jax.nn.silu(gate) * yn).astype(y_ref.dtype)

    @pl.when(t == nt - 1)
    def _():
        cp = pltpu.make_async_copy(s_ref, sout_hbm.at[b], sem.at[1])
        cp.start()
        cp.wait()


def _retention(x, w_in, s0, pos0, gn_g, gn_b, bsz, t_len, tb, chunk):
    nt = t_len // tb
    d = x.shape[1]
    half = DKC // 2
    inv = 1.0 / (ROPE_BASE ** (jnp.arange(half, dtype=f32) / half))
    ang = (pos0 + jnp.arange(t_len)).astype(f32)[:, None] * inv[None, :]
    cos, sin = jnp.cos(ang), jnp.sin(ang)
    zero_state = s0 is None
    if zero_state:
        s0 = jnp.zeros((1, 8, LANES), f32)
    hoist = nt == 1 and bsz > 1
    x_spec = (pl.BlockSpec(x.shape, lambda b, t: (0, 0)) if hoist
              else pl.BlockSpec((tb, d), lambda b, t: (b * nt + t, 0)))
    z_shape = (bsz * tb, w_in.shape[1]) if hoist else (8, LANES)
    return pl.pallas_call(
        functools.partial(_retention_kernel, chunk=chunk, nt=nt, tb=tb, zero_state=zero_state, hoist=hoist),
        grid=(bsz, nt),
        in_specs=[x_spec,
                  pl.BlockSpec(memory_space=pl.ANY),
                  pl.BlockSpec((tb, half), lambda b, t: (t, 0)),
                  pl.BlockSpec((tb, half), lambda b, t: (t, 0)),
                  pl.BlockSpec(memory_space=pl.ANY),
                  pl.BlockSpec((1, C_V), lambda b, t: (0, 0)),
                  pl.BlockSpec((1, C_V), lambda b, t: (0, 0))],
        out_specs=[pl.BlockSpec((tb, C_V), lambda b, t: (b * nt + t, 0)),
                   pl.BlockSpec(memory_space=pl.ANY)],
        out_shape=[jax.ShapeDtypeStruct((bsz * t_len, C_V), bf16),
                   jax.ShapeDtypeStruct((bsz, HC, DKC, DVC), f32)],
        scratch_shapes=[pltpu.VMEM(w_in.shape, bf16), pltpu.VMEM((HC, DKC, DVC), f32),
                        pltpu.VMEM(z_shape, f32), pltpu.SemaphoreType.DMA((2,))],
        compiler_params=_cparams(("arbitrary", "arbitrary")),
        name="retention",
    )(x, w_in, cos, sin, s0, gn_g.reshape(1, C_V), gn_b.reshape(1, C_V))


def _dispatch_kernel(ech_ref, est_ref, tab_ref, tabv_ref, ep_ref, es_ref, xp_ref, xs_ref,
                     xg_hbm, lrow_ref, sbuf, zchunk, pending, sem, *, np_tiles, nt):
    i = pl.program_id(0)
    td = xp_ref.shape[0]
    ls = sbuf.shape[1]
    slot = i % 2
    nchunks = xg_hbm.shape[0] // MOE_CH
    blk_chunks = MOE_BM // MOE_CH

    def drain(s):
        _wait_chunks(xg_hbm, sem.at[s], pending[s], (ls // MOE_CH).bit_length())

    @pl.when(i == 0)
    def _():
        pending[0] = 0
        pending[1] = 0
        zchunk[...] = jnp.zeros(zchunk.shape, zchunk.dtype)

        def zero_chunks(lo, hi):
            def put(c, carry):
                pltpu.make_async_copy(zchunk, _rows(xg_hbm, c, 1), sem.at[2]).start()
                return carry
            lax.fori_loop(lo, hi, put, 0)
            _wait_chunks(xg_hbm, sem.at[2], jnp.maximum(hi - lo, 0), nchunks.bit_length())

        def per_expert(e, carry):
            used = est_ref[e] + ech_ref[e]
            zero_chunks(used, used + jnp.bitwise_and(-ech_ref[e], blk_chunks - 1))
            return carry
        lax.fori_loop(0, N_EXPERTS, per_expert, 0)
        last = N_EXPERTS - 1
        end = est_ref[last] + ech_ref[last] + jnp.bitwise_and(-ech_ref[last], blk_chunks - 1)
        zero_chunks(end, nchunks)

    def sort_and_copy(e_ref, x_ref):
        e_t = e_ref[...].astype(f32).T
        erow = lax.broadcasted_iota(i32, (LANES, td), 0).astype(f32)
        picks = [erow == e_t[k:k + 1, :] for k in range(TOP_K)]
        member = jnp.zeros((LANES, td), f32)
        for sel in picks:
            member = member + sel.astype(f32)
        r = lax.broadcasted_iota(i32, (td, td), 0)
        c = lax.broadcasted_iota(i32, (td, td), 1)
        before = jnp.dot(member.astype(bf16), (r < c).astype(bf16), preferred_element_type=f32)
        offs = tabv_ref[0]
        offs = jnp.concatenate([offs] * (td // LANES), axis=1) if td >= LANES else offs[:, :td]
        in_tile = before + offs
        jrow = lax.broadcasted_iota(i32, (ls, td), 0)
        out_row = lax.broadcasted_iota(i32, (LANES, td), 0)
        place = None
        lrow = jnp.zeros((LANES, td), f32)
        for k, sel in enumerate(picks):
            row_k = jnp.sum(jnp.where(sel, in_tile, 0.0), axis=0, keepdims=True)
            hit = jrow == row_k.astype(i32)
            place = hit if place is None else place | hit
            lrow = jnp.where(out_row == k, row_k, lrow)
        lrow_ref[...] = lrow.T.astype(i32)
        place = jnp.where(place, 1.0, 0.0).astype(bf16)
        srt = jnp.dot(place, x_ref[...].astype(bf16), preferred_element_type=f32)
        drain(slot)
        sbuf[slot] = _pack_bf16_pairs(srt)

        total = 0
        for ei in range(N_EXPERTS):
            dst0 = tab_ref[0, 0, ei]
            n = tab_ref[0, 0, N_EXPERTS + ei]
            src0 = tab_ref[0, 0, 2 * N_EXPERTS + ei]
            _copy_run(sbuf.at[slot], src0, xg_hbm, dst0, n, sem.at[slot], ei % 2)
            total = total + n
        pending[slot] = total

    @pl.when(i < np_tiles)
    def _():
        sort_and_copy(ep_ref, xp_ref)

    @pl.when(i >= np_tiles)
    def _():
        sort_and_copy(es_ref, xs_ref)

    @pl.when(i == nt - 1)
    def _():
        drain(0)
        drain(1)


def _dispatch(xs, top_es, tab, tabv, ech, est, nblk, td):
    x_p, x_s = xs
    d = x_p.shape[1]
    np_tiles, ns_tiles = x_p.shape[0] // td, x_s.shape[0] // td
    nt = np_tiles + ns_tiles
    ls = td * TOP_K + N_EXPERTS * MOE_CH
    pspec = lambda w: pl.BlockSpec((td, w), lambda i, *_: (jnp.minimum(i, np_tiles - 1), 0))
    sspec = lambda w: pl.BlockSpec((td, w), lambda i, *_: (jnp.maximum(i - np_tiles, 0), 0))
    grid_spec = pltpu.PrefetchScalarGridSpec(
        num_scalar_prefetch=2,
        grid=(nt,),
        in_specs=[pl.BlockSpec((1, 1, LANES), lambda i, *_: (i, 0, 0), memory_space=pltpu.SMEM),
                  pl.BlockSpec((1, LANES, LANES), lambda i, *_: (i, 0, 0)),
                  pspec(LANES), sspec(LANES), pspec(d), sspec(d)],
        out_specs=[pl.BlockSpec(memory_space=pl.ANY),
                   pl.BlockSpec((td, LANES), lambda i, *_: (i, 0))],
        scratch_shapes=[pltpu.VMEM((2, ls, d // 2), u32), pltpu.VMEM((MOE_CH, d // 2), u32),
                        pltpu.SMEM((2,), i32), pltpu.SemaphoreType.DMA((3,))],
    )
    xg, lrow = pl.pallas_call(
        functools.partial(_dispatch_kernel, np_tiles=np_tiles, nt=nt),
        grid_spec=grid_spec,
        out_shape=[jax.ShapeDtypeStruct((nblk * MOE_BM, d // 2), u32),
                   jax.ShapeDtypeStruct((nt * td, LANES), i32)],
        compiler_params=_cparams(("arbitrary",)),
        name="moe_dispatch",
    )(ech, est, tab, tabv, top_es[0], top_es[1], x_p, x_s)
    return xg, lrow


def _moe_kernel(blk_e_ref, nact_ref, first_ref, seg_ref, nxt_ref, x_ref, wup_hbm, bup_ref, wdn_hbm, bdn_ref,
                y_ref, wup_f, wdn_f, wup_bf, wdn_bf, sem, *, layer):
    b0 = pl.program_id(0) * MOE_PAIR
    nact = nact_ref[0]
    bm = MOE_BM

    def weight_copies(e, slot):
        return (pltpu.make_async_copy(wup_hbm.at[layer, e], wup_f.at[slot], sem.at[slot, 0]),
                pltpu.make_async_copy(wdn_hbm.at[layer, e], wdn_f.at[slot], sem.at[slot, 1]))

    @pl.when(b0 < nact)
    def _():
        for sub in range(MOE_PAIR):
            b = b0 + sub

            @pl.when(first_ref[b] == 1)
            def _():
                slot = seg_ref[b] % 2

                @pl.when(b == 0)
                def _():
                    for c in weight_copies(blk_e_ref[0], 0):
                        c.start()

                for c in weight_copies(blk_e_ref[b], slot):
                    c.wait()

                @pl.when(nxt_ref[b] >= 0)
                def _():
                    for c in weight_copies(nxt_ref[b], 1 - slot):
                        c.start()

                wup_bf[slot] = wup_f[slot].astype(bf16)
                wdn_bf[slot] = wdn_f[slot].astype(bf16)

        for sub in range(MOE_PAIR):
            b = b0 + sub
            slot = seg_ref[b] % 2
            e = layer * N_EXPERTS + blk_e_ref[b]
            rows = slice(sub * bm, (sub + 1) * bm)
            hdn = jnp.dot(_unpack_bf16_pairs(x_ref[rows, :]), wup_bf[slot], preferred_element_type=f32)
            hdn = hdn + bup_ref[pl.ds(e, 1), :]
            glu = jnp.minimum(hdn[:, :D_FF], SWIGLU_LIMIT)
            lin = jnp.clip(hdn[:, D_FF:], -SWIGLU_LIMIT, SWIGLU_LIMIT)
            act = glu * jax.nn.sigmoid(SWIGLU_ALPHA * glu) * (lin + 1.0)
            y_ref[rows, :] = (jnp.dot(act.astype(bf16), wdn_bf[slot], preferred_element_type=f32)
                              + bdn_ref[pl.ds(e, 1), :])

    @pl.when(b0 >= nact)
    def _():
        y_ref[...] = jnp.zeros(y_ref.shape, f32)


def _moe_experts(xg, blk_e, nact, first, seg, nxt, l, w_up, b_up, w_down, b_down):
    r, half = xg.shape
    d = 2 * half
    rows = MOE_BM * MOE_PAIR
    depth = w_up.shape[0]
    whole = lambda shape: pl.BlockSpec(shape, lambda s, *_: (0,) * len(shape))
    grid_spec = pltpu.PrefetchScalarGridSpec(
        num_scalar_prefetch=5,
        grid=(r // rows,),
        in_specs=[
            pl.BlockSpec((rows, half), lambda s, be, na, *_: (jnp.minimum(s, (na[0] - 1) // MOE_PAIR), 0)),
            pl.BlockSpec(memory_space=pl.ANY),
            whole((depth * N_EXPERTS, 2 * D_FF)),
            pl.BlockSpec(memory_space=pl.ANY),
            whole((depth * N_EXPERTS, d)),
        ],
        out_specs=pl.BlockSpec((rows, d), lambda s, *_: (s, 0)),
        scratch_shapes=[pltpu.VMEM((2, d, 2 * D_FF), f32), pltpu.VMEM((2, D_FF, d), f32),
                        pltpu.VMEM((2, d, 2 * D_FF), bf16), pltpu.VMEM((2, D_FF, d), bf16),
                        pltpu.SemaphoreType.DMA((2, 2))],
    )
    return pl.pallas_call(
        functools.partial(_moe_kernel, layer=l),
        grid_spec=grid_spec,
        out_shape=jax.ShapeDtypeStruct((r, d), f32),
        compiler_params=_cparams(("arbitrary",)),
        name="moe_experts",
    )(blk_e, nact, first, seg, nxt, xg, w_up, b_up.reshape(depth * N_EXPERTS, 2 * D_FF),
      w_down, b_down.reshape(depth * N_EXPERTS, d))


def _split_bf16(v):
    hi = v.astype(bf16)
    return hi, (v - hi.astype(f32)).astype(bf16)


def _combine_ln_kernel(tab_cur_ref, tab_nxt_ref, y_hbm, lrow_ref, gate_ref, x_ref, g_ref, b_ref, o_ref,
                       ybuf, pending, sem, *, nt):
    i = pl.program_id(0)
    td = x_ref.shape[0]
    ls = ybuf.shape[1]
    slot = i % 2

    def gather(tab_ref, s):
        total = 0
        for ei in range(N_EXPERTS):
            src0 = tab_ref[0, 0, ei]
            n = tab_ref[0, 0, N_EXPERTS + ei]
            dst0 = tab_ref[0, 0, 2 * N_EXPERTS + ei]
            _copy_run(y_hbm, src0, ybuf.at[s], dst0, n, sem.at[s], ei % 2)
            total = total + n
        pending[s] = total

    @pl.when(i == 0)
    def _():
        ybuf[...] = jnp.zeros(ybuf.shape, f32)
        gather(tab_cur_ref, 0)

    @pl.when(i + 1 < nt)
    def _():
        gather(tab_nxt_ref, (i + 1) % 2)

    _wait_chunks(y_hbm, sem.at[slot], pending[slot], (ls // MOE_CH).bit_length())

    lrow = lrow_ref[...]
    gates = gate_ref[...]
    col = lax.broadcasted_iota(i32, (td, ls), 1)
    weight = jnp.zeros((td, ls), f32)
    for k in range(TOP_K):
        weight = weight + jnp.where(col == lrow[:, k:k + 1], gates[:, k:k + 1], 0.0)
    wh, wl = _split_bf16(weight)
    yb = ybuf[slot].astype(bf16)
    moe = jnp.dot(wh, yb, preferred_element_type=f32) + jnp.dot(wl, yb, preferred_element_type=f32)
    o_ref[...] = _layernorm(ALPHA * x_ref[...] + moe, g_ref[...], b_ref[...])


def _combine_ln(y, tab, lrow, blk0, gates, x, g, b, td):
    rows, d = x.shape
    nt = rows // td
    ls = td * TOP_K + N_EXPERTS * MOE_CH
    grid_spec = pltpu.PrefetchScalarGridSpec(
        num_scalar_prefetch=0,
        grid=(nt,),
        in_specs=[
            pl.BlockSpec((1, 1, LANES), lambda i: (blk0 + i, 0, 0), memory_space=pltpu.SMEM),
            pl.BlockSpec((1, 1, LANES), lambda i: (blk0 + jnp.minimum(i + 1, nt - 1), 0, 0),
                         memory_space=pltpu.SMEM),
            pl.BlockSpec(memory_space=pl.ANY),
            pl.BlockSpec((td, LANES), lambda i: (blk0 + i, 0)),
            pl.BlockSpec((td, LANES), lambda i: (i, 0)),
            pl.BlockSpec((td, d), lambda i: (i, 0)),
            pl.BlockSpec((1, d), lambda i: (0, 0)),
            pl.BlockSpec((1, d), lambda i: (0, 0)),
        ],
        out_specs=pl.BlockSpec((td, d), lambda i: (i, 0)),
        scratch_shapes=[pltpu.VMEM((2, ls, d), f32), pltpu.SMEM((2,), i32), pltpu.SemaphoreType.DMA((2,))],
    )
    return pl.pallas_call(
        functools.partial(_combine_ln_kernel, nt=nt),
        grid_spec=grid_spec,
        out_shape=jax.ShapeDtypeStruct((rows, d), f32),
        compiler_params=_cparams(("arbitrary",)),
        name="combine_ln",
    )(tab, tab, y, lrow, gates, x, g.reshape(1, d), b.reshape(1, d))


def _moe_ln(xs, top_es, gates, cnts, td, l, w_up, b_up, w_down, b_down, g, b):
    x_p, x_s = xs
    n_p, n_s = x_p.shape[0], x_s.shape[0]
    n = n_p + n_s
    bm = MOE_BM
    nt = n // td
    nblk = -(-(n * TOP_K + nt * N_EXPERTS * (MOE_CH - 1)) // bm) + N_EXPERTS
    nblk = -(-nblk // MOE_PAIR) * MOE_PAIR
    cnt = jnp.concatenate([c[:, 0, :N_EXPERTS] for c in cnts], axis=0).astype(i32)
    run = (cnt + MOE_CH - 1) // MOE_CH
    in_tile = jnp.cumsum(run, axis=1) - run
    ech = jnp.sum(run, axis=0)
    blk_chunks = bm // MOE_CH
    padded_ch = (ech + blk_chunks - 1) // blk_chunks * blk_chunks
    est = jnp.cumsum(padded_ch) - padded_ch
    in_all = est[None, :] + jnp.cumsum(run, axis=0) - run
    tab = jnp.concatenate([in_all, run, in_tile, jnp.zeros_like(run)], axis=1).reshape(nt, 1, LANES)
    tabv = jnp.pad((in_tile * MOE_CH).astype(f32), ((0, 0), (0, LANES - N_EXPERTS)))
    tabv = jnp.broadcast_to(tabv[:, :, None], (nt, LANES, LANES))
    xg, lrow = _dispatch(xs, top_es, tab, tabv, ech, est, nblk, td)
    padded = padded_ch * MOE_CH
    ends = jnp.cumsum(padded)
    blk = jnp.arange(nblk, dtype=i32)
    blk_e = jnp.minimum(jnp.sum((ends[None, :] <= (blk * bm)[:, None]).astype(i32), axis=1), N_EXPERTS - 1)
    nact = (ends[-1] // bm).astype(i32).reshape(1)
    first = ((blk < nact[0]) & ((blk == 0) | (blk_e != jnp.roll(blk_e, 1)))).astype(i32)
    seg = jnp.cumsum(first) - 1
    eid = jnp.arange(N_EXPERTS, dtype=i32)
    later_nonempty = (padded[None, :] > 0) & (eid[None, :] > eid[:, None])
    nxt_of_e = jnp.min(jnp.where(later_nonempty, eid[None, :], N_EXPERTS), axis=1)
    nxt = jnp.where(nxt_of_e < N_EXPERTS, nxt_of_e, -1)[blk_e].astype(i32)
    y = _moe_experts(xg, blk_e, nact, first, seg, nxt, l, w_up, b_up, w_down, b_down)
    return (_combine_ln(y, tab, lrow, 0, gates[0], x_p, g, b, td),
            _combine_ln(y, tab, lrow, n_p // td, gates[1], x_s, g, b, td))


def kernel(x_prompt, x_sample, cache_a_k, cache_a_v, state_b_conv, state_b_h, state_c_s, ab_w_in, ab_rel_bias, ab_conv_w, ab_conv_b, ab_w_rg, ab_b_rg, ab_w_ig, ab_b_ig, ab_lambda, ab_w_out, c_w_in, c_gn_g, c_gn_b, c_w_out, ln1_g, ln1_b, ln2_g, ln2_b, moe_w_router, moe_b_router, moe_w_up, moe_b_up, moe_w_down, moe_b_down):
    bp, tp, d = x_prompt.shape
    bs, ts, _ = x_sample.shape
    assert bp == 1 and tp % RET_TB == 0 and ts <= CHUNK
    n_p, n_s = bp * tp, bs * ts
    xs = (x_prompt.reshape(n_p, d), x_sample.reshape(n_s, d))
    tms = tuple(_pick(m, (512, 256, 128, 64)) for m in (n_p, n_s))
    td = _pick(math.gcd(n_p, n_s), (256, 128, 64, 32))
    moe = (moe_w_up, moe_b_up, moe_w_down, moe_b_down)
    router = (moe_w_router, moe_b_router)

    def per_group(fn, *groups):
        return tuple(zip(*[fn(*args) for args in zip(*groups)]))

    j = 0
    w_in = ab_w_in[j].astype(bf16)
    z_p, z_s = (_matmul(x, w_in, tm, w_in.shape[1]) for x, tm in zip(xs, tms))
    att = (_attn_prompt(z_p, ab_rel_bias[j], tp),
           _attn_sample(z_s, cache_a_k[j], cache_a_v[j], ab_rel_bias[j], 0, bs, ts))
    lru_w = (ab_conv_w[j], ab_conv_b[j], _block_diag(ab_w_rg[j]).astype(bf16), ab_b_rg[j],
             _block_diag(ab_w_ig[j]).astype(bf16), ab_b_ig[j], jax.nn.softplus(-ab_lambda[j]))
    rec_p, bc_p, bh_p = _rglru(z_p, jnp.zeros((bp, CONV_W - 1, B_WIDTH), f32), jnp.zeros((bp, B_WIDTH), f32),
                               *lru_w, 0, bp, tp, _pick(tp, (LRU_TB, 128, 64)))
    rec_s, bc_s, bh_s = _rglru(z_s, state_b_conv[j], state_b_h[j], *lru_w, 0, bs, ts, ts)
    w_out = ab_w_out[j].astype(bf16)
    w_out = [w_out[:A_WIDTH], w_out[A_WIDTH:]]
    xs, top_es, gates, cnts = per_group(
        lambda a, r, x, tm: _proj_ln([a, r], w_out, x, ln1_g[0], ln1_b[0], *router, 0, tm, td),
        att, (rec_p, rec_s), xs, tms)
    xs = _moe_ln(xs, top_es, gates, cnts, td, 0, *moe, ln2_g[0], ln2_b[0])

    keep = min(A_PAST_CHUNKS * CHUNK, tp)
    ak_p = z_p[n_p - keep:, A_WIDTH:2 * A_WIDTH].reshape(1, bp, keep, HA, DHA)
    av_p = z_p[n_p - keep:, 2 * A_WIDTH:3 * A_WIDTH].reshape(1, bp, keep, HA, DHA)
    ak_s = z_s[:, A_WIDTH:2 * A_WIDTH].reshape(1, bs, ts, HA, DHA)
    av_s = z_s[:, 2 * A_WIDTH:3 * A_WIDTH].reshape(1, bs, ts, HA, DHA)

    w_in = c_w_in[j].astype(bf16)
    y_p, cs_p = _retention(xs[0], w_in, None, 0, c_gn_g[j], c_gn_b[j], bp, tp, RET_TB, CHUNK)
    y_s, cs_s = _retention(xs[1], w_in, state_c_s[j], PAST_LEN, c_gn_g[j], c_gn_b[j], bs, ts, ts, ts)
    w_out = [c_w_out[j].astype(bf16)]
    xs, top_es, gates, cnts = per_group(
        lambda y, x, tm: _proj_ln([y], w_out, x, ln1_g[1], ln1_b[1], *router, 1, tm, td),
        (y_p, y_s), xs, tms)
    out_p, out_s = _moe_ln(xs, top_es, gates, cnts, td, 1, *moe, ln2_g[1], ln2_b[1])

    return (out_p.reshape(bp, tp, d), out_s.reshape(bs, ts, d),
            ak_p, av_p, bc_p[None], bh_p.reshape(1, bp, B_WIDTH), cs_p[None],
            ak_s, av_s, bc_s[None], bh_s.reshape(1, bs, B_WIDTH), cs_s[None])
```

```python
import functools
import math

import jax
import jax.numpy as jnp
from jax import lax
from jax.experimental import pallas as pl
from jax.experimental.pallas import tpu as pltpu

f32 = jnp.float32
bf16 = jnp.bfloat16
i32 = jnp.int32
u32 = jnp.uint32

DEPTH = 2
PAST_LEN = 1024
CHUNK = 64
A_PAST_CHUNKS = 8
REL_CLIP = 128
HA, DHA = 8, 64
A_WIDTH = HA * DHA
B_WIDTH = 512
HB = 8
CONV_W = 4
LRU_C = 8.0
HC, DKC, DVC = 4, 256, 512
C_QK, C_V = HC * DKC, HC * DVC
ROPE_BASE = 10000.0
GN_EPS = 1e-5
N_EXPERTS = 32
TOP_K = 4
D_FF = 1024
SWIGLU_LIMIT = 7.0
SWIGLU_ALPHA = 1.702
ALPHA = (2.0 * DEPTH) ** 0.25
LN_EPS = 1e-5
NEG = -1e30

LANES = 128
VMEM_LIMIT = 56 * 1024 * 1024
ATT_TQ = 256
MOE_BM = 256
MOE_PAIR = 2
MOE_CH = 8
MOE_BIG = 4
RET_TB = 256
LRU_TB = 256


def _pick(n, cands):
    for c in cands:
        if n % c == 0:
            return c
    raise ValueError(f"no tile for {n} in {cands}")


def _cparams(sem):
    return pltpu.CompilerParams(dimension_semantics=sem, vmem_limit_bytes=VMEM_LIMIT)


def _layernorm(v, g, b):
    mu = jnp.mean(v, axis=-1, keepdims=True)
    d = v - mu
    var = jnp.mean(d * d, axis=-1, keepdims=True)
    return d * lax.rsqrt(var + LN_EPS) * g + b


def _pack_bf16_pairs(v):
    w = v.shape[1] // 2
    hi = lax.bitcast_convert_type(v[:, :w], u32)
    lo = lax.bitcast_convert_type(v[:, w:], u32)
    return (hi & jnp.uint32(0xFFFF0000)) | (lo >> 16)


def _unpack_bf16_pairs(p):
    hi = lax.bitcast_convert_type(p & jnp.uint32(0xFFFF0000), f32)
    lo = lax.bitcast_convert_type(p << 16, f32)
    return jnp.concatenate([hi, lo], axis=1).astype(bf16)


def _rows(ref, chunk, nchunks):
    return ref.at[pl.ds(pl.multiple_of(chunk * MOE_CH, MOE_CH), nchunks * MOE_CH)]


def _copy_run(src, src_chunk, dst, dst_chunk, n, sem, priority):
    nbig = lax.shift_right_logical(n, MOE_BIG.bit_length() - 1)

    def big(c, carry):
        pltpu.make_async_copy(_rows(src, src_chunk + c * MOE_BIG, MOE_BIG),
                              _rows(dst, dst_chunk + c * MOE_BIG, MOE_BIG), sem).start(priority=priority)
        return carry
    lax.fori_loop(0, nbig, big, 0)

    def small(c, carry):
        pltpu.make_async_copy(_rows(src, src_chunk + c, 1), _rows(dst, dst_chunk + c, 1),
                              sem).start(priority=priority)
        return carry
    lax.fori_loop(nbig * MOE_BIG, n, small, 0)


def _wait_chunks(hbm, sem, count, max_bits):
    for b in reversed(range(max_bits)):
        @pl.when(jnp.bitwise_and(count, 1 << b) != 0)
        def _():
            span = _rows(hbm, 0, 1 << b)
            pltpu.make_async_copy(span, span, sem).wait()


def _rel_bias_matrix(table, nq, nk, d0):
    p = nq + nk - 1
    j = jnp.arange(p)
    u = table[:, jnp.clip(d0 + nq - 1 - j, -REL_CLIP, REL_CLIP) + REL_CLIP]
    u = jnp.roll(u, -(nq - 1), axis=1)
    flat = jnp.tile(u, (1, nq))[:, :nq * (p - 1)]
    return flat.reshape(table.shape[0], nq, p - 1)[:, :, :nk].astype(f32)


def _mm_kernel(x_ref, w_ref, o_ref):
    o_ref[...] = jnp.dot(x_ref[...].astype(bf16), w_ref[...], preferred_element_type=f32)


def _matmul(x, w, tm, tn):
    m, k = x.shape
    n = w.shape[1]
    return pl.pallas_call(
        _mm_kernel,
        grid=(m // tm, n // tn),
        in_specs=[pl.BlockSpec((tm, k), lambda i, j: (i, 0)),
                  pl.BlockSpec((k, tn), lambda i, j: (0, j))],
        out_specs=pl.BlockSpec((tm, tn), lambda i, j: (i, j)),
        out_shape=jax.ShapeDtypeStruct((m, n), f32),
        compiler_params=_cparams(("parallel", "parallel")),
        name="dense_proj",
    )(x, w)


def _route_top_k(x, w_t, b_col):
    logits = lax.dot_general(w_t.astype(bf16), x.astype(bf16), (((1,), (1,)), ((), ())),
                             preferred_element_type=f32) + b_col
    tm = logits.shape[1]
    row = lax.broadcasted_iota(i32, (N_EXPERTS, tm), 0)
    member = jnp.zeros((N_EXPERTS, tm), f32)
    ids, ps = [], []
    den = jnp.zeros((1, tm), f32)
    for k in range(TOP_K):
        v = jnp.max(logits, axis=0, keepdims=True)
        idx = jnp.min(jnp.where(logits == v, row, N_EXPERTS), axis=0, keepdims=True)
        chosen = row == idx
        logits = jnp.where(chosen, -jnp.inf, logits)
        member = member + chosen.astype(f32)
        p = jnp.exp(v - ps[0][1]) if k else jnp.ones((1, tm), f32)
        ps.append((p, v))
        den = den + p
        ids.append(idx.astype(f32))
    out_row = lax.broadcasted_iota(i32, (LANES, tm), 0)
    e_rows = jnp.zeros((LANES, tm), f32)
    g_rows = jnp.zeros((LANES, tm), f32)
    for k in range(TOP_K):
        e_rows = jnp.where(out_row == k, ids[k], e_rows)
        g_rows = jnp.where(out_row == k, ps[k][0] / den, g_rows)
    return e_rows.T.astype(i32), g_rows.T, member


def _proj_ln_kernel(*refs, n_in):
    a_refs = refs[:n_in]
    w_refs = refs[n_in:2 * n_in]
    x_ref, g_ref, b_ref, wr_ref, br_ref, o_ref, e_ref, gate_ref, cnt_ref = refs[2 * n_in:]
    acc = ALPHA * x_ref[...]
    for a_ref, w_ref in zip(a_refs, w_refs):
        acc = acc + jnp.dot(a_ref[...].astype(bf16), w_ref[...], preferred_element_type=f32)
    y = _layernorm(acc, g_ref[...], b_ref[...])
    o_ref[...] = y
    e_out, gates, member = _route_top_k(y, wr_ref[0], br_ref[0])
    e_ref[...] = e_out
    gate_ref[...] = gates
    nsub = cnt_ref.shape[0]
    td = e_out.shape[0] // nsub
    ones = jnp.ones((8, td), bf16)
    for u in range(nsub):
        cnt = lax.dot_general(ones, member[:, u * td:(u + 1) * td].astype(bf16), (((1,), (1,)), ((), ())),
                              preferred_element_type=f32)
        cnt_ref[u] = jnp.concatenate([cnt, jnp.zeros((8, LANES - N_EXPERTS), f32)], axis=1)


def _proj_ln(acts, ws, x, g, b, w_router, b_router, l, tm, td):
    m, d = x.shape
    n_in = len(acts)
    depth = w_router.shape[0]
    nt = m // tm
    nsub = tm // td
    cur = lambda i: (i, 0)
    in_specs = ([pl.BlockSpec((tm, a.shape[1]), cur) for a in acts]
                + [pl.BlockSpec(w.shape, lambda i: (0, 0)) for w in ws]
                + [pl.BlockSpec((tm, d), cur),
                   pl.BlockSpec((1, d), lambda i: (0, 0)),
                   pl.BlockSpec((1, d), lambda i: (0, 0)),
                   pl.BlockSpec((1, N_EXPERTS, d), lambda i: (l, 0, 0)),
                   pl.BlockSpec((1, N_EXPERTS, 1), lambda i: (l, 0, 0))])
    return pl.pallas_call(
        functools.partial(_proj_ln_kernel, n_in=n_in),
        grid=(nt,),
        in_specs=in_specs,
        out_specs=[pl.BlockSpec((tm, d), cur),
                   pl.BlockSpec((tm, LANES), cur),
                   pl.BlockSpec((tm, LANES), cur),
                   pl.BlockSpec((nsub, 8, LANES), lambda i: (i, 0, 0))],
        out_shape=[jax.ShapeDtypeStruct((m, d), f32),
                   jax.ShapeDtypeStruct((m, LANES), i32),
                   jax.ShapeDtypeStruct((m, LANES), f32),
                   jax.ShapeDtypeStruct((m // td, 8, LANES), f32)],
        compiler_params=_cparams(("parallel",)),
        name="proj_ln",
    )(*acts, *ws, x, g.reshape(1, d), b.reshape(1, d), jnp.swapaxes(w_router, 1, 2),
      b_router.reshape(depth, N_EXPERTS, 1))


def _attn_prompt_kernel(q_ref, k0_ref, k1_ref, k2_ref, v0_ref, v1_ref, v2_ref, bias_ref, o_ref):
    i = pl.program_id(0)
    tq = q_ref.shape[0]
    q = q_ref[...]
    k = jnp.concatenate([k0_ref[...], k1_ref[...], k2_ref[...]], axis=0)
    v = jnp.concatenate([v0_ref[...], v1_ref[...], v2_ref[...]], axis=0)
    kcol = lax.broadcasted_iota(i32, (1, 3 * tq), 1)
    tile_ok = (kcol // tq + i) >= 2
    qb = (q * (DHA ** -0.5)).astype(bf16)
    kb = k.astype(bf16)
    vb = v.astype(bf16)
    outs = []
    for h in range(HA):
        sl = slice(h * DHA, (h + 1) * DHA)
        s = lax.dot_general(qb[:, sl], kb[:, sl], (((1,), (1,)), ((), ())), preferred_element_type=f32)
        s = jnp.where(tile_ok, s + bias_ref[h], NEG)
        p = jnp.exp(s - jnp.max(s, axis=-1, keepdims=True))
        o = jnp.dot(p.astype(bf16), vb[:, sl], preferred_element_type=f32)
        outs.append(o / jnp.sum(p, axis=-1, keepdims=True))
    o_ref[...] = jnp.concatenate(outs, axis=-1)


def _attn_prompt(z, rel_bias, tp):
    tq = ATT_TQ
    nq = tp // tq
    qq = jnp.arange(tq)[:, None]
    kk = jnp.arange(3 * tq)[None, :]
    band = (kk // CHUNK >= qq // CHUNK) & (kk // CHUNK <= qq // CHUNK + A_PAST_CHUNKS)
    bias = jnp.where(band[None], _rel_bias_matrix(rel_bias, tq, 3 * tq, 2 * tq), NEG)

    def kv_spec(back, col):
        return pl.BlockSpec((tq, A_WIDTH), lambda i: (jnp.maximum(i - back, 0), col))

    return pl.pallas_call(
        _attn_prompt_kernel,
        grid=(nq,),
        in_specs=[pl.BlockSpec((tq, A_WIDTH), lambda i: (i, 0)),
                  kv_spec(2, 1), kv_spec(1, 1), kv_spec(0, 1),
                  kv_spec(2, 2), kv_spec(1, 2), kv_spec(0, 2),
                  pl.BlockSpec((HA, tq, 3 * tq), lambda i: (0, 0, 0))],
        out_specs=pl.BlockSpec((tq, A_WIDTH), lambda i: (i, 0)),
        out_shape=jax.ShapeDtypeStruct((tp, A_WIDTH), f32),
        compiler_params=_cparams(("parallel",)),
        name="attn_prompt",
    )(z, z, z, z, z, z, z, bias)


def _attn_sample_kernel(q_ref, kn_ref, vn_ref, kc_ref, vc_ref, bias_ref, o_ref):
    p_len = kc_ref.shape[1]
    q, kn, vn = q_ref[...], kn_ref[...], vn_ref[...]
    kc, vc = kc_ref[0], vc_ref[0]
    outs = []
    for h in range(HA):
        sl = slice(h * DHA, (h + 1) * DHA)
        qh = q[:, sl].astype(bf16)
        dims = (((1,), (1,)), ((), ()))
        sc = lax.dot_general(qh, kc[:, sl].astype(bf16), dims, preferred_element_type=f32)
        sn = lax.dot_general(qh, kn[:, sl].astype(bf16), dims, preferred_element_type=f32)
        b = bias_ref[h]
        sc = sc * (DHA ** -0.5) + b[:, :p_len]
        sn = sn * (DHA ** -0.5) + b[:, p_len:]
        m = jnp.maximum(jnp.max(sc, axis=-1, keepdims=True), jnp.max(sn, axis=-1, keepdims=True))
        pc, pn = jnp.exp(sc - m), jnp.exp(sn - m)
        den = jnp.sum(pc, axis=-1, keepdims=True) + jnp.sum(pn, axis=-1, keepdims=True)
        pc, pn = pc / den, pn / den
        outs.append(jnp.dot(pc.astype(bf16), vc[:, sl].astype(bf16), preferred_element_type=f32)
                    + jnp.dot(pn.astype(bf16), vn[:, sl].astype(bf16), preferred_element_type=f32))
    o_ref[...] = jnp.concatenate(outs, axis=-1)


def _attn_sample(z, k_cache, v_cache, rel_bias, row0, bs, ts):
    p_len = k_cache.shape[1]
    kc = k_cache.reshape(bs, p_len, A_WIDTH)
    vc = v_cache.reshape(bs, p_len, A_WIDTH)
    bias = _rel_bias_matrix(rel_bias, ts, p_len + ts, p_len)
    blk0 = row0 // ts

    def z_spec(col):
        return pl.BlockSpec((ts, A_WIDTH), lambda b: (blk0 + b, col))

    return pl.pallas_call(
        _attn_sample_kernel,
        grid=(bs,),
        in_specs=[z_spec(0), z_spec(1), z_spec(2),
                  pl.BlockSpec((1, p_len, A_WIDTH), lambda b: (b, 0, 0)),
                  pl.BlockSpec((1, p_len, A_WIDTH), lambda b: (b, 0, 0)),
                  pl.BlockSpec((HA, ts, p_len + ts), lambda b: (0, 0, 0))],
        out_specs=pl.BlockSpec((ts, A_WIDTH), lambda b: (b, 0)),
        out_shape=jax.ShapeDtypeStruct((bs * ts, A_WIDTH), f32),
        compiler_params=_cparams(("parallel",)),
        name="attn_sample",
    )(z, z, z, kc, vc, bias)


def _gelu_tanh(x):
    return 0.5 * x * (1.0 + jnp.tanh(math.sqrt(2.0 / math.pi) * (x + 0.044715 * (x * x * x))))


def _rglru_kernel(xb_ref, gb_ref, conv0_ref, h0_ref, cw_ref, cb_ref, wrg_ref, brg_ref,
                  wig_ref, big_ref, sp_ref, rec_ref, convn_ref, hl_ref, xp_ref, hc_ref):
    t = pl.program_id(1)
    tb = xb_ref.shape[0]
    pad = 8

    @pl.when(t == 0)
    def _():
        xp_ref[0:pad, :] = jnp.zeros((pad, B_WIDTH), f32)
        xp_ref[pad - (CONV_W - 1):pad, :] = conv0_ref[0]
        hc_ref[...] = h0_ref[0]

    xb = xb_ref[...]
    xp_ref[pad:pad + tb, :] = xb
    cw = cw_ref[...]
    u = cb_ref[...] + cw[CONV_W - 1:CONV_W, :] * xb
    for j in range(CONV_W - 1):
        sh = CONV_W - 1 - j
        u = u + cw[j:j + 1, :] * xp_ref[pad - sh:pad - sh + tb, :]
    convn_ref[0] = xp_ref[pad + tb - (CONV_W - 1):pad + tb, :]
    xp_ref[0:pad, :] = xp_ref[tb:tb + pad, :]

    ub = u.astype(bf16)
    r = jax.nn.sigmoid(jnp.dot(ub, wrg_ref[...], preferred_element_type=f32) + brg_ref[...])
    ig = jax.nn.sigmoid(jnp.dot(ub, wig_ref[...], preferred_element_type=f32) + big_ref[...])
    log_a = -LRU_C * r * sp_ref[...]
    a = jnp.exp(log_a)
    bt = jnp.sqrt(-jnp.tanh(log_a) * (a * a + 1.0)) * (ig * u)

    grp = 8
    row = lax.broadcasted_iota(i32, (tb, 1), 0) % grp
    s = 1
    while s < grp:
        keep = row >= s
        a_sh = pltpu.roll(a, s, axis=0)
        b_sh = pltpu.roll(bt, s, axis=0)
        bt = jnp.where(keep, a * b_sh + bt, bt)
        a = jnp.where(keep, a * a_sh, a)
        s *= 2
    carry = hc_ref[...]
    for g in range(tb // grp):
        rows = slice(g * grp, (g + 1) * grp)
        hg = a[rows] * carry + bt[rows]
        carry = hg[grp - 1:grp, :]
        rec_ref[rows, :] = hg * _gelu_tanh(gb_ref[rows, :])
    hc_ref[...] = carry
    hl_ref[0] = carry


def _rglru(z, conv0, h0, cw, cb, wrg, brg, wig, big, sp, row0, bsz, t_len, tb):
    nt = t_len // tb
    blk0 = row0 // tb

    def z_spec(col):
        return pl.BlockSpec((tb, B_WIDTH), lambda b, t: (blk0 + b * nt + t, col))

    def const(shape):
        return pl.BlockSpec(shape, lambda b, t: (0,) * len(shape))

    row = lambda v: v.reshape(1, B_WIDTH)
    return pl.pallas_call(
        _rglru_kernel,
        grid=(bsz, nt),
        in_specs=[z_spec(3), z_spec(4),
                  pl.BlockSpec((1, CONV_W - 1, B_WIDTH), lambda b, t: (b, 0, 0)),
                  pl.BlockSpec((1, 1, B_WIDTH), lambda b, t: (b, 0, 0)),
                  const((CONV_W, B_WIDTH)), const((1, B_WIDTH)),
                  const((B_WIDTH, B_WIDTH)), const((1, B_WIDTH)),
                  const((B_WIDTH, B_WIDTH)), const((1, B_WIDTH)), const((1, B_WIDTH))],
        out_specs=[pl.BlockSpec((tb, B_WIDTH), lambda b, t: (b * nt + t, 0)),
                   pl.BlockSpec((1, CONV_W - 1, B_WIDTH), lambda b, t: (b, 0, 0)),
                   pl.BlockSpec((1, 1, B_WIDTH), lambda b, t: (b, 0, 0))],
        out_shape=[jax.ShapeDtypeStruct((bsz * t_len, B_WIDTH), f32),
                   jax.ShapeDtypeStruct((bsz, CONV_W - 1, B_WIDTH), f32),
                   jax.ShapeDtypeStruct((bsz, 1, B_WIDTH), f32)],
        scratch_shapes=[pltpu.VMEM((tb + 8, B_WIDTH), f32), pltpu.VMEM((1, B_WIDTH), f32)],
        compiler_params=_cparams(("parallel", "arbitrary")),
        name="rglru",
    )(z, z, conv0, h0.reshape(bsz, 1, B_WIDTH), cw, row(cb), wrg, row(brg), wig, row(big), row(sp))


def _block_diag(w):
    hb, d, _ = w.shape
    eye = jnp.eye(hb, dtype=w.dtype)
    return (eye[:, None, :, None] * w[:, :, None, :]).reshape(hb * d, hb * d)


def _retention_kernel(x_ref, w_hbm, cos_ref, sin_ref, s0_hbm, gng_ref, gnb_ref,
                      y_ref, sout_hbm, w_ref, s_ref, z_ref, sem, *, chunk, nt, tb, zero_state, hoist):
    b = pl.program_id(0)
    t = pl.program_id(1)
    half = DKC // 2

    @pl.when((b == 0) & (t == 0))
    def _():
        cp = pltpu.make_async_copy(w_hbm, w_ref, sem.at[0])
        cp.start()
        cp.wait()
        if hoist:
            xa = x_ref[...].astype(bf16)
            for c in range(0, w_ref.shape[1], C_QK):
                z_ref[:, c:c + C_QK] = jnp.dot(xa, w_ref[:, c:c + C_QK], preferred_element_type=f32)

    @pl.when(t == 0)
    def _():
        if zero_state:
            s_ref[...] = jnp.zeros(s_ref.shape, f32)
        else:
            cp = pltpu.make_async_copy(s0_hbm.at[b], s_ref, sem.at[1])
            cp.start()
            cp.wait()

    cos, sin = cos_ref[...], sin_ref[...]
    n = lax.broadcasted_iota(i32, (tb, 1), 0)
    m = lax.broadcasted_iota(i32, (1, tb), 1)
    dist = n - m
    same = (n // chunk) == (m // chunk)
    expo = jnp.where(same, jnp.abs(dist), dist).astype(f32)
    visible = same | (dist > 0)
    nf = n.astype(f32)

    if hoist:
        row0 = pl.multiple_of(b * tb, tb)

        def proj(col, width):
            return z_ref[pl.ds(row0, tb), col:col + width]
    else:
        xb = x_ref[...].astype(bf16)

        def proj(col, width):
            return jnp.dot(xb, w_ref[:, col:col + width], preferred_element_type=f32)

    def rot(v, scale):
        x1, x2 = v[:, :half], v[:, half:]
        return jnp.concatenate([x1 * cos - x2 * sin, x2 * cos + x1 * sin], axis=-1) * scale

    for h in range(HC):
        lg = math.log(1.0 - 2.0 ** (-5.0 - h))
        qr = rot(proj(h * DKC, DKC), 1.0)
        kr = rot(proj(C_QK + h * DKC, DKC), DKC ** -0.5)
        vb = proj(2 * C_QK + h * DVC, DVC).astype(bf16)
        gate = proj(2 * C_QK + C_V + h * DVC, DVC)
        qb = qr.astype(bf16)
        dmat = jnp.where(visible, jnp.exp(lg * expo), 0.0)
        s = lax.dot_general(qb, kr.astype(bf16), (((1,), (1,)), ((), ())),
                            preferred_element_type=f32) * dmat
        s_old = s_ref[h]
        o = jnp.dot(s.astype(bf16), vb, preferred_element_type=f32)
        o = o + jnp.dot(qb, s_old.astype(bf16), preferred_element_type=f32) * jnp.exp(lg * (nf + 1.0))
        kd = (kr * jnp.exp(lg * (tb - 1.0 - nf))).astype(bf16)
        s_ref[h] = math.exp(lg * tb) * s_old + lax.dot_general(
            kd, vb, (((0,), (0,)), ((), ())), preferred_element_type=f32)

        mu = jnp.mean(o, axis=-1, keepdims=True)
        dlt = o - mu
        var = jnp.mean(dlt * dlt, axis=-1, keepdims=True)
        vs = slice(h * DVC, (h + 1) * DVC)
        yn = dlt * lax.rsqrt(var + GN_EPS) * gng_ref[:, vs] + gnb_ref[:, vs]
        y_ref[:, vs] = (jax.nn.silu(gate) * yn).astype(y_ref.dtype)

    @pl.when(t == nt - 1)
    def _():
        cp = pltpu.make_async_copy(s_ref, sout_hbm.at[b], sem.at[1])
        cp.start()
        cp.wait()


def _retention(x, w_in, s0, pos0, gn_g, gn_b, bsz, t_len, tb, chunk):
    nt = t_len // tb
    d = x.shape[1]
    half = DKC // 2
    inv = 1.0 / (ROPE_BASE ** (jnp.arange(half, dtype=f32) / half))
    ang = (pos0 + jnp.arange(t_len)).astype(f32)[:, None] * inv[None, :]
    cos, sin = jnp.cos(ang), jnp.sin(ang)
    zero_state = s0 is None
    if zero_state:
        s0 = jnp.zeros((1, 8, LANES), f32)
    hoist = nt == 1 and bsz > 1
    x_spec = (pl.BlockSpec(x.shape, lambda b, t: (0, 0)) if hoist
              else pl.BlockSpec((tb, d), lambda b, t: (b * nt + t, 0)))
    z_shape = (bsz * tb, w_in.shape[1]) if hoist else (8, LANES)
    return pl.pallas_call(
        functools.partial(_retention_kernel, chunk=chunk, nt=nt, tb=tb, zero_state=zero_state, hoist=hoist),
        grid=(bsz, nt),
        in_specs=[x_spec,
                  pl.BlockSpec(memory_space=pl.ANY),
                  pl.BlockSpec((tb, half), lambda b, t: (t, 0)),
                  pl.BlockSpec((tb, half), lambda b, t: (t, 0)),
                  pl.BlockSpec(memory_space=pl.ANY),
                  pl.BlockSpec((1, C_V), lambda b, t: (0, 0)),
                  pl.BlockSpec((1, C_V), lambda b, t: (0, 0))],
        out_specs=[pl.BlockSpec((tb, C_V), lambda b, t: (b * nt + t, 0)),
                   pl.BlockSpec(memory_space=pl.ANY)],
        out_shape=[jax.ShapeDtypeStruct((bsz * t_len, C_V), bf16),
                   jax.ShapeDtypeStruct((bsz, HC, DKC, DVC), f32)],
        scratch_shapes=[pltpu.VMEM(w_in.shape, bf16), pltpu.VMEM((HC, DKC, DVC), f32),
                        pltpu.VMEM(z_shape, f32), pltpu.SemaphoreType.DMA((2,))],
        compiler_params=_cparams(("arbitrary", "arbitrary")),
        name="retention",
    )(x, w_in, cos, sin, s0, gn_g.reshape(1, C_V), gn_b.reshape(1, C_V))


def _dispatch_kernel(ech_ref, est_ref, tab_ref, tabv_ref, ep_ref, es_ref, xp_ref, xs_ref,
                     xg_hbm, lrow_ref, sbuf, zchunk, pending, sem, *, np_tiles, nt):
    i = pl.program_id(0)
    td = xp_ref.shape[0]
    ls = sbuf.shape[1]
    slot = i % 2
    nchunks = xg_hbm.shape[0] // MOE_CH
    blk_chunks = MOE_BM // MOE_CH

    def drain(s):
        _wait_chunks(xg_hbm, sem.at[s], pending[s], (ls // MOE_CH).bit_length())

    @pl.when(i == 0)
    def _():
        pending[0] = 0
        pending[1] = 0
        zchunk[...] = jnp.zeros(zchunk.shape, zchunk.dtype)

        def zero_chunks(lo, hi):
            def put(c, carry):
                pltpu.make_async_copy(zchunk, _rows(xg_hbm, c, 1), sem.at[2]).start()
                return carry
            lax.fori_loop(lo, hi, put, 0)
            _wait_chunks(xg_hbm, sem.at[2], jnp.maximum(hi - lo, 0), nchunks.bit_length())

        def per_expert(e, carry):
            used = est_ref[e] + ech_ref[e]
            zero_chunks(used, used + jnp.bitwise_and(-ech_ref[e], blk_chunks - 1))
            return carry
        lax.fori_loop(0, N_EXPERTS, per_expert, 0)
        last = N_EXPERTS - 1
        end = est_ref[last] + ech_ref[last] + jnp.bitwise_and(-ech_ref[last], blk_chunks - 1)
        zero_chunks(end, nchunks)

    def sort_and_copy(e_ref, x_ref):
        e_t = e_ref[...].astype(f32).T
        erow = lax.broadcasted_iota(i32, (LANES, td), 0).astype(f32)
        picks = [erow == e_t[k:k + 1, :] for k in range(TOP_K)]
        member = jnp.zeros((LANES, td), f32)
        for sel in picks:
            member = member + sel.astype(f32)
        r = lax.broadcasted_iota(i32, (td, td), 0)
        c = lax.broadcasted_iota(i32, (td, td), 1)
        before = jnp.dot(member.astype(bf16), (r < c).astype(bf16), preferred_element_type=f32)
        offs = tabv_ref[0]
        offs = jnp.concatenate([offs] * (td // LANES), axis=1) if td >= LANES else offs[:, :td]
        in_tile = before + offs
        jrow = lax.broadcasted_iota(i32, (ls, td), 0)
        out_row = lax.broadcasted_iota(i32, (LANES, td), 0)
        place = None
        lrow = jnp.zeros((LANES, td), f32)
        for k, sel in enumerate(picks):
            row_k = jnp.sum(jnp.where(sel, in_tile, 0.0), axis=0, keepdims=True)
            hit = jrow == row_k.astype(i32)
            place = hit if place is None else place | hit
            lrow = jnp.where(out_row == k, row_k, lrow)
        lrow_ref[...] = lrow.T.astype(i32)
        place = jnp.where(place, 1.0, 0.0).astype(bf16)
        srt = jnp.dot(place, x_ref[...].astype(bf16), preferred_element_type=f32)
        drain(slot)
        sbuf[slot] = _pack_bf16_pairs(srt)

        total = 0
        for ei in range(N_EXPERTS):
            dst0 = tab_ref[0, 0, ei]
            n = tab_ref[0, 0, N_EXPERTS + ei]
            src0 = tab_ref[0, 0, 2 * N_EXPERTS + ei]
            _copy_run(sbuf.at[slot], src0, xg_hbm, dst0, n, sem.at[slot], ei % 2)
            total = total + n
        pending[slot] = total

    @pl.when(i < np_tiles)
    def _():
        sort_and_copy(ep_ref, xp_ref)

    @pl.when(i >= np_tiles)
    def _():
        sort_and_copy(es_ref, xs_ref)

    @pl.when(i == nt - 1)
    def _():
        drain(0)
        drain(1)


def _dispatch(xs, top_es, tab, tabv, ech, est, nblk, td):
    x_p, x_s = xs
    d = x_p.shape[1]
    np_tiles, ns_tiles = x_p.shape[0] // td, x_s.shape[0] // td
    nt = np_tiles + ns_tiles
    ls = td * TOP_K + N_EXPERTS * MOE_CH
    pspec = lambda w: pl.BlockSpec((td, w), lambda i, *_: (jnp.minimum(i, np_tiles - 1), 0))
    sspec = lambda w: pl.BlockSpec((td, w), lambda i, *_: (jnp.maximum(i - np_tiles, 0), 0))
    grid_spec = pltpu.PrefetchScalarGridSpec(
        num_scalar_prefetch=2,
        grid=(nt,),
        in_specs=[pl.BlockSpec((1, 1, LANES), lambda i, *_: (i, 0, 0), memory_space=pltpu.SMEM),
                  pl.BlockSpec((1, LANES, LANES), lambda i, *_: (i, 0, 0)),
                  pspec(LANES), sspec(LANES), pspec(d), sspec(d)],
        out_specs=[pl.BlockSpec(memory_space=pl.ANY),
                   pl.BlockSpec((td, LANES), lambda i, *_: (i, 0))],
        scratch_shapes=[pltpu.VMEM((2, ls, d // 2), u32), pltpu.VMEM((MOE_CH, d // 2), u32),
                        pltpu.SMEM((2,), i32), pltpu.SemaphoreType.DMA((3,))],
    )
    xg, lrow = pl.pallas_call(
        functools.partial(_dispatch_kernel, np_tiles=np_tiles, nt=nt),
        grid_spec=grid_spec,
        out_shape=[jax.ShapeDtypeStruct((nblk * MOE_BM, d // 2), u32),
                   jax.ShapeDtypeStruct((nt * td, LANES), i32)],
        compiler_params=_cparams(("arbitrary",)),
        name="moe_dispatch",
    )(ech, est, tab, tabv, top_es[0], top_es[1], x_p, x_s)
    return xg, lrow


def _moe_kernel(blk_e_ref, nact_ref, first_ref, seg_ref, nxt_ref, x_ref, wup_hbm, bup_ref, wdn_hbm, bdn_ref,
                y_ref, wup_f, wdn_f, wup_bf, wdn_bf, sem, *, layer):
    b0 = pl.program_id(0) * MOE_PAIR
    nact = nact_ref[0]
    bm = MOE_BM

    def weight_copies(e, slot):
        return (pltpu.make_async_copy(wup_hbm.at[layer, e], wup_f.at[slot], sem.at[slot, 0]),
                pltpu.make_async_copy(wdn_hbm.at[layer, e], wdn_f.at[slot], sem.at[slot, 1]))

    @pl.when(b0 < nact)
    def _():
        for sub in range(MOE_PAIR):
            b = b0 + sub

            @pl.when(first_ref[b] == 1)
            def _():
                slot = seg_ref[b] % 2

                @pl.when(b == 0)
                def _():
                    for c in weight_copies(blk_e_ref[0], 0):
                        c.start()

                for c in weight_copies(blk_e_ref[b], slot):
                    c.wait()

                @pl.when(nxt_ref[b] >= 0)
                def _():
                    for c in weight_copies(nxt_ref[b], 1 - slot):
                        c.start()

                wup_bf[slot] = wup_f[slot].astype(bf16)
                wdn_bf[slot] = wdn_f[slot].astype(bf16)

        for sub in range(MOE_PAIR):
            b = b0 + sub
            slot = seg_ref[b] % 2
            e = layer * N_EXPERTS + blk_e_ref[b]
            rows = slice(sub * bm, (sub + 1) * bm)
            hdn = jnp.dot(_unpack_bf16_pairs(x_ref[rows, :]), wup_bf[slot], preferred_element_type=f32)
            hdn = hdn + bup_ref[pl.ds(e, 1), :]
            glu = jnp.minimum(hdn[:, :D_FF], SWIGLU_LIMIT)
            lin = jnp.clip(hdn[:, D_FF:], -SWIGLU_LIMIT, SWIGLU_LIMIT)
            act = glu * jax.nn.sigmoid(SWIGLU_ALPHA * glu) * (lin + 1.0)
            y_ref[rows, :] = (jnp.dot(act.astype(bf16), wdn_bf[slot], preferred_element_type=f32)
                              + bdn_ref[pl.ds(e, 1), :])

    @pl.when(b0 >= nact)
    def _():
        y_ref[...] = jnp.zeros(y_ref.shape, f32)


def _moe_experts(xg, blk_e, nact, first, seg, nxt, l, w_up, b_up, w_down, b_down):
    r, half = xg.shape
    d = 2 * half
    rows = MOE_BM * MOE_PAIR
    depth = w_up.shape[0]
    whole = lambda shape: pl.BlockSpec(shape, lambda s, *_: (0,) * len(shape))
    grid_spec = pltpu.PrefetchScalarGridSpec(
        num_scalar_prefetch=5,
        grid=(r // rows,),
        in_specs=[
            pl.BlockSpec((rows, half), lambda s, be, na, *_: (jnp.minimum(s, (na[0] - 1) // MOE_PAIR), 0)),
            pl.BlockSpec(memory_space=pl.ANY),
            whole((depth * N_EXPERTS, 2 * D_FF)),
            pl.BlockSpec(memory_space=pl.ANY),
            whole((depth * N_EXPERTS, d)),
        ],
        out_specs=pl.BlockSpec((rows, d), lambda s, *_: (s, 0)),
        scratch_shapes=[pltpu.VMEM((2, d, 2 * D_FF), f32), pltpu.VMEM((2, D_FF, d), f32),
                        pltpu.VMEM((2, d, 2 * D_FF), bf16), pltpu.VMEM((2, D_FF, d), bf16),
                        pltpu.SemaphoreType.DMA((2, 2))],
    )
    return pl.pallas_call(
        functools.partial(_moe_kernel, layer=l),
        grid_spec=grid_spec,
        out_shape=jax.ShapeDtypeStruct((r, d), f32),
        compiler_params=_cparams(("arbitrary",)),
        name="moe_experts",
    )(blk_e, nact, first, seg, nxt, xg, w_up, b_up.reshape(depth * N_EXPERTS, 2 * D_FF),
      w_down, b_down.reshape(depth * N_EXPERTS, d))


def _split_bf16(v):
    hi = v.astype(bf16)
    return hi, (v - hi.astype(f32)).astype(bf16)


def _combine_ln_kernel(tab_cur_ref, tab_nxt_ref, y_hbm, lrow_ref, gate_ref, x_ref, g_ref, b_ref, o_ref,
                       ybuf, pending, sem, *, nt):
    i = pl.program_id(0)
    td = x_ref.shape[0]
    ls = ybuf.shape[1]
    slot = i % 2

    def gather(tab_ref, s):
        total = 0
        for ei in range(N_EXPERTS):
            src0 = tab_ref[0, 0, ei]
            n = tab_ref[0, 0, N_EXPERTS + ei]
            dst0 = tab_ref[0, 0, 2 * N_EXPERTS + ei]
            _copy_run(y_hbm, src0, ybuf.at[s], dst0, n, sem.at[s], ei % 2)
            total = total + n
        pending[s] = total

    @pl.when(i == 0)
    def _():
        ybuf[...] = jnp.zeros(ybuf.shape, f32)
        gather(tab_cur_ref, 0)

    @pl.when(i + 1 < nt)
    def _():
        gather(tab_nxt_ref, (i + 1) % 2)

    _wait_chunks(y_hbm, sem.at[slot], pending[slot], (ls // MOE_CH).bit_length())

    lrow = lrow_ref[...]
    gates = gate_ref[...]
    col = lax.broadcasted_iota(i32, (td, ls), 1)
    weight = jnp.zeros((td, ls), f32)
    for k in range(TOP_K):
        weight = weight + jnp.where(col == lrow[:, k:k + 1], gates[:, k:k + 1], 0.0)
    wh, wl = _split_bf16(weight)
    yb = ybuf[slot].astype(bf16)
    moe = jnp.dot(wh, yb, preferred_element_type=f32) + jnp.dot(wl, yb, preferred_element_type=f32)
    o_ref[...] = _layernorm(ALPHA * x_ref[...] + moe, g_ref[...], b_ref[...])


def _combine_ln(y, tab, lrow, blk0, gates, x, g, b, td):
    rows, d = x.shape
    nt = rows // td
    ls = td * TOP_K + N_EXPERTS * MOE_CH
    grid_spec = pltpu.PrefetchScalarGridSpec(
        num_scalar_prefetch=0,
        grid=(nt,),
        in_specs=[
            pl.BlockSpec((1, 1, LANES), lambda i: (blk0 + i, 0, 0), memory_space=pltpu.SMEM),
            pl.BlockSpec((1, 1, LANES), lambda i: (blk0 + jnp.minimum(i + 1, nt - 1), 0, 0),
                         memory_space=pltpu.SMEM),
            pl.BlockSpec(memory_space=pl.ANY),
            pl.BlockSpec((td, LANES), lambda i: (blk0 + i, 0)),
            pl.BlockSpec((td, LANES), lambda i: (i, 0)),
            pl.BlockSpec((td, d), lambda i: (i, 0)),
            pl.BlockSpec((1, d), lambda i: (0, 0)),
            pl.BlockSpec((1, d), lambda i: (0, 0)),
        ],
        out_specs=pl.BlockSpec((td, d), lambda i: (i, 0)),
        scratch_shapes=[pltpu.VMEM((2, ls, d), f32), pltpu.SMEM((2,), i32), pltpu.SemaphoreType.DMA((2,))],
    )
    return pl.pallas_call(
        functools.partial(_combine_ln_kernel, nt=nt),
        grid_spec=grid_spec,
        out_shape=jax.ShapeDtypeStruct((rows, d), f32),
        compiler_params=_cparams(("arbitrary",)),
        name="combine_ln",
    )(tab, tab, y, lrow, gates, x, g.reshape(1, d), b.reshape(1, d))


def _moe_ln(xs, top_es, gates, cnts, td, l, w_up, b_up, w_down, b_down, g, b):
    x_p, x_s = xs
    n_p, n_s = x_p.shape[0], x_s.shape[0]
    n = n_p + n_s
    bm = MOE_BM
    nt = n // td
    nblk = -(-(n * TOP_K + nt * N_EXPERTS * (MOE_CH - 1)) // bm) + N_EXPERTS
    nblk = -(-nblk // MOE_PAIR) * MOE_PAIR
    cnt = jnp.concatenate([c[:, 0, :N_EXPERTS] for c in cnts], axis=0).astype(i32)
    run = (cnt + MOE_CH - 1) // MOE_CH
    in_tile = jnp.cumsum(run, axis=1) - run
    ech = jnp.sum(run, axis=0)
    blk_chunks = bm // MOE_CH
    padded_ch = (ech + blk_chunks - 1) // blk_chunks * blk_chunks
    est = jnp.cumsum(padded_ch) - padded_ch
    in_all = est[None, :] + jnp.cumsum(run, axis=0) - run
    tab = jnp.concatenate([in_all, run, in_tile, jnp.zeros_like(run)], axis=1).reshape(nt, 1, LANES)
    tabv = jnp.pad((in_tile * MOE_CH).astype(f32), ((0, 0), (0, LANES - N_EXPERTS)))
    tabv = jnp.broadcast_to(tabv[:, :, None], (nt, LANES, LANES))
    xg, lrow = _dispatch(xs, top_es, tab, tabv, ech, est, nblk, td)
    padded = padded_ch * MOE_CH
    ends = jnp.cumsum(padded)
    blk = jnp.arange(nblk, dtype=i32)
    blk_e = jnp.minimum(jnp.sum((ends[None, :] <= (blk * bm)[:, None]).astype(i32), axis=1), N_EXPERTS - 1)
    nact = (ends[-1] // bm).astype(i32).reshape(1)
    first = ((blk < nact[0]) & ((blk == 0) | (blk_e != jnp.roll(blk_e, 1)))).astype(i32)
    seg = jnp.cumsum(first) - 1
    eid = jnp.arange(N_EXPERTS, dtype=i32)
    later_nonempty = (padded[None, :] > 0) & (eid[None, :] > eid[:, None])
    nxt_of_e = jnp.min(jnp.where(later_nonempty, eid[None, :], N_EXPERTS), axis=1)
    nxt = jnp.where(nxt_of_e < N_EXPERTS, nxt_of_e, -1)[blk_e].astype(i32)
    y = _moe_experts(xg, blk_e, nact, first, seg, nxt, l, w_up, b_up, w_down, b_down)
    return (_combine_ln(y, tab, lrow, 0, gates[0], x_p, g, b, td),
            _combine_ln(y, tab, lrow, n_p // td, gates[1], x_s, g, b, td))


def kernel(x_prompt, x_sample, cache_a_k, cache_a_v, state_b_conv, state_b_h, state_c_s, ab_w_in, ab_rel_bias, ab_conv_w, ab_conv_b, ab_w_rg, ab_b_rg, ab_w_ig, ab_b_ig, ab_lambda, ab_w_out, c_w_in, c_gn_g, c_gn_b, c_w_out, ln1_g, ln1_b, ln2_g, ln2_b, moe_w_router, moe_b_router, moe_w_up, moe_b_up, moe_w_down, moe_b_down):
    bp, tp, d = x_prompt.shape
    bs, ts, _ = x_sample.shape
    assert bp == 1 and tp % RET_TB == 0 and ts <= CHUNK
    n_p, n_s = bp * tp, bs * ts
    xs = (x_prompt.reshape(n_p, d), x_sample.reshape(n_s, d))
    tms = tuple(_pick(m, (512, 256, 128, 64)) for m in (n_p, n_s))
    td = _pick(math.gcd(n_p, n_s), (256, 128, 64, 32))
    moe = (moe_w_up, moe_b_up, moe_w_down, moe_b_down)
    router = (moe_w_router, moe_b_router)

    def per_group(fn, *groups):
        return tuple(zip(*[fn(*args) for args in zip(*groups)]))

    j = 0
    w_in = ab_w_in[j].astype(bf16)
    z_p, z_s = (_matmul(x, w_in, tm, w_in.shape[1]) for x, tm in zip(xs, tms))
    att = (_attn_prompt(z_p, ab_rel_bias[j], tp),
           _attn_sample(z_s, cache_a_k[j], cache_a_v[j], ab_rel_bias[j], 0, bs, ts))
    lru_w = (ab_conv_w[j], ab_conv_b[j], _block_diag(ab_w_rg[j]).astype(bf16), ab_b_rg[j],
             _block_diag(ab_w_ig[j]).astype(bf16), ab_b_ig[j], jax.nn.softplus(-ab_lambda[j]))
    rec_p, bc_p, bh_p = _rglru(z_p, jnp.zeros((bp, CONV_W - 1, B_WIDTH), f32), jnp.zeros((bp, B_WIDTH), f32),
                               *lru_w, 0, bp, tp, _pick(tp, (LRU_TB, 128, 64)))
    rec_s, bc_s, bh_s = _rglru(z_s, state_b_conv[j], state_b_h[j], *lru_w, 0, bs, ts, ts)
    w_out = ab_w_out[j].astype(bf16)
    w_out = [w_out[:A_WIDTH], w_out[A_WIDTH:]]
    xs, top_es, gates, cnts = per_group(
        lambda a, r, x, tm: _proj_ln([a, r], w_out, x, ln1_g[0], ln1_b[0], *router, 0, tm, td),
        att, (rec_p, rec_s), xs, tms)
    xs = _moe_ln(xs, top_es, gates, cnts, td, 0, *moe, ln2_g[0], ln2_b[0])

    keep = min(A_PAST_CHUNKS * CHUNK, tp)
    ak_p = z_p[n_p - keep:, A_WIDTH:2 * A_WIDTH].reshape(1, bp, keep, HA, DHA)
    av_p = z_p[n_p - keep:, 2 * A_WIDTH:3 * A_WIDTH].reshape(1, bp, keep, HA, DHA)
    ak_s = z_s[:, A_WIDTH:2 * A_WIDTH].reshape(1, bs, ts, HA, DHA)
    av_s = z_s[:, 2 * A_WIDTH:3 * A_WIDTH].reshape(1, bs, ts, HA, DHA)

    w_in = c_w_in[j].astype(bf16)
    y_p, cs_p = _retention(xs[0], w_in, None, 0, c_gn_g[j], c_gn_b[j], bp, tp, RET_TB, CHUNK)
    y_s, cs_s = _retention(xs[1], w_in, state_c_s[j], PAST_LEN, c_gn_g[j], c_gn_b[j], bs, ts, ts, ts)
    w_out = [c_w_out[j].astype(bf16)]
    xs, top_es, gates, cnts = per_group(
        lambda y, x, tm: _proj_ln([y], w_out, x, ln1_g[1], ln1_b[1], *router, 1, tm, td),
        (y_p, y_s), xs, tms)
    out_p, out_s = _moe_ln(xs, top_es, gates, cnts, td, 1, *moe, ln2_g[1], ln2_b[1])

    return (out_p.reshape(bp, tp, d), out_s.reshape(bs, ts, d),
            ak_p, av_p, bc_p[None], bh_p.reshape(1, bp, B_WIDTH), cs_p[None],
            ak_s, av_s, bc_s[None], bh_s.reshape(1, bs, B_WIDTH), cs_s[None])
```
